```python
import math
import jax, jax.numpy as jnp
from jax import lax
import numpy as np

D_MODEL = 1024
BATCH = 4
SEQ = 4096
DEPTH = 1

EPS = 1e-6
GMLP_WIDTH = 512
GMLP_GROUPS = 8
CHUNK = 128
N_HEADS = 8
HEAD_DIM = 64
ATTN_WIDTH = N_HEADS * HEAD_DIM
DILATED_PATTERNS = ((128, 1), (512, 4), (2048, 16))
Q_BLOCK = 128
N_EXPERTS = 16
EXPERT_HIDDEN = 1024
CAPACITY_FACTOR = 2
IN_COLS = 2 * GMLP_WIDTH + 3 * ATTN_WIDTH + 2 * D_MODEL

kernel_name = "hybrid_gmlp_dilated_attn_ec_moe"


def _rmsnorm(x, g):
    xf = x.astype(jnp.float32)
    y = xf * lax.rsqrt(jnp.mean(xf * xf, axis=-1, keepdims=True) + EPS)
    return (y * g.astype(jnp.float32)).astype(x.dtype)


def _gmlp_branch(u, v, gmlp_norm_g, w_spatial, b_spatial):
    B, S, _ = v.shape
    gd = GMLP_WIDTH // GMLP_GROUPS
    v = _rmsnorm(v, gmlp_norm_g)
    vc = v.reshape(B, S // CHUNK, CHUNK, GMLP_GROUPS, gd)
    mixed = jnp.einsum('gts,bcsgd->bctgd', w_spatial.astype(v.dtype), vc)
    mixed = mixed + b_spatial.T.astype(v.dtype)[None, None, :, :, None]
    return u * mixed.reshape(B, S, GMLP_WIDTH)


def _dilated_attention(q, k, v, slopes):
    B, H, S, hd = q.shape
    scale = hd ** -0.5
    n_blocks = S // Q_BLOCK

    def block(i):
        q0 = i * Q_BLOCK
        qb = lax.dynamic_slice_in_dim(q, q0, Q_BLOCK, axis=2).astype(jnp.float32)
        t = q0 + jnp.arange(Q_BLOCK)
        outs, lses = [], []
        for win, dil in DILATED_PATTERNS:
            half = win // (2 * dil)
            offs = dil * jnp.arange(-half, half + 1)
            pos = t[:, None] + offs[None, :]
            valid = (pos >= 0) & (pos < S)
            pos_c = jnp.clip(pos, 0, S - 1)
            kb = jnp.take(k, pos_c, axis=2).astype(jnp.float32)
            vb = jnp.take(v, pos_c, axis=2).astype(jnp.float32)
            s = jnp.einsum('bhqd,bhqkd->bhqk', qb, kb) * scale
            s = s - slopes[None, :, None, None] * jnp.abs(offs).astype(jnp.float32)[None, None, None, :]
            s = jnp.where(valid[None, None], s, -jnp.inf)
            m = jnp.max(s, axis=-1, keepdims=True)
            p = jnp.exp(s - m)
            den = jnp.sum(p, axis=-1)
            o = jnp.einsum('bhqk,bhqkd->bhqd', p, vb) / den[..., None]
            outs.append(o)
            lses.append(m[..., 0] + jnp.log(den))
        w = jax.nn.softmax(jnp.stack(lses, axis=0), axis=0)
        return jnp.sum(w[..., None] * jnp.stack(outs, axis=0), axis=0)

    o = lax.map(block, jnp.arange(n_blocks))
    o = o.transpose(1, 2, 0, 3, 4).reshape(B, H, S, hd)
    return o.astype(q.dtype)


def _expert_choice_moe(h, w_router, w_e_gate, w_e_up, w_e_down):
    B, S, D = h.shape
    cap = CAPACITY_FACTOR * S // N_EXPERTS
    logits = jnp.einsum('bsd,de->bse', h, w_router).astype(jnp.float32)
    aff = jax.nn.softmax(logits, axis=-1)
    gate, idx = lax.top_k(aff.transpose(0, 2, 1), cap)
    xe = jax.vmap(lambda hb, ib: hb[ib])(h, idx)
    g = jnp.einsum('becd,edf->becf', xe, w_e_gate)
    up = jnp.einsum('becd,edf->becf', xe, w_e_up)
    ye = jnp.einsum('becf,efd->becd', jax.nn.silu(g) * up, w_e_down)
    ye = ye * gate.astype(ye.dtype)[..., None]
    out = jax.vmap(lambda ib, yb: jnp.zeros((S, D), yb.dtype).at[ib.reshape(-1)].add(yb.reshape(-1, D)))(idx, ye)
    return out


def setup_inputs(seed: int = 0) -> dict:
    key = jax.random.key(seed)
    ks = jax.random.split(key, 20)
    f32 = jnp.float32
    L, D = DEPTH, D_MODEL
    nrm = lambda k, shape, fan_in: jax.random.normal(k, shape, f32) * (fan_in ** -0.5)
    gain = lambda k, shape: 1.0 + 0.05 * jax.random.normal(k, shape, f32)
    return {
        "x": jax.random.normal(ks[0], (BATCH, SEQ, D), f32),
        "norm_mix_g": gain(ks[1], (L, D)),
        "w_in": nrm(ks[2], (L, D, IN_COLS), D),
        "b_gate": 0.02 * jax.random.normal(ks[3], (L, 2 * D), f32),
        "gmlp_norm_g": gain(ks[4], (L, GMLP_WIDTH)),
        "w_spatial": nrm(ks[5], (L, GMLP_GROUPS, CHUNK, CHUNK), CHUNK),
        "b_spatial": gain(ks[6], (L, GMLP_GROUPS, CHUNK)),
        "w_proj_a": nrm(ks[7], (L, GMLP_WIDTH, D), GMLP_WIDTH),
        "w_proj_b": nrm(ks[8], (L, ATTN_WIDTH, D), ATTN_WIDTH),
        "w_out": nrm(ks[9], (L, D, D), D),
        "norm_ffn_g": gain(ks[10], (L, D)),
        "w_router": nrm(ks[11], (L, D, N_EXPERTS), D),
        "w_e_gate": nrm(ks[12], (L, N_EXPERTS, D, EXPERT_HIDDEN), D),
        "w_e_up": nrm(ks[13], (L, N_EXPERTS, D, EXPERT_HIDDEN), D),
        "w_e_down": nrm(ks[14], (L, N_EXPERTS, EXPERT_HIDDEN, D), EXPERT_HIDDEN),
        "norm_final_g": gain(ks[15], (D,)),
    }


def reference(x, norm_mix_g, w_in, b_gate, gmlp_norm_g, w_spatial, b_spatial, w_proj_a, w_proj_b,
              w_out, norm_ffn_g, w_router, w_e_gate, w_e_up, w_e_down, norm_final_g):
    B, S, D = x.shape
    slopes = 2.0 ** (-8.0 * jnp.arange(1, N_HEADS + 1, dtype=jnp.float32) / N_HEADS)
    splits = np.cumsum([GMLP_WIDTH, GMLP_WIDTH, ATTN_WIDTH, ATTN_WIDTH, ATTN_WIDTH, D_MODEL]).tolist()
    for l in range(DEPTH):
        h = _rmsnorm(x, norm_mix_g[l])
        proj = jnp.einsum('bsd,dc->bsc', h, w_in[l])
        u, v, q, k, va, ga, gb = jnp.split(proj, splits, axis=-1)
        ga = jax.nn.sigmoid(ga + b_gate[l, :D].astype(ga.dtype))
        gb = jax.nn.sigmoid(gb + b_gate[l, D:].astype(gb.dtype))
        a = _gmlp_branch(jax.nn.gelu(u), jax.nn.gelu(v), gmlp_norm_g[l], w_spatial[l], b_spatial[l])
        to_heads = lambda t: t.reshape(B, S, N_HEADS, HEAD_DIM).transpose(0, 2, 1, 3)
        o = _dilated_attention(to_heads(q), to_heads(k), to_heads(va), slopes)
        o = o.transpose(0, 2, 1, 3).reshape(B, S, ATTN_WIDTH)
        merged = ga * jnp.einsum('bsc,cd->bsd', a, w_proj_a[l]) + gb * jnp.einsum('bsc,cd->bsd', o, w_proj_b[l])
        x = x + jnp.einsum('bsd,de->bse', merged, w_out[l])
        h2 = _rmsnorm(x, norm_ffn_g[l])
        x = x + _expert_choice_moe(h2, w_router[l], w_e_gate[l], w_e_up[l], w_e_down[l])
    return _rmsnorm(x, norm_final_g)
```

```python
import functools

import jax
import jax.numpy as jnp
from jax import lax
from jax.experimental import pallas as pl
from jax.experimental.pallas import tpu as pltpu

F32 = jnp.float32
BF16 = jnp.bfloat16
I32 = jnp.int32

EPS = 1e-6
GMLP_WIDTH = 512
GMLP_GROUPS = 8
CHUNK = 128
N_HEADS = 8
HEAD_DIM = 64
ATTN_WIDTH = N_HEADS * HEAD_DIM
DILATIONS = (1, 4, 16)
HALF_WINDOW = 64
N_EXPERTS = 16
CAPACITY_FACTOR = 2

LANES = 128
Q_TILE = 128
KEY_TILE = 2 * Q_TILE
SLOT_TILE = 128
ROW_TILE = 512
MIB = 1024 * 1024


def _cparams(sem, vmem_mib):
    return pltpu.CompilerParams(dimension_semantics=sem, vmem_limit_bytes=vmem_mib * MIB)


def _gelu_tanh(x):
    return 0.5 * x * (1.0 + jnp.tanh(0.7978845608028654 * (x + 0.044715 * (x * x * x))))


def _sigmoid(x):
    return 1.0 / (1.0 + jnp.exp(-x))


def _rms(x, g):
    return x * lax.rsqrt(jnp.mean(x * x, axis=-1, keepdims=True) + EPS) * g


def _dot(a, b):
    return jnp.dot(a, b, preferred_element_type=F32)


def _dot_nt(a, b):
    return lax.dot_general(a, b, (((1,), (1,)), ((), ())), preferred_element_type=F32)


def _dot_tn(a, b):
    return lax.dot_general(a, b, (((0,), (0,)), ((), ())), preferred_element_type=F32)


def _mix_in_kernel(x_ref, g_ref, win_ref, bg_ref, g2_ref, ws_ref, bsp_ref, pa_ref,
                   q_ref, k_ref, v_ref, ta_ref, gb_ref):
    d_model = x_ref.shape[1]
    h = _rms(x_ref[...], g_ref[...]).astype(BF16)

    def proj(lo, width):
        return _dot(h, win_ref[:, lo:lo + width])

    c0 = 0
    u = _gelu_tanh(proj(c0, GMLP_WIDTH)); c0 += GMLP_WIDTH
    v = _gelu_tanh(proj(c0, GMLP_WIDTH)); c0 += GMLP_WIDTH
    q_ref[...] = proj(c0, ATTN_WIDTH).astype(BF16); c0 += ATTN_WIDTH
    k_ref[...] = proj(c0, ATTN_WIDTH).astype(BF16); c0 += ATTN_WIDTH
    v_ref[...] = proj(c0, ATTN_WIDTH).astype(BF16); c0 += ATTN_WIDTH
    ga = _sigmoid(proj(c0, d_model) + bg_ref[:, :d_model]); c0 += d_model
    gb = _sigmoid(proj(c0, d_model) + bg_ref[:, d_model:])
    gb_ref[...] = gb.astype(BF16)

    vn = _rms(v, g2_ref[...]).astype(BF16)
    lane_lo = lax.broadcasted_iota(I32, (CHUNK, LANES), 1) < HEAD_DIM
    bsp = bsp_ref[...]
    a_chunks = []
    for c in range(x_ref.shape[0] // CHUNK):
        rows = slice(c * CHUNK, (c + 1) * CHUNK)
        cols = []
        for p in range(GMLP_WIDTH // LANES):
            r = _dot(ws_ref[p], vn[rows, p * LANES:(p + 1) * LANES])
            cols.append(jnp.where(lane_lo, r[:CHUNK], r[CHUNK:]))
        mixed = jnp.concatenate(cols, axis=1) + bsp
        a_chunks.append((u[rows] * mixed).astype(BF16))
    a = jnp.concatenate(a_chunks, axis=0)
    ta_ref[...] = (ga * _dot(a, pa_ref[...])).astype(BF16)


def _mix_in(x2, g, w_in, b_gate, g2, ws_pairs, bsp, w_pa):
    n, d = x2.shape
    const = lambda shape: pl.BlockSpec(shape, lambda i: (0,) * len(shape))
    row = lambda w: pl.BlockSpec((ROW_TILE, w), lambda i: (i, 0))
    return pl.pallas_call(
        _mix_in_kernel,
        grid=(n // ROW_TILE,),
        in_specs=[row(d), const(g.shape), const(w_in.shape), const(b_gate.shape), const(g2.shape),
                  const(ws_pairs.shape), const(bsp.shape), const(w_pa.shape)],
        out_specs=[row(ATTN_WIDTH), row(ATTN_WIDTH), row(ATTN_WIDTH), row(d), row(d)],
        out_shape=[jax.ShapeDtypeStruct((n, ATTN_WIDTH), BF16)] * 3
                  + [jax.ShapeDtypeStruct((n, d), BF16)] * 2,
        compiler_params=_cparams(("parallel",), 56),
        name="mix_in",
    )(x2, g, w_in, b_gate, g2, ws_pairs, bsp, w_pa)


def _attn_kernel(*refs, dil, q_rows, has_prev, is_last):
    if has_prev:
        q_ref, k_ref, v_ref, op_ref, lp_ref = refs[:5]
        outs = refs[5:]
    else:
        q_ref, k_ref, v_ref = refs[:3]
        outs = refs[3:]
    o_ref = outs[0]
    seq = k_ref.shape[0]
    t = pl.program_id(2)
    lane_lo = lax.broadcasted_iota(I32, (Q_TILE, LANES), 1) < HEAD_DIM
    mask_lo = jnp.where(lane_lo, 1.0, 0.0).astype(BF16)
    mask_hi = jnp.where(lane_lo, 0.0, 1.0).astype(BF16)
    ii = lax.broadcasted_iota(I32, (Q_TILE, KEY_TILE), 0)
    jj = lax.broadcasted_iota(I32, (Q_TILE, KEY_TILE), 1)
    scale = HEAD_DIM ** -0.5
    for qi in range(q_rows // Q_TILE):
        rows = slice(qi * Q_TILE, (qi + 1) * Q_TILE)
        i0 = t * q_rows + qi * Q_TILE
        start = pl.multiple_of(jnp.clip(i0 - HALF_WINDOW, 0, seq - KEY_TILE), HALF_WINDOW)
        absd = jnp.abs(jj - ii + (start - i0))
        valid = absd <= HALF_WINDOW
        absf = absd.astype(F32)
        for p in range(ATTN_WIDTH // LANES):
            cs = slice(p * LANES, (p + 1) * LANES)
            qp = q_ref[rows, cs]
            kp = k_ref[pl.ds(start, KEY_TILE), cs]
            vp = v_ref[pl.ds(start, KEY_TILE), cs]
            q2 = jnp.concatenate([qp * mask_lo, qp * mask_hi], axis=0)
            s2 = _dot_nt(q2, kp)
            probs, maxes, dens = [], [], []
            for hh in range(2):
                slope = 2.0 ** (-8.0 * (2 * p + hh + 1) / N_HEADS)
                s = s2[hh * Q_TILE:(hh + 1) * Q_TILE] * scale - (slope * dil) * absf
                s = jnp.where(valid, s, -jnp.inf)
                m = jnp.max(s, axis=-1, keepdims=True)
                e = jnp.exp(s - m)
                probs.append(e)
                maxes.append(m)
                dens.append(jnp.sum(e, axis=-1, keepdims=True))
            o2 = _dot(jnp.concatenate(probs, axis=0).astype(BF16), vp)
            o_c = jnp.where(lane_lo, o2[:Q_TILE] / dens[0], o2[Q_TILE:] / dens[1])
            lse_c = jnp.where(lane_lo, maxes[0] + jnp.log(dens[0]), maxes[1] + jnp.log(dens[1]))
            if has_prev:
                o_p = op_ref[rows, cs]
                lse_p = lp_ref[rows, cs]
                m2 = jnp.maximum(lse_p, lse_c)
                w_p = jnp.exp(lse_p - m2)
                w_c = jnp.exp(lse_c - m2)
                den = w_p + w_c
                o_c = (w_p * o_p + w_c * o_c) / den
                lse_c = m2 + jnp.log(den)
            o_ref[rows, cs] = o_c.astype(o_ref.dtype)
            if not is_last:
                outs[1][rows, cs] = lse_c


def _attn_pattern(q, k, v, prev, dil, is_last):
    b, s, w = q.shape
    seq = s // dil
    q_rows = min(seq, 2 * Q_TILE)
    view = lambda a: a.reshape(b, seq, dil * w)
    qspec = pl.BlockSpec((None, q_rows, w), lambda bi, r, t: (bi, t, r))
    kspec = pl.BlockSpec((None, seq, w), lambda bi, r, t: (bi, 0, r))
    args = [view(q), view(k), view(v)]
    in_specs = [qspec, kspec, kspec]
    if prev is not None:
        args += [view(prev[0]), view(prev[1])]
        in_specs += [qspec, qspec]
    if is_last:
        out_shape = [jax.ShapeDtypeStruct((b, seq, dil * w), BF16)]
        out_specs = [qspec]
    else:
        out_shape = [jax.ShapeDtypeStruct((b, seq, dil * w), F32)] * 2
        out_specs = [qspec, qspec]
    outs = pl.pallas_call(
        functools.partial(_attn_kernel, dil=dil, q_rows=q_rows, has_prev=prev is not None,
                          is_last=is_last),
        grid=(b, dil, seq // q_rows),
        in_specs=in_specs,
        out_specs=out_specs,
        out_shape=out_shape,
        compiler_params=_cparams(("parallel", "parallel", "arbitrary"), 48),
        name=f"attn_d{dil}",
    )(*args)
    return [o.reshape(b, s, w) for o in outs]


def _mix_out_kernel(x_ref, ta_ref, gb_ref, o_ref, pb_ref, wo_ref, g_ref, wr_ref,
                    x1_ref, h2_ref, aff_ref):
    ob = _dot(o_ref[...], pb_ref[...])
    merged = (ta_ref[...].astype(F32) + gb_ref[...].astype(F32) * ob).astype(BF16)
    x1 = x_ref[...] + _dot(merged, wo_ref[...])
    x1_ref[...] = x1
    h2 = _rms(x1, g_ref[...])
    h2_ref[...] = h2.astype(BF16)
    h_hi = h2.astype(BF16)
    h_lo = (h2 - h_hi.astype(F32)).astype(BF16)
    wr = wr_ref[...]
    w_hi = wr.astype(BF16)
    w_lo = (wr - w_hi.astype(F32)).astype(BF16)
    logits = _dot_nt(w_hi, h_hi) + (_dot_nt(w_hi, h_lo) + _dot_nt(w_lo, h_hi))
    e = jnp.exp(logits - jnp.max(logits, axis=0, keepdims=True))
    aff_ref[...] = e / jnp.sum(e, axis=0, keepdims=True)


def _mix_out(x, ta, gb, o, w_pb, w_out, g, w_router_t):
    b, s, d = x.shape
    n_e = w_router_t.shape[0]
    const = lambda shape: pl.BlockSpec(shape, lambda bi, t: (0,) * len(shape))
    row = lambda w: pl.BlockSpec((None, ROW_TILE, w), lambda bi, t: (bi, t, 0))
    return pl.pallas_call(
        _mix_out_kernel,
        grid=(b, s // ROW_TILE),
        in_specs=[row(d), row(d), row(d), row(ATTN_WIDTH), const(w_pb.shape), const(w_out.shape),
                  const(g.shape), const(w_router_t.shape)],
        out_specs=[row(d), row(d), pl.BlockSpec((None, n_e, ROW_TILE), lambda bi, t: (bi, 0, t))],
        out_shape=[jax.ShapeDtypeStruct((b, s, d), F32), jax.ShapeDtypeStruct((b, s, d), BF16),
                   jax.ShapeDtypeStruct((b, n_e, s), F32)],
        compiler_params=_cparams(("parallel", "parallel"), 48),
        name="mix_out",
    )(x, ta, gb, o, w_pb, w_out, g, w_router_t)


def _topk_kernel(aff_ref, rank_ref, cum_ref, *, cap):
    n_e, s = aff_ref.shape
    n_blk = s // SLOT_TILE
    aff = aff_ref[...]
    thr = jnp.zeros((n_e, 1), I32)
    for bit in range(30, -1, -1):
        cand = thr | (1 << bit)
        cnt = jnp.sum((aff >= pltpu.bitcast(cand, F32)).astype(I32), axis=1, keepdims=True)
        thr = jnp.where(cnt >= cap, cand, thr)
    above = aff >= pltpu.bitcast(thr + 1, F32)
    tie = jnp.logical_and(aff >= pltpu.bitcast(thr, F32), jnp.logical_not(above))
    need = (cap - jnp.sum(above.astype(I32), axis=1, keepdims=True)).astype(F32)
    r_i = lax.broadcasted_iota(I32, (SLOT_TILE, SLOT_TILE), 0)
    c_i = lax.broadcasted_iota(I32, (SLOT_TILE, SLOT_TILE), 1)
    tri = jnp.where(r_i < c_i, 1.0, 0.0).astype(BF16)
    lane = lax.broadcasted_iota(I32, (n_e, LANES), 1)
    run_tie = jnp.zeros((n_e, 1), F32)
    run_sel = jnp.zeros((n_e, 1), F32)
    cum = jnp.zeros((n_e, LANES), F32)
    for j in range(n_blk):
        cs = slice(j * SLOT_TILE, (j + 1) * SLOT_TILE)
        tie_f = jnp.where(tie[:, cs], 1.0, 0.0)
        tie_rank = _dot(tie_f.astype(BF16), tri) + run_tie
        run_tie = run_tie + jnp.sum(tie_f, axis=1, keepdims=True)
        sel_f = jnp.where(above[:, cs], 1.0, jnp.where(tie_rank < need, tie_f, 0.0))
        rank = _dot(sel_f.astype(BF16), tri) + run_sel
        rank_ref[:, cs] = jnp.where(sel_f > 0.0, rank, -1.0).astype(I32)
        cum = jnp.where(lane == j, run_sel, cum)
        run_sel = run_sel + jnp.sum(sel_f, axis=1, keepdims=True)
    cum = jnp.where(lane == n_blk, run_sel, cum)
    cum_ref[...] = cum.astype(I32)


def _topk(aff, cap):
    b, n_e, s = aff.shape
    return pl.pallas_call(
        functools.partial(_topk_kernel, cap=cap),
        grid=(b,),
        in_specs=[pl.BlockSpec((None, n_e, s), lambda bi: (bi, 0, 0))],
        out_specs=[pl.BlockSpec((None, n_e, s), lambda bi: (bi, 0, 0)),
                   pl.BlockSpec((None, n_e, LANES), lambda bi: (bi, 0, 0))],
        out_shape=[jax.ShapeDtypeStruct((b, n_e, s), I32),
                   jax.ShapeDtypeStruct((b, n_e, LANES), I32)],
        compiler_params=_cparams(("parallel",), 32),
        name="topk",
    )(aff)


def _moe_kernel(cum_ref, h2_ref, rank_ref, aff_ref, wg_ref, wu_ref, wd_ref, out_ref,
                xe_ref, ye_ref, gate_ref, *, cap):
    s = h2_ref.shape[0]
    n_blk = s // SLOT_TILE
    n_tile = cap // SLOT_TILE
    bi = pl.program_id(0)
    e = pl.program_id(1)
    base = (bi * pl.num_programs(1) + e) * LANES
    slot_i = lax.broadcasted_iota(I32, (SLOT_TILE, SLOT_TILE), 0)

    @pl.when(e == 0)
    def _():
        out_ref[...] = jnp.zeros_like(out_ref)

    xe_ref[...] = jnp.zeros_like(xe_ref)
    gate_ref[...] = jnp.zeros_like(gate_ref)

    def overlaps(t, j):
        return jnp.logical_and(cum_ref[base + j] < (t + 1) * SLOT_TILE,
                               cum_ref[base + j + 1] > t * SLOT_TILE)

    def onehot(t, j):
        rk = rank_ref[:, pl.ds(pl.multiple_of(j * SLOT_TILE, SLOT_TILE), SLOT_TILE)]
        return (slot_i + t * SLOT_TILE) == rk

    for t in range(n_tile):
        rows = slice(t * SLOT_TILE, (t + 1) * SLOT_TILE)

        def gather_blk(j, carry, t=t, rows=rows):
            @pl.when(overlaps(t, j))
            def _():
                tok = pl.ds(pl.multiple_of(j * SLOT_TILE, SLOT_TILE), SLOT_TILE)
                hit = onehot(t, j)
                xe_ref[rows, :] += _dot(jnp.where(hit, 1.0, 0.0).astype(BF16), h2_ref[tok, :])
                gate_ref[rows, :] += jnp.sum(jnp.where(hit, aff_ref[:, tok], 0.0), axis=1,
                                             keepdims=True)
            return carry

        lax.fori_loop(0, n_blk, gather_blk, 0)

    xe = xe_ref[...].astype(BF16)
    gate_h = _dot(xe, wg_ref[...])
    up_h = _dot(xe, wu_ref[...])
    hidden = (gate_h * _sigmoid(gate_h) * up_h).astype(BF16)
    ye = _dot(hidden, wd_ref[...])
    ye_ref[...] = (ye * gate_ref[:, 0:1]).astype(BF16)

    for t in range(n_tile):
        rows = slice(t * SLOT_TILE, (t + 1) * SLOT_TILE)

        def scatter_blk(j, carry, t=t, rows=rows):
            @pl.when(overlaps(t, j))
            def _():
                tok = pl.ds(pl.multiple_of(j * SLOT_TILE, SLOT_TILE), SLOT_TILE)
                hit = jnp.where(onehot(t, j), 1.0, 0.0).astype(BF16)
                out_ref[tok, :] += _dot_tn(hit, ye_ref[rows, :])
            return carry

        lax.fori_loop(0, n_blk, scatter_blk, 0)


def _moe(cum_flat, h2, rank4, aff4, wg, wu, wd, cap):
    b, s, d = h2.shape
    n_e, _, hid = wg.shape
    once = pl.Buffered(1)
    grid_spec = pltpu.PrefetchScalarGridSpec(
        num_scalar_prefetch=1,
        grid=(b, n_e),
        in_specs=[
            pl.BlockSpec((None, s, d), lambda bi, e, c: (bi, 0, 0), pipeline_mode=once),
            pl.BlockSpec((None, None, 1, s), lambda bi, e, c: (bi, e, 0, 0)),
            pl.BlockSpec((None, None, 1, s), lambda bi, e, c: (bi, e, 0, 0)),
            pl.BlockSpec((None, d, hid), lambda bi, e, c: (e, 0, 0)),
            pl.BlockSpec((None, d, hid), lambda bi, e, c: (e, 0, 0)),
            pl.BlockSpec((None, hid, d), lambda bi, e, c: (e, 0, 0)),
        ],
        out_specs=pl.BlockSpec((None, s, d), lambda bi, e, c: (bi, 0, 0), pipeline_mode=once),
        scratch_shapes=[pltpu.VMEM((cap, d), F32), pltpu.VMEM((cap, d), BF16),
                        pltpu.VMEM((cap, LANES), F32)],
    )
    return pl.pallas_call(
        functools.partial(_moe_kernel, cap=cap),
        grid_spec=grid_spec,
        out_shape=jax.ShapeDtypeStruct((b, s, d), F32),
        compiler_params=_cparams(("arbitrary", "arbitrary"), 60),
        name="moe",
    )(cum_flat, h2, rank4, aff4, wg, wu, wd)


def _final_kernel(x1_ref, moe_ref, g_ref, y_ref):
    y_ref[...] = _rms(x1_ref[...] + moe_ref[...], g_ref[...])


def _final(x1, moe, g):
    n, d = x1.shape
    row = pl.BlockSpec((ROW_TILE, d), lambda i: (i, 0))
    return pl.pallas_call(
        _final_kernel,
        grid=(n // ROW_TILE,),
        in_specs=[row, row, pl.BlockSpec(g.shape, lambda i: (0, 0))],
        out_specs=row,
        out_shape=jax.ShapeDtypeStruct((n, d), F32),
        compiler_params=_cparams(("parallel",), 32),
        name="final_norm",
    )(x1, moe, g)


def kernel(x, norm_mix_g, w_in, b_gate, gmlp_norm_g, w_spatial, b_spatial, w_proj_a, w_proj_b,
           w_out, norm_ffn_g, w_router, w_e_gate, w_e_up, w_e_down, norm_final_g):
    b, s, d = x.shape
    depth = w_in.shape[0]
    cap = CAPACITY_FACTOR * s // N_EXPERTS
    group_width = GMLP_WIDTH // GMLP_GROUPS
    for l in range(depth):
        ws_pairs = w_spatial[l].astype(BF16).reshape(GMLP_GROUPS // 2, 2 * CHUNK, CHUNK)
        bsp = jnp.repeat(b_spatial[l].T, group_width, axis=1)
        q, k, v, ta, gb = _mix_in(
            x.reshape(b * s, d), norm_mix_g[l][None], w_in[l].astype(BF16), b_gate[l][None],
            gmlp_norm_g[l][None], ws_pairs, bsp, w_proj_a[l].astype(BF16))
        q, k, v = (a.reshape(b, s, ATTN_WIDTH) for a in (q, k, v))
        prev = None
        for i, dil in enumerate(DILATIONS):
            prev = _attn_pattern(q, k, v, prev, dil, is_last=i == len(DILATIONS) - 1)
        o = prev[0]
        x1, h2, aff = _mix_out(
            x, ta.reshape(b, s, d), gb.reshape(b, s, d), o, w_proj_b[l].astype(BF16),
            w_out[l].astype(BF16), norm_ffn_g[l][None], w_router[l].T)
        rank, cum = _topk(aff, cap)
        moe = _moe(cum.reshape(-1), h2, rank.reshape(b, N_EXPERTS, 1, s),
                   aff.reshape(b, N_EXPERTS, 1, s), w_e_gate[l].astype(BF16),
                   w_e_up[l].astype(BF16), w_e_down[l].astype(BF16), cap)
        gfin = norm_final_g[None] if l == depth - 1 else None
        if gfin is None:
            x = x1 + moe
        else:
            x = _final(x1.reshape(b * s, d), moe.reshape(b * s, d), gfin).reshape(b, s, d)
    return x
```

```python
import functools

import jax
import jax.numpy as jnp
from jax import lax
from jax.experimental import pallas as pl
from jax.experimental.pallas import tpu as pltpu

F32 = jnp.float32
BF16 = jnp.bfloat16
I32 = jnp.int32

EPS = 1e-6
GMLP_WIDTH = 512
GMLP_GROUPS = 8
CHUNK = 128
N_HEADS = 8
HEAD_DIM = 64
ATTN_WIDTH = N_HEADS * HEAD_DIM
DILATIONS = (1, 4, 16)
HALF_WINDOW = 64
N_EXPERTS = 16
CAPACITY_FACTOR = 2

LANES = 128
Q_TILE = 128
KEY_TILE = 2 * Q_TILE
ATTN_STEP_ROWS = 1024
LSE_LANES = LANES // N_HEADS
SLOT_TILE = 128
SLOT_ALIGN = 16
SLOT_WINDOW = SLOT_TILE + SLOT_ALIGN
ROW_TILE = 512
MIB = 1024 * 1024


def _cparams(sem, vmem_mib):
    return pltpu.CompilerParams(dimension_semantics=sem, vmem_limit_bytes=vmem_mib * MIB)


def _gelu_tanh(x):
    return 0.5 * x * (1.0 + jnp.tanh(0.7978845608028654 * (x + 0.044715 * (x * x * x))))


def _sigmoid(x):
    return 1.0 / (1.0 + jnp.exp(-x))


def _rms(x, g):
    return x * lax.rsqrt(jnp.mean(x * x, axis=-1, keepdims=True) + EPS) * g


def _dot(a, b):
    return jnp.dot(a, b, preferred_element_type=F32)


def _dot_nt(a, b):
    return lax.dot_general(a, b, (((1,), (1,)), ((), ())), preferred_element_type=F32)


def _dot_tn(a, b):
    return lax.dot_general(a, b, (((0,), (0,)), ((), ())), preferred_element_type=F32)


def _mix_in_kernel(x_ref, g_ref, win_ref, bg_ref, g2_ref, ws_ref, bsp_ref, pa_ref, *refs):
    n_qkv = 3 * len(DILATIONS)
    qkv_refs = refs[:n_qkv]
    ta_ref, gb_ref = refs[n_qkv:n_qkv + 2]
    stage_refs = refs[n_qkv + 2:]
    rows, d_model = x_ref.shape
    h = _rms(x_ref[...], g_ref[...]).astype(BF16)

    def proj(lo, width):
        return _dot(h, win_ref[:, lo:lo + width])

    c0 = 0
    u = _gelu_tanh(proj(c0, GMLP_WIDTH)); c0 += GMLP_WIDTH
    v = _gelu_tanh(proj(c0, GMLP_WIDTH)); c0 += GMLP_WIDTH
    for i in range(3):
        val = proj(c0, ATTN_WIDTH); c0 += ATTN_WIDTH
        if i == 0:
            val = val * (HEAD_DIM ** -0.5)
        stage = stage_refs[i]
        for p in range(ATTN_WIDTH // LANES):
            stage[p] = val[:, p * LANES:(p + 1) * LANES]
        for di, dil in enumerate(DILATIONS):
            out = qkv_refs[3 * di + i]
            if dil == 1:
                out[0] = val.astype(BF16)
                continue
            for r in range(dil):
                for p in range(ATTN_WIDTH // LANES):
                    out[r, :, p * LANES:(p + 1) * LANES] = (
                        stage[p, pl.ds(r, rows // dil, stride=dil), :].astype(BF16))
    ga = _sigmoid(proj(c0, d_model) + bg_ref[:, :d_model]); c0 += d_model
    gb = _sigmoid(proj(c0, d_model) + bg_ref[:, d_model:])
    gb_ref[...] = gb.astype(BF16)

    vn = _rms(v, g2_ref[...]).astype(BF16)
    lane_lo = lax.broadcasted_iota(I32, (CHUNK, LANES), 1) < HEAD_DIM
    bsp = bsp_ref[...]
    a_chunks = []
    for c in range(rows // CHUNK):
        rs = slice(c * CHUNK, (c + 1) * CHUNK)
        cols = []
        for p in range(GMLP_WIDTH // LANES):
            r = _dot(ws_ref[p], vn[rs, p * LANES:(p + 1) * LANES])
            cols.append(jnp.where(lane_lo, r[:CHUNK], r[CHUNK:]))
        mixed = jnp.concatenate(cols, axis=1) + bsp
        a_chunks.append((u[rs] * mixed).astype(BF16))
    a = jnp.concatenate(a_chunks, axis=0)
    ta_ref[...] = (ga * _dot(a, pa_ref[...])).astype(BF16)


def _mix_in(x, g, w_in, b_gate, g2, ws_pairs, bsp, w_pa):
    b, s, d = x.shape
    const = lambda shape: pl.BlockSpec(shape, lambda bi, t: (0,) * len(shape))
    row = lambda w: pl.BlockSpec((None, ROW_TILE, w), lambda bi, t: (bi, t, 0))
    qkv_specs, qkv_shapes = [], []
    for dil in DILATIONS:
        spec = pl.BlockSpec((None, dil, ROW_TILE // dil, ATTN_WIDTH), lambda bi, t: (bi, 0, t, 0))
        qkv_specs += [spec] * 3
        qkv_shapes += [jax.ShapeDtypeStruct((b, dil, s // dil, ATTN_WIDTH), BF16)] * 3
    outs = pl.pallas_call(
        _mix_in_kernel,
        grid=(b, s // ROW_TILE),
        in_specs=[row(d), const(g.shape), const(w_in.shape), const(b_gate.shape), const(g2.shape),
                  const(ws_pairs.shape), const(bsp.shape), const(w_pa.shape)],
        out_specs=qkv_specs + [row(d), row(d)],
        out_shape=qkv_shapes + [jax.ShapeDtypeStruct((b, s, d), BF16)] * 2,
        scratch_shapes=[pltpu.VMEM((ATTN_WIDTH // LANES, ROW_TILE, LANES), F32)] * 3,
        compiler_params=_cparams(("parallel", "parallel"), 56),
        name="mix_in",
    )(x, g, w_in, b_gate, g2, ws_pairs, bsp, w_pa)
    n_qkv = 3 * len(DILATIONS)
    qkv = [outs[3 * i:3 * i + 3] for i in range(len(DILATIONS))]
    return qkv, outs[n_qkv], outs[n_qkv + 1]


def _attn_kernel(q_ref, k_ref, v_ref, o_ref, l_ref, bias_ref, *, dil):
    n_res, rows, _ = q_ref.shape
    seq = k_ref.shape[1]
    t = pl.program_id(2)
    first = jnp.logical_and(jnp.logical_and(pl.program_id(0) == 0, pl.program_id(1) == 0), t == 0)

    @pl.when(first)
    def _():
        ii = lax.broadcasted_iota(I32, (Q_TILE, KEY_TILE), 0)
        jj = lax.broadcasted_iota(I32, (Q_TILE, KEY_TILE), 1)
        for var in range(3):
            absd = jnp.abs(jj - ii - var * HALF_WINDOW)
            valid = absd <= HALF_WINDOW
            absf = absd.astype(F32)
            for h in range(N_HEADS):
                slope = 2.0 ** (-8.0 * (h + 1) / N_HEADS)
                bias_ref[var, h] = jnp.where(valid, -(slope * dil) * absf, -jnp.inf)

    lane = lax.broadcasted_iota(I32, (Q_TILE, LANES), 1)
    lane_lo = lane < HEAD_DIM
    mask_lo = jnp.where(lane_lo, 1.0, 0.0).astype(BF16)
    mask_hi = jnp.where(lane_lo, 0.0, 1.0).astype(BF16)
    lane_head = lane // LSE_LANES
    for rr in range(n_res):
        for qi in range(rows // Q_TILE):
            rs = slice(qi * Q_TILE, (qi + 1) * Q_TILE)
            i0 = t * rows + qi * Q_TILE
            start = pl.multiple_of(jnp.clip(i0 - HALF_WINDOW, 0, seq - KEY_TILE), HALF_WINDOW)
            var = (i0 - start) // HALF_WINDOW
            lse_tile = jnp.zeros((Q_TILE, LANES), F32)
            for p in range(ATTN_WIDTH // LANES):
                cs = slice(p * LANES, (p + 1) * LANES)
                qp = q_ref[rr, rs, cs]
                kp = k_ref[rr, pl.ds(start, KEY_TILE), cs]
                vp = v_ref[rr, pl.ds(start, KEY_TILE), cs]
                q2 = jnp.concatenate([qp * mask_lo, qp * mask_hi], axis=0)
                s2 = _dot_nt(q2, kp)
                probs, inv_dens = [], []
                for hh in range(2):
                    h = 2 * p + hh
                    s = s2[hh * Q_TILE:(hh + 1) * Q_TILE] + bias_ref[var, h]
                    m = jnp.max(s, axis=-1, keepdims=True)
                    e = jnp.exp(s - m)
                    den = jnp.sum(e, axis=-1, keepdims=True)
                    probs.append(e)
                    inv_dens.append(1.0 / den)
                    lse_tile = jnp.where(lane_head == h, m + jnp.log(den), lse_tile)
                o2 = _dot(jnp.concatenate(probs, axis=0).astype(BF16), vp)
                o_c = jnp.where(lane_lo, o2[:Q_TILE] * inv_dens[0], o2[Q_TILE:] * inv_dens[1])
                o_ref[rr, rs, cs] = o_c.astype(BF16)
            l_ref[rr, rs, :] = lse_tile


def _attn_pattern(q, k, v, dil):
    b, _, seq, w = q.shape
    rows = min(seq, ATTN_STEP_ROWS)
    n_res = ATTN_STEP_ROWS // rows
    qspec = lambda width: pl.BlockSpec((None, n_res, rows, width), lambda bi, r, t: (bi, r, t, 0))
    kspec = pl.BlockSpec((None, n_res, seq, w), lambda bi, r, t: (bi, r, 0, 0))
    return pl.pallas_call(
        functools.partial(_attn_kernel, dil=dil),
        grid=(b, dil // n_res, seq // rows),
        in_specs=[qspec(w), kspec, kspec],
        out_specs=[qspec(w), qspec(LANES)],
        out_shape=[jax.ShapeDtypeStruct(q.shape, BF16),
                   jax.ShapeDtypeStruct((b, dil, seq, LANES), F32)],
        scratch_shapes=[pltpu.VMEM((3, N_HEADS, Q_TILE, KEY_TILE), F32)],
        compiler_params=_cparams(("arbitrary", "arbitrary", "arbitrary"), 48),
        name=f"attn_d{dil}",
    )(q, k, v)


def _mix_out_kernel(x_ref, ta_ref, gb_ref, *refs):
    n_pat = len(DILATIONS)
    o_refs = refs[:n_pat]
    l_refs = refs[n_pat:2 * n_pat]
    pb_ref, wo_ref, g_ref, wr_ref, x1_ref, h2_ref, aff_ref = refs[2 * n_pat:2 * n_pat + 7]
    stage_refs = refs[2 * n_pat + 7:]
    rows = x_ref.shape[0]

    outs, lses = [], []
    for di, dil in enumerate(DILATIONS):
        if dil == 1:
            outs.append(o_refs[di][0].astype(F32))
            lses.append(l_refs[di][0])
            continue
        o_st, l_st = stage_refs[2 * (di - 1)], stage_refs[2 * (di - 1) + 1]
        n_slab = ATTN_WIDTH // LANES
        for r in range(dil):
            dst = pl.ds(r, rows // dil, stride=dil)
            o_r = o_refs[di][r].astype(F32)
            for p in range(n_slab):
                o_st[p, dst, :] = o_r[:, p * LANES:(p + 1) * LANES]
            l_st[dst, :] = l_refs[di][r]
        outs.append(jnp.concatenate([o_st[p] for p in range(n_slab)], axis=1))
        lses.append(l_st[...])

    m = functools.reduce(jnp.maximum, lses)
    ws = [jnp.exp(l - m) for l in lses]
    inv = 1.0 / functools.reduce(lambda a, c: a + c, ws)
    k_i = lax.broadcasted_iota(I32, (LANES, ATTN_WIDTH), 0)
    c_i = lax.broadcasted_iota(I32, (LANES, ATTN_WIDTH), 1)
    spread = jnp.where(k_i == (c_i // HEAD_DIM) * LSE_LANES, 1.0, 0.0).astype(BF16)
    o = None
    for w, o_p in zip(ws, outs):
        w = w * inv
        w_hi = w.astype(BF16)
        w_lo = (w - w_hi.astype(F32)).astype(BF16)
        term = (_dot(w_hi, spread) + _dot(w_lo, spread)) * o_p
        o = term if o is None else o + term

    ob = _dot(o.astype(BF16), pb_ref[...])
    merged = (ta_ref[...].astype(F32) + gb_ref[...].astype(F32) * ob).astype(BF16)
    x1 = x_ref[...] + _dot(merged, wo_ref[...])
    x1_ref[...] = x1
    h2 = _rms(x1, g_ref[...])
    h2_ref[...] = h2.astype(BF16)
    h_hi = h2.astype(BF16)
    h_lo = (h2 - h_hi.astype(F32)).astype(BF16)
    wr = wr_ref[...]
    w_hi = wr.astype(BF16)
    w_lo = (wr - w_hi.astype(F32)).astype(BF16)
    logits = _dot_nt(w_hi, h_hi) + (_dot_nt(w_hi, h_lo) + _dot_nt(w_lo, h_hi))
    e = jnp.exp(logits - jnp.max(logits, axis=0, keepdims=True))
    aff_ref[...] = e / jnp.sum(e, axis=0, keepdims=True)


def _mix_out(x, ta, gb, os_, ls_, w_pb, w_out, g, w_router_t):
    b, s, d = x.shape
    n_e = w_router_t.shape[0]
    const = lambda shape: pl.BlockSpec(shape, lambda bi, t: (0,) * len(shape))
    row = lambda w: pl.BlockSpec((None, ROW_TILE, w), lambda bi, t: (bi, t, 0))
    res = lambda dil, w: pl.BlockSpec((None, dil, ROW_TILE // dil, w), lambda bi, t: (bi, 0, t, 0))
    stage = []
    for dil in DILATIONS[1:]:
        stage += [pltpu.VMEM((ATTN_WIDTH // LANES, ROW_TILE, LANES), F32),
                  pltpu.VMEM((ROW_TILE, LANES), F32)]
    return pl.pallas_call(
        _mix_out_kernel,
        grid=(b, s // ROW_TILE),
        in_specs=[row(d), row(d), row(d)]
                 + [res(dil, ATTN_WIDTH) for dil in DILATIONS]
                 + [res(dil, LANES) for dil in DILATIONS]
                 + [const(w_pb.shape), const(w_out.shape), const(g.shape), const(w_router_t.shape)],
        out_specs=[row(d), row(d), pl.BlockSpec((None, n_e, ROW_TILE), lambda bi, t: (bi, 0, t))],
        out_shape=[jax.ShapeDtypeStruct((b, s, d), F32), jax.ShapeDtypeStruct((b, s, d), BF16),
                   jax.ShapeDtypeStruct((b, n_e, s), F32)],
        scratch_shapes=stage,
        compiler_params=_cparams(("parallel", "parallel"), 48),
        name="mix_out",
    )(x, ta, gb, *os_, *ls_, w_pb, w_out, g, w_router_t)


def _topk_kernel(aff_ref, rank_ref, cum_ref, *, cap):
    n_e, s = aff_ref.shape
    n_blk = s // SLOT_TILE
    aff = aff_ref[...]
    thr = jnp.zeros((n_e, 1), I32)
    for bit in range(30, -1, -1):
        cand = thr | (1 << bit)
        cnt = jnp.sum((aff >= pltpu.bitcast(cand, F32)).astype(I32), axis=1, keepdims=True)
        thr = jnp.where(cnt >= cap, cand, thr)
    above = aff >= pltpu.bitcast(thr + 1, F32)
    tie = jnp.logical_and(aff >= pltpu.bitcast(thr, F32), jnp.logical_not(above))
    need = (cap - jnp.sum(above.astype(I32), axis=1, keepdims=True)).astype(F32)
    r_i = lax.broadcasted_iota(I32, (SLOT_TILE, SLOT_TILE), 0)
    c_i = lax.broadcasted_iota(I32, (SLOT_TILE, SLOT_TILE), 1)
    tri = jnp.where(r_i < c_i, 1.0, 0.0).astype(BF16)
    lane = lax.broadcasted_iota(I32, (n_e, LANES), 1)
    run_tie = jnp.zeros((n_e, 1), F32)
    run_sel = jnp.zeros((n_e, 1), F32)
    cum = jnp.zeros((n_e, LANES), F32)
    for j in range(n_blk):
        cs = slice(j * SLOT_TILE, (j + 1) * SLOT_TILE)
        tie_f = jnp.where(tie[:, cs], 1.0, 0.0)
        tie_rank = _dot(tie_f.astype(BF16), tri) + run_tie
        run_tie = run_tie + jnp.sum(tie_f, axis=1, keepdims=True)
        sel_f = jnp.where(above[:, cs], 1.0, jnp.where(tie_rank < need, tie_f, 0.0))
        rank = _dot(sel_f.astype(BF16), tri) + run_sel
        rank_ref[:, cs] = jnp.where(sel_f > 0.0, rank, -1.0).astype(I32)
        cum = jnp.where(lane == j, run_sel, cum)
        run_sel = run_sel + jnp.sum(sel_f, axis=1, keepdims=True)
    cum = jnp.where(lane == n_blk, run_sel, cum)
    cum_ref[...] = cum.astype(I32)


def _topk(aff, cap):
    b, n_e, s = aff.shape
    return pl.pallas_call(
        functools.partial(_topk_kernel, cap=cap),
        grid=(b,),
        in_specs=[pl.BlockSpec((None, n_e, s), lambda bi: (bi, 0, 0))],
        out_specs=[pl.BlockSpec((None, n_e, s), lambda bi: (bi, 0, 0)),
                   pl.BlockSpec((None, n_e, LANES), lambda bi: (bi, 0, 0))],
        out_shape=[jax.ShapeDtypeStruct((b, n_e, s), I32),
                   jax.ShapeDtypeStruct((b, n_e, LANES), I32)],
        compiler_params=_cparams(("parallel",), 32),
        name="topk",
    )(aff)


def _moe_kernel(cum_ref, h2_ref, rank_ref, aff_ref, wg_ref, wu_ref, wd_ref, out_ref,
                xe_ref, ye_ref, gate_ref, *, cap):
    s = h2_ref.shape[0]
    n_blk = s // SLOT_TILE
    bi = pl.program_id(0)
    e = pl.program_id(1)
    base = (bi * pl.num_programs(1) + e) * LANES
    row_i = lax.broadcasted_iota(I32, (SLOT_WINDOW, SLOT_TILE), 0)

    @pl.when(e == 0)
    def _():
        out_ref[...] = jnp.zeros_like(out_ref)

    xe_ref[...] = jnp.zeros_like(xe_ref)
    gate_ref[...] = jnp.zeros_like(gate_ref)

    def block(j):
        first = cum_ref[base + j]
        lo = pl.multiple_of((first // SLOT_ALIGN) * SLOT_ALIGN, SLOT_ALIGN)
        tok = pl.ds(pl.multiple_of(j * SLOT_TILE, SLOT_TILE), SLOT_TILE)
        hit = (row_i + lo) == rank_ref[:, tok]
        return pl.ds(lo, SLOT_WINDOW), tok, hit

    def gather_blk(j, carry):
        win, tok, hit = block(j)
        xe_ref[win, :] += _dot(jnp.where(hit, 1.0, 0.0).astype(BF16), h2_ref[tok, :])
        gate_ref[win, :] += jnp.sum(jnp.where(hit, aff_ref[:, tok], 0.0), axis=1, keepdims=True)
        return carry

    lax.fori_loop(0, n_blk, gather_blk, 0, unroll=4)

    xe = xe_ref[:cap, :].astype(BF16)
    gate_h = _dot(xe, wg_ref[...])
    up_h = _dot(xe, wu_ref[...])
    hidden = (gate_h * _sigmoid(gate_h) * up_h).astype(BF16)
    ye_ref[:cap, :] = (_dot(hidden, wd_ref[...]) * gate_ref[:cap, 0:1]).astype(BF16)
    ye_ref[cap:, :] = jnp.zeros((SLOT_WINDOW, ye_ref.shape[1]), BF16)

    def scatter_blk(j, carry):
        win, tok, hit = block(j)
        out_ref[tok, :] += _dot_tn(jnp.where(hit, 1.0, 0.0).astype(BF16), ye_ref[win, :])
        return carry

    lax.fori_loop(0, n_blk, scatter_blk, 0, unroll=4)


def _moe(cum_flat, h2, rank4, aff4, wg, wu, wd, cap):
    b, s, d = h2.shape
    n_e, _, hid = wg.shape
    once = pl.Buffered(1)
    grid_spec = pltpu.PrefetchScalarGridSpec(
        num_scalar_prefetch=1,
        grid=(b, n_e),
        in_specs=[
            pl.BlockSpec((None, s, d), lambda bi, e, c: (bi, 0, 0), pipeline_mode=once),
            pl.BlockSpec((None, None, 1, s), lambda bi, e, c: (bi, e, 0, 0)),
            pl.BlockSpec((None, None, 1, s), lambda bi, e, c: (bi, e, 0, 0)),
            pl.BlockSpec((None, d, hid), lambda bi, e, c: (e, 0, 0)),
            pl.BlockSpec((None, d, hid), lambda bi, e, c: (e, 0, 0)),
            pl.BlockSpec((None, hid, d), lambda bi, e, c: (e, 0, 0)),
        ],
        out_specs=pl.BlockSpec((None, s, d), lambda bi, e, c: (bi, 0, 0), pipeline_mode=once),
        scratch_shapes=[pltpu.VMEM((cap + SLOT_WINDOW, d), F32),
                        pltpu.VMEM((cap + SLOT_WINDOW, d), BF16),
                        pltpu.VMEM((cap + SLOT_WINDOW, LANES), F32)],
    )
    return pl.pallas_call(
        functools.partial(_moe_kernel, cap=cap),
        grid_spec=grid_spec,
        out_shape=jax.ShapeDtypeStruct((b, s, d), F32),
        compiler_params=_cparams(("arbitrary", "arbitrary"), 60),
        name="moe",
    )(cum_flat, h2, rank4, aff4, wg, wu, wd)


def _final_kernel(x1_ref, moe_ref, g_ref, y_ref):
    y_ref[...] = _rms(x1_ref[...] + moe_ref[...], g_ref[...])


def _final(x1, moe, g):
    n, d = x1.shape
    row = pl.BlockSpec((ROW_TILE, d), lambda i: (i, 0))
    return pl.pallas_call(
        _final_kernel,
        grid=(n // ROW_TILE,),
        in_specs=[row, row, pl.BlockSpec(g.shape, lambda i: (0, 0))],
        out_specs=row,
        out_shape=jax.ShapeDtypeStruct((n, d), F32),
        compiler_params=_cparams(("parallel",), 32),
        name="final_norm",
    )(x1, moe, g)


def kernel(x, norm_mix_g, w_in, b_gate, gmlp_norm_g, w_spatial, b_spatial, w_proj_a, w_proj_b,
           w_out, norm_ffn_g, w_router, w_e_gate, w_e_up, w_e_down, norm_final_g):
    b, s, d = x.shape
    assert w_in.shape[0] == 1, "single-layer block"
    cap = CAPACITY_FACTOR * s // N_EXPERTS
    group_width = GMLP_WIDTH // GMLP_GROUPS
    ws_pairs = w_spatial[0].astype(BF16).reshape(GMLP_GROUPS // 2, 2 * CHUNK, CHUNK)
    bsp = jnp.repeat(b_spatial[0].T, group_width, axis=1)
    qkv, ta, gb = _mix_in(x, norm_mix_g, w_in[0].astype(BF16), b_gate, gmlp_norm_g, ws_pairs, bsp,
                          w_proj_a[0].astype(BF16))
    os_, ls_ = [], []
    for (q, k, v), dil in zip(qkv, DILATIONS):
        o, lse = _attn_pattern(q, k, v, dil)
        os_.append(o)
        ls_.append(lse)
    x1, h2, aff = _mix_out(x, ta, gb, os_, ls_, w_proj_b[0].astype(BF16), w_out[0].astype(BF16),
                           norm_ffn_g, w_router[0].T)
    rank, cum = _topk(aff, cap)
    moe = _moe(cum.reshape(-1), h2, rank.reshape(b, N_EXPERTS, 1, s),
               aff.reshape(b, N_EXPERTS, 1, s), w_e_gate[0].astype(BF16),
               w_e_up[0].astype(BF16), w_e_down[0].astype(BF16), cap)
    y = _final(x1.reshape(b * s, d), moe.reshape(b * s, d), norm_final_g[None])
    return y.reshape(b, s, d)
```

```python
import functools

import jax
import jax.numpy as jnp
from jax import lax
from jax.experimental import pallas as pl
from jax.experimental.pallas import tpu as pltpu

F32 = jnp.float32
BF16 = jnp.bfloat16
I32 = jnp.int32

EPS = 1e-6
GMLP_WIDTH = 512
GMLP_GROUPS = 8
CHUNK = 128
N_HEADS = 8
HEAD_DIM = 64
ATTN_WIDTH = N_HEADS * HEAD_DIM
DILATIONS = (1, 4, 16)
HALF_WINDOW = 64
N_EXPERTS = 16
CAPACITY_FACTOR = 2

LANES = 128
Q_TILE = 128
KEY_TILE = 2 * Q_TILE
ATTN_STEP_ROWS = 1024
LSE_LANES = LANES // N_HEADS
SLOT_TILE = 128
SLOT_ALIGN = 16
SLOT_WINDOW = SLOT_TILE + SLOT_ALIGN
FAST_WINDOW = 64
ROW_TILE = 512
MIB = 1024 * 1024


def _cparams(sem, vmem_mib):
    return pltpu.CompilerParams(dimension_semantics=sem, vmem_limit_bytes=vmem_mib * MIB)


def _gelu_tanh(x):
    return 0.5 * x * (1.0 + jnp.tanh(0.7978845608028654 * (x + 0.044715 * (x * x * x))))


def _sigmoid(x):
    return 1.0 / (1.0 + jnp.exp(-x))


def _rms(x, g):
    return x * lax.rsqrt(jnp.mean(x * x, axis=-1, keepdims=True) + EPS) * g


def _dot(a, b):
    return jnp.dot(a, b, preferred_element_type=F32)


def _dot_nt(a, b):
    return lax.dot_general(a, b, (((1,), (1,)), ((), ())), preferred_element_type=F32)


def _dot_tn(a, b):
    return lax.dot_general(a, b, (((0,), (0,)), ((), ())), preferred_element_type=F32)


def _mix_in_kernel(x_ref, g_ref, win_ref, bg_ref, g2_ref, ws_ref, bsp_ref, pa_ref, *refs):
    n_qkv = 3 * len(DILATIONS)
    qkv_refs = refs[:n_qkv]
    ta_ref, gb_ref = refs[n_qkv:n_qkv + 2]
    stage_refs = refs[n_qkv + 2:]
    rows, d_model = x_ref.shape
    h = _rms(x_ref[...], g_ref[...]).astype(BF16)

    def proj(lo, width):
        return _dot(h, win_ref[:, lo:lo + width])

    c0 = 0
    u = _gelu_tanh(proj(c0, GMLP_WIDTH)); c0 += GMLP_WIDTH
    v = _gelu_tanh(proj(c0, GMLP_WIDTH)); c0 += GMLP_WIDTH
    for i in range(3):
        val = proj(c0, ATTN_WIDTH); c0 += ATTN_WIDTH
        if i == 0:
            val = val * (HEAD_DIM ** -0.5)
        stage = stage_refs[i]
        for p in range(ATTN_WIDTH // LANES):
            stage[p] = val[:, p * LANES:(p + 1) * LANES]
        for di, dil in enumerate(DILATIONS):
            out = qkv_refs[3 * di + i]
            if dil == 1:
                out[0] = val.astype(BF16)
                continue
            for r in range(dil):
                for p in range(ATTN_WIDTH // LANES):
                    out[r, :, p * LANES:(p + 1) * LANES] = (
                        stage[p, pl.ds(r, rows // dil, stride=dil), :].astype(BF16))
    ga = _sigmoid(proj(c0, d_model) + bg_ref[:, :d_model]); c0 += d_model
    gb = _sigmoid(proj(c0, d_model) + bg_ref[:, d_model:])
    gb_ref[...] = gb.astype(BF16)

    vn = _rms(v, g2_ref[...]).astype(BF16)
    lane_lo = lax.broadcasted_iota(I32, (CHUNK, LANES), 1) < HEAD_DIM
    bsp = bsp_ref[...]
    a_chunks = []
    for c in range(rows // CHUNK):
        rs = slice(c * CHUNK, (c + 1) * CHUNK)
        cols = []
        for p in range(GMLP_WIDTH // LANES):
            r = _dot(ws_ref[p], vn[rs, p * LANES:(p + 1) * LANES])
            cols.append(jnp.where(lane_lo, r[:CHUNK], r[CHUNK:]))
        mixed = jnp.concatenate(cols, axis=1) + bsp
        a_chunks.append((u[rs] * mixed).astype(BF16))
    a = jnp.concatenate(a_chunks, axis=0)
    ta_ref[...] = (ga * _dot(a, pa_ref[...])).astype(BF16)


def _mix_in(x, g, w_in, b_gate, g2, ws_pairs, bsp, w_pa):
    b, s, d = x.shape
    const = lambda shape: pl.BlockSpec(shape, lambda bi, t: (0,) * len(shape))
    row = lambda w: pl.BlockSpec((None, ROW_TILE, w), lambda bi, t: (bi, t, 0))
    qkv_specs, qkv_shapes = [], []
    for dil in DILATIONS:
        spec = pl.BlockSpec((None, dil, ROW_TILE // dil, ATTN_WIDTH), lambda bi, t: (bi, 0, t, 0))
        qkv_specs += [spec] * 3
        qkv_shapes += [jax.ShapeDtypeStruct((b, dil, s // dil, ATTN_WIDTH), BF16)] * 3
    outs = pl.pallas_call(
        _mix_in_kernel,
        grid=(b, s // ROW_TILE),
        in_specs=[row(d), const(g.shape), const(w_in.shape), const(b_gate.shape), const(g2.shape),
                  const(ws_pairs.shape), const(bsp.shape), const(w_pa.shape)],
        out_specs=qkv_specs + [row(d), row(d)],
        out_shape=qkv_shapes + [jax.ShapeDtypeStruct((b, s, d), BF16)] * 2,
        scratch_shapes=[pltpu.VMEM((ATTN_WIDTH // LANES, ROW_TILE, LANES), F32)] * 3,
        compiler_params=_cparams(("parallel", "parallel"), 56),
        name="mix_in",
    )(x, g, w_in, b_gate, g2, ws_pairs, bsp, w_pa)
    n_qkv = 3 * len(DILATIONS)
    qkv = [outs[3 * i:3 * i + 3] for i in range(len(DILATIONS))]
    return qkv, outs[n_qkv], outs[n_qkv + 1]


def _attn_kernel(q_ref, k_ref, v_ref, o_ref, l_ref, bias_ref, *, dil):
    n_res, rows, _ = q_ref.shape
    seq = k_ref.shape[1]
    t = pl.program_id(2)
    first = jnp.logical_and(jnp.logical_and(pl.program_id(0) == 0, pl.program_id(1) == 0), t == 0)

    @pl.when(first)
    def _():
        ii = lax.broadcasted_iota(I32, (Q_TILE, KEY_TILE), 0)
        jj = lax.broadcasted_iota(I32, (Q_TILE, KEY_TILE), 1)
        for var in range(3):
            absd = jnp.abs(jj - ii - var * HALF_WINDOW)
            valid = absd <= HALF_WINDOW
            absf = absd.astype(F32)
            for h in range(N_HEADS):
                slope = 2.0 ** (-8.0 * (h + 1) / N_HEADS)
                bias_ref[var, h] = jnp.where(valid, -(slope * dil) * absf, -jnp.inf)

    lane = lax.broadcasted_iota(I32, (Q_TILE, LANES), 1)
    lane_lo = lane < HEAD_DIM
    mask_lo = jnp.where(lane_lo, 1.0, 0.0).astype(BF16)
    mask_hi = jnp.where(lane_lo, 0.0, 1.0).astype(BF16)
    lane_head = lane // LSE_LANES
    for rr in range(n_res):
        for qi in range(rows // Q_TILE):
            rs = slice(qi * Q_TILE, (qi + 1) * Q_TILE)
            i0 = t * rows + qi * Q_TILE
            start = pl.multiple_of(jnp.clip(i0 - HALF_WINDOW, 0, seq - KEY_TILE), HALF_WINDOW)
            var = (i0 - start) // HALF_WINDOW
            lse_tile = jnp.zeros((Q_TILE, LANES), F32)
            for p in range(ATTN_WIDTH // LANES):
                cs = slice(p * LANES, (p + 1) * LANES)
                qp = q_ref[rr, rs, cs]
                kp = k_ref[rr, pl.ds(start, KEY_TILE), cs]
                vp = v_ref[rr, pl.ds(start, KEY_TILE), cs]
                q2 = jnp.concatenate([qp * mask_lo, qp * mask_hi], axis=0)
                s2 = _dot_nt(q2, kp)
                probs, inv_dens = [], []
                for hh in range(2):
                    h = 2 * p + hh
                    s = s2[hh * Q_TILE:(hh + 1) * Q_TILE] + bias_ref[var, h]
                    m = jnp.max(s, axis=-1, keepdims=True)
                    e = jnp.exp(s - m)
                    den = jnp.sum(e, axis=-1, keepdims=True)
                    probs.append(e)
                    inv_dens.append(1.0 / den)
                    lse_tile = jnp.where(lane_head == h, m + jnp.log(den), lse_tile)
                o2 = _dot(jnp.concatenate(probs, axis=0).astype(BF16), vp)
                o_c = jnp.where(lane_lo, o2[:Q_TILE] * inv_dens[0], o2[Q_TILE:] * inv_dens[1])
                o_ref[rr, rs, cs] = o_c.astype(BF16)
            l_ref[rr, rs, :] = lse_tile


def _attn_pattern(q, k, v, dil):
    b, _, seq, w = q.shape
    rows = min(seq, ATTN_STEP_ROWS)
    n_res = ATTN_STEP_ROWS // rows
    qspec = lambda width: pl.BlockSpec((None, n_res, rows, width), lambda bi, r, t: (bi, r, t, 0))
    kspec = pl.BlockSpec((None, n_res, seq, w), lambda bi, r, t: (bi, r, 0, 0))
    return pl.pallas_call(
        functools.partial(_attn_kernel, dil=dil),
        grid=(b, dil // n_res, seq // rows),
        in_specs=[qspec(w), kspec, kspec],
        out_specs=[qspec(w), qspec(LANES)],
        out_shape=[jax.ShapeDtypeStruct(q.shape, BF16),
                   jax.ShapeDtypeStruct((b, dil, seq, LANES), F32)],
        scratch_shapes=[pltpu.VMEM((3, N_HEADS, Q_TILE, KEY_TILE), F32)],
        compiler_params=_cparams(("arbitrary", "arbitrary", "arbitrary"), 48),
        name=f"attn_d{dil}",
    )(q, k, v)


def _mix_out_kernel(x_ref, ta_ref, gb_ref, *refs):
    n_pat = len(DILATIONS)
    o_refs = refs[:n_pat]
    l_refs = refs[n_pat:2 * n_pat]
    pb_ref, wo_ref, g_ref, wr_ref, x1_ref, h2_ref, aff_ref = refs[2 * n_pat:2 * n_pat + 7]
    stage_refs = refs[2 * n_pat + 7:]
    rows = x_ref.shape[0]

    outs, lses = [], []
    for di, dil in enumerate(DILATIONS):
        if dil == 1:
            outs.append(o_refs[di][0].astype(F32))
            lses.append(l_refs[di][0])
            continue
        o_st, l_st = stage_refs[2 * (di - 1)], stage_refs[2 * (di - 1) + 1]
        n_slab = ATTN_WIDTH // LANES
        for r in range(dil):
            dst = pl.ds(r, rows // dil, stride=dil)
            o_r = o_refs[di][r].astype(F32)
            for p in range(n_slab):
                o_st[p, dst, :] = o_r[:, p * LANES:(p + 1) * LANES]
            l_st[dst, :] = l_refs[di][r]
        outs.append(jnp.concatenate([o_st[p] for p in range(n_slab)], axis=1))
        lses.append(l_st[...])

    m = functools.reduce(jnp.maximum, lses)
    ws = [jnp.exp(l - m) for l in lses]
    inv = 1.0 / functools.reduce(lambda a, c: a + c, ws)
    k_i = lax.broadcasted_iota(I32, (LANES, ATTN_WIDTH), 0)
    c_i = lax.broadcasted_iota(I32, (LANES, ATTN_WIDTH), 1)
    spread = jnp.where(k_i == (c_i // HEAD_DIM) * LSE_LANES, 1.0, 0.0).astype(BF16)
    o = None
    for w, o_p in zip(ws, outs):
        w = w * inv
        w_hi = w.astype(BF16)
        w_lo = (w - w_hi.astype(F32)).astype(BF16)
        term = (_dot(w_hi, spread) + _dot(w_lo, spread)) * o_p
        o = term if o is None else o + term

    ob = _dot(o.astype(BF16), pb_ref[...])
    merged = (ta_ref[...].astype(F32) + gb_ref[...].astype(F32) * ob).astype(BF16)
    x1 = x_ref[...] + _dot(merged, wo_ref[...])
    x1_ref[...] = x1
    h2 = _rms(x1, g_ref[...])
    h2_ref[...] = h2.astype(BF16)
    h_hi = h2.astype(BF16)
    h_lo = (h2 - h_hi.astype(F32)).astype(BF16)
    wr = wr_ref[...]
    w_hi = wr.astype(BF16)
    w_lo = (wr - w_hi.astype(F32)).astype(BF16)
    logits = _dot_nt(w_hi, h_hi) + (_dot_nt(w_hi, h_lo) + _dot_nt(w_lo, h_hi))
    e = jnp.exp(logits - jnp.max(logits, axis=0, keepdims=True))
    aff_ref[...] = e / jnp.sum(e, axis=0, keepdims=True)


def _mix_out(x, ta, gb, os_, ls_, w_pb, w_out, g, w_router_t):
    b, s, d = x.shape
    n_e = w_router_t.shape[0]
    const = lambda shape: pl.BlockSpec(shape, lambda bi, t: (0,) * len(shape))
    row = lambda w: pl.BlockSpec((None, ROW_TILE, w), lambda bi, t: (bi, t, 0))
    res = lambda dil, w: pl.BlockSpec((None, dil, ROW_TILE // dil, w), lambda bi, t: (bi, 0, t, 0))
    stage = []
    for dil in DILATIONS[1:]:
        stage += [pltpu.VMEM((ATTN_WIDTH // LANES, ROW_TILE, LANES), F32),
                  pltpu.VMEM((ROW_TILE, LANES), F32)]
    return pl.pallas_call(
        _mix_out_kernel,
        grid=(b, s // ROW_TILE),
        in_specs=[row(d), row(d), row(d)]
                 + [res(dil, ATTN_WIDTH) for dil in DILATIONS]
                 + [res(dil, LANES) for dil in DILATIONS]
                 + [const(w_pb.shape), const(w_out.shape), const(g.shape), const(w_router_t.shape)],
        out_specs=[row(d), row(d), pl.BlockSpec((None, n_e, ROW_TILE), lambda bi, t: (bi, 0, t))],
        out_shape=[jax.ShapeDtypeStruct((b, s, d), F32), jax.ShapeDtypeStruct((b, s, d), BF16),
                   jax.ShapeDtypeStruct((b, n_e, s), F32)],
        scratch_shapes=stage,
        compiler_params=_cparams(("parallel", "parallel"), 48),
        name="mix_out",
    )(x, ta, gb, *os_, *ls_, w_pb, w_out, g, w_router_t)


def _topk_kernel(aff_ref, rank_ref, cum_ref, *, cap):
    n_e, s = aff_ref.shape
    n_blk = s // SLOT_TILE
    aff = aff_ref[...]
    thr = jnp.zeros((n_e, 1), I32)
    for bit in range(30, -1, -1):
        cand = thr | (1 << bit)
        cnt = jnp.sum((aff >= pltpu.bitcast(cand, F32)).astype(I32), axis=1, keepdims=True)
        thr = jnp.where(cnt >= cap, cand, thr)
    above = aff >= pltpu.bitcast(thr + 1, F32)
    tie = jnp.logical_and(aff >= pltpu.bitcast(thr, F32), jnp.logical_not(above))
    need = (cap - jnp.sum(above.astype(I32), axis=1, keepdims=True)).astype(F32)
    r_i = lax.broadcasted_iota(I32, (SLOT_TILE, SLOT_TILE), 0)
    c_i = lax.broadcasted_iota(I32, (SLOT_TILE, SLOT_TILE), 1)
    tri = jnp.where(r_i < c_i, 1.0, 0.0).astype(BF16)
    lane = lax.broadcasted_iota(I32, (n_e, LANES), 1)
    run_tie = jnp.zeros((n_e, 1), F32)
    run_sel = jnp.zeros((n_e, 1), F32)
    cum = jnp.zeros((n_e, LANES), F32)
    for j in range(n_blk):
        cs = slice(j * SLOT_TILE, (j + 1) * SLOT_TILE)
        tie_f = jnp.where(tie[:, cs], 1.0, 0.0)
        tie_rank = _dot(tie_f.astype(BF16), tri) + run_tie
        run_tie = run_tie + jnp.sum(tie_f, axis=1, keepdims=True)
        sel_f = jnp.where(above[:, cs], 1.0, jnp.where(tie_rank < need, tie_f, 0.0))
        rank = _dot(sel_f.astype(BF16), tri) + run_sel
        rank_ref[:, cs] = jnp.where(sel_f > 0.0, rank, -1.0).astype(I32)
        cum = jnp.where(lane == j, run_sel, cum)
        run_sel = run_sel + jnp.sum(sel_f, axis=1, keepdims=True)
    cum = jnp.where(lane == n_blk, run_sel, cum)
    cum_ref[...] = cum.astype(I32)


def _topk(aff, cap):
    b, n_e, s = aff.shape
    return pl.pallas_call(
        functools.partial(_topk_kernel, cap=cap),
        grid=(b,),
        in_specs=[pl.BlockSpec((None, n_e, s), lambda bi: (bi, 0, 0))],
        out_specs=[pl.BlockSpec((None, n_e, s), lambda bi: (bi, 0, 0)),
                   pl.BlockSpec((None, n_e, LANES), lambda bi: (bi, 0, 0))],
        out_shape=[jax.ShapeDtypeStruct((b, n_e, s), I32),
                   jax.ShapeDtypeStruct((b, n_e, LANES), I32)],
        compiler_params=_cparams(("parallel",), 32),
        name="topk",
    )(aff)


def _slot_window(first):
    return pl.multiple_of((first // SLOT_ALIGN) * SLOT_ALIGN, SLOT_ALIGN)


def _moe_ffn_kernel(cum_ref, h2_ref, rank_ref, aff_ref, wg_ref, wu_ref, wd_ref, ye_ref,
                    xe_ref, gate_ref, *, cap):
    s = h2_ref.shape[0]
    n_blk = s // SLOT_TILE
    base = (pl.program_id(0) * pl.num_programs(1) + pl.program_id(1)) * LANES
    row_i = lax.broadcasted_iota(I32, (SLOT_WINDOW, SLOT_TILE), 0)

    xe_ref[...] = jnp.zeros_like(xe_ref)
    gate_ref[...] = jnp.zeros_like(gate_ref)

    def block(j):
        lo = _slot_window(cum_ref[base + j])
        tok = pl.ds(pl.multiple_of(j * SLOT_TILE, SLOT_TILE), SLOT_TILE)
        hit = (row_i + lo) == rank_ref[:, tok]
        return pl.ds(lo, SLOT_WINDOW), tok, hit

    def gather_blk(j, carry):
        win, tok, hit = block(j)
        xe_ref[win, :] += _dot(jnp.where(hit, 1.0, 0.0).astype(BF16), h2_ref[tok, :])
        gate_ref[win, :] += jnp.sum(jnp.where(hit, aff_ref[:, tok], 0.0), axis=1, keepdims=True)
        return carry

    lax.fori_loop(0, n_blk, gather_blk, 0, unroll=4)

    xe = xe_ref[:cap, :].astype(BF16)
    gate_h = _dot(xe, wg_ref[...].astype(BF16))
    up_h = _dot(xe, wu_ref[...].astype(BF16))
    hidden = (gate_h * _sigmoid(gate_h) * up_h).astype(BF16)
    ye = _dot(hidden, wd_ref[...].astype(BF16)) * gate_ref[:cap, 0:1]
    ye_ref[:cap, :] = ye.astype(BF16)
    ye_ref[cap:, :] = jnp.zeros((SLOT_WINDOW, ye_ref.shape[1]), BF16)


def _moe_ffn(cum_flat, h2, rank4, aff4, wg, wu, wd, cap):
    b, s, d = h2.shape
    n_e, _, hid = wg.shape
    rows = cap + SLOT_WINDOW
    grid_spec = pltpu.PrefetchScalarGridSpec(
        num_scalar_prefetch=1,
        grid=(b, n_e),
        in_specs=[
            pl.BlockSpec((None, s, d), lambda bi, e, c: (bi, 0, 0), pipeline_mode=pl.Buffered(1)),
            pl.BlockSpec((None, None, 1, s), lambda bi, e, c: (bi, e, 0, 0)),
            pl.BlockSpec((None, None, 1, s), lambda bi, e, c: (bi, e, 0, 0)),
            pl.BlockSpec((None, d, hid), lambda bi, e, c: (e, 0, 0)),
            pl.BlockSpec((None, d, hid), lambda bi, e, c: (e, 0, 0)),
            pl.BlockSpec((None, hid, d), lambda bi, e, c: (e, 0, 0)),
        ],
        out_specs=pl.BlockSpec((None, None, rows, d), lambda bi, e, c: (bi, e, 0, 0)),
        scratch_shapes=[pltpu.VMEM((rows, d), F32), pltpu.VMEM((rows, LANES), F32)],
    )
    return pl.pallas_call(
        functools.partial(_moe_ffn_kernel, cap=cap),
        grid_spec=grid_spec,
        out_shape=jax.ShapeDtypeStruct((b, n_e, rows, d), BF16),
        compiler_params=_cparams(("arbitrary", "arbitrary"), 60),
        name="moe_ffn",
    )(cum_flat, h2, rank4, aff4, wg, wu, wd)


def _combine_kernel(cum_ref, rank_ref, ye_ref, x1_ref, g_ref, y_ref, rhs_ref, moe_ref):
    n_e = rank_ref.shape[0]
    rows = x1_ref.shape[0]
    bi = pl.program_id(0)
    t = pl.program_id(1)
    row_fast = lax.broadcasted_iota(I32, (FAST_WINDOW, SLOT_TILE), 0)
    row_slow = lax.broadcasted_iota(I32, (SLOT_WINDOW, SLOT_TILE), 0)
    for jj in range(rows // SLOT_TILE):
        toks = slice(jj * SLOT_TILE, (jj + 1) * SLOT_TILE)
        j = t * (rows // SLOT_TILE) + jj
        los, fits = [], None
        for e in range(n_e):
            at = (bi * n_e + e) * LANES + j
            lo = _slot_window(cum_ref[at])
            ok = cum_ref[at + 1] - lo <= FAST_WINDOW
            los.append(lo)
            fits = ok if fits is None else jnp.logical_and(fits, ok)

        def hits(e, row_i):
            return jnp.where((row_i + los[e]) == rank_ref[e:e + 1, toks], 1.0, 0.0).astype(BF16)

        @pl.when(fits)
        def _():
            for e in range(n_e):
                rhs_ref[e * FAST_WINDOW:(e + 1) * FAST_WINDOW, :] = (
                    ye_ref[e, pl.ds(los[e], FAST_WINDOW), :])
            stack = jnp.concatenate([hits(e, row_fast) for e in range(n_e)], axis=0)
            moe_ref[...] = _dot_tn(stack, rhs_ref[...])

        @pl.when(jnp.logical_not(fits))
        def _():
            acc = _dot_tn(hits(0, row_slow), ye_ref[0, pl.ds(los[0], SLOT_WINDOW), :])
            for e in range(1, n_e):
                acc = acc + _dot_tn(hits(e, row_slow), ye_ref[e, pl.ds(los[e], SLOT_WINDOW), :])
            moe_ref[...] = acc

        y_ref[toks, :] = _rms(x1_ref[toks, :] + moe_ref[...], g_ref[...])


def _combine(cum_flat, rank, ye, x1, g):
    b, s, d = x1.shape
    n_e, ye_rows = ye.shape[1], ye.shape[2]
    row = pl.BlockSpec((None, ROW_TILE, d), lambda bi, t, c: (bi, t, 0))
    grid_spec = pltpu.PrefetchScalarGridSpec(
        num_scalar_prefetch=1,
        grid=(b, s // ROW_TILE),
        in_specs=[
            pl.BlockSpec((None, n_e, ROW_TILE), lambda bi, t, c: (bi, 0, t)),
            pl.BlockSpec((None, n_e, ye_rows, d), lambda bi, t, c: (bi, 0, 0, 0),
                         pipeline_mode=pl.Buffered(1)),
            row,
            pl.BlockSpec(g.shape, lambda bi, t, c: (0, 0)),
        ],
        out_specs=row,
        scratch_shapes=[pltpu.VMEM((n_e * FAST_WINDOW, d), BF16),
                        pltpu.VMEM((SLOT_TILE, d), F32)],
    )
    return pl.pallas_call(
        _combine_kernel,
        grid_spec=grid_spec,
        out_shape=jax.ShapeDtypeStruct((b, s, d), F32),
        compiler_params=_cparams(("arbitrary", "arbitrary"), 56),
        name="combine",
    )(cum_flat, rank, ye, x1, g)


def _moe_stages(aff, h2, x1, wg, wu, wd, g_final, cap):
    b, n_e, s = aff.shape
    rank, cum = _topk(aff, cap)
    cum_flat = cum.reshape(-1)
    ye = _moe_ffn(cum_flat, h2, rank.reshape(b, n_e, 1, s), aff.reshape(b, n_e, 1, s),
                  wg, wu, wd, cap)
    return _combine(cum_flat, rank, ye, x1, g_final)


def kernel(x, norm_mix_g, w_in, b_gate, gmlp_norm_g, w_spatial, b_spatial, w_proj_a, w_proj_b,
           w_out, norm_ffn_g, w_router, w_e_gate, w_e_up, w_e_down, norm_final_g):
    b, s, d = x.shape
    assert w_in.shape[0] == 1, "single-layer block"
    cap = CAPACITY_FACTOR * s // N_EXPERTS
    group_width = GMLP_WIDTH // GMLP_GROUPS
    ws_pairs = w_spatial[0].astype(BF16).reshape(GMLP_GROUPS // 2, 2 * CHUNK, CHUNK)
    bsp = jnp.repeat(b_spatial[0].T, group_width, axis=1)
    qkv, ta, gb = _mix_in(x, norm_mix_g, w_in[0].astype(BF16), b_gate, gmlp_norm_g, ws_pairs, bsp,
                          w_proj_a[0].astype(BF16))
    os_, ls_ = [], []
    for (q, k, v), dil in zip(qkv, DILATIONS):
        o, lse = _attn_pattern(q, k, v, dil)
        os_.append(o)
        ls_.append(lse)
    x1, h2, aff = _mix_out(x, ta, gb, os_, ls_, w_proj_b[0].astype(BF16), w_out[0].astype(BF16),
                           norm_ffn_g, w_router[0].T)
    return _moe_stages(aff, h2, x1, w_e_gate[0], w_e_up[0], w_e_down[0], norm_final_g[None], cap)
```

```python
import functools

import jax
import jax.numpy as jnp
from jax import lax
from jax.experimental import pallas as pl
from jax.experimental.pallas import tpu as pltpu

F32 = jnp.float32
BF16 = jnp.bfloat16
I32 = jnp.int32

EPS = 1e-6
GMLP_WIDTH = 512
GMLP_GROUPS = 8
CHUNK = 128
N_HEADS = 8
HEAD_DIM = 64
ATTN_WIDTH = N_HEADS * HEAD_DIM
DILATIONS = (1, 4, 16)
HALF_WINDOW = 64
N_EXPERTS = 16
CAPACITY_FACTOR = 2

LANES = 128
Q_TILE = 128
KEY_TILE = 2 * Q_TILE
ATTN_STEP_ROWS = 1024
LSE_LANES = LANES // N_HEADS
SLOT_TILE = 128
SLOT_ALIGN = 16
SLOT_WINDOW = SLOT_TILE + SLOT_ALIGN
GATHER_GROUPS = 2
FAST_WINDOW = 48
ROW_TILE = 512
MIB = 1024 * 1024


def _cparams(sem, vmem_mib):
    return pltpu.CompilerParams(dimension_semantics=sem, vmem_limit_bytes=vmem_mib * MIB)


def _gelu_tanh(x):
    return 0.5 * x * (1.0 + jnp.tanh(0.7978845608028654 * (x + 0.044715 * (x * x * x))))


def _sigmoid(x):
    return 1.0 / (1.0 + jnp.exp(-x))


def _rms(x, g):
    return x * lax.rsqrt(jnp.mean(x * x, axis=-1, keepdims=True) + EPS) * g


def _dot(a, b):
    return jnp.dot(a, b, preferred_element_type=F32)


def _dot_nt(a, b):
    return lax.dot_general(a, b, (((1,), (1,)), ((), ())), preferred_element_type=F32)


def _dot_tn(a, b):
    return lax.dot_general(a, b, (((0,), (0,)), ((), ())), preferred_element_type=F32)


def _mix_in_kernel(x_ref, g_ref, win_ref, bg_ref, g2_ref, ws_ref, bsp_ref, pa_ref, *refs):
    n_qkv = 3 * len(DILATIONS)
    qkv_refs = refs[:n_qkv]
    ta_ref, gb_ref = refs[n_qkv:n_qkv + 2]
    stage_refs = refs[n_qkv + 2:]
    rows, d_model = x_ref.shape
    h = _rms(x_ref[...], g_ref[...]).astype(BF16)

    def proj(lo, width):
        return _dot(h, win_ref[:, lo:lo + width])

    c0 = 0
    u = _gelu_tanh(proj(c0, GMLP_WIDTH)); c0 += GMLP_WIDTH
    v = _gelu_tanh(proj(c0, GMLP_WIDTH)); c0 += GMLP_WIDTH
    for i in range(3):
        val = proj(c0, ATTN_WIDTH); c0 += ATTN_WIDTH
        if i == 0:
            val = val * (HEAD_DIM ** -0.5)
        stage = stage_refs[i]
        for p in range(ATTN_WIDTH // LANES):
            stage[p] = val[:, p * LANES:(p + 1) * LANES]
        for di, dil in enumerate(DILATIONS):
            out = qkv_refs[3 * di + i]
            if dil == 1:
                out[0] = val.astype(BF16)
                continue
            for r in range(dil):
                for p in range(ATTN_WIDTH // LANES):
                    out[r, :, p * LANES:(p + 1) * LANES] = (
                        stage[p, pl.ds(r, rows // dil, stride=dil), :].astype(BF16))
    ga = _sigmoid(proj(c0, d_model) + bg_ref[:, :d_model]); c0 += d_model
    gb = _sigmoid(proj(c0, d_model) + bg_ref[:, d_model:])
    gb_ref[...] = gb.astype(BF16)

    vn = _rms(v, g2_ref[...]).astype(BF16)
    lane_lo = lax.broadcasted_iota(I32, (CHUNK, LANES), 1) < HEAD_DIM
    bsp = bsp_ref[...]
    n_chunk = rows // CHUNK
    mixed_slabs = []
    for p in range(GMLP_WIDTH // LANES):
        slab = jnp.concatenate(
            [vn[c * CHUNK:(c + 1) * CHUNK, p * LANES:(p + 1) * LANES] for c in range(n_chunk)],
            axis=1)
        r = _dot(ws_ref[p], slab)
        mixed_slabs.append([jnp.where(lane_lo, r[:CHUNK, c * LANES:(c + 1) * LANES],
                                      r[CHUNK:, c * LANES:(c + 1) * LANES])
                            for c in range(n_chunk)])
    a_chunks = []
    for c in range(n_chunk):
        rs = slice(c * CHUNK, (c + 1) * CHUNK)
        mixed = jnp.concatenate([slabs[c] for slabs in mixed_slabs], axis=1) + bsp
        a_chunks.append((u[rs] * mixed).astype(BF16))
    a = jnp.concatenate(a_chunks, axis=0)
    ta_ref[...] = (ga * _dot(a, pa_ref[...])).astype(BF16)


def _mix_in(x, g, w_in, b_gate, g2, ws_pairs, bsp, w_pa):
    b, s, d = x.shape
    const = lambda shape: pl.BlockSpec(shape, lambda bi, t: (0,) * len(shape))
    row = lambda w: pl.BlockSpec((None, ROW_TILE, w), lambda bi, t: (bi, t, 0))
    qkv_specs, qkv_shapes = [], []
    for dil in DILATIONS:
        spec = pl.BlockSpec((None, dil, ROW_TILE // dil, ATTN_WIDTH), lambda bi, t: (bi, 0, t, 0))
        qkv_specs += [spec] * 3
        qkv_shapes += [jax.ShapeDtypeStruct((b, dil, s // dil, ATTN_WIDTH), BF16)] * 3
    outs = pl.pallas_call(
        _mix_in_kernel,
        grid=(b, s // ROW_TILE),
        in_specs=[row(d), const(g.shape), const(w_in.shape), const(b_gate.shape), const(g2.shape),
                  const(ws_pairs.shape), const(bsp.shape), const(w_pa.shape)],
        out_specs=qkv_specs + [row(d), row(d)],
        out_shape=qkv_shapes + [jax.ShapeDtypeStruct((b, s, d), BF16)] * 2,
        scratch_shapes=[pltpu.VMEM((ATTN_WIDTH // LANES, ROW_TILE, LANES), F32)] * 3,
        compiler_params=_cparams(("parallel", "parallel"), 56),
        name="mix_in",
    )(x, g, w_in, b_gate, g2, ws_pairs, bsp, w_pa)
    n_qkv = 3 * len(DILATIONS)
    qkv = [outs[3 * i:3 * i + 3] for i in range(len(DILATIONS))]
    return qkv, outs[n_qkv], outs[n_qkv + 1]


def _attn_kernel(q_ref, k_ref, v_ref, o_ref, l_ref, bias_ref, *, dil):
    n_res, rows, _ = q_ref.shape
    seq = k_ref.shape[1]
    t = pl.program_id(2)
    first = jnp.logical_and(jnp.logical_and(pl.program_id(0) == 0, pl.program_id(1) == 0), t == 0)

    @pl.when(first)
    def _():
        ii = lax.broadcasted_iota(I32, (Q_TILE, KEY_TILE), 0)
        jj = lax.broadcasted_iota(I32, (Q_TILE, KEY_TILE), 1)
        for var in range(3):
            absd = jnp.abs(jj - ii - var * HALF_WINDOW)
            valid = absd <= HALF_WINDOW
            absf = absd.astype(F32)
            for h in range(N_HEADS):
                slope = 2.0 ** (-8.0 * (h + 1) / N_HEADS)
                bias_ref[var, h] = jnp.where(valid, -(slope * dil) * absf, -jnp.inf)

    lane = lax.broadcasted_iota(I32, (Q_TILE, LANES), 1)
    lane_lo = lane < HEAD_DIM
    mask_lo = jnp.where(lane_lo, 1.0, 0.0).astype(BF16)
    mask_hi = jnp.where(lane_lo, 0.0, 1.0).astype(BF16)
    lane_head = lane // LSE_LANES
    for rr in range(n_res):
        for qi in range(rows // Q_TILE):
            rs = slice(qi * Q_TILE, (qi + 1) * Q_TILE)
            i0 = t * rows + qi * Q_TILE
            start = pl.multiple_of(jnp.clip(i0 - HALF_WINDOW, 0, seq - KEY_TILE), HALF_WINDOW)
            var = (i0 - start) // HALF_WINDOW
            lse_tile = jnp.zeros((Q_TILE, LANES), F32)
            for p in range(ATTN_WIDTH // LANES):
                cs = slice(p * LANES, (p + 1) * LANES)
                qp = q_ref[rr, rs, cs]
                kp = k_ref[rr, pl.ds(start, KEY_TILE), cs]
                vp = v_ref[rr, pl.ds(start, KEY_TILE), cs]
                q2 = jnp.concatenate([qp * mask_lo, qp * mask_hi], axis=0)
                s2 = _dot_nt(q2, kp)
                probs, inv_dens = [], []
                for hh in range(2):
                    h = 2 * p + hh
                    s = s2[hh * Q_TILE:(hh + 1) * Q_TILE] + bias_ref[var, h]
                    m = jnp.max(s, axis=-1, keepdims=True)
                    e = jnp.exp(s - m)
                    den = jnp.sum(e, axis=-1, keepdims=True)
                    probs.append(e)
                    inv_dens.append(1.0 / den)
                    lse_tile = jnp.where(lane_head == h, m + jnp.log(den), lse_tile)
                o2 = _dot(jnp.concatenate(probs, axis=0).astype(BF16), vp)
                o_c = jnp.where(lane_lo, o2[:Q_TILE] * inv_dens[0], o2[Q_TILE:] * inv_dens[1])
                o_ref[rr, rs, cs] = o_c.astype(BF16)
            l_ref[rr, rs, :] = lse_tile


def _attn_pattern(q, k, v, dil):
    b, _, seq, w = q.shape
    rows = min(seq, ATTN_STEP_ROWS)
    n_res = ATTN_STEP_ROWS // rows
    qspec = lambda width: pl.BlockSpec((None, n_res, rows, width), lambda bi, r, t: (bi, r, t, 0))
    kspec = pl.BlockSpec((None, n_res, seq, w), lambda bi, r, t: (bi, r, 0, 0))
    return pl.pallas_call(
        functools.partial(_attn_kernel, dil=dil),
        grid=(b, dil // n_res, seq // rows),
        in_specs=[qspec(w), kspec, kspec],
        out_specs=[qspec(w), qspec(LANES)],
        out_shape=[jax.ShapeDtypeStruct(q.shape, BF16),
                   jax.ShapeDtypeStruct((b, dil, seq, LANES), F32)],
        scratch_shapes=[pltpu.VMEM((3, N_HEADS, Q_TILE, KEY_TILE), F32)],
        compiler_params=_cparams(("arbitrary", "arbitrary", "arbitrary"), 48),
        name=f"attn_d{dil}",
    )(q, k, v)


def _mix_out_kernel(x_ref, ta_ref, gb_ref, *refs):
    n_pat = len(DILATIONS)
    o_refs = refs[:n_pat]
    l_refs = refs[n_pat:2 * n_pat]
    pb_ref, wo_ref, g_ref, wr_ref, x1_ref, h2_ref, aff_ref = refs[2 * n_pat:2 * n_pat + 7]
    stage_refs = refs[2 * n_pat + 7:]
    rows = x_ref.shape[0]

    outs, lses = [], []
    for di, dil in enumerate(DILATIONS):
        if dil == 1:
            outs.append(o_refs[di][0].astype(F32))
            lses.append(l_refs[di][0])
            continue
        o_st, l_st = stage_refs[2 * (di - 1)], stage_refs[2 * (di - 1) + 1]
        n_slab = ATTN_WIDTH // LANES
        for r in range(dil):
            dst = pl.ds(r, rows // dil, stride=dil)
            o_r = o_refs[di][r].astype(F32)
            for p in range(n_slab):
                o_st[p, dst, :] = o_r[:, p * LANES:(p + 1) * LANES]
            l_st[dst, :] = l_refs[di][r]
        outs.append(jnp.concatenate([o_st[p] for p in range(n_slab)], axis=1))
        lses.append(l_st[...])

    m = functools.reduce(jnp.maximum, lses)
    ws = [jnp.exp(l - m) for l in lses]
    inv = 1.0 / functools.reduce(lambda a, c: a + c, ws)
    k_i = lax.broadcasted_iota(I32, (LANES, ATTN_WIDTH), 0)
    c_i = lax.broadcasted_iota(I32, (LANES, ATTN_WIDTH), 1)
    spread = jnp.where(k_i == (c_i // HEAD_DIM) * LSE_LANES, 1.0, 0.0).astype(BF16)
    o = None
    for w, o_p in zip(ws, outs):
        w = w * inv
        w_hi = w.astype(BF16)
        w_lo = (w - w_hi.astype(F32)).astype(BF16)
        term = (_dot(w_hi, spread) + _dot(w_lo, spread)) * o_p
        o = term if o is None else o + term

    ob = _dot(o.astype(BF16), pb_ref[...])
    merged = (ta_ref[...].astype(F32) + gb_ref[...].astype(F32) * ob).astype(BF16)
    x1 = x_ref[...] + _dot(merged, wo_ref[...])
    x1_ref[...] = x1
    h2 = _rms(x1, g_ref[...])
    h2_ref[...] = h2.astype(BF16)
    h_hi = h2.astype(BF16)
    h_lo = (h2 - h_hi.astype(F32)).astype(BF16)
    wr = wr_ref[...]
    w_hi = wr.astype(BF16)
    w_lo = (wr - w_hi.astype(F32)).astype(BF16)
    logits = _dot_nt(w_hi, h_hi) + (_dot_nt(w_hi, h_lo) + _dot_nt(w_lo, h_hi))
    e = jnp.exp(logits - jnp.max(logits, axis=0, keepdims=True))
    aff_ref[...] = e / jnp.sum(e, axis=0, keepdims=True)


def _mix_out(x, ta, gb, os_, ls_, w_pb, w_out, g, w_router_t):
    b, s, d = x.shape
    n_e = w_router_t.shape[0]
    const = lambda shape: pl.BlockSpec(shape, lambda bi, t: (0,) * len(shape))
    row = lambda w: pl.BlockSpec((None, ROW_TILE, w), lambda bi, t: (bi, t, 0))
    res = lambda dil, w: pl.BlockSpec((None, dil, ROW_TILE // dil, w), lambda bi, t: (bi, 0, t, 0))
    stage = []
    for dil in DILATIONS[1:]:
        stage += [pltpu.VMEM((ATTN_WIDTH // LANES, ROW_TILE, LANES), F32),
                  pltpu.VMEM((ROW_TILE, LANES), F32)]
    return pl.pallas_call(
        _mix_out_kernel,
        grid=(b, s // ROW_TILE),
        in_specs=[row(d), row(d), row(d)]
                 + [res(dil, ATTN_WIDTH) for dil in DILATIONS]
                 + [res(dil, LANES) for dil in DILATIONS]
                 + [const(w_pb.shape), const(w_out.shape), const(g.shape), const(w_router_t.shape)],
        out_specs=[row(d), row(d), pl.BlockSpec((None, n_e, ROW_TILE), lambda bi, t: (bi, 0, t))],
        out_shape=[jax.ShapeDtypeStruct((b, s, d), F32), jax.ShapeDtypeStruct((b, s, d), BF16),
                   jax.ShapeDtypeStruct((b, n_e, s), F32)],
        scratch_shapes=stage,
        compiler_params=_cparams(("parallel", "parallel"), 48),
        name="mix_out",
    )(x, ta, gb, *os_, *ls_, w_pb, w_out, g, w_router_t)


def _topk_kernel(aff_ref, rank_ref, cum_ref, *, cap):
    n_e, s = aff_ref.shape
    n_blk = s // SLOT_TILE
    aff = aff_ref[...]
    thr = jnp.zeros((n_e, 1), I32)
    for bit in range(30, -1, -1):
        cand = thr | (1 << bit)
        cnt = jnp.sum((aff >= pltpu.bitcast(cand, F32)).astype(I32), axis=1, keepdims=True)
        thr = jnp.where(cnt >= cap, cand, thr)
    above = aff >= pltpu.bitcast(thr + 1, F32)
    tie = jnp.logical_and(aff >= pltpu.bitcast(thr, F32), jnp.logical_not(above))
    need = (cap - jnp.sum(above.astype(I32), axis=1, keepdims=True)).astype(F32)
    r_i = lax.broadcasted_iota(I32, (SLOT_TILE, SLOT_TILE), 0)
    c_i = lax.broadcasted_iota(I32, (SLOT_TILE, SLOT_TILE), 1)
    tri = jnp.where(r_i < c_i, 1.0, 0.0).astype(BF16)
    lane = lax.broadcasted_iota(I32, (n_e, LANES), 1)
    run_tie = jnp.zeros((n_e, 1), F32)
    run_sel = jnp.zeros((n_e, 1), F32)
    cum = jnp.zeros((n_e, LANES), F32)
    for j in range(n_blk):
        cs = slice(j * SLOT_TILE, (j + 1) * SLOT_TILE)
        tie_f = jnp.where(tie[:, cs], 1.0, 0.0)
        tie_rank = _dot(tie_f.astype(BF16), tri) + run_tie
        run_tie = run_tie + jnp.sum(tie_f, axis=1, keepdims=True)
        sel_f = jnp.where(above[:, cs], 1.0, jnp.where(tie_rank < need, tie_f, 0.0))
        rank = _dot(sel_f.astype(BF16), tri) + run_sel
        rank_ref[:, cs] = jnp.where(sel_f > 0.0, rank, -1.0).astype(I32)
        cum = jnp.where(lane == j, run_sel, cum)
        run_sel = run_sel + jnp.sum(sel_f, axis=1, keepdims=True)
    cum = jnp.where(lane == n_blk, run_sel, cum)
    cum_ref[...] = cum.astype(I32)


def _topk(aff, cap):
    b, n_e, s = aff.shape
    return pl.pallas_call(
        functools.partial(_topk_kernel, cap=cap),
        grid=(b,),
        in_specs=[pl.BlockSpec((None, n_e, s), lambda bi: (bi, 0, 0))],
        out_specs=[pl.BlockSpec((None, n_e, s), lambda bi: (bi, 0, 0)),
                   pl.BlockSpec((None, n_e, LANES), lambda bi: (bi, 0, 0))],
        out_shape=[jax.ShapeDtypeStruct((b, n_e, s), I32),
                   jax.ShapeDtypeStruct((b, n_e, LANES), I32)],
        compiler_params=_cparams(("parallel",), 32),
        name="topk",
    )(aff)


def _slot_window(first):
    return pl.multiple_of((first // SLOT_ALIGN) * SLOT_ALIGN, SLOT_ALIGN)


def _block_windows(cum_ref, first_expert, n_e, j):
    los, fits = [], None
    for e in range(n_e):
        at = (first_expert + e) * LANES + j
        lo = _slot_window(cum_ref[at])
        ok = cum_ref[at + 1] - lo <= FAST_WINDOW
        los.append(lo)
        fits = ok if fits is None else jnp.logical_and(fits, ok)
    return los, fits


def _gather_kernel(cum_ref, rank_ref, aff_ref, h2_ref, xe_ref, gate_ref):
    n_e = rank_ref.shape[0]
    rows = h2_ref.shape[0]
    first_expert = (pl.program_id(0) * pl.num_programs(1) + pl.program_id(1)) * n_e
    t = pl.program_id(2)

    @pl.when(t == 0)
    def _():
        xe_ref[...] = jnp.zeros_like(xe_ref)
        gate_ref[...] = jnp.zeros_like(gate_ref)

    row_fast = lax.broadcasted_iota(I32, (FAST_WINDOW, SLOT_TILE), 0)
    row_slow = lax.broadcasted_iota(I32, (SLOT_WINDOW, SLOT_TILE), 0)
    for jj in range(rows // SLOT_TILE):
        toks = slice(jj * SLOT_TILE, (jj + 1) * SLOT_TILE)
        los, fits = _block_windows(cum_ref, first_expert, n_e, t * (rows // SLOT_TILE) + jj)

        def hit(e, row_i):
            return (row_i + los[e]) == rank_ref[e:e + 1, toks]

        def add_window(e, width, hit_e, rows_e):
            win = pl.ds(los[e], width)
            xe_ref[e, win, :] += rows_e.astype(BF16)
            gate_ref[e, win, :] += jnp.sum(jnp.where(hit_e, aff_ref[e:e + 1, toks], 0.0), axis=1,
                                           keepdims=True)

        @pl.when(fits)
        def _():
            hs = [hit(e, row_fast) for e in range(n_e)]
            stack = jnp.concatenate([jnp.where(h, 1.0, 0.0).astype(BF16) for h in hs], axis=0)
            res = _dot(stack, h2_ref[toks, :])
            for e in range(n_e):
                add_window(e, FAST_WINDOW, hs[e], res[e * FAST_WINDOW:(e + 1) * FAST_WINDOW])

        @pl.when(jnp.logical_not(fits))
        def _():
            for e in range(n_e):
                h = hit(e, row_slow)
                add_window(e, SLOT_WINDOW, h,
                           _dot(jnp.where(h, 1.0, 0.0).astype(BF16), h2_ref[toks, :]))


def _gather(cum_flat, rank, aff, h2, cap):
    b, s, d = h2.shape
    n_e = rank.shape[1]
    rows = cap + SLOT_WINDOW
    grp = n_e // GATHER_GROUPS
    per_tok = pl.BlockSpec((None, grp, ROW_TILE), lambda bi, eg, t, c: (bi, eg, t))
    whole = lambda w: pl.BlockSpec((None, grp, rows, w), lambda bi, eg, t, c: (bi, eg, 0, 0))
    grid_spec = pltpu.PrefetchScalarGridSpec(
        num_scalar_prefetch=1,
        grid=(b, GATHER_GROUPS, s // ROW_TILE),
        in_specs=[per_tok, per_tok,
                  pl.BlockSpec((None, ROW_TILE, d), lambda bi, eg, t, c: (bi, t, 0))],
        out_specs=[whole(d), whole(LANES)],
    )
    return pl.pallas_call(
        _gather_kernel,
        grid_spec=grid_spec,
        out_shape=[jax.ShapeDtypeStruct((b, n_e, rows, d), BF16),
                   jax.ShapeDtypeStruct((b, n_e, rows, LANES), F32)],
        compiler_params=_cparams(("arbitrary", "arbitrary", "arbitrary"), 56),
        name="gather",
    )(cum_flat, rank, aff, h2)


def _moe_ffn_kernel(xe_ref, gate_ref, wg_ref, wu_ref, wd_ref, ye_ref):
    cap = xe_ref.shape[0]
    xe = xe_ref[...]
    gate_h = _dot(xe, wg_ref[...].astype(BF16))
    up_h = _dot(xe, wu_ref[...].astype(BF16))
    hidden = (gate_h * _sigmoid(gate_h) * up_h).astype(BF16)
    ye = _dot(hidden, wd_ref[...].astype(BF16)) * gate_ref[:, 0:1]
    ye_ref[:cap, :] = ye.astype(BF16)
    ye_ref[cap:, :] = jnp.zeros((ye_ref.shape[0] - cap, ye_ref.shape[1]), BF16)


def _moe_ffn(xe, gate, wg, wu, wd, cap):
    b, n_e, rows, d = xe.shape
    hid = wg.shape[2]
    return pl.pallas_call(
        _moe_ffn_kernel,
        grid=(b, n_e),
        in_specs=[
            pl.BlockSpec((None, None, cap, d), lambda bi, e: (bi, e, 0, 0)),
            pl.BlockSpec((None, None, cap, LANES), lambda bi, e: (bi, e, 0, 0)),
            pl.BlockSpec((None, d, hid), lambda bi, e: (e, 0, 0)),
            pl.BlockSpec((None, d, hid), lambda bi, e: (e, 0, 0)),
            pl.BlockSpec((None, hid, d), lambda bi, e: (e, 0, 0)),
        ],
        out_specs=pl.BlockSpec((None, None, rows, d), lambda bi, e: (bi, e, 0, 0)),
        out_shape=jax.ShapeDtypeStruct((b, n_e, rows, d), BF16),
        compiler_params=_cparams(("parallel", "parallel"), 56),
        name="moe_ffn",
    )(xe, gate, wg, wu, wd)


def _combine_kernel(cum_ref, rank_ref, ye_ref, x1_ref, g_ref, y_ref, rhs_ref, moe_ref):
    n_e = rank_ref.shape[0]
    rows = x1_ref.shape[0]
    bi = pl.program_id(0)
    t = pl.program_id(1)
    row_fast = lax.broadcasted_iota(I32, (FAST_WINDOW, SLOT_TILE), 0)
    row_slow = lax.broadcasted_iota(I32, (SLOT_WINDOW, SLOT_TILE), 0)
    for jj in range(rows // SLOT_TILE):
        toks = slice(jj * SLOT_TILE, (jj + 1) * SLOT_TILE)
        los, fits = _block_windows(cum_ref, bi * n_e, n_e, t * (rows // SLOT_TILE) + jj)

        def hits(e, row_i):
            return jnp.where((row_i + los[e]) == rank_ref[e:e + 1, toks], 1.0, 0.0).astype(BF16)

        @pl.when(fits)
        def _():
            for e in range(n_e):
                rhs_ref[e * FAST_WINDOW:(e + 1) * FAST_WINDOW, :] = (
                    ye_ref[e, pl.ds(los[e], FAST_WINDOW), :])
            stack = jnp.concatenate([hits(e, row_fast) for e in range(n_e)], axis=0)
            moe_ref[...] = _dot_tn(stack, rhs_ref[...])

        @pl.when(jnp.logical_not(fits))
        def _():
            acc = _dot_tn(hits(0, row_slow), ye_ref[0, pl.ds(los[0], SLOT_WINDOW), :])
            for e in range(1, n_e):
                acc = acc + _dot_tn(hits(e, row_slow), ye_ref[e, pl.ds(los[e], SLOT_WINDOW), :])
            moe_ref[...] = acc

        y_ref[toks, :] = _rms(x1_ref[toks, :] + moe_ref[...], g_ref[...])


def _combine(cum_flat, rank, ye, x1, g):
    b, s, d = x1.shape
    n_e, ye_rows = ye.shape[1], ye.shape[2]
    row = pl.BlockSpec((None, ROW_TILE, d), lambda bi, t, c: (bi, t, 0))
    grid_spec = pltpu.PrefetchScalarGridSpec(
        num_scalar_prefetch=1,
        grid=(b, s // ROW_TILE),
        in_specs=[
            pl.BlockSpec((None, n_e, ROW_TILE), lambda bi, t, c: (bi, 0, t)),
            pl.BlockSpec((None, n_e, ye_rows, d), lambda bi, t, c: (bi, 0, 0, 0)),
            row,
            pl.BlockSpec(g.shape, lambda bi, t, c: (0, 0)),
        ],
        out_specs=row,
        scratch_shapes=[pltpu.VMEM((n_e * FAST_WINDOW, d), BF16),
                        pltpu.VMEM((SLOT_TILE, d), F32)],
    )
    return pl.pallas_call(
        _combine_kernel,
        grid_spec=grid_spec,
        out_shape=jax.ShapeDtypeStruct((b, s, d), F32),
        compiler_params=_cparams(("arbitrary", "arbitrary"), 60),
        name="combine",
    )(cum_flat, rank, ye, x1, g)


def _moe_stages(aff, h2, x1, wg, wu, wd, g_final, cap):
    b, n_e, s = aff.shape
    rank, cum = _topk(aff, cap)
    cum_flat = cum.reshape(-1)
    xe, gate = _gather(cum_flat, rank, aff, h2, cap)
    ye = _moe_ffn(xe, gate, wg, wu, wd, cap)
    return _combine(cum_flat, rank, ye, x1, g_final)


def kernel(x, norm_mix_g, w_in, b_gate, gmlp_norm_g, w_spatial, b_spatial, w_proj_a, w_proj_b,
           w_out, norm_ffn_g, w_router, w_e_gate, w_e_up, w_e_down, norm_final_g):
    b, s, d = x.shape
    assert w_in.shape[0] == 1, "single-layer block"
    cap = CAPACITY_FACTOR * s // N_EXPERTS
    group_width = GMLP_WIDTH // GMLP_GROUPS
    ws_pairs = w_spatial[0].astype(BF16).reshape(GMLP_GROUPS // 2, 2 * CHUNK, CHUNK)
    bsp = jnp.repeat(b_spatial[0].T, group_width, axis=1)
    qkv, ta, gb = _mix_in(x, norm_mix_g, w_in[0].astype(BF16), b_gate, gmlp_norm_g, ws_pairs, bsp,
                          w_proj_a[0].astype(BF16))
    os_, ls_ = [], []
    for (q, k, v), dil in zip(qkv, DILATIONS):
        o, lse = _attn_pattern(q, k, v, dil)
        os_.append(o)
        ls_.append(lse)
    x1, h2, aff = _mix_out(x, ta, gb, os_, ls_, w_proj_b[0].astype(BF16), w_out[0].astype(BF16),
                           norm_ffn_g, w_router[0].T)
    return _moe_stages(aff, h2, x1, w_e_gate[0], w_e_up[0], w_e_down[0], norm_final_g[None], cap)
```

```python
import functools

import jax
import jax.numpy as jnp
from jax import lax
from jax.experimental import pallas as pl
from jax.experimental.pallas import tpu as pltpu

F32 = jnp.float32
BF16 = jnp.bfloat16
I32 = jnp.int32

EPS = 1e-6
GMLP_WIDTH = 512
GMLP_GROUPS = 8
CHUNK = 128
N_HEADS = 8
HEAD_DIM = 64
ATTN_WIDTH = N_HEADS * HEAD_DIM
DILATIONS = (1, 4, 16)
HALF_WINDOW = 64
N_EXPERTS = 16
CAPACITY_FACTOR = 2

LANES = 128
Q_TILE = 128
KEY_TILE = 2 * Q_TILE
ATTN_STEP_ROWS = 1024
LSE_LANES = LANES // N_HEADS
SLOT_TILE = 128
SLOT_ALIGN = 16
SLOT_WINDOW = SLOT_TILE + SLOT_ALIGN
GATHER_GROUPS = 2
FAST_WINDOW = 48
ROW_TILE = 512
MIB = 1024 * 1024


def _cparams(sem, vmem_mib):
    return pltpu.CompilerParams(dimension_semantics=sem, vmem_limit_bytes=vmem_mib * MIB)


def _gelu_tanh(x):
    return 0.5 * x * (1.0 + jnp.tanh(0.7978845608028654 * (x + 0.044715 * (x * x * x))))


def _sigmoid(x):
    return 1.0 / (1.0 + jnp.exp(-x))


def _rms(x, g):
    return x * lax.rsqrt(jnp.mean(x * x, axis=-1, keepdims=True) + EPS) * g


def _dot(a, b):
    return jnp.dot(a, b, preferred_element_type=F32)


def _dot_nt(a, b):
    return lax.dot_general(a, b, (((1,), (1,)), ((), ())), preferred_element_type=F32)


def _dot_tn(a, b):
    return lax.dot_general(a, b, (((0,), (0,)), ((), ())), preferred_element_type=F32)


def _mix_in_kernel(x_ref, g_ref, win_ref, bg_ref, g2_ref, ws_ref, bsp_ref, pa_ref, *refs):
    n_qkv = 3 * len(DILATIONS)
    qkv_refs = refs[:n_qkv]
    ta_ref, gb_ref = refs[n_qkv:n_qkv + 2]
    stage_refs = refs[n_qkv + 2:]
    rows, d_model = x_ref.shape
    h = _rms(x_ref[...], g_ref[...]).astype(BF16)

    def proj(lo, width):
        return _dot(h, win_ref[:, lo:lo + width])

    c0 = 0
    u = _gelu_tanh(proj(c0, GMLP_WIDTH)); c0 += GMLP_WIDTH
    v = _gelu_tanh(proj(c0, GMLP_WIDTH)); c0 += GMLP_WIDTH
    for i in range(3):
        val = proj(c0, ATTN_WIDTH); c0 += ATTN_WIDTH
        if i == 0:
            val = val * (HEAD_DIM ** -0.5)
        stage = stage_refs[i]
        for p in range(ATTN_WIDTH // LANES):
            stage[p] = val[:, p * LANES:(p + 1) * LANES]
        qkv_refs[i][0] = val.astype(BF16)
    ga = _sigmoid(proj(c0, d_model) + bg_ref[:, :d_model]); c0 += d_model
    gb = _sigmoid(proj(c0, d_model) + bg_ref[:, d_model:])
    gb_ref[...] = gb.astype(BF16)
    for i in range(3):
        for di, dil in enumerate(DILATIONS[1:], start=1):
            out = qkv_refs[3 * di + i]
            for r in range(dil):
                for p in range(ATTN_WIDTH // LANES):
                    out[r, :, p * LANES:(p + 1) * LANES] = (
                        stage_refs[i][p, pl.ds(r, rows // dil, stride=dil), :].astype(BF16))

    vn = _rms(v, g2_ref[...]).astype(BF16)
    lane_lo = lax.broadcasted_iota(I32, (CHUNK, LANES), 1) < HEAD_DIM
    bsp = bsp_ref[...]
    n_chunk = rows // CHUNK
    mixed_slabs = []
    for p in range(GMLP_WIDTH // LANES):
        slab = jnp.concatenate(
            [vn[c * CHUNK:(c + 1) * CHUNK, p * LANES:(p + 1) * LANES] for c in range(n_chunk)],
            axis=1)
        r = _dot(ws_ref[p], slab)
        mixed_slabs.append([jnp.where(lane_lo, r[:CHUNK, c * LANES:(c + 1) * LANES],
                                      r[CHUNK:, c * LANES:(c + 1) * LANES])
                            for c in range(n_chunk)])
    a_chunks = []
    for c in range(n_chunk):
        rs = slice(c * CHUNK, (c + 1) * CHUNK)
        mixed = jnp.concatenate([slabs[c] for slabs in mixed_slabs], axis=1) + bsp
        a_chunks.append((u[rs] * mixed).astype(BF16))
    a = jnp.concatenate(a_chunks, axis=0)
    ta_ref[...] = (ga * _dot(a, pa_ref[...])).astype(BF16)


def _mix_in(x, g, w_in, b_gate, g2, ws_pairs, bsp, w_pa):
    b, s, d = x.shape
    const = lambda shape: pl.BlockSpec(shape, lambda bi, t: (0,) * len(shape))
    row = lambda w: pl.BlockSpec((None, ROW_TILE, w), lambda bi, t: (bi, t, 0))
    qkv_specs, qkv_shapes = [], []
    for dil in DILATIONS:
        spec = pl.BlockSpec((None, dil, ROW_TILE // dil, ATTN_WIDTH), lambda bi, t: (bi, 0, t, 0))
        qkv_specs += [spec] * 3
        qkv_shapes += [jax.ShapeDtypeStruct((b, dil, s // dil, ATTN_WIDTH), BF16)] * 3
    outs = pl.pallas_call(
        _mix_in_kernel,
        grid=(b, s // ROW_TILE),
        in_specs=[row(d), const(g.shape), const(w_in.shape), const(b_gate.shape), const(g2.shape),
                  const(ws_pairs.shape), const(bsp.shape), const(w_pa.shape)],
        out_specs=qkv_specs + [row(d), row(d)],
        out_shape=qkv_shapes + [jax.ShapeDtypeStruct((b, s, d), BF16)] * 2,
        scratch_shapes=[pltpu.VMEM((ATTN_WIDTH // LANES, ROW_TILE, LANES), F32)] * 3,
        compiler_params=_cparams(("parallel", "parallel"), 56),
        name="mix_in",
    )(x, g, w_in, b_gate, g2, ws_pairs, bsp, w_pa)
    n_qkv = 3 * len(DILATIONS)
    qkv = [outs[3 * i:3 * i + 3] for i in range(len(DILATIONS))]
    return qkv, outs[n_qkv], outs[n_qkv + 1]


def _attn_kernel(q_ref, k_ref, v_ref, o_ref, l_ref, bias_ref, *, dil):
    n_res, rows, _ = q_ref.shape
    seq = k_ref.shape[1]
    t = pl.program_id(2)
    first = jnp.logical_and(jnp.logical_and(pl.program_id(0) == 0, pl.program_id(1) == 0), t == 0)

    @pl.when(first)
    def _():
        ii = lax.broadcasted_iota(I32, (Q_TILE, KEY_TILE), 0)
        jj = lax.broadcasted_iota(I32, (Q_TILE, KEY_TILE), 1)
        for var in range(3):
            absd = jnp.abs(jj - ii - var * HALF_WINDOW)
            valid = absd <= HALF_WINDOW
            absf = absd.astype(F32)
            for h in range(N_HEADS):
                slope = 2.0 ** (-8.0 * (h + 1) / N_HEADS)
                bias_ref[var, h] = jnp.where(valid, -(slope * dil) * absf, -jnp.inf)

    lane = lax.broadcasted_iota(I32, (Q_TILE, LANES), 1)
    lane_lo = lane < HEAD_DIM
    mask_lo = jnp.where(lane_lo, 1.0, 0.0).astype(BF16)
    mask_hi = jnp.where(lane_lo, 0.0, 1.0).astype(BF16)
    lane_head = lane // LSE_LANES
    for rr in range(n_res):
        for qi in range(rows // Q_TILE):
            rs = slice(qi * Q_TILE, (qi + 1) * Q_TILE)
            i0 = t * rows + qi * Q_TILE
            start = pl.multiple_of(jnp.clip(i0 - HALF_WINDOW, 0, seq - KEY_TILE), HALF_WINDOW)
            var = (i0 - start) // HALF_WINDOW
            lse_tile = jnp.zeros((Q_TILE, LANES), F32)
            for p in range(ATTN_WIDTH // LANES):
                cs = slice(p * LANES, (p + 1) * LANES)
                qp = q_ref[rr, rs, cs]
                kp = k_ref[rr, pl.ds(start, KEY_TILE), cs]
                vp = v_ref[rr, pl.ds(start, KEY_TILE), cs]
                q2 = jnp.concatenate([qp * mask_lo, qp * mask_hi], axis=0)
                s2 = _dot_nt(q2, kp)
                probs, inv_dens = [], []
                for hh in range(2):
                    h = 2 * p + hh
                    s = s2[hh * Q_TILE:(hh + 1) * Q_TILE] + bias_ref[var, h]
                    m = jnp.max(s, axis=-1, keepdims=True)
                    e = jnp.exp(s - m)
                    den = jnp.sum(e, axis=-1, keepdims=True)
                    probs.append(e)
                    inv_dens.append(1.0 / den)
                    lse_tile = jnp.where(lane_head == h, m + jnp.log(den), lse_tile)
                o2 = _dot(jnp.concatenate(probs, axis=0).astype(BF16), vp)
                o_c = jnp.where(lane_lo, o2[:Q_TILE] * inv_dens[0], o2[Q_TILE:] * inv_dens[1])
                o_ref[rr, rs, cs] = o_c.astype(BF16)
            l_ref[rr, rs, :] = lse_tile


def _attn_pattern(q, k, v, dil):
    b, _, seq, w = q.shape
    rows = min(seq, ATTN_STEP_ROWS)
    n_res = ATTN_STEP_ROWS // rows
    qspec = lambda width: pl.BlockSpec((None, n_res, rows, width), lambda bi, r, t: (bi, r, t, 0))
    kspec = pl.BlockSpec((None, n_res, seq, w), lambda bi, r, t: (bi, r, 0, 0))
    return pl.pallas_call(
        functools.partial(_attn_kernel, dil=dil),
        grid=(b, dil // n_res, seq // rows),
        in_specs=[qspec(w), kspec, kspec],
        out_specs=[qspec(w), qspec(LANES)],
        out_shape=[jax.ShapeDtypeStruct(q.shape, BF16),
                   jax.ShapeDtypeStruct((b, dil, seq, LANES), F32)],
        scratch_shapes=[pltpu.VMEM((3, N_HEADS, Q_TILE, KEY_TILE), F32)],
        compiler_params=_cparams(("arbitrary", "arbitrary", "arbitrary"), 48),
        name=f"attn_d{dil}",
    )(q, k, v)


def _mix_out_kernel(x_ref, ta_ref, gb_ref, *refs):
    n_pat = len(DILATIONS)
    o_refs = refs[:n_pat]
    l_refs = refs[n_pat:2 * n_pat]
    pb_ref, wo_ref, g_ref, wr_ref, x1_ref, h2_ref, aff_ref = refs[2 * n_pat:2 * n_pat + 7]
    stage_refs = refs[2 * n_pat + 7:]
    rows = x_ref.shape[0]

    outs, lses = [], []
    for di, dil in enumerate(DILATIONS):
        if dil == 1:
            outs.append(o_refs[di][0].astype(F32))
            lses.append(l_refs[di][0])
            continue
        o_st, l_st = stage_refs[2 * (di - 1)], stage_refs[2 * (di - 1) + 1]
        n_slab = ATTN_WIDTH // LANES
        for r in range(dil):
            dst = pl.ds(r, rows // dil, stride=dil)
            o_r = o_refs[di][r].astype(F32)
            for p in range(n_slab):
                o_st[p, dst, :] = o_r[:, p * LANES:(p + 1) * LANES]
            l_st[dst, :] = l_refs[di][r]
        outs.append(jnp.concatenate([o_st[p] for p in range(n_slab)], axis=1))
        lses.append(l_st[...])

    m = functools.reduce(jnp.maximum, lses)
    ws = [jnp.exp(l - m) for l in lses]
    inv = 1.0 / functools.reduce(lambda a, c: a + c, ws)
    k_i = lax.broadcasted_iota(I32, (LANES, ATTN_WIDTH), 0)
    c_i = lax.broadcasted_iota(I32, (LANES, ATTN_WIDTH), 1)
    spread = jnp.where(k_i == (c_i // HEAD_DIM) * LSE_LANES, 1.0, 0.0).astype(BF16)
    o = None
    for w, o_p in zip(ws, outs):
        w = w * inv
        w_hi = w.astype(BF16)
        w_lo = (w - w_hi.astype(F32)).astype(BF16)
        term = (_dot(w_hi, spread) + _dot(w_lo, spread)) * o_p
        o = term if o is None else o + term

    ob = _dot(o.astype(BF16), pb_ref[...])
    merged = (ta_ref[...].astype(F32) + gb_ref[...].astype(F32) * ob).astype(BF16)
    x1 = x_ref[...] + _dot(merged, wo_ref[...])
    x1_ref[...] = x1
    h2 = _rms(x1, g_ref[...])
    h2_ref[...] = h2.astype(BF16)
    h_hi = h2.astype(BF16)
    h_lo = (h2 - h_hi.astype(F32)).astype(BF16)
    wr = wr_ref[...]
    w_hi = wr.astype(BF16)
    w_lo = (wr - w_hi.astype(F32)).astype(BF16)
    logits = _dot_nt(w_hi, h_hi) + (_dot_nt(w_hi, h_lo) + _dot_nt(w_lo, h_hi))
    e = jnp.exp(logits - jnp.max(logits, axis=0, keepdims=True))
    aff_ref[...] = e / jnp.sum(e, axis=0, keepdims=True)


def _mix_out(x, ta, gb, os_, ls_, w_pb, w_out, g, w_router_t):
    b, s, d = x.shape
    n_e = w_router_t.shape[0]
    const = lambda shape: pl.BlockSpec(shape, lambda bi, t: (0,) * len(shape))
    row = lambda w: pl.BlockSpec((None, ROW_TILE, w), lambda bi, t: (bi, t, 0))
    res = lambda dil, w: pl.BlockSpec((None, dil, ROW_TILE // dil, w), lambda bi, t: (bi, 0, t, 0))
    stage = []
    for dil in DILATIONS[1:]:
        stage += [pltpu.VMEM((ATTN_WIDTH // LANES, ROW_TILE, LANES), F32),
                  pltpu.VMEM((ROW_TILE, LANES), F32)]
    return pl.pallas_call(
        _mix_out_kernel,
        grid=(b, s // ROW_TILE),
        in_specs=[row(d), row(d), row(d)]
                 + [res(dil, ATTN_WIDTH) for dil in DILATIONS]
                 + [res(dil, LANES) for dil in DILATIONS]
                 + [const(w_pb.shape), const(w_out.shape), const(g.shape), const(w_router_t.shape)],
        out_specs=[row(d), row(d), pl.BlockSpec((None, n_e, ROW_TILE), lambda bi, t: (bi, 0, t))],
        out_shape=[jax.ShapeDtypeStruct((b, s, d), F32), jax.ShapeDtypeStruct((b, s, d), BF16),
                   jax.ShapeDtypeStruct((b, n_e, s), F32)],
        scratch_shapes=stage,
        compiler_params=_cparams(("parallel", "parallel"), 48),
        name="mix_out",
    )(x, ta, gb, *os_, *ls_, w_pb, w_out, g, w_router_t)


def _topk_kernel(aff_ref, rank_ref, cum_ref, *, cap):
    n_e, s = aff_ref.shape
    n_blk = s // SLOT_TILE
    aff = aff_ref[...]
    thr = jnp.zeros((n_e, 1), I32)
    for bit in range(30, -1, -1):
        cand = thr | (1 << bit)
        cnt = jnp.sum((aff >= pltpu.bitcast(cand, F32)).astype(I32), axis=1, keepdims=True)
        thr = jnp.where(cnt >= cap, cand, thr)
    above = aff >= pltpu.bitcast(thr + 1, F32)
    tie = jnp.logical_and(aff >= pltpu.bitcast(thr, F32), jnp.logical_not(above))
    need = (cap - jnp.sum(above.astype(I32), axis=1, keepdims=True)).astype(F32)
    r_i = lax.broadcasted_iota(I32, (SLOT_TILE, SLOT_TILE), 0)
    c_i = lax.broadcasted_iota(I32, (SLOT_TILE, SLOT_TILE), 1)
    tri = jnp.where(r_i < c_i, 1.0, 0.0).astype(BF16)
    lane = lax.broadcasted_iota(I32, (n_e, LANES), 1)
    run_tie = jnp.zeros((n_e, 1), F32)
    run_sel = jnp.zeros((n_e, 1), F32)
    cum = jnp.zeros((n_e, LANES), F32)
    for j in range(n_blk):
        cs = slice(j * SLOT_TILE, (j + 1) * SLOT_TILE)
        tie_f = jnp.where(tie[:, cs], 1.0, 0.0)
        tie_rank = _dot(tie_f.astype(BF16), tri) + run_tie
        run_tie = run_tie + jnp.sum(tie_f, axis=1, keepdims=True)
        sel_f = jnp.where(above[:, cs], 1.0, jnp.where(tie_rank < need, tie_f, 0.0))
        rank = _dot(sel_f.astype(BF16), tri) + run_sel
        rank_ref[:, cs] = jnp.where(sel_f > 0.0, rank, -1.0).astype(I32)
        cum = jnp.where(lane == j, run_sel, cum)
        run_sel = run_sel + jnp.sum(sel_f, axis=1, keepdims=True)
    cum = jnp.where(lane == n_blk, run_sel, cum)
    cum_ref[...] = cum.astype(I32)


def _topk(aff, cap):
    b, n_e, s = aff.shape
    return pl.pallas_call(
        functools.partial(_topk_kernel, cap=cap),
        grid=(b,),
        in_specs=[pl.BlockSpec((None, n_e, s), lambda bi: (bi, 0, 0))],
        out_specs=[pl.BlockSpec((None, n_e, s), lambda bi: (bi, 0, 0)),
                   pl.BlockSpec((None, n_e, LANES), lambda bi: (bi, 0, 0))],
        out_shape=[jax.ShapeDtypeStruct((b, n_e, s), I32),
                   jax.ShapeDtypeStruct((b, n_e, LANES), I32)],
        compiler_params=_cparams(("parallel",), 32),
        name="topk",
    )(aff)


def _slot_window(first):
    return pl.multiple_of((first // SLOT_ALIGN) * SLOT_ALIGN, SLOT_ALIGN)


def _block_windows(cum_ref, first_expert, n_e, j):
    los, fits = [], None
    for e in range(n_e):
        at = (first_expert + e) * LANES + j
        lo = _slot_window(cum_ref[at])
        ok = cum_ref[at + 1] - lo <= FAST_WINDOW
        los.append(lo)
        fits = ok if fits is None else jnp.logical_and(fits, ok)
    return los, fits


def _run_blocks(fits, fast, slow):
    all_fit = functools.reduce(jnp.logical_and, fits)

    @pl.when(all_fit)
    def _():
        for jj in range(len(fits)):
            fast(jj)

    @pl.when(jnp.logical_not(all_fit))
    def _():
        for jj, fit in enumerate(fits):
            pl.when(fit)(functools.partial(fast, jj))
            pl.when(jnp.logical_not(fit))(functools.partial(slow, jj))


def _gather_kernel(cum_ref, rank_ref, aff_ref, h2_ref, xe_ref, gate_ref):
    n_e = rank_ref.shape[0]
    rows = h2_ref.shape[0]
    first_expert = (pl.program_id(0) * pl.num_programs(1) + pl.program_id(1)) * n_e
    t = pl.program_id(2)

    @pl.when(t == 0)
    def _():
        xe_ref[...] = jnp.zeros_like(xe_ref)
        gate_ref[...] = jnp.zeros_like(gate_ref)

    row_fast = lax.broadcasted_iota(I32, (FAST_WINDOW, SLOT_TILE), 0)
    row_slow = lax.broadcasted_iota(I32, (SLOT_WINDOW, SLOT_TILE), 0)
    n_blk = rows // SLOT_TILE
    windows = [_block_windows(cum_ref, first_expert, n_e, t * n_blk + jj) for jj in range(n_blk)]

    def hit(jj, e, row_i):
        toks = slice(jj * SLOT_TILE, (jj + 1) * SLOT_TILE)
        return (row_i + windows[jj][0][e]) == rank_ref[e:e + 1, toks]

    def add_window(jj, e, width, hit_e, rows_e):
        toks = slice(jj * SLOT_TILE, (jj + 1) * SLOT_TILE)
        win = pl.ds(windows[jj][0][e], width)
        xe_ref[e, win, :] += rows_e.astype(BF16)
        gate_ref[e, win, :] += jnp.sum(jnp.where(hit_e, aff_ref[e:e + 1, toks], 0.0), axis=1,
                                       keepdims=True)

    def fast(jj):
        hs = [hit(jj, e, row_fast) for e in range(n_e)]
        stack = jnp.concatenate([jnp.where(h, 1.0, 0.0).astype(BF16) for h in hs], axis=0)
        res = _dot(stack, h2_ref[jj * SLOT_TILE:(jj + 1) * SLOT_TILE, :])
        for e in range(n_e):
            add_window(jj, e, FAST_WINDOW, hs[e], res[e * FAST_WINDOW:(e + 1) * FAST_WINDOW])

    def slow(jj):
        for e in range(n_e):
            h = hit(jj, e, row_slow)
            add_window(jj, e, SLOT_WINDOW, h,
                       _dot(jnp.where(h, 1.0, 0.0).astype(BF16),
                            h2_ref[jj * SLOT_TILE:(jj + 1) * SLOT_TILE, :]))

    _run_blocks([fits for _, fits in windows], fast, slow)


def _gather(cum_flat, rank, aff, h2, cap):
    b, s, d = h2.shape
    n_e = rank.shape[1]
    rows = cap + SLOT_WINDOW
    grp = n_e // GATHER_GROUPS
    per_tok = pl.BlockSpec((None, grp, ROW_TILE), lambda bi, eg, t, c: (bi, eg, t))
    whole = lambda w: pl.BlockSpec((None, grp, rows, w), lambda bi, eg, t, c: (bi, eg, 0, 0))
    grid_spec = pltpu.PrefetchScalarGridSpec(
        num_scalar_prefetch=1,
        grid=(b, GATHER_GROUPS, s // ROW_TILE),
        in_specs=[per_tok, per_tok,
                  pl.BlockSpec((None, ROW_TILE, d), lambda bi, eg, t, c: (bi, t, 0))],
        out_specs=[whole(d), whole(LANES)],
    )
    return pl.pallas_call(
        _gather_kernel,
        grid_spec=grid_spec,
        out_shape=[jax.ShapeDtypeStruct((b, n_e, rows, d), BF16),
                   jax.ShapeDtypeStruct((b, n_e, rows, LANES), F32)],
        compiler_params=_cparams(("arbitrary", "arbitrary", "arbitrary"), 56),
        name="gather",
    )(cum_flat, rank, aff, h2)


def _moe_ffn_kernel(xe_ref, gate_ref, wg_ref, wu_ref, wd_ref, ye_ref, wg_bf, wu_bf, wd_bf):
    cap = xe_ref.shape[0]

    @pl.when(pl.program_id(1) == 0)
    def _():
        wg_bf[...] = wg_ref[...].astype(BF16)
        wu_bf[...] = wu_ref[...].astype(BF16)
        wd_bf[...] = wd_ref[...].astype(BF16)

    xe = xe_ref[...]
    gate_h = _dot(xe, wg_bf[...])
    up_h = _dot(xe, wu_bf[...])
    hidden = (gate_h * _sigmoid(gate_h) * up_h).astype(BF16)
    ye = _dot(hidden, wd_bf[...]) * gate_ref[:, 0:1]
    ye_ref[:cap, :] = ye.astype(BF16)
    ye_ref[cap:, :] = jnp.zeros((ye_ref.shape[0] - cap, ye_ref.shape[1]), BF16)


def _moe_ffn(xe, gate, wg, wu, wd, cap):
    b, n_e, rows, d = xe.shape
    hid = wg.shape[2]
    return pl.pallas_call(
        _moe_ffn_kernel,
        grid=(n_e, b),
        in_specs=[
            pl.BlockSpec((None, None, cap, d), lambda e, bi: (bi, e, 0, 0)),
            pl.BlockSpec((None, None, cap, LANES), lambda e, bi: (bi, e, 0, 0)),
            pl.BlockSpec((None, d, hid), lambda e, bi: (e, 0, 0)),
            pl.BlockSpec((None, d, hid), lambda e, bi: (e, 0, 0)),
            pl.BlockSpec((None, hid, d), lambda e, bi: (e, 0, 0)),
        ],
        out_specs=pl.BlockSpec((None, None, rows, d), lambda e, bi: (bi, e, 0, 0)),
        out_shape=jax.ShapeDtypeStruct((b, n_e, rows, d), BF16),
        scratch_shapes=[pltpu.VMEM((d, hid), BF16), pltpu.VMEM((d, hid), BF16),
                        pltpu.VMEM((hid, d), BF16)],
        compiler_params=_cparams(("arbitrary", "arbitrary"), 56),
        name="moe_ffn",
    )(xe, gate, wg, wu, wd)


def _combine_kernel(cum_ref, rank_ref, ye_ref, x1_ref, g_ref, y_ref, rhs_ref):
    n_e = rank_ref.shape[0]
    n_blk = x1_ref.shape[0] // SLOT_TILE
    first_expert = pl.program_id(0) * n_e
    t = pl.program_id(1)
    row_fast = lax.broadcasted_iota(I32, (FAST_WINDOW, SLOT_TILE), 0)
    row_slow = lax.broadcasted_iota(I32, (SLOT_WINDOW, SLOT_TILE), 0)
    windows = [_block_windows(cum_ref, first_expert, n_e, t * n_blk + jj) for jj in range(n_blk)]

    def hits(jj, e, row_i):
        toks = slice(jj * SLOT_TILE, (jj + 1) * SLOT_TILE)
        hit = (row_i + windows[jj][0][e]) == rank_ref[e:e + 1, toks]
        return jnp.where(hit, 1.0, 0.0).astype(BF16)

    def finish(jj, moe):
        toks = slice(jj * SLOT_TILE, (jj + 1) * SLOT_TILE)
        y_ref[toks, :] = _rms(x1_ref[toks, :] + moe, g_ref[...])

    def fast(jj):
        rhs = rhs_ref.at[jj % rhs_ref.shape[0]]
        for e in range(n_e):
            rhs[e * FAST_WINDOW:(e + 1) * FAST_WINDOW, :] = (
                ye_ref[e, pl.ds(windows[jj][0][e], FAST_WINDOW), :])
        stack = jnp.concatenate([hits(jj, e, row_fast) for e in range(n_e)], axis=0)
        finish(jj, _dot_tn(stack, rhs[...]))

    def slow(jj):
        acc = None
        for e in range(n_e):
            part = _dot_tn(hits(jj, e, row_slow),
                           ye_ref[e, pl.ds(windows[jj][0][e], SLOT_WINDOW), :])
            acc = part if acc is None else acc + part
        finish(jj, acc)

    _run_blocks([fits for _, fits in windows], fast, slow)


def _combine(cum_flat, rank, ye, x1, g):
    b, s, d = x1.shape
    n_e, ye_rows = ye.shape[1], ye.shape[2]
    row = pl.BlockSpec((None, ROW_TILE, d), lambda bi, t, c: (bi, t, 0))
    grid_spec = pltpu.PrefetchScalarGridSpec(
        num_scalar_prefetch=1,
        grid=(b, s // ROW_TILE),
        in_specs=[
            pl.BlockSpec((None, n_e, ROW_TILE), lambda bi, t, c: (bi, 0, t)),
            pl.BlockSpec((None, n_e, ye_rows, d), lambda bi, t, c: (bi, 0, 0, 0)),
            row,
            pl.BlockSpec(g.shape, lambda bi, t, c: (0, 0)),
        ],
        out_specs=row,
        scratch_shapes=[pltpu.VMEM((2, n_e * FAST_WINDOW, d), BF16)],
    )
    return pl.pallas_call(
        _combine_kernel,
        grid_spec=grid_spec,
        out_shape=jax.ShapeDtypeStruct((b, s, d), F32),
        compiler_params=_cparams(("arbitrary", "arbitrary"), 60),
        name="combine",
    )(cum_flat, rank, ye, x1, g)


def _moe_stages(aff, h2, x1, wg, wu, wd, g_final, cap):
    b, n_e, s = aff.shape
    rank, cum = _topk(aff, cap)
    cum_flat = cum.reshape(-1)
    xe, gate = _gather(cum_flat, rank, aff, h2, cap)
    ye = _moe_ffn(xe, gate, wg, wu, wd, cap)
    return _combine(cum_flat, rank, ye, x1, g_final)


def kernel(x, norm_mix_g, w_in, b_gate, gmlp_norm_g, w_spatial, b_spatial, w_proj_a, w_proj_b,
           w_out, norm_ffn_g, w_router, w_e_gate, w_e_up, w_e_down, norm_final_g):
    b, s, d = x.shape
    assert w_in.shape[0] == 1, "single-layer block"
    cap = CAPACITY_FACTOR * s // N_EXPERTS
    group_width = GMLP_WIDTH // GMLP_GROUPS
    ws_pairs = w_spatial[0].astype(BF16).reshape(GMLP_GROUPS // 2, 2 * CHUNK, CHUNK)
    bsp = jnp.repeat(b_spatial[0].T, group_width, axis=1)
    qkv, ta, gb = _mix_in(x, norm_mix_g, w_in[0].astype(BF16), b_gate, gmlp_norm_g, ws_pairs, bsp,
                          w_proj_a[0].astype(BF16))
    os_, ls_ = [], []
    for (q, k, v), dil in zip(qkv, DILATIONS):
        o, lse = _attn_pattern(q, k, v, dil)
        os_.append(o)
        ls_.append(lse)
    x1, h2, aff = _mix_out(x, ta, gb, os_, ls_, w_proj_b[0].astype(BF16), w_out[0].astype(BF16),
                           norm_ffn_g, w_router[0].T)
    return _moe_stages(aff, h2, x1, w_e_gate[0], w_e_up[0], w_e_down[0], norm_final_g[None], cap)
```

```python
import functools

import jax
import jax.numpy as jnp
from jax import lax
from jax.experimental import pallas as pl
from jax.experimental.pallas import tpu as pltpu

F32 = jnp.float32
BF16 = jnp.bfloat16
I32 = jnp.int32

EPS = 1e-6
GMLP_WIDTH = 512
GMLP_GROUPS = 8
CHUNK = 128
N_HEADS = 8
HEAD_DIM = 64
ATTN_WIDTH = N_HEADS * HEAD_DIM
DILATIONS = (1, 4, 16)
DILATION_STEP = 4
assert all(b == a * DILATION_STEP for a, b in zip(DILATIONS, DILATIONS[1:]))
HALF_WINDOW = 64
N_EXPERTS = 16
CAPACITY_FACTOR = 2

LANES = 128
Q_TILE = 128
KEY_TILE = 2 * Q_TILE
ATTN_STEP_ROWS = 1024
STAT_LANES = LANES // N_HEADS
DEN_SHIFT = STAT_LANES // 2
SLOT_TILE = 128
SLOT_ALIGN = 16
SLOT_WINDOW = SLOT_TILE + SLOT_ALIGN
GATHER_GROUPS = 2
GATHER_TOKENS = 256
GATHER_FAST = 80
GATHER_SLOW = GATHER_TOKENS + SLOT_ALIGN
FAST_WINDOW = 48
ROW_TILE = 512
SUB_ROWS = 256
MIB = 1024 * 1024


def _cparams(sem, vmem_mib):
    return pltpu.CompilerParams(dimension_semantics=sem, vmem_limit_bytes=vmem_mib * MIB)


def _gelu_tanh(x):
    return 0.5 * x * (1.0 + jnp.tanh(0.7978845608028654 * (x + 0.044715 * (x * x * x))))


def _sigmoid(x):
    return 1.0 / (1.0 + jnp.exp(-x))


def _rms(x, g):
    return x * lax.rsqrt(jnp.mean(x * x, axis=-1, keepdims=True) + EPS) * g


def _dot(a, b):
    return jnp.dot(a, b, preferred_element_type=F32)


def _dot_nt(a, b):
    return lax.dot_general(a, b, (((1,), (1,)), ((), ())), preferred_element_type=F32)


def _dot_tn(a, b):
    return lax.dot_general(a, b, (((0,), (0,)), ((), ())), preferred_element_type=F32)


def _mix_in_kernel(x_ref, g_ref, win_ref, bg_ref, g2_ref, ws_ref, bsp_ref, pa_ref, *refs):
    n_qkv = 3 * len(DILATIONS)
    qkv_refs = refs[:n_qkv]
    ta_ref, gb_ref = refs[n_qkv:n_qkv + 2]
    stage_refs = refs[n_qkv + 2:]
    for sub in range(x_ref.shape[0] // SUB_ROWS):
        mine = stage_refs[6 * sub:6 * sub + 6]
        _mix_in_rows(sub, x_ref, g_ref, win_ref, bg_ref, g2_ref, ws_ref, bsp_ref, pa_ref,
                     qkv_refs, ta_ref, gb_ref, list(zip(mine[0::2], mine[1::2])))


def _mix_in_rows(sub, x_ref, g_ref, win_ref, bg_ref, g2_ref, ws_ref, bsp_ref, pa_ref,
                 qkv_refs, ta_ref, gb_ref, stage_refs):
    rows, d_model = SUB_ROWS, x_ref.shape[1]
    rs_tile = slice(sub * rows, (sub + 1) * rows)
    h = _rms(x_ref[rs_tile, :], g_ref[...]).astype(BF16)

    def proj(lo, width):
        return _dot(h, win_ref[:, lo:lo + width])

    c0 = 0
    u = _gelu_tanh(proj(c0, GMLP_WIDTH)); c0 += GMLP_WIDTH
    v = _gelu_tanh(proj(c0, GMLP_WIDTH)); c0 += GMLP_WIDTH
    for i in range(3):
        val = proj(c0, ATTN_WIDTH); c0 += ATTN_WIDTH
        if i == 0:
            val = val * (HEAD_DIM ** -0.5)
        qkv_refs[i][0, rs_tile, :] = val.astype(BF16)
        stage1, stage2 = stage_refs[i]
        n4, n16 = rows // DILATIONS[1], rows // DILATIONS[2]
        out4, out16 = qkv_refs[3 + i], qkv_refs[6 + i]
        for p in range(ATTN_WIDTH // LANES):
            cs = slice(p * LANES, (p + 1) * LANES)
            stage1[p] = val[:, cs]
            for r4 in range(DILATION_STEP):
                part = stage1[p, pl.ds(r4, n4, stride=DILATION_STEP), :]
                out4[r4, sub * n4:(sub + 1) * n4, cs] = part.astype(BF16)
                stage2[p, r4] = part
                for c in range(DILATION_STEP):
                    out16[r4 + DILATION_STEP * c, sub * n16:(sub + 1) * n16, cs] = (
                        stage2[p, r4, pl.ds(c, n16, stride=DILATION_STEP), :].astype(BF16))
    ga = _sigmoid(proj(c0, d_model) + bg_ref[:, :d_model]); c0 += d_model
    gb = _sigmoid(proj(c0, d_model) + bg_ref[:, d_model:])
    gb_ref[rs_tile, :] = gb.astype(BF16)

    vn = _rms(v, g2_ref[...]).astype(BF16)
    lane_lo = lax.broadcasted_iota(I32, (CHUNK, LANES), 1) < HEAD_DIM
    bsp = bsp_ref[...]
    n_chunk = rows // CHUNK
    mixed_slabs = []
    for p in range(GMLP_WIDTH // LANES):
        slab = jnp.concatenate(
            [vn[c * CHUNK:(c + 1) * CHUNK, p * LANES:(p + 1) * LANES] for c in range(n_chunk)],
            axis=1)
        r = _dot(ws_ref[p], slab)
        mixed_slabs.append([jnp.where(lane_lo, r[:CHUNK, c * LANES:(c + 1) * LANES],
                                      r[CHUNK:, c * LANES:(c + 1) * LANES])
                            for c in range(n_chunk)])
    a_chunks = []
    for c in range(n_chunk):
        rs = slice(c * CHUNK, (c + 1) * CHUNK)
        mixed = jnp.concatenate([slabs[c] for slabs in mixed_slabs], axis=1) + bsp
        a_chunks.append((u[rs] * mixed).astype(BF16))
    a = jnp.concatenate(a_chunks, axis=0)
    ta_ref[rs_tile, :] = (ga * _dot(a, pa_ref[...])).astype(BF16)


def _mix_in(x, g, w_in, b_gate, g2, ws_pairs, bsp, w_pa):
    b, s, d = x.shape
    const = lambda shape: pl.BlockSpec(shape, lambda bi, t: (0,) * len(shape))
    row = lambda w: pl.BlockSpec((None, ROW_TILE, w), lambda bi, t: (bi, t, 0))
    qkv_specs, qkv_shapes = [], []
    for dil in DILATIONS:
        spec = pl.BlockSpec((None, dil, ROW_TILE // dil, ATTN_WIDTH), lambda bi, t: (bi, 0, t, 0))
        qkv_specs += [spec] * 3
        qkv_shapes += [jax.ShapeDtypeStruct((b, dil, s // dil, ATTN_WIDTH), BF16)] * 3
    outs = pl.pallas_call(
        _mix_in_kernel,
        grid=(b, s // ROW_TILE),
        in_specs=[row(d), const(g.shape), const(w_in.shape), const(b_gate.shape), const(g2.shape),
                  const(ws_pairs.shape), const(bsp.shape), const(w_pa.shape)],
        out_specs=qkv_specs + [row(d), row(d)],
        out_shape=qkv_shapes + [jax.ShapeDtypeStruct((b, s, d), BF16)] * 2,
        scratch_shapes=[pltpu.VMEM((ATTN_WIDTH // LANES, SUB_ROWS, LANES), F32),
                        pltpu.VMEM((ATTN_WIDTH // LANES, DILATION_STEP, SUB_ROWS // DILATION_STEP,
                                    LANES), F32)] * (3 * (ROW_TILE // SUB_ROWS)),
        compiler_params=_cparams(("parallel", "parallel"), 56),
        name="mix_in",
    )(x, g, w_in, b_gate, g2, ws_pairs, bsp, w_pa)
    n_qkv = 3 * len(DILATIONS)
    qkv = [outs[3 * i:3 * i + 3] for i in range(len(DILATIONS))]
    return qkv, outs[n_qkv], outs[n_qkv + 1]


def _attn_kernel(q_ref, k_ref, v_ref, o_ref, l_ref, bias_ref, *, dil):
    n_res, rows, _ = q_ref.shape
    seq = k_ref.shape[1]
    t = pl.program_id(2)
    first = jnp.logical_and(jnp.logical_and(pl.program_id(0) == 0, pl.program_id(1) == 0), t == 0)

    @pl.when(first)
    def _():
        ii = lax.broadcasted_iota(I32, (Q_TILE, KEY_TILE), 0)
        jj = lax.broadcasted_iota(I32, (Q_TILE, KEY_TILE), 1)
        for var in range(3):
            absd = jnp.abs(jj - ii - var * HALF_WINDOW)
            valid = absd <= HALF_WINDOW
            absf = absd.astype(F32)
            for h in range(N_HEADS):
                slope = 2.0 ** (-8.0 * (h + 1) / N_HEADS)
                bias_ref[var, h] = jnp.where(valid, -(slope * dil) * absf, -jnp.inf)

    lane = lax.broadcasted_iota(I32, (Q_TILE, LANES), 1)
    lane_lo = lane < HEAD_DIM
    mask_lo = jnp.where(lane_lo, 1.0, 0.0).astype(BF16)
    mask_hi = jnp.where(lane_lo, 0.0, 1.0).astype(BF16)
    for rr in range(n_res):
        for qi in range(rows // Q_TILE):
            rs = slice(qi * Q_TILE, (qi + 1) * Q_TILE)
            i0 = t * rows + qi * Q_TILE
            start = pl.multiple_of(jnp.clip(i0 - HALF_WINDOW, 0, seq - KEY_TILE), HALF_WINDOW)
            var = (i0 - start) // HALF_WINDOW
            for p in range(ATTN_WIDTH // LANES):
                cs = slice(p * LANES, (p + 1) * LANES)
                qp = q_ref[rr, rs, cs]
                kp = k_ref[rr, pl.ds(start, KEY_TILE), cs]
                vp = v_ref[rr, pl.ds(start, KEY_TILE), cs]
                q2 = jnp.concatenate([qp * mask_lo, qp * mask_hi], axis=0)
                s2 = _dot_nt(q2, kp)
                probs = []
                for hh in range(2):
                    h = 2 * p + hh
                    s = s2[hh * Q_TILE:(hh + 1) * Q_TILE] + bias_ref[var, h]
                    m = jnp.max(s, axis=-1, keepdims=True)
                    e = jnp.exp(s - m)
                    den = jnp.sum(e, axis=-1, keepdims=True)
                    probs.append(e)
                    lo = h * STAT_LANES
                    l_ref[rr, rs, lo:lo + DEN_SHIFT] = jnp.broadcast_to(m, (Q_TILE, DEN_SHIFT))
                    l_ref[rr, rs, lo + DEN_SHIFT:lo + STAT_LANES] = jnp.broadcast_to(
                        den, (Q_TILE, DEN_SHIFT))
                o2 = _dot(jnp.concatenate(probs, axis=0).astype(BF16), vp)
                o_ref[rr, rs, cs] = jnp.where(lane_lo, o2[:Q_TILE], o2[Q_TILE:]).astype(BF16)


def _attn_pattern(q, k, v, dil):
    b, _, seq, w = q.shape
    rows = min(seq, ATTN_STEP_ROWS)
    n_res = ATTN_STEP_ROWS // rows
    qspec = lambda width: pl.BlockSpec((None, n_res, rows, width), lambda bi, r, t: (bi, r, t, 0))
    kspec = pl.BlockSpec((None, n_res, seq, w), lambda bi, r, t: (bi, r, 0, 0))
    return pl.pallas_call(
        functools.partial(_attn_kernel, dil=dil),
        grid=(b, dil // n_res, seq // rows),
        in_specs=[qspec(w), kspec, kspec],
        out_specs=[qspec(w), qspec(LANES)],
        out_shape=[jax.ShapeDtypeStruct(q.shape, BF16),
                   jax.ShapeDtypeStruct((b, dil, seq, LANES), F32)],
        scratch_shapes=[pltpu.VMEM((3, N_HEADS, Q_TILE, KEY_TILE), F32)],
        compiler_params=_cparams(("arbitrary", "arbitrary", "arbitrary"), 48),
        name=f"attn_d{dil}",
    )(q, k, v)


def _to_natural(src_ref, nat_ref, tmp_ref, rows):
    n_slab = nat_ref.shape[0]
    step = DILATION_STEP
    for p in range(n_slab):
        cs = slice(p * LANES, (p + 1) * LANES)
        for r4 in range(step):
            if tmp_ref is None:
                quarter = src_ref[r4][:, cs].astype(F32)
            else:
                for c in range(step):
                    tmp_ref[p, r4, pl.ds(c, rows // (step * step), stride=step), :] = (
                        src_ref[r4 + step * c][:, cs].astype(F32))
                quarter = tmp_ref[p, r4]
            nat_ref[p, pl.ds(r4, rows // step, stride=step), :] = quarter
    return jnp.concatenate([nat_ref[p] for p in range(n_slab)], axis=1)


def _mix_out_kernel(x_ref, ta_ref, gb_ref, *refs):
    n_pat = len(DILATIONS)
    o_refs = refs[:n_pat]
    l_refs = refs[n_pat:2 * n_pat]
    pb_ref, wo_ref, g_ref, wr_ref, x1_ref, h2_ref, aff_ref = refs[2 * n_pat:2 * n_pat + 7]
    stage_refs = refs[2 * n_pat + 7:]
    rows = x_ref.shape[0]

    stage_refs = list(stage_refs)
    outs = [o_refs[0][0].astype(F32)]
    lses = [l_refs[0][0]]
    for di in (1, 2):
        for src, dest in ((o_refs[di], outs), (l_refs[di], lses)):
            nat = stage_refs.pop(0)
            tmp = stage_refs.pop(0) if di == 2 else None
            dest.append(_to_natural(src, nat, tmp, rows))

    dens = [pltpu.roll(st, LANES - DEN_SHIFT, 1) for st in lses]
    lses = [st + jnp.log(den) for st, den in zip(lses, dens)]
    m = functools.reduce(jnp.maximum, lses)
    ws = [jnp.exp(l - m) for l in lses]
    inv = 1.0 / functools.reduce(lambda a, c: a + c, ws)
    lane = lax.broadcasted_iota(I32, (rows, LANES), 1)
    used = lane % STAT_LANES < DEN_SHIFT
    ws = [jnp.where(used, w * inv / den, 0.0) for w, den in zip(ws, dens)]
    k_i = lax.broadcasted_iota(I32, (LANES, ATTN_WIDTH), 0)
    c_i = lax.broadcasted_iota(I32, (LANES, ATTN_WIDTH), 1)
    spread = jnp.where(k_i == (c_i // HEAD_DIM) * STAT_LANES, 1.0, 0.0).astype(BF16)
    o = None
    for w, o_p in zip(ws, outs):
        w_hi = w.astype(BF16)
        w_lo = (w - w_hi.astype(F32)).astype(BF16)
        term = (_dot(w_hi, spread) + _dot(w_lo, spread)) * o_p
        o = term if o is None else o + term

    ob = _dot(o.astype(BF16), pb_ref[...])
    merged = (ta_ref[...].astype(F32) + gb_ref[...].astype(F32) * ob).astype(BF16)
    x1 = x_ref[...] + _dot(merged, wo_ref[...])
    x1_ref[...] = x1
    h2 = _rms(x1, g_ref[...])
    h2_ref[...] = h2.astype(BF16)
    h_hi = h2.astype(BF16)
    h_lo = (h2 - h_hi.astype(F32)).astype(BF16)
    wr = wr_ref[...]
    w_hi = wr.astype(BF16)
    w_lo = (wr - w_hi.astype(F32)).astype(BF16)
    n_e = wr.shape[0]
    by_hi = _dot_nt(jnp.concatenate([w_hi, w_lo], axis=0), h_hi)
    logits = by_hi[:n_e] + (_dot_nt(w_hi, h_lo) + by_hi[n_e:])
    e = jnp.exp(logits - jnp.max(logits, axis=0, keepdims=True))
    aff_ref[...] = e / jnp.sum(e, axis=0, keepdims=True)


def _mix_out(x, ta, gb, os_, ls_, w_pb, w_out, g, w_router_t):
    b, s, d = x.shape
    n_e = w_router_t.shape[0]
    const = lambda shape: pl.BlockSpec(shape, lambda bi, t: (0,) * len(shape))
    row = lambda w: pl.BlockSpec((None, ROW_TILE, w), lambda bi, t: (bi, t, 0))
    res = lambda dil, w: pl.BlockSpec((None, dil, ROW_TILE // dil, w), lambda bi, t: (bi, 0, t, 0))
    stage = []
    for di in (1, 2):
        for slabs in (ATTN_WIDTH // LANES, 1):
            stage.append(pltpu.VMEM((slabs, ROW_TILE, LANES), F32))
            if di == 2:
                stage.append(pltpu.VMEM((slabs, DILATION_STEP, ROW_TILE // DILATION_STEP, LANES),
                                        F32))
    return pl.pallas_call(
        _mix_out_kernel,
        grid=(b, s // ROW_TILE),
        in_specs=[row(d), row(d), row(d)]
                 + [res(dil, ATTN_WIDTH) for dil in DILATIONS]
                 + [res(dil, LANES) for dil in DILATIONS]
                 + [const(w_pb.shape), const(w_out.shape), const(g.shape), const(w_router_t.shape)],
        out_specs=[row(d), row(d), pl.BlockSpec((None, n_e, ROW_TILE), lambda bi, t: (bi, 0, t))],
        out_shape=[jax.ShapeDtypeStruct((b, s, d), F32), jax.ShapeDtypeStruct((b, s, d), BF16),
                   jax.ShapeDtypeStruct((b, n_e, s), F32)],
        scratch_shapes=stage,
        compiler_params=_cparams(("parallel", "parallel"), 48),
        name="mix_out",
    )(x, ta, gb, *os_, *ls_, w_pb, w_out, g, w_router_t)


def _topk_kernel(aff_ref, rank_ref, cum_ref, *, cap):
    n_e, s = aff_ref.shape
    n_blk = s // SLOT_TILE
    aff = aff_ref[...]
    thr = jnp.zeros((n_e, 1), I32)
    for bit in range(30, -1, -1):
        cand = thr | (1 << bit)
        cnt = jnp.sum((aff >= pltpu.bitcast(cand, F32)).astype(I32), axis=1, keepdims=True)
        thr = jnp.where(cnt >= cap, cand, thr)
    above = aff >= pltpu.bitcast(thr + 1, F32)
    tie = jnp.logical_and(aff >= pltpu.bitcast(thr, F32), jnp.logical_not(above))
    need = (cap - jnp.sum(above.astype(I32), axis=1, keepdims=True)).astype(F32)
    r_i = lax.broadcasted_iota(I32, (SLOT_TILE, SLOT_TILE), 0)
    c_i = lax.broadcasted_iota(I32, (SLOT_TILE, SLOT_TILE), 1)
    tri = jnp.where(r_i < c_i, 1.0, 0.0).astype(BF16)
    lane = lax.broadcasted_iota(I32, (n_e, LANES), 1)
    run_tie = jnp.zeros((n_e, 1), F32)
    run_sel = jnp.zeros((n_e, 1), F32)
    cum = jnp.zeros((n_e, LANES), F32)
    for j in range(n_blk):
        cs = slice(j * SLOT_TILE, (j + 1) * SLOT_TILE)
        tie_f = jnp.where(tie[:, cs], 1.0, 0.0)
        tie_rank = _dot(tie_f.astype(BF16), tri) + run_tie
        run_tie = run_tie + jnp.sum(tie_f, axis=1, keepdims=True)
        sel_f = jnp.where(above[:, cs], 1.0, jnp.where(tie_rank < need, tie_f, 0.0))
        rank = _dot(sel_f.astype(BF16), tri) + run_sel
        rank_ref[:, cs] = jnp.where(sel_f > 0.0, rank, -1.0).astype(I32)
        cum = jnp.where(lane == j, run_sel, cum)
        run_sel = run_sel + jnp.sum(sel_f, axis=1, keepdims=True)
    cum = jnp.where(lane == n_blk, run_sel, cum)
    cum_ref[...] = cum.astype(I32)


def _topk(aff, cap):
    b, n_e, s = aff.shape
    rows = b * n_e
    rank, cum = pl.pallas_call(
        functools.partial(_topk_kernel, cap=cap),
        grid=(1,),
        in_specs=[pl.BlockSpec((rows, s), lambda i: (0, 0))],
        out_specs=[pl.BlockSpec((rows, s), lambda i: (0, 0)),
                   pl.BlockSpec((rows, LANES), lambda i: (0, 0))],
        out_shape=[jax.ShapeDtypeStruct((rows, s), I32),
                   jax.ShapeDtypeStruct((rows, LANES), I32)],
        compiler_params=_cparams(("arbitrary",), 32),
        name="topk",
    )(aff.reshape(rows, s))
    return rank.reshape(b, n_e, s), cum.reshape(b, n_e, LANES)


def _slot_window(first):
    return pl.multiple_of((first // SLOT_ALIGN) * SLOT_ALIGN, SLOT_ALIGN)


def _block_windows(cum_ref, first_expert, n_e, j, tiles=1, fast=FAST_WINDOW):
    los, fits = [], None
    for e in range(n_e):
        at = (first_expert + e) * LANES + j * tiles
        lo = _slot_window(cum_ref[at])
        ok = cum_ref[at + tiles] - lo <= fast
        los.append(lo)
        fits = ok if fits is None else jnp.logical_and(fits, ok)
    return los, fits


def _run_blocks(fits, fast, slow):
    all_fit = functools.reduce(jnp.logical_and, fits)

    @pl.when(all_fit)
    def _():
        for jj in range(len(fits)):
            fast(jj)

    @pl.when(jnp.logical_not(all_fit))
    def _():
        for jj, fit in enumerate(fits):
            pl.when(fit)(functools.partial(fast, jj))
            pl.when(jnp.logical_not(fit))(functools.partial(slow, jj))


def _gather_kernel(cum_ref, rank_ref, aff_ref, h2_ref, xe_ref, gate_ref):
    n_e = rank_ref.shape[0]
    rows = h2_ref.shape[0]
    first_expert = (pl.program_id(0) * pl.num_programs(1) + pl.program_id(1)) * n_e
    t = pl.program_id(2)

    @pl.when(t == 0)
    def _():
        xe_ref[...] = jnp.zeros_like(xe_ref)
        gate_ref[...] = jnp.zeros_like(gate_ref)

    row_fast = lax.broadcasted_iota(I32, (GATHER_FAST, GATHER_TOKENS), 0)
    row_slow = lax.broadcasted_iota(I32, (GATHER_SLOW, GATHER_TOKENS), 0)
    n_blk = rows // GATHER_TOKENS
    windows = [_block_windows(cum_ref, first_expert, n_e, t * n_blk + jj,
                              tiles=GATHER_TOKENS // SLOT_TILE, fast=GATHER_FAST)
               for jj in range(n_blk)]

    def hit(jj, e, row_i):
        toks = slice(jj * GATHER_TOKENS, (jj + 1) * GATHER_TOKENS)
        return (row_i + windows[jj][0][e]) == rank_ref[e:e + 1, toks]

    def add_window(jj, e, width, hit_e, rows_e):
        toks = slice(jj * GATHER_TOKENS, (jj + 1) * GATHER_TOKENS)
        win = pl.ds(windows[jj][0][e], width)
        xe_ref[e, win, :] += rows_e.astype(BF16)
        gate_ref[e, win, :] += jnp.sum(jnp.where(hit_e, aff_ref[e:e + 1, toks], 0.0), axis=1,
                                       keepdims=True)

    def fast(jj):
        hs = [hit(jj, e, row_fast) for e in range(n_e)]
        stack = jnp.concatenate([jnp.where(h, 1.0, 0.0).astype(BF16) for h in hs], axis=0)
        res = _dot(stack, h2_ref[jj * GATHER_TOKENS:(jj + 1) * GATHER_TOKENS, :])
        for e in range(n_e):
            add_window(jj, e, GATHER_FAST, hs[e], res[e * GATHER_FAST:(e + 1) * GATHER_FAST])

    def slow(jj):
        for e in range(n_e):
            h = hit(jj, e, row_slow)
            add_window(jj, e, GATHER_SLOW, h,
                       _dot(jnp.where(h, 1.0, 0.0).astype(BF16),
                            h2_ref[jj * GATHER_TOKENS:(jj + 1) * GATHER_TOKENS, :]))

    _run_blocks([fits for _, fits in windows], fast, slow)


def _gather(cum_flat, rank, aff, h2, cap):
    b, s, d = h2.shape
    n_e = rank.shape[1]
    rows = cap + GATHER_SLOW
    grp = n_e // GATHER_GROUPS
    per_tok = pl.BlockSpec((None, grp, ROW_TILE), lambda bi, eg, t, c: (bi, eg, t))
    whole = lambda w: pl.BlockSpec((None, grp, rows, w), lambda bi, eg, t, c: (bi, eg, 0, 0))
    grid_spec = pltpu.PrefetchScalarGridSpec(
        num_scalar_prefetch=1,
        grid=(b, GATHER_GROUPS, s // ROW_TILE),
        in_specs=[per_tok, per_tok,
                  pl.BlockSpec((None, ROW_TILE, d), lambda bi, eg, t, c: (bi, t, 0))],
        out_specs=[whole(d), whole(LANES)],
    )
    return pl.pallas_call(
        _gather_kernel,
        grid_spec=grid_spec,
        out_shape=[jax.ShapeDtypeStruct((b, n_e, rows, d), BF16),
                   jax.ShapeDtypeStruct((b, n_e, rows, LANES), F32)],
        compiler_params=_cparams(("arbitrary", "arbitrary", "arbitrary"), 56),
        name="gather",
    )(cum_flat, rank, aff, h2)


def _moe_ffn_kernel(xe_ref, gate_ref, wg_ref, wu_ref, wd_ref, ye_ref, wg_bf, wu_bf, wd_bf):
    cap = xe_ref.shape[0]

    @pl.when(pl.program_id(1) == 0)
    def _():
        wg_bf[...] = wg_ref[...].astype(BF16)
        wu_bf[...] = wu_ref[...].astype(BF16)
        wd_bf[...] = wd_ref[...].astype(BF16)

    xe = xe_ref[...]
    gate_h = _dot(xe, wg_bf[...])
    up_h = _dot(xe, wu_bf[...])
    hidden = (gate_h * _sigmoid(gate_h) * up_h).astype(BF16)
    ye = _dot(hidden, wd_bf[...]) * gate_ref[:, 0:1]
    ye_ref[:cap, :] = ye.astype(BF16)
    ye_ref[cap:, :] = jnp.zeros((ye_ref.shape[0] - cap, ye_ref.shape[1]), BF16)


def _moe_ffn(xe, gate, wg, wu, wd, cap):
    b, n_e, _, d = xe.shape
    hid = wg.shape[2]
    rows = cap + SLOT_WINDOW
    return pl.pallas_call(
        _moe_ffn_kernel,
        grid=(n_e, b),
        in_specs=[
            pl.BlockSpec((None, None, cap, d), lambda e, bi: (bi, e, 0, 0)),
            pl.BlockSpec((None, None, cap, LANES), lambda e, bi: (bi, e, 0, 0)),
            pl.BlockSpec((None, d, hid), lambda e, bi: (e, 0, 0)),
            pl.BlockSpec((None, d, hid), lambda e, bi: (e, 0, 0)),
            pl.BlockSpec((None, hid, d), lambda e, bi: (e, 0, 0)),
        ],
        out_specs=pl.BlockSpec((None, None, rows, d), lambda e, bi: (bi, e, 0, 0)),
        out_shape=jax.ShapeDtypeStruct((b, n_e, rows, d), BF16),
        scratch_shapes=[pltpu.VMEM((d, hid), BF16), pltpu.VMEM((d, hid), BF16),
                        pltpu.VMEM((hid, d), BF16)],
        compiler_params=_cparams(("arbitrary", "arbitrary"), 56),
        name="moe_ffn",
    )(xe, gate, wg, wu, wd)


def _combine_kernel(cum_ref, rank_ref, ye_ref, x1_ref, g_ref, y_ref, rhs_ref):
    n_e = rank_ref.shape[0]
    n_blk = x1_ref.shape[0] // SLOT_TILE
    first_expert = pl.program_id(0) * n_e
    t = pl.program_id(1)
    row_fast = lax.broadcasted_iota(I32, (FAST_WINDOW, SLOT_TILE), 0)
    row_slow = lax.broadcasted_iota(I32, (SLOT_WINDOW, SLOT_TILE), 0)
    windows = [_block_windows(cum_ref, first_expert, n_e, t * n_blk + jj) for jj in range(n_blk)]

    def hits(jj, e, row_i):
        toks = slice(jj * SLOT_TILE, (jj + 1) * SLOT_TILE)
        hit = (row_i + windows[jj][0][e]) == rank_ref[e:e + 1, toks]
        return jnp.where(hit, 1.0, 0.0).astype(BF16)

    def finish(jj, moe):
        toks = slice(jj * SLOT_TILE, (jj + 1) * SLOT_TILE)
        y_ref[toks, :] = _rms(x1_ref[toks, :] + moe, g_ref[...])

    def fast(jj):
        rhs = rhs_ref.at[jj % rhs_ref.shape[0]]
        for e in range(n_e):
            rhs[e * FAST_WINDOW:(e + 1) * FAST_WINDOW, :] = (
                ye_ref[e, pl.ds(windows[jj][0][e], FAST_WINDOW), :])
        stack = jnp.concatenate([hits(jj, e, row_fast) for e in range(n_e)], axis=0)
        finish(jj, _dot_tn(stack, rhs[...]))

    def slow(jj):
        acc = None
        for e in range(n_e):
            part = _dot_tn(hits(jj, e, row_slow),
                           ye_ref[e, pl.ds(windows[jj][0][e], SLOT_WINDOW), :])
            acc = part if acc is None else acc + part
        finish(jj, acc)

    _run_blocks([fits for _, fits in windows], fast, slow)


def _combine(cum_flat, rank, ye, x1, g):
    b, s, d = x1.shape
    n_e, ye_rows = ye.shape[1], ye.shape[2]
    row = pl.BlockSpec((None, ROW_TILE, d), lambda bi, t, c: (bi, t, 0))
    grid_spec = pltpu.PrefetchScalarGridSpec(
        num_scalar_prefetch=1,
        grid=(b, s // ROW_TILE),
        in_specs=[
            pl.BlockSpec((None, n_e, ROW_TILE), lambda bi, t, c: (bi, 0, t)),
            pl.BlockSpec((None, n_e, ye_rows, d), lambda bi, t, c: (bi, 0, 0, 0)),
            row,
            pl.BlockSpec(g.shape, lambda bi, t, c: (0, 0)),
        ],
        out_specs=row,
        scratch_shapes=[pltpu.VMEM((2, n_e * FAST_WINDOW, d), BF16)],
    )
    return pl.pallas_call(
        _combine_kernel,
        grid_spec=grid_spec,
        out_shape=jax.ShapeDtypeStruct((b, s, d), F32),
        compiler_params=_cparams(("arbitrary", "arbitrary"), 60),
        name="combine",
    )(cum_flat, rank, ye, x1, g)


def _moe_stages(aff, h2, x1, wg, wu, wd, g_final, cap):
    b, n_e, s = aff.shape
    rank, cum = _topk(aff, cap)
    cum_flat = cum.reshape(-1)
    xe, gate = _gather(cum_flat, rank, aff, h2, cap)
    ye = _moe_ffn(xe, gate, wg, wu, wd, cap)
    return _combine(cum_flat, rank, ye, x1, g_final)


def kernel(x, norm_mix_g, w_in, b_gate, gmlp_norm_g, w_spatial, b_spatial, w_proj_a, w_proj_b,
           w_out, norm_ffn_g, w_router, w_e_gate, w_e_up, w_e_down, norm_final_g):
    b, s, d = x.shape
    assert w_in.shape[0] == 1, "single-layer block"
    cap = CAPACITY_FACTOR * s // N_EXPERTS
    group_width = GMLP_WIDTH // GMLP_GROUPS
    ws_pairs = w_spatial[0].astype(BF16).reshape(GMLP_GROUPS // 2, 2 * CHUNK, CHUNK)
    bsp = jnp.repeat(b_spatial[0].T, group_width, axis=1)
    qkv, ta, gb = _mix_in(x, norm_mix_g, w_in[0].astype(BF16), b_gate, gmlp_norm_g, ws_pairs, bsp,
                          w_proj_a[0].astype(BF16))
    os_, ls_ = [], []
    for (q, k, v), dil in zip(qkv, DILATIONS):
        o, lse = _attn_pattern(q, k, v, dil)
        os_.append(o)
        ls_.append(lse)
    x1, h2, aff = _mix_out(x, ta, gb, os_, ls_, w_proj_b[0].astype(BF16), w_out[0].astype(BF16),
                           norm_ffn_g, w_router[0].T)
    return _moe_stages(aff, h2, x1, w_e_gate[0], w_e_up[0], w_e_down[0], norm_final_g[None], cap)
```

```python
import functools

import jax
import jax.numpy as jnp
from jax import lax
from jax.experimental import pallas as pl
from jax.experimental.pallas import tpu as pltpu

F32 = jnp.float32
BF16 = jnp.bfloat16
I32 = jnp.int32

EPS = 1e-6
GMLP_WIDTH = 512
GMLP_GROUPS = 8
CHUNK = 128
N_HEADS = 8
HEAD_DIM = 64
ATTN_WIDTH = N_HEADS * HEAD_DIM
DILATIONS = (1, 4, 16)
DILATION_STEP = 4
assert all(b == a * DILATION_STEP for a, b in zip(DILATIONS, DILATIONS[1:]))
HALF_WINDOW = 64
N_EXPERTS = 16
CAPACITY_FACTOR = 2

LANES = 128
Q_TILE = 128
KEY_TILE = 2 * Q_TILE
ATTN_STEP_ROWS = 1024
STAT_LANES = LANES // N_HEADS
DEN_SHIFT = STAT_LANES // 2
SLOT_TILE = 128
SLOT_ALIGN = 16
SLOT_WINDOW = SLOT_TILE + SLOT_ALIGN
GATHER_GROUPS = 2
FFN_SEQS = 2
GATHER_STEP_TOKENS = 1024
GATHER_TOKENS = 256
GATHER_FAST = 80
GATHER_SLOW = GATHER_TOKENS + SLOT_ALIGN
FAST_WINDOW = 48
ROW_TILE = 512
SUB_ROWS = 512
MIB = 1024 * 1024


def _cparams(sem, vmem_mib):
    return pltpu.CompilerParams(dimension_semantics=sem, vmem_limit_bytes=vmem_mib * MIB)


def _gelu_tanh(x):
    return 0.5 * x * (1.0 + jnp.tanh(0.7978845608028654 * (x + 0.044715 * (x * x * x))))


def _sigmoid(x):
    return 1.0 / (1.0 + jnp.exp(-x))


def _rms(x, g):
    return x * lax.rsqrt(jnp.mean(x * x, axis=-1, keepdims=True) + EPS) * g


def _dot(a, b):
    return jnp.dot(a, b, preferred_element_type=F32)


def _dot_nt(a, b):
    return lax.dot_general(a, b, (((1,), (1,)), ((), ())), preferred_element_type=F32)


def _dot_tn(a, b):
    return lax.dot_general(a, b, (((0,), (0,)), ((), ())), preferred_element_type=F32)


def _mix_in_kernel(x_ref, g_ref, win_ref, bg_ref, g2_ref, ws_ref, bsp_ref, pa_ref, *refs):
    n_qkv = 3 * len(DILATIONS)
    qkv_refs = refs[:n_qkv]
    ta_ref, gb_ref = refs[n_qkv:n_qkv + 2]
    stage_refs = refs[n_qkv + 2:]
    for sub in range(x_ref.shape[0] // SUB_ROWS):
        mine = stage_refs[6 * sub:6 * sub + 6]
        _mix_in_rows(sub, x_ref, g_ref, win_ref, bg_ref, g2_ref, ws_ref, bsp_ref, pa_ref,
                     qkv_refs, ta_ref, gb_ref, list(zip(mine[0::2], mine[1::2])))


def _mix_in_rows(sub, x_ref, g_ref, win_ref, bg_ref, g2_ref, ws_ref, bsp_ref, pa_ref,
                 qkv_refs, ta_ref, gb_ref, stage_refs):
    rows, d_model = SUB_ROWS, x_ref.shape[1]
    rs_tile = slice(sub * rows, (sub + 1) * rows)
    h = _rms(x_ref[rs_tile, :], g_ref[...]).astype(BF16)

    def proj(lo, width):
        return _dot(h, win_ref[:, lo:lo + width])

    c0 = 0
    u = _gelu_tanh(proj(c0, GMLP_WIDTH)); c0 += GMLP_WIDTH
    v = _gelu_tanh(proj(c0, GMLP_WIDTH)); c0 += GMLP_WIDTH
    for i in range(3):
        val = proj(c0, ATTN_WIDTH); c0 += ATTN_WIDTH
        if i == 0:
            val = val * (HEAD_DIM ** -0.5)
        qkv_refs[i][0, rs_tile, :] = val.astype(BF16)
        stage1, stage2 = stage_refs[i]
        n4, n16 = rows // DILATIONS[1], rows // DILATIONS[2]
        out4, out16 = qkv_refs[3 + i], qkv_refs[6 + i]
        for p in range(ATTN_WIDTH // LANES):
            cs = slice(p * LANES, (p + 1) * LANES)
            stage1[p] = val[:, cs]
            for r4 in range(DILATION_STEP):
                part = stage1[p, pl.ds(r4, n4, stride=DILATION_STEP), :]
                out4[r4, sub * n4:(sub + 1) * n4, cs] = part.astype(BF16)
                stage2[p, r4] = part
                for c in range(DILATION_STEP):
                    out16[r4 + DILATION_STEP * c, sub * n16:(sub + 1) * n16, cs] = (
                        stage2[p, r4, pl.ds(c, n16, stride=DILATION_STEP), :].astype(BF16))
    ga = _sigmoid(proj(c0, d_model) + bg_ref[:, :d_model]); c0 += d_model
    gb = _sigmoid(proj(c0, d_model) + bg_ref[:, d_model:])
    gb_ref[rs_tile, :] = gb.astype(BF16)

    vn = _rms(v, g2_ref[...]).astype(BF16)
    lane_lo = lax.broadcasted_iota(I32, (CHUNK, LANES), 1) < HEAD_DIM
    bsp = bsp_ref[...]
    n_chunk = rows // CHUNK
    mixed_slabs = []
    for p in range(GMLP_WIDTH // LANES):
        slab = jnp.concatenate(
            [vn[c * CHUNK:(c + 1) * CHUNK, p * LANES:(p + 1) * LANES] for c in range(n_chunk)],
            axis=1)
        r = _dot(ws_ref[p], slab)
        mixed_slabs.append([jnp.where(lane_lo, r[:CHUNK, c * LANES:(c + 1) * LANES],
                                      r[CHUNK:, c * LANES:(c + 1) * LANES])
                            for c in range(n_chunk)])
    a_chunks = []
    for c in range(n_chunk):
        rs = slice(c * CHUNK, (c + 1) * CHUNK)
        mixed = jnp.concatenate([slabs[c] for slabs in mixed_slabs], axis=1) + bsp
        a_chunks.append((u[rs] * mixed).astype(BF16))
    a = jnp.concatenate(a_chunks, axis=0)
    ta_ref[rs_tile, :] = (ga * _dot(a, pa_ref[...])).astype(BF16)


def _mix_in(x, g, w_in, b_gate, g2, ws_pairs, bsp, w_pa):
    b, s, d = x.shape
    const = lambda shape: pl.BlockSpec(shape, lambda bi, t: (0,) * len(shape))
    row = lambda w: pl.BlockSpec((None, ROW_TILE, w), lambda bi, t: (bi, t, 0))
    qkv_specs, qkv_shapes = [], []
    for dil in DILATIONS:
        spec = pl.BlockSpec((None, dil, ROW_TILE // dil, ATTN_WIDTH), lambda bi, t: (bi, 0, t, 0))
        qkv_specs += [spec] * 3
        qkv_shapes += [jax.ShapeDtypeStruct((b, dil, s // dil, ATTN_WIDTH), BF16)] * 3
    outs = pl.pallas_call(
        _mix_in_kernel,
        grid=(b, s // ROW_TILE),
        in_specs=[row(d), const(g.shape), const(w_in.shape), const(b_gate.shape), const(g2.shape),
                  const(ws_pairs.shape), const(bsp.shape), const(w_pa.shape)],
        out_specs=qkv_specs + [row(d), row(d)],
        out_shape=qkv_shapes + [jax.ShapeDtypeStruct((b, s, d), BF16)] * 2,
        scratch_shapes=[pltpu.VMEM((ATTN_WIDTH // LANES, SUB_ROWS, LANES), F32),
                        pltpu.VMEM((ATTN_WIDTH // LANES, DILATION_STEP, SUB_ROWS // DILATION_STEP,
                                    LANES), F32)] * (3 * (ROW_TILE // SUB_ROWS)),
        compiler_params=_cparams(("parallel", "parallel"), 56),
        name="mix_in",
    )(x, g, w_in, b_gate, g2, ws_pairs, bsp, w_pa)
    n_qkv = 3 * len(DILATIONS)
    qkv = [outs[3 * i:3 * i + 3] for i in range(len(DILATIONS))]
    return qkv, outs[n_qkv], outs[n_qkv + 1]


def _attn_kernel(q_ref, k_ref, v_ref, o_ref, l_ref, bias_ref, *, dil):
    n_res, rows, _ = q_ref.shape
    seq = k_ref.shape[1]
    t = pl.program_id(2)
    first = jnp.logical_and(jnp.logical_and(pl.program_id(0) == 0, pl.program_id(1) == 0), t == 0)

    @pl.when(first)
    def _():
        ii = lax.broadcasted_iota(I32, (Q_TILE, KEY_TILE), 0)
        jj = lax.broadcasted_iota(I32, (Q_TILE, KEY_TILE), 1)
        for var in range(3):
            absd = jnp.abs(jj - ii - var * HALF_WINDOW)
            valid = absd <= HALF_WINDOW
            absf = absd.astype(F32)
            for h in range(N_HEADS):
                slope = 2.0 ** (-8.0 * (h + 1) / N_HEADS)
                bias_ref[var, h] = jnp.where(valid, -(slope * dil) * absf, -jnp.inf)

    lane = lax.broadcasted_iota(I32, (Q_TILE, LANES), 1)
    lane_lo = lane < HEAD_DIM
    mask_lo = jnp.where(lane_lo, 1.0, 0.0).astype(BF16)
    mask_hi = jnp.where(lane_lo, 0.0, 1.0).astype(BF16)
    for rr in range(n_res):
        for qi in range(rows // Q_TILE):
            rs = slice(qi * Q_TILE, (qi + 1) * Q_TILE)
            i0 = t * rows + qi * Q_TILE
            start = pl.multiple_of(jnp.clip(i0 - HALF_WINDOW, 0, seq - KEY_TILE), HALF_WINDOW)
            var = (i0 - start) // HALF_WINDOW
            for p in range(ATTN_WIDTH // LANES):
                cs = slice(p * LANES, (p + 1) * LANES)
                qp = q_ref[rr, rs, cs]
                kp = k_ref[rr, pl.ds(start, KEY_TILE), cs]
                vp = v_ref[rr, pl.ds(start, KEY_TILE), cs]
                q2 = jnp.concatenate([qp * mask_lo, qp * mask_hi], axis=0)
                s2 = _dot_nt(q2, kp)
                probs = []
                for hh in range(2):
                    h = 2 * p + hh
                    s = s2[hh * Q_TILE:(hh + 1) * Q_TILE] + bias_ref[var, h]
                    m = jnp.max(s, axis=-1, keepdims=True)
                    e = jnp.exp(s - m)
                    den = jnp.sum(e, axis=-1, keepdims=True)
                    probs.append(e)
                    lo = h * STAT_LANES
                    l_ref[rr, rs, lo:lo + DEN_SHIFT] = jnp.broadcast_to(m, (Q_TILE, DEN_SHIFT))
                    l_ref[rr, rs, lo + DEN_SHIFT:lo + STAT_LANES] = jnp.broadcast_to(
                        den, (Q_TILE, DEN_SHIFT))
                o2 = _dot(jnp.concatenate(probs, axis=0).astype(BF16), vp)
                o_ref[rr, rs, cs] = jnp.where(lane_lo, o2[:Q_TILE], o2[Q_TILE:]).astype(BF16)


def _attn_pattern(q, k, v, dil):
    b, _, seq, w = q.shape
    rows = min(seq, ATTN_STEP_ROWS)
    n_res = ATTN_STEP_ROWS // rows
    qspec = lambda width: pl.BlockSpec((None, n_res, rows, width), lambda bi, r, t: (bi, r, t, 0))
    kspec = pl.BlockSpec((None, n_res, seq, w), lambda bi, r, t: (bi, r, 0, 0))
    return pl.pallas_call(
        functools.partial(_attn_kernel, dil=dil),
        grid=(b, dil // n_res, seq // rows),
        in_specs=[qspec(w), kspec, kspec],
        out_specs=[qspec(w), qspec(LANES)],
        out_shape=[jax.ShapeDtypeStruct(q.shape, BF16),
                   jax.ShapeDtypeStruct((b, dil, seq, LANES), F32)],
        scratch_shapes=[pltpu.VMEM((3, N_HEADS, Q_TILE, KEY_TILE), F32)],
        compiler_params=_cparams(("arbitrary", "arbitrary", "arbitrary"), 48),
        name=f"attn_d{dil}",
    )(q, k, v)


def _to_natural(src_ref, nat_ref, tmp_ref, rows):
    n_slab = nat_ref.shape[0]
    step = DILATION_STEP
    for p in range(n_slab):
        cs = slice(p * LANES, (p + 1) * LANES)
        for r4 in range(step):
            if tmp_ref is None:
                quarter = src_ref[r4][:, cs].astype(F32)
            else:
                for c in range(step):
                    tmp_ref[p, r4, pl.ds(c, rows // (step * step), stride=step), :] = (
                        src_ref[r4 + step * c][:, cs].astype(F32))
                quarter = tmp_ref[p, r4]
            nat_ref[p, pl.ds(r4, rows // step, stride=step), :] = quarter
    return jnp.concatenate([nat_ref[p] for p in range(n_slab)], axis=1)


def _mix_out_kernel(x_ref, ta_ref, gb_ref, *refs):
    n_pat = len(DILATIONS)
    o_refs = refs[:n_pat]
    l_refs = refs[n_pat:2 * n_pat]
    pb_ref, wo_ref, g_ref, wr_ref, x1_ref, h2_ref, aff_ref = refs[2 * n_pat:2 * n_pat + 7]
    stage_refs = refs[2 * n_pat + 7:]
    rows = x_ref.shape[0]

    stage_refs = list(stage_refs)
    outs = [o_refs[0][0].astype(F32)]
    lses = [l_refs[0][0]]
    for di in (1, 2):
        for src, dest in ((o_refs[di], outs), (l_refs[di], lses)):
            nat = stage_refs.pop(0)
            tmp = stage_refs.pop(0) if di == 2 else None
            dest.append(_to_natural(src, nat, tmp, rows))

    dens = [pltpu.roll(st, LANES - DEN_SHIFT, 1) for st in lses]
    lses = [st + jnp.log(den) for st, den in zip(lses, dens)]
    m = functools.reduce(jnp.maximum, lses)
    ws = [jnp.exp(l - m) for l in lses]
    inv = 1.0 / functools.reduce(lambda a, c: a + c, ws)
    lane = lax.broadcasted_iota(I32, (rows, LANES), 1)
    used = lane % STAT_LANES < DEN_SHIFT
    ws = [jnp.where(used, w * inv / den, 0.0) for w, den in zip(ws, dens)]
    k_i = lax.broadcasted_iota(I32, (LANES, ATTN_WIDTH), 0)
    c_i = lax.broadcasted_iota(I32, (LANES, ATTN_WIDTH), 1)
    spread = jnp.where(k_i == (c_i // HEAD_DIM) * STAT_LANES, 1.0, 0.0).astype(BF16)
    o = None
    for w, o_p in zip(ws, outs):
        w_hi = w.astype(BF16)
        w_lo = (w - w_hi.astype(F32)).astype(BF16)
        term = (_dot(w_hi, spread) + _dot(w_lo, spread)) * o_p
        o = term if o is None else o + term

    ob = _dot(o.astype(BF16), pb_ref[...])
    merged = (ta_ref[...].astype(F32) + gb_ref[...].astype(F32) * ob).astype(BF16)
    x1 = x_ref[...] + _dot(merged, wo_ref[...])
    x1_ref[...] = x1
    h2 = _rms(x1, g_ref[...])
    h2_ref[...] = h2.astype(BF16)
    h_hi = h2.astype(BF16)
    h_lo = (h2 - h_hi.astype(F32)).astype(BF16)
    wr = wr_ref[...]
    w_hi = wr.astype(BF16)
    w_lo = (wr - w_hi.astype(F32)).astype(BF16)
    n_e = wr.shape[0]
    by_hi = _dot_nt(jnp.concatenate([w_hi, w_lo], axis=0), h_hi)
    logits = by_hi[:n_e] + (_dot_nt(w_hi, h_lo) + by_hi[n_e:])
    e = jnp.exp(logits - jnp.max(logits, axis=0, keepdims=True))
    aff_ref[...] = e / jnp.sum(e, axis=0, keepdims=True)


def _mix_out(x, ta, gb, os_, ls_, w_pb, w_out, g, w_router_t):
    b, s, d = x.shape
    n_e = w_router_t.shape[0]
    const = lambda shape: pl.BlockSpec(shape, lambda bi, t: (0,) * len(shape))
    row = lambda w: pl.BlockSpec((None, ROW_TILE, w), lambda bi, t: (bi, t, 0))
    res = lambda dil, w: pl.BlockSpec((None, dil, ROW_TILE // dil, w), lambda bi, t: (bi, 0, t, 0))
    stage = []
    for di in (1, 2):
        for slabs in (ATTN_WIDTH // LANES, 1):
            stage.append(pltpu.VMEM((slabs, ROW_TILE, LANES), F32))
            if di == 2:
                stage.append(pltpu.VMEM((slabs, DILATION_STEP, ROW_TILE // DILATION_STEP, LANES),
                                        F32))
    return pl.pallas_call(
        _mix_out_kernel,
        grid=(b, s // ROW_TILE),
        in_specs=[row(d), row(d), row(d)]
                 + [res(dil, ATTN_WIDTH) for dil in DILATIONS]
                 + [res(dil, LANES) for dil in DILATIONS]
                 + [const(w_pb.shape), const(w_out.shape), const(g.shape), const(w_router_t.shape)],
        out_specs=[row(d), row(d), pl.BlockSpec((None, n_e, ROW_TILE), lambda bi, t: (bi, 0, t))],
        out_shape=[jax.ShapeDtypeStruct((b, s, d), F32), jax.ShapeDtypeStruct((b, s, d), BF16),
                   jax.ShapeDtypeStruct((b, n_e, s), F32)],
        scratch_shapes=stage,
        compiler_params=_cparams(("parallel", "parallel"), 48),
        name="mix_out",
    )(x, ta, gb, *os_, *ls_, w_pb, w_out, g, w_router_t)


def _topk_kernel(aff_ref, rank_ref, cum_ref, *, cap):
    n_e, s = aff_ref.shape
    n_blk = s // SLOT_TILE
    aff = aff_ref[...]
    thr = jnp.zeros((n_e, 1), I32)
    for bit in range(30, -1, -1):
        cand = thr | (1 << bit)
        cnt = jnp.sum((aff >= pltpu.bitcast(cand, F32)).astype(I32), axis=1, keepdims=True)
        thr = jnp.where(cnt >= cap, cand, thr)
    above = aff >= pltpu.bitcast(thr + 1, F32)
    tie = jnp.logical_and(aff >= pltpu.bitcast(thr, F32), jnp.logical_not(above))
    need = (cap - jnp.sum(above.astype(I32), axis=1, keepdims=True)).astype(F32)
    r_i = lax.broadcasted_iota(I32, (SLOT_TILE, SLOT_TILE), 0)
    c_i = lax.broadcasted_iota(I32, (SLOT_TILE, SLOT_TILE), 1)
    tri = jnp.where(r_i < c_i, 1.0, 0.0).astype(BF16)
    lane = lax.broadcasted_iota(I32, (n_e, LANES), 1)
    run_tie = jnp.zeros((n_e, 1), F32)
    run_sel = jnp.zeros((n_e, 1), F32)
    cum = jnp.zeros((n_e, LANES), F32)
    for j in range(n_blk):
        cs = slice(j * SLOT_TILE, (j + 1) * SLOT_TILE)
        tie_f = jnp.where(tie[:, cs], 1.0, 0.0)
        tie_rank = _dot(tie_f.astype(BF16), tri) + run_tie
        run_tie = run_tie + jnp.sum(tie_f, axis=1, keepdims=True)
        sel_f = jnp.where(above[:, cs], 1.0, jnp.where(tie_rank < need, tie_f, 0.0))
        rank = _dot(sel_f.astype(BF16), tri) + run_sel
        rank_ref[:, cs] = jnp.where(sel_f > 0.0, rank, -1.0).astype(I32)
        cum = jnp.where(lane == j, run_sel, cum)
        run_sel = run_sel + jnp.sum(sel_f, axis=1, keepdims=True)
    cum = jnp.where(lane == n_blk, run_sel, cum)
    cum_ref[...] = cum.astype(I32)


def _topk(aff, cap):
    b, n_e, s = aff.shape
    rows = b * n_e
    rank, cum = pl.pallas_call(
        functools.partial(_topk_kernel, cap=cap),
        grid=(1,),
        in_specs=[pl.BlockSpec((rows, s), lambda i: (0, 0))],
        out_specs=[pl.BlockSpec((rows, s), lambda i: (0, 0)),
                   pl.BlockSpec((rows, LANES), lambda i: (0, 0))],
        out_shape=[jax.ShapeDtypeStruct((rows, s), I32),
                   jax.ShapeDtypeStruct((rows, LANES), I32)],
        compiler_params=_cparams(("arbitrary",), 32),
        name="topk",
    )(aff.reshape(rows, s))
    return rank.reshape(b, n_e, s), cum.reshape(b, n_e, LANES)


def _slot_window(first):
    return pl.multiple_of((first // SLOT_ALIGN) * SLOT_ALIGN, SLOT_ALIGN)


def _block_windows(cum_ref, first_expert, n_e, j, tiles=1, fast=FAST_WINDOW):
    los, fits = [], None
    for e in range(n_e):
        at = (first_expert + e) * LANES + j * tiles
        lo = _slot_window(cum_ref[at])
        ok = cum_ref[at + tiles] - lo <= fast
        los.append(lo)
        fits = ok if fits is None else jnp.logical_and(fits, ok)
    return los, fits


def _run_blocks(fits, fast, slow):
    all_fit = functools.reduce(jnp.logical_and, fits)

    @pl.when(all_fit)
    def _():
        for jj in range(len(fits)):
            fast(jj)

    @pl.when(jnp.logical_not(all_fit))
    def _():
        for jj, fit in enumerate(fits):
            pl.when(fit)(functools.partial(fast, jj))
            pl.when(jnp.logical_not(fit))(functools.partial(slow, jj))


def _gather_kernel(cum_ref, rank_ref, aff_ref, h2_ref, xe_ref, gate_ref):
    n_e = rank_ref.shape[0]
    rows = h2_ref.shape[0]
    first_expert = (pl.program_id(0) * pl.num_programs(1) + pl.program_id(1)) * n_e
    t = pl.program_id(2)

    @pl.when(t == 0)
    def _():
        xe_ref[...] = jnp.zeros_like(xe_ref)
        gate_ref[...] = jnp.zeros_like(gate_ref)

    row_fast = lax.broadcasted_iota(I32, (GATHER_FAST, GATHER_TOKENS), 0)
    row_slow = lax.broadcasted_iota(I32, (GATHER_SLOW, GATHER_TOKENS), 0)
    n_blk = rows // GATHER_TOKENS
    windows = [_block_windows(cum_ref, first_expert, n_e, t * n_blk + jj,
                              tiles=GATHER_TOKENS // SLOT_TILE, fast=GATHER_FAST)
               for jj in range(n_blk)]

    def hit(jj, e, row_i):
        toks = slice(jj * GATHER_TOKENS, (jj + 1) * GATHER_TOKENS)
        return (row_i + windows[jj][0][e]) == rank_ref[e:e + 1, toks]

    def add_window(jj, e, width, hit_e, rows_e):
        toks = slice(jj * GATHER_TOKENS, (jj + 1) * GATHER_TOKENS)
        win = pl.ds(windows[jj][0][e], width)
        xe_ref[e, win, :] += rows_e.astype(BF16)
        gate_ref[e, win, :] += jnp.sum(jnp.where(hit_e, aff_ref[e:e + 1, toks], 0.0), axis=1,
                                       keepdims=True)

    def fast(jj):
        hs = [hit(jj, e, row_fast) for e in range(n_e)]
        stack = jnp.concatenate([jnp.where(h, 1.0, 0.0).astype(BF16) for h in hs], axis=0)
        res = _dot(stack, h2_ref[jj * GATHER_TOKENS:(jj + 1) * GATHER_TOKENS, :])
        for e in range(n_e):
            add_window(jj, e, GATHER_FAST, hs[e], res[e * GATHER_FAST:(e + 1) * GATHER_FAST])

    def slow(jj):
        for e in range(n_e):
            h = hit(jj, e, row_slow)
            add_window(jj, e, GATHER_SLOW, h,
                       _dot(jnp.where(h, 1.0, 0.0).astype(BF16),
                            h2_ref[jj * GATHER_TOKENS:(jj + 1) * GATHER_TOKENS, :]))

    _run_blocks([fits for _, fits in windows], fast, slow)


def _gather(cum_flat, rank, aff, h2, cap):
    b, s, d = h2.shape
    n_e = rank.shape[1]
    rows = cap + GATHER_SLOW
    grp = n_e // GATHER_GROUPS
    per_tok = pl.BlockSpec((None, grp, GATHER_STEP_TOKENS), lambda bi, eg, t, c: (bi, eg, t))
    whole = lambda w: pl.BlockSpec((None, grp, rows, w), lambda bi, eg, t, c: (bi, eg, 0, 0))
    grid_spec = pltpu.PrefetchScalarGridSpec(
        num_scalar_prefetch=1,
        grid=(b, GATHER_GROUPS, s // GATHER_STEP_TOKENS),
        in_specs=[per_tok, per_tok,
                  pl.BlockSpec((None, GATHER_STEP_TOKENS, d), lambda bi, eg, t, c: (bi, t, 0))],
        out_specs=[whole(d), whole(LANES)],
    )
    return pl.pallas_call(
        _gather_kernel,
        grid_spec=grid_spec,
        out_shape=[jax.ShapeDtypeStruct((b, n_e, rows, d), BF16),
                   jax.ShapeDtypeStruct((b, n_e, rows, LANES), F32)],
        compiler_params=_cparams(("arbitrary", "arbitrary", "arbitrary"), 56),
        name="gather",
    )(cum_flat, rank, aff, h2)


def _moe_ffn_kernel(xe_ref, gate_ref, wg_ref, wu_ref, wd_ref, ye_ref, wg_bf, wu_bf, wd_bf):
    n_seq, cap, d = xe_ref.shape

    @pl.when(pl.program_id(1) == 0)
    def _():
        wg_bf[...] = wg_ref[...].astype(BF16)
        wu_bf[...] = wu_ref[...].astype(BF16)
        wd_bf[...] = wd_ref[...].astype(BF16)

    xe = xe_ref[...].reshape(n_seq * cap, d)
    gate_h = _dot(xe, wg_bf[...])
    up_h = _dot(xe, wu_bf[...])
    hidden = (gate_h * _sigmoid(gate_h) * up_h).astype(BF16)
    ye = _dot(hidden, wd_bf[...]) * gate_ref[...].reshape(n_seq * cap, LANES)[:, 0:1]
    for i in range(n_seq):
        ye_ref[i, :cap, :] = ye[i * cap:(i + 1) * cap].astype(BF16)
        ye_ref[i, cap:, :] = jnp.zeros((ye_ref.shape[1] - cap, d), BF16)


def _moe_ffn(xe, gate, wg, wu, wd, cap):
    b, n_e, _, d = xe.shape
    hid = wg.shape[2]
    rows = cap + SLOT_WINDOW
    return pl.pallas_call(
        _moe_ffn_kernel,
        grid=(n_e, b // FFN_SEQS),
        in_specs=[
            pl.BlockSpec((FFN_SEQS, None, cap, d), lambda e, bi: (bi, e, 0, 0)),
            pl.BlockSpec((FFN_SEQS, None, cap, LANES), lambda e, bi: (bi, e, 0, 0)),
            pl.BlockSpec((None, d, hid), lambda e, bi: (e, 0, 0)),
            pl.BlockSpec((None, d, hid), lambda e, bi: (e, 0, 0)),
            pl.BlockSpec((None, hid, d), lambda e, bi: (e, 0, 0)),
        ],
        out_specs=pl.BlockSpec((FFN_SEQS, None, rows, d), lambda e, bi: (bi, e, 0, 0)),
        out_shape=jax.ShapeDtypeStruct((b, n_e, rows, d), BF16),
        scratch_shapes=[pltpu.VMEM((d, hid), BF16), pltpu.VMEM((d, hid), BF16),
                        pltpu.VMEM((hid, d), BF16)],
        compiler_params=_cparams(("arbitrary", "arbitrary"), 56),
        name="moe_ffn",
    )(xe, gate, wg, wu, wd)


def _combine_kernel(cum_ref, rank_ref, ye_ref, x1_ref, g_ref, y_ref, rhs_ref):
    n_e = rank_ref.shape[0]
    n_blk = x1_ref.shape[0] // SLOT_TILE
    first_expert = pl.program_id(0) * n_e
    t = pl.program_id(1)
    row_fast = lax.broadcasted_iota(I32, (FAST_WINDOW, SLOT_TILE), 0)
    row_slow = lax.broadcasted_iota(I32, (SLOT_WINDOW, SLOT_TILE), 0)
    windows = [_block_windows(cum_ref, first_expert, n_e, t * n_blk + jj) for jj in range(n_blk)]

    def hits(jj, e, row_i):
        toks = slice(jj * SLOT_TILE, (jj + 1) * SLOT_TILE)
        hit = (row_i + windows[jj][0][e]) == rank_ref[e:e + 1, toks]
        return jnp.where(hit, 1.0, 0.0).astype(BF16)

    def finish(jj, moe):
        toks = slice(jj * SLOT_TILE, (jj + 1) * SLOT_TILE)
        y_ref[toks, :] = _rms(x1_ref[toks, :] + moe, g_ref[...])

    def fast(jj):
        rhs = rhs_ref.at[jj % rhs_ref.shape[0]]
        for e in range(n_e):
            rhs[e * FAST_WINDOW:(e + 1) * FAST_WINDOW, :] = (
                ye_ref[e, pl.ds(windows[jj][0][e], FAST_WINDOW), :])
        stack = jnp.concatenate([hits(jj, e, row_fast) for e in range(n_e)], axis=0)
        finish(jj, _dot_tn(stack, rhs[...]))

    def slow(jj):
        acc = None
        for e in range(n_e):
            part = _dot_tn(hits(jj, e, row_slow),
                           ye_ref[e, pl.ds(windows[jj][0][e], SLOT_WINDOW), :])
            acc = part if acc is None else acc + part
        finish(jj, acc)

    _run_blocks([fits for _, fits in windows], fast, slow)


def _combine(cum_flat, rank, ye, x1, g):
    b, s, d = x1.shape
    n_e, ye_rows = ye.shape[1], ye.shape[2]
    row = pl.BlockSpec((None, ROW_TILE, d), lambda bi, t, c: (bi, t, 0))
    grid_spec = pltpu.PrefetchScalarGridSpec(
        num_scalar_prefetch=1,
        grid=(b, s // ROW_TILE),
        in_specs=[
            pl.BlockSpec((None, n_e, ROW_TILE), lambda bi, t, c: (bi, 0, t)),
            pl.BlockSpec((None, n_e, ye_rows, d), lambda bi, t, c: (bi, 0, 0, 0)),
            row,
            pl.BlockSpec(g.shape, lambda bi, t, c: (0, 0)),
        ],
        out_specs=row,
        scratch_shapes=[pltpu.VMEM((2, n_e * FAST_WINDOW, d), BF16)],
    )
    return pl.pallas_call(
        _combine_kernel,
        grid_spec=grid_spec,
        out_shape=jax.ShapeDtypeStruct((b, s, d), F32),
        compiler_params=_cparams(("arbitrary", "arbitrary"), 60),
        name="combine",
    )(cum_flat, rank, ye, x1, g)


def _moe_stages(aff, h2, x1, wg, wu, wd, g_final, cap):
    b, n_e, s = aff.shape
    rank, cum = _topk(aff, cap)
    cum_flat = cum.reshape(-1)
    xe, gate = _gather(cum_flat, rank, aff, h2, cap)
    ye = _moe_ffn(xe, gate, wg, wu, wd, cap)
    return _combine(cum_flat, rank, ye, x1, g_final)


def kernel(x, norm_mix_g, w_in, b_gate, gmlp_norm_g, w_spatial, b_spatial, w_proj_a, w_proj_b,
           w_out, norm_ffn_g, w_router, w_e_gate, w_e_up, w_e_down, norm_final_g):
    b, s, d = x.shape
    assert w_in.shape[0] == 1, "single-layer block"
    cap = CAPACITY_FACTOR * s // N_EXPERTS
    group_width = GMLP_WIDTH // GMLP_GROUPS
    ws_pairs = w_spatial[0].astype(BF16).reshape(GMLP_GROUPS // 2, 2 * CHUNK, CHUNK)
    bsp = jnp.repeat(b_spatial[0].T, group_width, axis=1)
    qkv, ta, gb = _mix_in(x, norm_mix_g, w_in[0].astype(BF16), b_gate, gmlp_norm_g, ws_pairs, bsp,
                          w_proj_a[0].astype(BF16))
    os_, ls_ = [], []
    for (q, k, v), dil in zip(qkv, DILATIONS):
        o, lse = _attn_pattern(q, k, v, dil)
        os_.append(o)
        ls_.append(lse)
    x1, h2, aff = _mix_out(x, ta, gb, os_, ls_, w_proj_b[0].astype(BF16), w_out[0].astype(BF16),
                           norm_ffn_g, w_router[0].T)
    return _moe_stages(aff, h2, x1, w_e_gate[0], w_e_up[0], w_e_down[0], norm_final_g[None], cap)
```

```python
import functools

import jax
import jax.numpy as jnp
from jax import lax
from jax.experimental import pallas as pl
from jax.experimental.pallas import tpu as pltpu

F32 = jnp.float32
BF16 = jnp.bfloat16
I32 = jnp.int32

EPS = 1e-6
GMLP_WIDTH = 512
GMLP_GROUPS = 8
CHUNK = 128
N_HEADS = 8
HEAD_DIM = 64
ATTN_WIDTH = N_HEADS * HEAD_DIM
DILATIONS = (1, 4, 16)
DILATION_STEP = 4
assert all(b == a * DILATION_STEP for a, b in zip(DILATIONS, DILATIONS[1:]))
HALF_WINDOW = 64
N_EXPERTS = 16
CAPACITY_FACTOR = 2

LANES = 128
Q_TILE = 128
KEY_TILE = 2 * Q_TILE
ATTN_STEP_ROWS = 2048
STAT_LANES = LANES // N_HEADS
DEN_SHIFT = STAT_LANES // 2
SLOT_TILE = 128
SLOT_ALIGN = 16
SLOT_WINDOW = SLOT_TILE + SLOT_ALIGN
GATHER_GROUPS = 2
FFN_SEQS = 2
GATHER_STEP_TOKENS = 1024
GATHER_TOKENS = 256
GATHER_FAST = 80
GATHER_SLOW = GATHER_TOKENS + SLOT_ALIGN
FAST_WINDOW = 48
ROW_TILE = 512
SUB_ROWS = 512
MIB = 1024 * 1024


def _cparams(sem, vmem_mib):
    return pltpu.CompilerParams(dimension_semantics=sem, vmem_limit_bytes=vmem_mib * MIB)


def _gelu_tanh(x):
    return 0.5 * x * (1.0 + jnp.tanh(0.7978845608028654 * (x + 0.044715 * (x * x * x))))


def _sigmoid(x):
    return 1.0 / (1.0 + jnp.exp(-x))


def _rms(x, g):
    return x * lax.rsqrt(jnp.mean(x * x, axis=-1, keepdims=True) + EPS) * g


def _dot(a, b):
    return jnp.dot(a, b, preferred_element_type=F32)


def _dot_nt(a, b):
    return lax.dot_general(a, b, (((1,), (1,)), ((), ())), preferred_element_type=F32)


def _dot_tn(a, b):
    return lax.dot_general(a, b, (((0,), (0,)), ((), ())), preferred_element_type=F32)


def _mix_in_kernel(x_ref, g_ref, win_ref, bg_ref, g2_ref, ws_ref, bsp_ref, pa_ref, *refs):
    n_qkv = 3 * len(DILATIONS)
    qkv_refs = refs[:n_qkv]
    ta_ref, gb_ref = refs[n_qkv:n_qkv + 2]
    win_bf = refs[n_qkv + 2]
    stage_refs = refs[n_qkv + 3:]

    @pl.when(jnp.logical_and(pl.program_id(0) == 0, pl.program_id(1) == 0))
    def _():
        for c in range(0, win_ref.shape[1], GMLP_WIDTH):
            win_bf[:, c:c + GMLP_WIDTH] = win_ref[:, c:c + GMLP_WIDTH].astype(BF16)

    for sub in range(x_ref.shape[0] // SUB_ROWS):
        mine = stage_refs[6 * sub:6 * sub + 6]
        _mix_in_rows(sub, x_ref, g_ref, win_bf, bg_ref, g2_ref, ws_ref, bsp_ref, pa_ref,
                     qkv_refs, ta_ref, gb_ref, list(zip(mine[0::2], mine[1::2])))


def _mix_in_rows(sub, x_ref, g_ref, win_ref, bg_ref, g2_ref, ws_ref, bsp_ref, pa_ref,
                 qkv_refs, ta_ref, gb_ref, stage_refs):
    rows, d_model = SUB_ROWS, x_ref.shape[1]
    rs_tile = slice(sub * rows, (sub + 1) * rows)
    h = _rms(x_ref[rs_tile, :], g_ref[...]).astype(BF16)

    def proj(lo, width):
        return _dot(h, win_ref[:, lo:lo + width])

    c0 = 0
    u = _gelu_tanh(proj(c0, GMLP_WIDTH)); c0 += GMLP_WIDTH
    v = _gelu_tanh(proj(c0, GMLP_WIDTH)); c0 += GMLP_WIDTH
    for i in range(3):
        val = proj(c0, ATTN_WIDTH); c0 += ATTN_WIDTH
        if i == 0:
            val = val * (HEAD_DIM ** -0.5)
        qkv_refs[i][0, rs_tile, :] = val.astype(BF16)
        stage1, stage2 = stage_refs[i]
        n4, n16 = rows // DILATIONS[1], rows // DILATIONS[2]
        out4, out16 = qkv_refs[3 + i], qkv_refs[6 + i]
        for p in range(ATTN_WIDTH // LANES):
            cs = slice(p * LANES, (p + 1) * LANES)
            stage1[p] = val[:, cs]
            for r4 in range(DILATION_STEP):
                part = stage1[p, pl.ds(r4, n4, stride=DILATION_STEP), :]
                out4[r4, sub * n4:(sub + 1) * n4, cs] = part.astype(BF16)
                stage2[p, r4] = part
                for c in range(DILATION_STEP):
                    out16[r4 + DILATION_STEP * c, sub * n16:(sub + 1) * n16, cs] = (
                        stage2[p, r4, pl.ds(c, n16, stride=DILATION_STEP), :].astype(BF16))
    ga = _sigmoid(proj(c0, d_model) + bg_ref[:, :d_model]); c0 += d_model
    gb = _sigmoid(proj(c0, d_model) + bg_ref[:, d_model:])
    gb_ref[rs_tile, :] = gb.astype(BF16)

    vn = _rms(v, g2_ref[...]).astype(BF16)
    lane_lo = lax.broadcasted_iota(I32, (CHUNK, LANES), 1) < HEAD_DIM
    bsp = bsp_ref[...]
    n_chunk = rows // CHUNK
    mixed_slabs = []
    for p in range(GMLP_WIDTH // LANES):
        slab = jnp.concatenate(
            [vn[c * CHUNK:(c + 1) * CHUNK, p * LANES:(p + 1) * LANES] for c in range(n_chunk)],
            axis=1)
        r = _dot(ws_ref[p], slab)
        mixed_slabs.append([jnp.where(lane_lo, r[:CHUNK, c * LANES:(c + 1) * LANES],
                                      r[CHUNK:, c * LANES:(c + 1) * LANES])
                            for c in range(n_chunk)])
    a_chunks = []
    for c in range(n_chunk):
        rs = slice(c * CHUNK, (c + 1) * CHUNK)
        mixed = jnp.concatenate([slabs[c] for slabs in mixed_slabs], axis=1) + bsp
        a_chunks.append((u[rs] * mixed).astype(BF16))
    a = jnp.concatenate(a_chunks, axis=0)
    ta_ref[rs_tile, :] = (ga * _dot(a, pa_ref[...])).astype(BF16)


def _mix_in(x, g, w_in, b_gate, g2, ws_pairs, bsp, w_pa):
    b, s, d = x.shape
    const = lambda shape: pl.BlockSpec(shape, lambda bi, t: (0,) * len(shape))
    row = lambda w: pl.BlockSpec((None, ROW_TILE, w), lambda bi, t: (bi, t, 0))
    qkv_specs, qkv_shapes = [], []
    for dil in DILATIONS:
        spec = pl.BlockSpec((None, dil, ROW_TILE // dil, ATTN_WIDTH), lambda bi, t: (bi, 0, t, 0))
        qkv_specs += [spec] * 3
        qkv_shapes += [jax.ShapeDtypeStruct((b, dil, s // dil, ATTN_WIDTH), BF16)] * 3
    outs = pl.pallas_call(
        _mix_in_kernel,
        grid=(b, s // ROW_TILE),
        in_specs=[row(d), const(g.shape),
                  pl.BlockSpec(w_in.shape, lambda bi, t: (0, 0), pipeline_mode=pl.Buffered(1)),
                  const(b_gate.shape), const(g2.shape),
                  const(ws_pairs.shape), const(bsp.shape), const(w_pa.shape)],
        out_specs=qkv_specs + [row(d), row(d)],
        out_shape=qkv_shapes + [jax.ShapeDtypeStruct((b, s, d), BF16)] * 2,
        scratch_shapes=[pltpu.VMEM(w_in.shape, BF16)]
                       + [pltpu.VMEM((ATTN_WIDTH // LANES, SUB_ROWS, LANES), F32),
                          pltpu.VMEM((ATTN_WIDTH // LANES, DILATION_STEP,
                                      SUB_ROWS // DILATION_STEP, LANES), F32)]
                       * (3 * (ROW_TILE // SUB_ROWS)),
        compiler_params=_cparams(("arbitrary", "arbitrary"), 60),
        name="mix_in",
    )(x, g, w_in, b_gate, g2, ws_pairs, bsp, w_pa)
    n_qkv = 3 * len(DILATIONS)
    qkv = [outs[3 * i:3 * i + 3] for i in range(len(DILATIONS))]
    return qkv, outs[n_qkv], outs[n_qkv + 1]


def _attn_kernel(q_ref, k_ref, v_ref, o_ref, l_ref, bias_ref, *, dil):
    n_res, rows, _ = q_ref.shape
    seq = k_ref.shape[1]
    t = pl.program_id(2)
    first = jnp.logical_and(jnp.logical_and(pl.program_id(0) == 0, pl.program_id(1) == 0), t == 0)

    @pl.when(first)
    def _():
        ii = lax.broadcasted_iota(I32, (Q_TILE, KEY_TILE), 0)
        jj = lax.broadcasted_iota(I32, (Q_TILE, KEY_TILE), 1)
        for var in range(3):
            absd = jnp.abs(jj - ii - var * HALF_WINDOW)
            valid = absd <= HALF_WINDOW
            absf = absd.astype(F32)
            for h in range(N_HEADS):
                slope = 2.0 ** (-8.0 * (h + 1) / N_HEADS)
                bias_ref[var, h] = jnp.where(valid, -(slope * dil) * absf, -jnp.inf)

    lane = lax.broadcasted_iota(I32, (Q_TILE, LANES), 1)
    lane_lo = lane < HEAD_DIM
    mask_lo = jnp.where(lane_lo, 1.0, 0.0).astype(BF16)
    mask_hi = jnp.where(lane_lo, 0.0, 1.0).astype(BF16)
    for rr in range(n_res):
        for qi in range(rows // Q_TILE):
            rs = slice(qi * Q_TILE, (qi + 1) * Q_TILE)
            i0 = t * rows + qi * Q_TILE
            start = pl.multiple_of(jnp.clip(i0 - HALF_WINDOW, 0, seq - KEY_TILE), HALF_WINDOW)
            var = (i0 - start) // HALF_WINDOW
            for p in range(ATTN_WIDTH // LANES):
                cs = slice(p * LANES, (p + 1) * LANES)
                qp = q_ref[rr, rs, cs]
                kp = k_ref[rr, pl.ds(start, KEY_TILE), cs]
                vp = v_ref[rr, pl.ds(start, KEY_TILE), cs]
                q2 = jnp.concatenate([qp * mask_lo, qp * mask_hi], axis=0)
                s2 = _dot_nt(q2, kp)
                probs = []
                for hh in range(2):
                    h = 2 * p + hh
                    s = s2[hh * Q_TILE:(hh + 1) * Q_TILE] + bias_ref[var, h]
                    m = jnp.max(s, axis=-1, keepdims=True)
                    e = jnp.exp(s - m)
                    den = jnp.sum(e, axis=-1, keepdims=True)
                    probs.append(e)
                    lo = h * STAT_LANES
                    l_ref[rr, rs, lo:lo + DEN_SHIFT] = jnp.broadcast_to(m, (Q_TILE, DEN_SHIFT))
                    l_ref[rr, rs, lo + DEN_SHIFT:lo + STAT_LANES] = jnp.broadcast_to(
                        den, (Q_TILE, DEN_SHIFT))
                o2 = _dot(jnp.concatenate(probs, axis=0).astype(BF16), vp)
                o_ref[rr, rs, cs] = jnp.where(lane_lo, o2[:Q_TILE], o2[Q_TILE:]).astype(BF16)


def _attn_pattern(q, k, v, dil):
    b, _, seq, w = q.shape
    rows = min(seq, ATTN_STEP_ROWS)
    n_res = ATTN_STEP_ROWS // rows
    qspec = lambda width: pl.BlockSpec((None, n_res, rows, width), lambda bi, r, t: (bi, r, t, 0))
    kspec = pl.BlockSpec((None, n_res, seq, w), lambda bi, r, t: (bi, r, 0, 0))
    return pl.pallas_call(
        functools.partial(_attn_kernel, dil=dil),
        grid=(b, dil // n_res, seq // rows),
        in_specs=[qspec(w), kspec, kspec],
        out_specs=[qspec(w), qspec(LANES)],
        out_shape=[jax.ShapeDtypeStruct(q.shape, BF16),
                   jax.ShapeDtypeStruct((b, dil, seq, LANES), F32)],
        scratch_shapes=[pltpu.VMEM((3, N_HEADS, Q_TILE, KEY_TILE), F32)],
        compiler_params=_cparams(("arbitrary", "arbitrary", "arbitrary"), 48),
        name=f"attn_d{dil}",
    )(q, k, v)


def _to_natural(src_ref, nat_ref, tmp_ref, rows):
    n_slab = nat_ref.shape[0]
    step = DILATION_STEP
    for p in range(n_slab):
        cs = slice(p * LANES, (p + 1) * LANES)
        for r4 in range(step):
            if tmp_ref is None:
                quarter = src_ref[r4][:, cs].astype(F32)
            else:
                for c in range(step):
                    tmp_ref[p, r4, pl.ds(c, rows // (step * step), stride=step), :] = (
                        src_ref[r4 + step * c][:, cs].astype(F32))
                quarter = tmp_ref[p, r4]
            nat_ref[p, pl.ds(r4, rows // step, stride=step), :] = quarter
    return jnp.concatenate([nat_ref[p] for p in range(n_slab)], axis=1)


def _mix_out_kernel(x_ref, ta_ref, gb_ref, *refs):
    n_pat = len(DILATIONS)
    o_refs = refs[:n_pat]
    l_refs = refs[n_pat:2 * n_pat]
    pb_f32, wo_f32, g_ref, wr_ref, x1_ref, h2_ref, aff_ref = refs[2 * n_pat:2 * n_pat + 7]
    pb_ref, wo_ref = refs[2 * n_pat + 7:2 * n_pat + 9]
    stage_refs = refs[2 * n_pat + 9:]
    rows = x_ref.shape[0]

    @pl.when(jnp.logical_and(pl.program_id(0) == 0, pl.program_id(1) == 0))
    def _():
        pb_ref[...] = pb_f32[...].astype(BF16)
        wo_ref[...] = wo_f32[...].astype(BF16)

    stage_refs = list(stage_refs)
    outs = [o_refs[0][0].astype(F32)]
    lses = [l_refs[0][0]]
    for di in (1, 2):
        for src, dest in ((o_refs[di], outs), (l_refs[di], lses)):
            nat = stage_refs.pop(0)
            tmp = stage_refs.pop(0) if di == 2 else None
            dest.append(_to_natural(src, nat, tmp, rows))

    dens = [pltpu.roll(st, LANES - DEN_SHIFT, 1) for st in lses]
    lses = [st + jnp.log(den) for st, den in zip(lses, dens)]
    m = functools.reduce(jnp.maximum, lses)
    ws = [jnp.exp(l - m) for l in lses]
    inv = 1.0 / functools.reduce(lambda a, c: a + c, ws)
    lane = lax.broadcasted_iota(I32, (rows, LANES), 1)
    used = lane % STAT_LANES < DEN_SHIFT
    ws = [jnp.where(used, w * inv / den, 0.0) for w, den in zip(ws, dens)]
    k_i = lax.broadcasted_iota(I32, (LANES, ATTN_WIDTH), 0)
    c_i = lax.broadcasted_iota(I32, (LANES, ATTN_WIDTH), 1)
    spread = jnp.where(k_i == (c_i // HEAD_DIM) * STAT_LANES, 1.0, 0.0).astype(BF16)
    o = None
    for w, o_p in zip(ws, outs):
        w_hi = w.astype(BF16)
        w_lo = (w - w_hi.astype(F32)).astype(BF16)
        term = (_dot(w_hi, spread) + _dot(w_lo, spread)) * o_p
        o = term if o is None else o + term

    ob = _dot(o.astype(BF16), pb_ref[...])
    merged = (ta_ref[...].astype(F32) + gb_ref[...].astype(F32) * ob).astype(BF16)
    x1 = x_ref[...] + _dot(merged, wo_ref[...])
    x1_ref[...] = x1
    h2 = _rms(x1, g_ref[...])
    h2_ref[...] = h2.astype(BF16)
    h_hi = h2.astype(BF16)
    h_lo = (h2 - h_hi.astype(F32)).astype(BF16)
    wr = wr_ref[...]
    w_hi = wr.astype(BF16)
    w_lo = (wr - w_hi.astype(F32)).astype(BF16)
    n_e = wr.shape[0]
    by_hi = _dot_nt(jnp.concatenate([w_hi, w_lo], axis=0), h_hi)
    logits = by_hi[:n_e] + (_dot_nt(w_hi, h_lo) + by_hi[n_e:])
    e = jnp.exp(logits - jnp.max(logits, axis=0, keepdims=True))
    aff_ref[...] = e / jnp.sum(e, axis=0, keepdims=True)


def _mix_out(x, ta, gb, os_, ls_, w_pb, w_out, g, w_router_t):
    b, s, d = x.shape
    n_e = w_router_t.shape[0]
    const = lambda shape: pl.BlockSpec(shape, lambda bi, t: (0,) * len(shape))
    once = lambda shape: pl.BlockSpec(shape, lambda bi, t: (0,) * len(shape),
                                      pipeline_mode=pl.Buffered(1))
    row = lambda w: pl.BlockSpec((None, ROW_TILE, w), lambda bi, t: (bi, t, 0))
    res = lambda dil, w: pl.BlockSpec((None, dil, ROW_TILE // dil, w), lambda bi, t: (bi, 0, t, 0))
    stage = []
    for di in (1, 2):
        for slabs in (ATTN_WIDTH // LANES, 1):
            stage.append(pltpu.VMEM((slabs, ROW_TILE, LANES), F32))
            if di == 2:
                stage.append(pltpu.VMEM((slabs, DILATION_STEP, ROW_TILE // DILATION_STEP, LANES),
                                        F32))
    return pl.pallas_call(
        _mix_out_kernel,
        grid=(b, s // ROW_TILE),
        in_specs=[row(d), row(d), row(d)]
                 + [res(dil, ATTN_WIDTH) for dil in DILATIONS]
                 + [res(dil, LANES) for dil in DILATIONS]
                 + [once(w_pb.shape), once(w_out.shape), const(g.shape), const(w_router_t.shape)],
        out_specs=[row(d), row(d), pl.BlockSpec((None, n_e, ROW_TILE), lambda bi, t: (bi, 0, t))],
        out_shape=[jax.ShapeDtypeStruct((b, s, d), F32), jax.ShapeDtypeStruct((b, s, d), BF16),
                   jax.ShapeDtypeStruct((b, n_e, s), F32)],
        scratch_shapes=[pltpu.VMEM(w_pb.shape, BF16), pltpu.VMEM(w_out.shape, BF16)] + stage,
        compiler_params=_cparams(("arbitrary", "arbitrary"), 48),
        name="mix_out",
    )(x, ta, gb, *os_, *ls_, w_pb, w_out, g, w_router_t)


def _topk_kernel(aff_ref, rank_ref, cum_ref, *, cap):
    n_e, s = aff_ref.shape
    n_blk = s // SLOT_TILE
    aff = aff_ref[...]
    thr = jnp.zeros((n_e, 1), I32)
    for bit in range(30, -1, -1):
        cand = thr | (1 << bit)
        cnt = jnp.sum((aff >= pltpu.bitcast(cand, F32)).astype(I32), axis=1, keepdims=True)
        thr = jnp.where(cnt >= cap, cand, thr)
    above = aff >= pltpu.bitcast(thr + 1, F32)
    tie = jnp.logical_and(aff >= pltpu.bitcast(thr, F32), jnp.logical_not(above))
    need = (cap - jnp.sum(above.astype(I32), axis=1, keepdims=True)).astype(F32)
    r_i = lax.broadcasted_iota(I32, (SLOT_TILE, SLOT_TILE), 0)
    c_i = lax.broadcasted_iota(I32, (SLOT_TILE, SLOT_TILE), 1)
    tri = jnp.where(r_i < c_i, 1.0, 0.0).astype(BF16)
    lane = lax.broadcasted_iota(I32, (n_e, LANES), 1)
    run_tie = jnp.zeros((n_e, 1), F32)
    run_sel = jnp.zeros((n_e, 1), F32)
    cum = jnp.zeros((n_e, LANES), F32)
    for j in range(n_blk):
        cs = slice(j * SLOT_TILE, (j + 1) * SLOT_TILE)
        tie_f = jnp.where(tie[:, cs], 1.0, 0.0)
        tie_rank = _dot(tie_f.astype(BF16), tri) + run_tie
        run_tie = run_tie + jnp.sum(tie_f, axis=1, keepdims=True)
        sel_f = jnp.where(above[:, cs], 1.0, jnp.where(tie_rank < need, tie_f, 0.0))
        rank = _dot(sel_f.astype(BF16), tri) + run_sel
        rank_ref[:, cs] = jnp.where(sel_f > 0.0, rank, -1.0).astype(I32)
        cum = jnp.where(lane == j, run_sel, cum)
        run_sel = run_sel + jnp.sum(sel_f, axis=1, keepdims=True)
    cum = jnp.where(lane == n_blk, run_sel, cum)
    cum_ref[...] = cum.astype(I32)


def _topk(aff, cap):
    b, n_e, s = aff.shape
    rows = b * n_e
    rank, cum = pl.pallas_call(
        functools.partial(_topk_kernel, cap=cap),
        grid=(1,),
        in_specs=[pl.BlockSpec((rows, s), lambda i: (0, 0))],
        out_specs=[pl.BlockSpec((rows, s), lambda i: (0, 0)),
                   pl.BlockSpec((rows, LANES), lambda i: (0, 0))],
        out_shape=[jax.ShapeDtypeStruct((rows, s), I32),
                   jax.ShapeDtypeStruct((rows, LANES), I32)],
        compiler_params=_cparams(("arbitrary",), 32),
        name="topk",
    )(aff.reshape(rows, s))
    return rank.reshape(b, n_e, s), cum.reshape(b, n_e, LANES)


def _slot_window(first):
    return pl.multiple_of((first // SLOT_ALIGN) * SLOT_ALIGN, SLOT_ALIGN)


def _block_windows(cum_ref, first_expert, n_e, j, tiles=1, fast=FAST_WINDOW):
    los, fits = [], None
    for e in range(n_e):
        at = (first_expert + e) * LANES + j * tiles
        lo = _slot_window(cum_ref[at])
        ok = cum_ref[at + tiles] - lo <= fast
        los.append(lo)
        fits = ok if fits is None else jnp.logical_and(fits, ok)
    return los, fits


def _run_blocks(fits, fast, slow):
    all_fit = functools.reduce(jnp.logical_and, fits)

    @pl.when(all_fit)
    def _():
        for jj in range(len(fits)):
            fast(jj)

    @pl.when(jnp.logical_not(all_fit))
    def _():
        for jj, fit in enumerate(fits):
            pl.when(fit)(functools.partial(fast, jj))
            pl.when(jnp.logical_not(fit))(functools.partial(slow, jj))


def _gather_kernel(cum_ref, rank_ref, aff_ref, h2_ref, xe_ref, gate_ref):
    n_e = rank_ref.shape[0]
    rows = h2_ref.shape[0]
    first_expert = (pl.program_id(0) * pl.num_programs(1) + pl.program_id(1)) * n_e
    t = pl.program_id(2)

    @pl.when(t == 0)
    def _():
        xe_ref[...] = jnp.zeros_like(xe_ref)
        gate_ref[...] = jnp.zeros_like(gate_ref)

    row_fast = lax.broadcasted_iota(I32, (GATHER_FAST, GATHER_TOKENS), 0)
    row_slow = lax.broadcasted_iota(I32, (GATHER_SLOW, GATHER_TOKENS), 0)
    n_blk = rows // GATHER_TOKENS
    windows = [_block_windows(cum_ref, first_expert, n_e, t * n_blk + jj,
                              tiles=GATHER_TOKENS // SLOT_TILE, fast=GATHER_FAST)
               for jj in range(n_blk)]

    def hit(jj, e, row_i):
        toks = slice(jj * GATHER_TOKENS, (jj + 1) * GATHER_TOKENS)
        return (row_i + windows[jj][0][e]) == rank_ref[e:e + 1, toks]

    def add_window(jj, e, width, hit_e, rows_e):
        toks = slice(jj * GATHER_TOKENS, (jj + 1) * GATHER_TOKENS)
        win = pl.ds(windows[jj][0][e], width)
        xe_ref[e, win, :] += rows_e.astype(BF16)
        gate_ref[e, win, :] += jnp.sum(jnp.where(hit_e, aff_ref[e:e + 1, toks], 0.0), axis=1,
                                       keepdims=True)

    def fast(jj):
        hs = [hit(jj, e, row_fast) for e in range(n_e)]
        stack = jnp.concatenate([jnp.where(h, 1.0, 0.0).astype(BF16) for h in hs], axis=0)
        res = _dot(stack, h2_ref[jj * GATHER_TOKENS:(jj + 1) * GATHER_TOKENS, :])
        for e in range(n_e):
            add_window(jj, e, GATHER_FAST, hs[e], res[e * GATHER_FAST:(e + 1) * GATHER_FAST])

    def slow(jj):
        for e in range(n_e):
            h = hit(jj, e, row_slow)
            add_window(jj, e, GATHER_SLOW, h,
                       _dot(jnp.where(h, 1.0, 0.0).astype(BF16),
                            h2_ref[jj * GATHER_TOKENS:(jj + 1) * GATHER_TOKENS, :]))

    _run_blocks([fits for _, fits in windows], fast, slow)


def _gather(cum_flat, rank, aff, h2, cap):
    b, s, d = h2.shape
    n_e = rank.shape[1]
    rows = cap + GATHER_SLOW
    grp = n_e // GATHER_GROUPS
    per_tok = pl.BlockSpec((None, grp, GATHER_STEP_TOKENS), lambda bi, eg, t, c: (bi, eg, t))
    whole = lambda w: pl.BlockSpec((None, grp, rows, w), lambda bi, eg, t, c: (bi, eg, 0, 0))
    grid_spec = pltpu.PrefetchScalarGridSpec(
        num_scalar_prefetch=1,
        grid=(b, GATHER_GROUPS, s // GATHER_STEP_TOKENS),
        in_specs=[per_tok, per_tok,
                  pl.BlockSpec((None, GATHER_STEP_TOKENS, d), lambda bi, eg, t, c: (bi, t, 0))],
        out_specs=[whole(d), whole(LANES)],
    )
    return pl.pallas_call(
        _gather_kernel,
        grid_spec=grid_spec,
        out_shape=[jax.ShapeDtypeStruct((b, n_e, rows, d), BF16),
                   jax.ShapeDtypeStruct((b, n_e, rows, LANES), F32)],
        compiler_params=_cparams(("arbitrary", "arbitrary", "arbitrary"), 56),
        name="gather",
    )(cum_flat, rank, aff, h2)


def _moe_ffn_kernel(xe_ref, gate_ref, wg_ref, wu_ref, wd_ref, ye_ref, wg_bf, wu_bf, wd_bf):
    n_seq, cap, d = xe_ref.shape

    @pl.when(pl.program_id(1) == 0)
    def _():
        wg_bf[...] = wg_ref[...].astype(BF16)
        wu_bf[...] = wu_ref[...].astype(BF16)
        wd_bf[...] = wd_ref[...].astype(BF16)

    xe = xe_ref[...].reshape(n_seq * cap, d)
    gate_h = _dot(xe, wg_bf[...])
    up_h = _dot(xe, wu_bf[...])
    hidden = (gate_h * _sigmoid(gate_h) * up_h).astype(BF16)
    ye = _dot(hidden, wd_bf[...]) * gate_ref[...].reshape(n_seq * cap, LANES)[:, 0:1]
    for i in range(n_seq):
        ye_ref[i, :cap, :] = ye[i * cap:(i + 1) * cap].astype(BF16)
        ye_ref[i, cap:, :] = jnp.zeros((ye_ref.shape[1] - cap, d), BF16)


def _moe_ffn(xe, gate, wg, wu, wd, cap):
    b, n_e, _, d = xe.shape
    hid = wg.shape[2]
    rows = cap + SLOT_WINDOW
    return pl.pallas_call(
        _moe_ffn_kernel,
        grid=(n_e, b // FFN_SEQS),
        in_specs=[
            pl.BlockSpec((FFN_SEQS, None, cap, d), lambda e, bi: (bi, e, 0, 0)),
            pl.BlockSpec((FFN_SEQS, None, cap, LANES), lambda e, bi: (bi, e, 0, 0)),
            pl.BlockSpec((None, d, hid), lambda e, bi: (e, 0, 0)),
            pl.BlockSpec((None, d, hid), lambda e, bi: (e, 0, 0)),
            pl.BlockSpec((None, hid, d), lambda e, bi: (e, 0, 0)),
        ],
        out_specs=pl.BlockSpec((FFN_SEQS, None, rows, d), lambda e, bi: (bi, e, 0, 0)),
        out_shape=jax.ShapeDtypeStruct((b, n_e, rows, d), BF16),
        scratch_shapes=[pltpu.VMEM((d, hid), BF16), pltpu.VMEM((d, hid), BF16),
                        pltpu.VMEM((hid, d), BF16)],
        compiler_params=_cparams(("arbitrary", "arbitrary"), 56),
        name="moe_ffn",
    )(xe, gate, wg, wu, wd)


def _combine_kernel(cum_ref, rank_ref, ye_ref, x1_ref, g_ref, y_ref, rhs_ref):
    n_e = rank_ref.shape[0]
    n_blk = x1_ref.shape[0] // SLOT_TILE
    first_expert = pl.program_id(0) * n_e
    t = pl.program_id(1)
    row_fast = lax.broadcasted_iota(I32, (FAST_WINDOW, SLOT_TILE), 0)
    row_slow = lax.broadcasted_iota(I32, (SLOT_WINDOW, SLOT_TILE), 0)
    windows = [_block_windows(cum_ref, first_expert, n_e, t * n_blk + jj) for jj in range(n_blk)]

    def hits(jj, e, row_i):
        toks = slice(jj * SLOT_TILE, (jj + 1) * SLOT_TILE)
        hit = (row_i + windows[jj][0][e]) == rank_ref[e:e + 1, toks]
        return jnp.where(hit, 1.0, 0.0).astype(BF16)

    def finish(jj, moe):
        toks = slice(jj * SLOT_TILE, (jj + 1) * SLOT_TILE)
        y_ref[toks, :] = _rms(x1_ref[toks, :] + moe, g_ref[...])

    def fast(jj):
        rhs = rhs_ref.at[jj % rhs_ref.shape[0]]
        for e in range(n_e):
            rhs[e * FAST_WINDOW:(e + 1) * FAST_WINDOW, :] = (
                ye_ref[e, pl.ds(windows[jj][0][e], FAST_WINDOW), :])
        stack = jnp.concatenate([hits(jj, e, row_fast) for e in range(n_e)], axis=0)
        finish(jj, _dot_tn(stack, rhs[...]))

    def slow(jj):
        acc = None
        for e in range(n_e):
            part = _dot_tn(hits(jj, e, row_slow),
                           ye_ref[e, pl.ds(windows[jj][0][e], SLOT_WINDOW), :])
            acc = part if acc is None else acc + part
        finish(jj, acc)

    _run_blocks([fits for _, fits in windows], fast, slow)


def _combine(cum_flat, rank, ye, x1, g):
    b, s, d = x1.shape
    n_e, ye_rows = ye.shape[1], ye.shape[2]
    row = pl.BlockSpec((None, ROW_TILE, d), lambda bi, t, c: (bi, t, 0))
    grid_spec = pltpu.PrefetchScalarGridSpec(
        num_scalar_prefetch=1,
        grid=(b, s // ROW_TILE),
        in_specs=[
            pl.BlockSpec((None, n_e, ROW_TILE), lambda bi, t, c: (bi, 0, t)),
            pl.BlockSpec((None, n_e, ye_rows, d), lambda bi, t, c: (bi, 0, 0, 0)),
            row,
            pl.BlockSpec(g.shape, lambda bi, t, c: (0, 0)),
        ],
        out_specs=row,
        scratch_shapes=[pltpu.VMEM((2, n_e * FAST_WINDOW, d), BF16)],
    )
    return pl.pallas_call(
        _combine_kernel,
        grid_spec=grid_spec,
        out_shape=jax.ShapeDtypeStruct((b, s, d), F32),
        compiler_params=_cparams(("arbitrary", "arbitrary"), 60),
        name="combine",
    )(cum_flat, rank, ye, x1, g)


def _moe_stages(aff, h2, x1, wg, wu, wd, g_final, cap):
    b, n_e, s = aff.shape
    rank, cum = _topk(aff, cap)
    cum_flat = cum.reshape(-1)
    xe, gate = _gather(cum_flat, rank, aff, h2, cap)
    ye = _moe_ffn(xe, gate, wg, wu, wd, cap)
    return _combine(cum_flat, rank, ye, x1, g_final)


def kernel(x, norm_mix_g, w_in, b_gate, gmlp_norm_g, w_spatial, b_spatial, w_proj_a, w_proj_b,
           w_out, norm_ffn_g, w_router, w_e_gate, w_e_up, w_e_down, norm_final_g):
    b, s, d = x.shape
    assert w_in.shape[0] == 1, "single-layer block"
    cap = CAPACITY_FACTOR * s // N_EXPERTS
    group_width = GMLP_WIDTH // GMLP_GROUPS
    ws_pairs = w_spatial[0].astype(BF16).reshape(GMLP_GROUPS // 2, 2 * CHUNK, CHUNK)
    bsp = jnp.repeat(b_spatial[0].T, group_width, axis=1)
    qkv, ta, gb = _mix_in(x, norm_mix_g, w_in[0], b_gate, gmlp_norm_g, ws_pairs, bsp,
                          w_proj_a[0].astype(BF16))
    os_, ls_ = [], []
    for (q, k, v), dil in zip(qkv, DILATIONS):
        o, lse = _attn_pattern(q, k, v, dil)
        os_.append(o)
        ls_.append(lse)
    x1, h2, aff = _mix_out(x, ta, gb, os_, ls_, w_proj_b[0], w_out[0], norm_ffn_g, w_router[0].T)
    return _moe_stages(aff, h2, x1, w_e_gate[0], w_e_up[0], w_e_down[0], norm_final_g[None], cap)
```

```python
import functools

import jax
import jax.numpy as jnp
from jax import lax
from jax.experimental import pallas as pl
from jax.experimental.pallas import tpu as pltpu

F32 = jnp.float32
BF16 = jnp.bfloat16
I32 = jnp.int32

EPS = 1e-6
GMLP_WIDTH = 512
GMLP_GROUPS = 8
CHUNK = 128
N_HEADS = 8
HEAD_DIM = 64
ATTN_WIDTH = N_HEADS * HEAD_DIM
DILATIONS = (1, 4, 16)
DILATION_STEP = 4
assert all(b == a * DILATION_STEP for a, b in zip(DILATIONS, DILATIONS[1:]))
HALF_WINDOW = 64
N_EXPERTS = 16
CAPACITY_FACTOR = 2

LANES = 128
Q_TILE = 128
KEY_TILE = 2 * Q_TILE
ATTN_STEP_ROWS = 2048
STAT_LANES = LANES // N_HEADS
DEN_SHIFT = STAT_LANES // 2
SLOT_TILE = 128
SLOT_ALIGN = 16
SLOT_WINDOW = SLOT_TILE + SLOT_ALIGN
GATHER_GROUPS = 2
COMBINE_ROWS = 512
FFN_SEQS = 2
GATHER_STEP_TOKENS = 1024
GATHER_TOKENS = 256
GATHER_FAST = 80
GATHER_SLOW = GATHER_TOKENS + SLOT_ALIGN
FAST_WINDOW = 48
ROW_TILE = 512
SUB_ROWS = 512
MIB = 1024 * 1024


def _cparams(sem, vmem_mib):
    return pltpu.CompilerParams(dimension_semantics=sem, vmem_limit_bytes=vmem_mib * MIB)


def _gelu_tanh(x):
    return 0.5 * x * (1.0 + jnp.tanh(0.7978845608028654 * (x + 0.044715 * (x * x * x))))


def _sigmoid(x):
    return 1.0 / (1.0 + jnp.exp(-x))


def _rms(x, g):
    return x * lax.rsqrt(jnp.mean(x * x, axis=-1, keepdims=True) + EPS) * g


def _dot(a, b):
    return jnp.dot(a, b, preferred_element_type=F32)


def _dot_nt(a, b):
    return lax.dot_general(a, b, (((1,), (1,)), ((), ())), preferred_element_type=F32)


def _dot_tn(a, b):
    return lax.dot_general(a, b, (((0,), (0,)), ((), ())), preferred_element_type=F32)


def _mix_in_kernel(x_ref, g_ref, win_ref, bg_ref, g2_ref, ws_ref, bsp_ref, pa_ref, *refs):
    n_qkv = 3 * len(DILATIONS)
    qkv_refs = refs[:n_qkv]
    ta_ref, gb_ref = refs[n_qkv:n_qkv + 2]
    win_bf = refs[n_qkv + 2]
    stage_refs = refs[n_qkv + 3:]

    @pl.when(jnp.logical_and(pl.program_id(0) == 0, pl.program_id(1) == 0))
    def _():
        for c in range(0, win_ref.shape[1], GMLP_WIDTH):
            win_bf[:, c:c + GMLP_WIDTH] = win_ref[:, c:c + GMLP_WIDTH].astype(BF16)

    for sub in range(x_ref.shape[0] // SUB_ROWS):
        mine = stage_refs[6 * sub:6 * sub + 6]
        _mix_in_rows(sub, x_ref, g_ref, win_bf, bg_ref, g2_ref, ws_ref, bsp_ref, pa_ref,
                     qkv_refs, ta_ref, gb_ref, list(zip(mine[0::2], mine[1::2])))


def _mix_in_rows(sub, x_ref, g_ref, win_ref, bg_ref, g2_ref, ws_ref, bsp_ref, pa_ref,
                 qkv_refs, ta_ref, gb_ref, stage_refs):
    rows, d_model = SUB_ROWS, x_ref.shape[1]
    rs_tile = slice(sub * rows, (sub + 1) * rows)
    h = _rms(x_ref[rs_tile, :], g_ref[...]).astype(BF16)

    def proj(lo, width):
        return _dot(h, win_ref[:, lo:lo + width])

    c0 = 0
    u = _gelu_tanh(proj(c0, GMLP_WIDTH)); c0 += GMLP_WIDTH
    v = _gelu_tanh(proj(c0, GMLP_WIDTH)); c0 += GMLP_WIDTH
    for i in range(3):
        val = proj(c0, ATTN_WIDTH); c0 += ATTN_WIDTH
        if i == 0:
            val = val * (HEAD_DIM ** -0.5)
        qkv_refs[i][0, rs_tile, :] = val.astype(BF16)
        stage1, stage2 = stage_refs[i]
        n4, n16 = rows // DILATIONS[1], rows // DILATIONS[2]
        out4, out16 = qkv_refs[3 + i], qkv_refs[6 + i]
        for p in range(ATTN_WIDTH // LANES):
            cs = slice(p * LANES, (p + 1) * LANES)
            stage1[p] = val[:, cs]
            for r4 in range(DILATION_STEP):
                part = stage1[p, pl.ds(r4, n4, stride=DILATION_STEP), :]
                out4[r4, sub * n4:(sub + 1) * n4, cs] = part.astype(BF16)
                stage2[p, r4] = part
                for c in range(DILATION_STEP):
                    out16[r4 + DILATION_STEP * c, sub * n16:(sub + 1) * n16, cs] = (
                        stage2[p, r4, pl.ds(c, n16, stride=DILATION_STEP), :].astype(BF16))
    ga = _sigmoid(proj(c0, d_model) + bg_ref[:, :d_model]); c0 += d_model
    gb = _sigmoid(proj(c0, d_model) + bg_ref[:, d_model:])
    gb_ref[rs_tile, :] = gb.astype(BF16)

    vn = _rms(v, g2_ref[...]).astype(BF16)
    lane_lo = lax.broadcasted_iota(I32, (CHUNK, LANES), 1) < HEAD_DIM
    bsp = bsp_ref[...]
    n_chunk = rows // CHUNK
    mixed_slabs = []
    for p in range(GMLP_WIDTH // LANES):
        slab = jnp.concatenate(
            [vn[c * CHUNK:(c + 1) * CHUNK, p * LANES:(p + 1) * LANES] for c in range(n_chunk)],
            axis=1)
        r = _dot(ws_ref[p], slab)
        mixed_slabs.append([jnp.where(lane_lo, r[:CHUNK, c * LANES:(c + 1) * LANES],
                                      r[CHUNK:, c * LANES:(c + 1) * LANES])
                            for c in range(n_chunk)])
    a_chunks = []
    for c in range(n_chunk):
        rs = slice(c * CHUNK, (c + 1) * CHUNK)
        mixed = jnp.concatenate([slabs[c] for slabs in mixed_slabs], axis=1) + bsp
        a_chunks.append((u[rs] * mixed).astype(BF16))
    a = jnp.concatenate(a_chunks, axis=0)
    ta_ref[rs_tile, :] = (ga * _dot(a, pa_ref[...])).astype(BF16)


def _mix_in(x, g, w_in, b_gate, g2, ws_pairs, bsp, w_pa):
    b, s, d = x.shape
    const = lambda shape: pl.BlockSpec(shape, lambda bi, t: (0,) * len(shape))
    row = lambda w: pl.BlockSpec((None, ROW_TILE, w), lambda bi, t: (bi, t, 0))
    qkv_specs, qkv_shapes = [], []
    for dil in DILATIONS:
        spec = pl.BlockSpec((None, dil, ROW_TILE // dil, ATTN_WIDTH), lambda bi, t: (bi, 0, t, 0))
        qkv_specs += [spec] * 3
        qkv_shapes += [jax.ShapeDtypeStruct((b, dil, s // dil, ATTN_WIDTH), BF16)] * 3
    outs = pl.pallas_call(
        _mix_in_kernel,
        grid=(b, s // ROW_TILE),
        in_specs=[row(d), const(g.shape),
                  pl.BlockSpec(w_in.shape, lambda bi, t: (0, 0), pipeline_mode=pl.Buffered(1)),
                  const(b_gate.shape), const(g2.shape),
                  const(ws_pairs.shape), const(bsp.shape), const(w_pa.shape)],
        out_specs=qkv_specs + [row(d), row(d)],
        out_shape=qkv_shapes + [jax.ShapeDtypeStruct((b, s, d), BF16)] * 2,
        scratch_shapes=[pltpu.VMEM(w_in.shape, BF16)]
                       + [pltpu.VMEM((ATTN_WIDTH // LANES, SUB_ROWS, LANES), F32),
                          pltpu.VMEM((ATTN_WIDTH // LANES, DILATION_STEP,
                                      SUB_ROWS // DILATION_STEP, LANES), F32)]
                       * (3 * (ROW_TILE // SUB_ROWS)),
        compiler_params=_cparams(("arbitrary", "arbitrary"), 60),
        name="mix_in",
    )(x, g, w_in, b_gate, g2, ws_pairs, bsp, w_pa)
    n_qkv = 3 * len(DILATIONS)
    qkv = [outs[3 * i:3 * i + 3] for i in range(len(DILATIONS))]
    return qkv, outs[n_qkv], outs[n_qkv + 1]


def _attn_kernel(q_ref, k_ref, v_ref, o_ref, l_ref, bias_ref, *, dil):
    n_res, rows, _ = q_ref.shape
    seq = k_ref.shape[1]
    t = pl.program_id(2)
    first = jnp.logical_and(jnp.logical_and(pl.program_id(0) == 0, pl.program_id(1) == 0), t == 0)

    @pl.when(first)
    def _():
        ii = lax.broadcasted_iota(I32, (Q_TILE, KEY_TILE), 0)
        jj = lax.broadcasted_iota(I32, (Q_TILE, KEY_TILE), 1)
        for var in range(3):
            absd = jnp.abs(jj - ii - var * HALF_WINDOW)
            valid = absd <= HALF_WINDOW
            absf = absd.astype(F32)
            for h in range(N_HEADS):
                slope = 2.0 ** (-8.0 * (h + 1) / N_HEADS)
                bias_ref[var, h] = jnp.where(valid, -(slope * dil) * absf, -jnp.inf)

    lane = lax.broadcasted_iota(I32, (Q_TILE, LANES), 1)
    lane_lo = lane < HEAD_DIM
    mask_lo = jnp.where(lane_lo, 1.0, 0.0).astype(BF16)
    mask_hi = jnp.where(lane_lo, 0.0, 1.0).astype(BF16)
    for rr in range(n_res):
        for qi in range(rows // Q_TILE):
            rs = slice(qi * Q_TILE, (qi + 1) * Q_TILE)
            i0 = t * rows + qi * Q_TILE
            start = pl.multiple_of(jnp.clip(i0 - HALF_WINDOW, 0, seq - KEY_TILE), HALF_WINDOW)
            var = (i0 - start) // HALF_WINDOW
            for p in range(ATTN_WIDTH // LANES):
                cs = slice(p * LANES, (p + 1) * LANES)
                qp = q_ref[rr, rs, cs]
                kp = k_ref[rr, pl.ds(start, KEY_TILE), cs]
                vp = v_ref[rr, pl.ds(start, KEY_TILE), cs]
                q2 = jnp.concatenate([qp * mask_lo, qp * mask_hi], axis=0)
                s2 = _dot_nt(q2, kp)
                probs = []
                for hh in range(2):
                    h = 2 * p + hh
                    s = s2[hh * Q_TILE:(hh + 1) * Q_TILE] + bias_ref[var, h]
                    m = jnp.max(s, axis=-1, keepdims=True)
                    e = jnp.exp(s - m)
                    den = jnp.sum(e, axis=-1, keepdims=True)
                    probs.append(e)
                    lo = h * STAT_LANES
                    l_ref[rr, rs, lo:lo + DEN_SHIFT] = jnp.broadcast_to(m, (Q_TILE, DEN_SHIFT))
                    l_ref[rr, rs, lo + DEN_SHIFT:lo + STAT_LANES] = jnp.broadcast_to(
                        den, (Q_TILE, DEN_SHIFT))
                o2 = _dot(jnp.concatenate(probs, axis=0).astype(BF16), vp)
                o_ref[rr, rs, cs] = jnp.where(lane_lo, o2[:Q_TILE], o2[Q_TILE:]).astype(BF16)


def _attn_pattern(q, k, v, dil):
    b, _, seq, w = q.shape
    rows = min(seq, ATTN_STEP_ROWS)
    n_res = ATTN_STEP_ROWS // rows
    qspec = lambda width: pl.BlockSpec((None, n_res, rows, width), lambda bi, r, t: (bi, r, t, 0))
    kspec = pl.BlockSpec((None, n_res, seq, w), lambda bi, r, t: (bi, r, 0, 0))
    return pl.pallas_call(
        functools.partial(_attn_kernel, dil=dil),
        grid=(b, dil // n_res, seq // rows),
        in_specs=[qspec(w), kspec, kspec],
        out_specs=[qspec(w), qspec(LANES)],
        out_shape=[jax.ShapeDtypeStruct(q.shape, BF16),
                   jax.ShapeDtypeStruct((b, dil, seq, LANES), F32)],
        scratch_shapes=[pltpu.VMEM((3, N_HEADS, Q_TILE, KEY_TILE), F32)],
        compiler_params=_cparams(("arbitrary", "arbitrary", "arbitrary"), 48),
        name=f"attn_d{dil}",
    )(q, k, v)


def _to_natural(src_ref, nat_ref, tmp_ref, rows):
    n_slab = nat_ref.shape[0]
    step = DILATION_STEP
    for p in range(n_slab):
        cs = slice(p * LANES, (p + 1) * LANES)
        for r4 in range(step):
            if tmp_ref is None:
                quarter = src_ref[r4][:, cs].astype(F32)
            else:
                for c in range(step):
                    tmp_ref[p, r4, pl.ds(c, rows // (step * step), stride=step), :] = (
                        src_ref[r4 + step * c][:, cs].astype(F32))
                quarter = tmp_ref[p, r4]
            nat_ref[p, pl.ds(r4, rows // step, stride=step), :] = quarter
    return jnp.concatenate([nat_ref[p] for p in range(n_slab)], axis=1)


def _mix_out_kernel(x_ref, ta_ref, gb_ref, *refs):
    n_pat = len(DILATIONS)
    o_refs = refs[:n_pat]
    l_refs = refs[n_pat:2 * n_pat]
    pb_f32, wo_f32, g_ref, wr_ref, x1_ref, h2_ref, aff_ref = refs[2 * n_pat:2 * n_pat + 7]
    pb_ref, wo_ref = refs[2 * n_pat + 7:2 * n_pat + 9]
    stage_refs = refs[2 * n_pat + 9:]
    rows = x_ref.shape[0]

    @pl.when(jnp.logical_and(pl.program_id(0) == 0, pl.program_id(1) == 0))
    def _():
        pb_ref[...] = pb_f32[...].astype(BF16)
        wo_ref[...] = wo_f32[...].astype(BF16)

    stage_refs = list(stage_refs)
    outs = [o_refs[0][0].astype(F32)]
    lses = [l_refs[0][0]]
    for di in (1, 2):
        for src, dest in ((o_refs[di], outs), (l_refs[di], lses)):
            nat = stage_refs.pop(0)
            tmp = stage_refs.pop(0) if di == 2 else None
            dest.append(_to_natural(src, nat, tmp, rows))

    dens = [pltpu.roll(st, LANES - DEN_SHIFT, 1) for st in lses]
    lses = [st + jnp.log(den) for st, den in zip(lses, dens)]
    m = functools.reduce(jnp.maximum, lses)
    ws = [jnp.exp(l - m) for l in lses]
    inv = 1.0 / functools.reduce(lambda a, c: a + c, ws)
    lane = lax.broadcasted_iota(I32, (rows, LANES), 1)
    used = lane % STAT_LANES < DEN_SHIFT
    ws = [jnp.where(used, w * inv / den, 0.0) for w, den in zip(ws, dens)]
    k_i = lax.broadcasted_iota(I32, (LANES, ATTN_WIDTH), 0)
    c_i = lax.broadcasted_iota(I32, (LANES, ATTN_WIDTH), 1)
    spread = jnp.where(k_i == (c_i // HEAD_DIM) * STAT_LANES, 1.0, 0.0).astype(BF16)
    o = None
    for w, o_p in zip(ws, outs):
        w_hi = w.astype(BF16)
        w_lo = (w - w_hi.astype(F32)).astype(BF16)
        term = (_dot(w_hi, spread) + _dot(w_lo, spread)) * o_p
        o = term if o is None else o + term

    ob = _dot(o.astype(BF16), pb_ref[...])
    merged = (ta_ref[...].astype(F32) + gb_ref[...].astype(F32) * ob).astype(BF16)
    x1 = x_ref[...] + _dot(merged, wo_ref[...])
    x1_ref[...] = x1
    h2 = _rms(x1, g_ref[...])
    h2_ref[...] = h2.astype(BF16)
    h_hi = h2.astype(BF16)
    h_lo = (h2 - h_hi.astype(F32)).astype(BF16)
    wr = wr_ref[...]
    w_hi = wr.astype(BF16)
    w_lo = (wr - w_hi.astype(F32)).astype(BF16)
    n_e = wr.shape[0]
    by_hi = _dot_nt(jnp.concatenate([w_hi, w_lo], axis=0), h_hi)
    logits = by_hi[:n_e] + (_dot_nt(w_hi, h_lo) + by_hi[n_e:])
    e = jnp.exp(logits - jnp.max(logits, axis=0, keepdims=True))
    aff_ref[...] = e / jnp.sum(e, axis=0, keepdims=True)


def _mix_out(x, ta, gb, os_, ls_, w_pb, w_out, g, w_router_t):
    b, s, d = x.shape
    n_e = w_router_t.shape[0]
    const = lambda shape: pl.BlockSpec(shape, lambda bi, t: (0,) * len(shape))
    once = lambda shape: pl.BlockSpec(shape, lambda bi, t: (0,) * len(shape),
                                      pipeline_mode=pl.Buffered(1))
    row = lambda w: pl.BlockSpec((None, ROW_TILE, w), lambda bi, t: (bi, t, 0))
    res = lambda dil, w: pl.BlockSpec((None, dil, ROW_TILE // dil, w), lambda bi, t: (bi, 0, t, 0))
    stage = []
    for di in (1, 2):
        for slabs in (ATTN_WIDTH // LANES, 1):
            stage.append(pltpu.VMEM((slabs, ROW_TILE, LANES), F32))
            if di == 2:
                stage.append(pltpu.VMEM((slabs, DILATION_STEP, ROW_TILE // DILATION_STEP, LANES),
                                        F32))
    return pl.pallas_call(
        _mix_out_kernel,
        grid=(b, s // ROW_TILE),
        in_specs=[row(d), row(d), row(d)]
                 + [res(dil, ATTN_WIDTH) for dil in DILATIONS]
                 + [res(dil, LANES) for dil in DILATIONS]
                 + [once(w_pb.shape), once(w_out.shape), const(g.shape), const(w_router_t.shape)],
        out_specs=[row(d), row(d), pl.BlockSpec((None, n_e, ROW_TILE), lambda bi, t: (bi, 0, t))],
        out_shape=[jax.ShapeDtypeStruct((b, s, d), F32), jax.ShapeDtypeStruct((b, s, d), BF16),
                   jax.ShapeDtypeStruct((b, n_e, s), F32)],
        scratch_shapes=[pltpu.VMEM(w_pb.shape, BF16), pltpu.VMEM(w_out.shape, BF16)] + stage,
        compiler_params=_cparams(("arbitrary", "arbitrary"), 48),
        name="mix_out",
    )(x, ta, gb, *os_, *ls_, w_pb, w_out, g, w_router_t)


def _topk_kernel(aff_ref, rank_ref, cum_ref, *, cap):
    n_e, s = aff_ref.shape
    n_blk = s // SLOT_TILE
    aff = aff_ref[...]
    thr = jnp.zeros((n_e, 1), I32)
    for bit in range(30, -1, -1):
        cand = thr | (1 << bit)
        cnt = jnp.sum((aff >= pltpu.bitcast(cand, F32)).astype(I32), axis=1, keepdims=True)
        thr = jnp.where(cnt >= cap, cand, thr)
    above = aff >= pltpu.bitcast(thr + 1, F32)
    tie = jnp.logical_and(aff >= pltpu.bitcast(thr, F32), jnp.logical_not(above))
    need = (cap - jnp.sum(above.astype(I32), axis=1, keepdims=True)).astype(F32)
    r_i = lax.broadcasted_iota(I32, (SLOT_TILE, SLOT_TILE), 0)
    c_i = lax.broadcasted_iota(I32, (SLOT_TILE, SLOT_TILE), 1)
    tri = jnp.where(r_i < c_i, 1.0, 0.0).astype(BF16)
    lane = lax.broadcasted_iota(I32, (n_e, LANES), 1)
    run_tie = jnp.zeros((n_e, 1), F32)
    run_sel = jnp.zeros((n_e, 1), F32)
    cum = jnp.zeros((n_e, LANES), F32)
    for j in range(n_blk):
        cs = slice(j * SLOT_TILE, (j + 1) * SLOT_TILE)
        tie_f = jnp.where(tie[:, cs], 1.0, 0.0)
        tie_rank = _dot(tie_f.astype(BF16), tri) + run_tie
        run_tie = run_tie + jnp.sum(tie_f, axis=1, keepdims=True)
        sel_f = jnp.where(above[:, cs], 1.0, jnp.where(tie_rank < need, tie_f, 0.0))
        rank = _dot(sel_f.astype(BF16), tri) + run_sel
        rank_ref[:, cs] = jnp.where(sel_f > 0.0, rank, -1.0).astype(I32)
        cum = jnp.where(lane == j, run_sel, cum)
        run_sel = run_sel + jnp.sum(sel_f, axis=1, keepdims=True)
    cum = jnp.where(lane == n_blk, run_sel, cum)
    cum_ref[...] = cum.astype(I32)


def _topk(aff, cap):
    b, n_e, s = aff.shape
    rows = b * n_e
    rank, cum = pl.pallas_call(
        functools.partial(_topk_kernel, cap=cap),
        grid=(1,),
        in_specs=[pl.BlockSpec((rows, s), lambda i: (0, 0))],
        out_specs=[pl.BlockSpec((rows, s), lambda i: (0, 0)),
                   pl.BlockSpec((rows, LANES), lambda i: (0, 0))],
        out_shape=[jax.ShapeDtypeStruct((rows, s), I32),
                   jax.ShapeDtypeStruct((rows, LANES), I32)],
        compiler_params=_cparams(("arbitrary",), 32),
        name="topk",
    )(aff.reshape(rows, s))
    return rank.reshape(b, n_e, s), cum.reshape(b, n_e, LANES)


def _slot_window(first):
    return pl.multiple_of((first // SLOT_ALIGN) * SLOT_ALIGN, SLOT_ALIGN)


def _window(lo, width, total):
    assert (total - width) % SLOT_ALIGN == 0
    return pl.multiple_of(jnp.minimum(lo, total - width), SLOT_ALIGN)


def _block_windows(cum_ref, first_expert, n_e, j, tiles=1, fast=FAST_WINDOW):
    los, fits = [], None
    for e in range(n_e):
        at = (first_expert + e) * LANES + j * tiles
        lo = _slot_window(cum_ref[at])
        ok = cum_ref[at + tiles] - lo <= fast
        los.append(lo)
        fits = ok if fits is None else jnp.logical_and(fits, ok)
    return los, fits


def _run_blocks(fits, fast, slow):
    all_fit = functools.reduce(jnp.logical_and, fits)

    @pl.when(all_fit)
    def _():
        for jj in range(len(fits)):
            fast(jj)

    @pl.when(jnp.logical_not(all_fit))
    def _():
        for jj, fit in enumerate(fits):
            pl.when(fit)(functools.partial(fast, jj))
            pl.when(jnp.logical_not(fit))(functools.partial(slow, jj))


def _gather_kernel(cum_ref, rank_ref, aff_ref, h2_ref, xe_ref, gate_ref):
    n_e = rank_ref.shape[0]
    rows = h2_ref.shape[0]
    first_expert = (pl.program_id(0) * pl.num_programs(1) + pl.program_id(1)) * n_e
    t = pl.program_id(2)

    @pl.when(t == 0)
    def _():
        xe_ref[...] = jnp.zeros_like(xe_ref)
        gate_ref[...] = jnp.zeros_like(gate_ref)

    row_fast = lax.broadcasted_iota(I32, (GATHER_FAST, GATHER_TOKENS), 0)
    row_slow = lax.broadcasted_iota(I32, (GATHER_SLOW, GATHER_TOKENS), 0)
    n_blk = rows // GATHER_TOKENS
    windows = [_block_windows(cum_ref, first_expert, n_e, t * n_blk + jj,
                              tiles=GATHER_TOKENS // SLOT_TILE, fast=GATHER_FAST)
               for jj in range(n_blk)]

    cap = xe_ref.shape[1]

    def hit(jj, e, row_i):
        toks = slice(jj * GATHER_TOKENS, (jj + 1) * GATHER_TOKENS)
        lo = _window(windows[jj][0][e], row_i.shape[0], cap)
        return lo, (row_i + lo) == rank_ref[e:e + 1, toks]

    def add_window(jj, e, lo, hit_e, rows_e):
        toks = slice(jj * GATHER_TOKENS, (jj + 1) * GATHER_TOKENS)
        win = pl.ds(lo, hit_e.shape[0])
        xe_ref[e, win, :] += rows_e.astype(BF16)
        gate_ref[e, win, :] += jnp.sum(jnp.where(hit_e, aff_ref[e:e + 1, toks], 0.0), axis=1,
                                       keepdims=True)

    def fast(jj):
        hs = [hit(jj, e, row_fast) for e in range(n_e)]
        stack = jnp.concatenate([jnp.where(h, 1.0, 0.0).astype(BF16) for _, h in hs], axis=0)
        res = _dot(stack, h2_ref[jj * GATHER_TOKENS:(jj + 1) * GATHER_TOKENS, :])
        for e, (lo, h) in enumerate(hs):
            add_window(jj, e, lo, h, res[e * GATHER_FAST:(e + 1) * GATHER_FAST])

    def slow(jj):
        for e in range(n_e):
            lo, h = hit(jj, e, row_slow)
            add_window(jj, e, lo, h,
                       _dot(jnp.where(h, 1.0, 0.0).astype(BF16),
                            h2_ref[jj * GATHER_TOKENS:(jj + 1) * GATHER_TOKENS, :]))

    _run_blocks([fits for _, fits in windows], fast, slow)


def _gather(cum_flat, rank, aff, h2, cap):
    b, s, d = h2.shape
    n_e = rank.shape[1]
    rows = cap
    grp = n_e // GATHER_GROUPS
    per_tok = pl.BlockSpec((None, grp, GATHER_STEP_TOKENS), lambda bi, eg, t, c: (bi, eg, t))
    whole = lambda w: pl.BlockSpec((None, grp, rows, w), lambda bi, eg, t, c: (bi, eg, 0, 0))
    grid_spec = pltpu.PrefetchScalarGridSpec(
        num_scalar_prefetch=1,
        grid=(b, GATHER_GROUPS, s // GATHER_STEP_TOKENS),
        in_specs=[per_tok, per_tok,
                  pl.BlockSpec((None, GATHER_STEP_TOKENS, d), lambda bi, eg, t, c: (bi, t, 0))],
        out_specs=[whole(d), whole(LANES)],
    )
    return pl.pallas_call(
        _gather_kernel,
        grid_spec=grid_spec,
        out_shape=[jax.ShapeDtypeStruct((b, n_e, rows, d), BF16),
                   jax.ShapeDtypeStruct((b, n_e, rows, LANES), F32)],
        compiler_params=_cparams(("arbitrary", "arbitrary", "arbitrary"), 56),
        name="gather",
    )(cum_flat, rank, aff, h2)


def _moe_ffn_kernel(xe_ref, gate_ref, wg_ref, wu_ref, wd_ref, ye_ref, wg_bf, wu_bf, wd_bf):
    n_seq, cap, d = xe_ref.shape

    @pl.when(pl.program_id(1) == 0)
    def _():
        wg_bf[...] = wg_ref[...].astype(BF16)
        wu_bf[...] = wu_ref[...].astype(BF16)
        wd_bf[...] = wd_ref[...].astype(BF16)

    xe = xe_ref[...].reshape(n_seq * cap, d)
    gate_h = _dot(xe, wg_bf[...])
    up_h = _dot(xe, wu_bf[...])
    hidden = (gate_h * _sigmoid(gate_h) * up_h).astype(BF16)
    ye = _dot(hidden, wd_bf[...]) * gate_ref[...].reshape(n_seq * cap, LANES)[:, 0:1]
    ye_ref[...] = ye.astype(BF16).reshape(n_seq, cap, d)


def _moe_ffn(xe, gate, wg, wu, wd, cap):
    b, n_e, rows, d = xe.shape
    hid = wg.shape[2]
    return pl.pallas_call(
        _moe_ffn_kernel,
        grid=(n_e, b // FFN_SEQS),
        in_specs=[
            pl.BlockSpec((FFN_SEQS, None, cap, d), lambda e, bi: (bi, e, 0, 0)),
            pl.BlockSpec((FFN_SEQS, None, cap, LANES), lambda e, bi: (bi, e, 0, 0)),
            pl.BlockSpec((None, d, hid), lambda e, bi: (e, 0, 0)),
            pl.BlockSpec((None, d, hid), lambda e, bi: (e, 0, 0)),
            pl.BlockSpec((None, hid, d), lambda e, bi: (e, 0, 0)),
        ],
        out_specs=pl.BlockSpec((FFN_SEQS, None, rows, d), lambda e, bi: (bi, e, 0, 0)),
        out_shape=jax.ShapeDtypeStruct((b, n_e, rows, d), BF16),
        scratch_shapes=[pltpu.VMEM((d, hid), BF16), pltpu.VMEM((d, hid), BF16),
                        pltpu.VMEM((hid, d), BF16)],
        compiler_params=_cparams(("arbitrary", "arbitrary"), 56),
        name="moe_ffn",
    )(xe, gate, wg, wu, wd)


def _combine_kernel(cum_ref, rank_ref, ye_ref, x1_ref, g_ref, y_ref, rhs_ref):
    n_e = rank_ref.shape[0]
    n_blk = x1_ref.shape[0] // SLOT_TILE
    first_expert = pl.program_id(0) * n_e
    t = pl.program_id(1)
    row_fast = lax.broadcasted_iota(I32, (FAST_WINDOW, SLOT_TILE), 0)
    row_slow = lax.broadcasted_iota(I32, (SLOT_WINDOW, SLOT_TILE), 0)
    windows = [_block_windows(cum_ref, first_expert, n_e, t * n_blk + jj) for jj in range(n_blk)]

    cap = ye_ref.shape[1]

    def hits(jj, e, row_i):
        toks = slice(jj * SLOT_TILE, (jj + 1) * SLOT_TILE)
        lo = _window(windows[jj][0][e], row_i.shape[0], cap)
        hit = (row_i + lo) == rank_ref[e:e + 1, toks]
        return pl.ds(lo, row_i.shape[0]), jnp.where(hit, 1.0, 0.0).astype(BF16)

    def finish(jj, moe):
        toks = slice(jj * SLOT_TILE, (jj + 1) * SLOT_TILE)
        y_ref[toks, :] = _rms(x1_ref[toks, :] + moe, g_ref[...])

    def fast(jj):
        rhs = rhs_ref.at[jj % rhs_ref.shape[0]]
        stack = []
        for e in range(n_e):
            win, hit = hits(jj, e, row_fast)
            rhs[e * FAST_WINDOW:(e + 1) * FAST_WINDOW, :] = ye_ref[e, win, :]
            stack.append(hit)
        finish(jj, _dot_tn(jnp.concatenate(stack, axis=0), rhs[...]))

    def slow(jj):
        acc = None
        for e in range(n_e):
            win, hit = hits(jj, e, row_slow)
            part = _dot_tn(hit, ye_ref[e, win, :])
            acc = part if acc is None else acc + part
        finish(jj, acc)

    _run_blocks([fits for _, fits in windows], fast, slow)


def _combine(cum_flat, rank, ye, x1, g):
    b, s, d = x1.shape
    n_e, ye_rows = ye.shape[1], ye.shape[2]
    row = pl.BlockSpec((None, COMBINE_ROWS, d), lambda bi, t, c: (bi, t, 0))
    grid_spec = pltpu.PrefetchScalarGridSpec(
        num_scalar_prefetch=1,
        grid=(b, s // COMBINE_ROWS),
        in_specs=[
            pl.BlockSpec((None, n_e, COMBINE_ROWS), lambda bi, t, c: (bi, 0, t)),
            pl.BlockSpec((None, n_e, ye_rows, d), lambda bi, t, c: (bi, 0, 0, 0)),
            row,
            pl.BlockSpec(g.shape, lambda bi, t, c: (0, 0)),
        ],
        out_specs=row,
        scratch_shapes=[pltpu.VMEM((2, n_e * FAST_WINDOW, d), BF16)],
    )
    return pl.pallas_call(
        _combine_kernel,
        grid_spec=grid_spec,
        out_shape=jax.ShapeDtypeStruct((b, s, d), F32),
        compiler_params=_cparams(("arbitrary", "arbitrary"), 60),
        name="combine",
    )(cum_flat, rank, ye, x1, g)


def _moe_stages(aff, h2, x1, wg, wu, wd, g_final, cap):
    b, n_e, s = aff.shape
    rank, cum = _topk(aff, cap)
    cum_flat = cum.reshape(-1)
    xe, gate = _gather(cum_flat, rank, aff, h2, cap)
    ye = _moe_ffn(xe, gate, wg, wu, wd, cap)
    return _combine(cum_flat, rank, ye, x1, g_final)


def kernel(x, norm_mix_g, w_in, b_gate, gmlp_norm_g, w_spatial, b_spatial, w_proj_a, w_proj_b,
           w_out, norm_ffn_g, w_router, w_e_gate, w_e_up, w_e_down, norm_final_g):
    b, s, d = x.shape
    assert w_in.shape[0] == 1, "single-layer block"
    cap = CAPACITY_FACTOR * s // N_EXPERTS
    group_width = GMLP_WIDTH // GMLP_GROUPS
    ws_pairs = w_spatial[0].astype(BF16).reshape(GMLP_GROUPS // 2, 2 * CHUNK, CHUNK)
    bsp = jnp.repeat(b_spatial[0].T, group_width, axis=1)
    qkv, ta, gb = _mix_in(x, norm_mix_g, w_in[0], b_gate, gmlp_norm_g, ws_pairs, bsp,
                          w_proj_a[0].astype(BF16))
    os_, ls_ = [], []
    for (q, k, v), dil in zip(qkv, DILATIONS):
        o, lse = _attn_pattern(q, k, v, dil)
        os_.append(o)
        ls_.append(lse)
    x1, h2, aff = _mix_out(x, ta, gb, os_, ls_, w_proj_b[0], w_out[0], norm_ffn_g, w_router[0].T)
    return _moe_stages(aff, h2, x1, w_e_gate[0], w_e_up[0], w_e_down[0], norm_final_g[None], cap)
```

```python
import functools

from typing import NamedTuple

import jax
import jax.numpy as jnp
from jax import lax
from jax.experimental import pallas as pl
from jax.experimental.pallas import tpu as pltpu

F32 = jnp.float32
BF16 = jnp.bfloat16
I32 = jnp.int32

EPS = 1e-6
GMLP_WIDTH = 512
GMLP_GROUPS = 8
CHUNK = 128
N_HEADS = 8
HEAD_DIM = 64
ATTN_WIDTH = N_HEADS * HEAD_DIM
DILATIONS = (1, 4, 16)
DILATION_STEP = 4
assert all(b == a * DILATION_STEP for a, b in zip(DILATIONS, DILATIONS[1:]))
HALF_WINDOW = 64
N_EXPERTS = 16
CAPACITY_FACTOR = 2

LANES = 128
Q_TILE = 128
KEY_TILE = 2 * Q_TILE
ATTN_STEP_ROWS = 2048
STAT_LANES = LANES // N_HEADS
DEN_SHIFT = STAT_LANES // 2
SLOT_TILE = 128
SLOT_ALIGN = 16
COMBINE_ROWS = 512
FFN_SEQS = 2
GATHER_STEP_TOKENS = 1024
GATHER_TOKENS = 256
ROW_TILE = 512
SUB_ROWS = 512
MIB = 1024 * 1024


def _cparams(sem, vmem_mib):
    return pltpu.CompilerParams(dimension_semantics=sem, vmem_limit_bytes=vmem_mib * MIB)


def _gelu_tanh(x):
    return 0.5 * x * (1.0 + jnp.tanh(0.7978845608028654 * (x + 0.044715 * (x * x * x))))


def _sigmoid(x):
    return 1.0 / (1.0 + jnp.exp(-x))


def _rms(x, g):
    return x * lax.rsqrt(jnp.mean(x * x, axis=-1, keepdims=True) + EPS) * g


def _dot(a, b):
    return jnp.dot(a, b, preferred_element_type=F32)


def _dot_nt(a, b):
    return lax.dot_general(a, b, (((1,), (1,)), ((), ())), preferred_element_type=F32)


def _dot_tn(a, b):
    return lax.dot_general(a, b, (((0,), (0,)), ((), ())), preferred_element_type=F32)


def _mix_in_kernel(x_ref, g_ref, win_ref, bg_ref, g2_ref, ws_ref, bsp_ref, pa_ref, *refs):
    n_qkv = 3 * len(DILATIONS)
    qkv_refs = refs[:n_qkv]
    ta_ref, gb_ref = refs[n_qkv:n_qkv + 2]
    win_bf = refs[n_qkv + 2]
    stage_refs = refs[n_qkv + 3:]

    @pl.when(jnp.logical_and(pl.program_id(0) == 0, pl.program_id(1) == 0))
    def _():
        for c in range(0, win_ref.shape[1], GMLP_WIDTH):
            win_bf[:, c:c + GMLP_WIDTH] = win_ref[:, c:c + GMLP_WIDTH].astype(BF16)

    for sub in range(x_ref.shape[0] // SUB_ROWS):
        mine = stage_refs[6 * sub:6 * sub + 6]
        _mix_in_rows(sub, x_ref, g_ref, win_bf, bg_ref, g2_ref, ws_ref, bsp_ref, pa_ref,
                     qkv_refs, ta_ref, gb_ref, list(zip(mine[0::2], mine[1::2])))


def _mix_in_rows(sub, x_ref, g_ref, win_ref, bg_ref, g2_ref, ws_ref, bsp_ref, pa_ref,
                 qkv_refs, ta_ref, gb_ref, stage_refs):
    rows, d_model = SUB_ROWS, x_ref.shape[1]
    rs_tile = slice(sub * rows, (sub + 1) * rows)
    h = _rms(x_ref[rs_tile, :], g_ref[...]).astype(BF16)

    def proj(lo, width):
        return _dot(h, win_ref[:, lo:lo + width])

    c0 = 0
    u = _gelu_tanh(proj(c0, GMLP_WIDTH)); c0 += GMLP_WIDTH
    v = _gelu_tanh(proj(c0, GMLP_WIDTH)); c0 += GMLP_WIDTH
    for i in range(3):
        val = proj(c0, ATTN_WIDTH); c0 += ATTN_WIDTH
        if i == 0:
            val = val * (HEAD_DIM ** -0.5)
        qkv_refs[i][0, rs_tile, :] = val.astype(BF16)
        stage1, stage2 = stage_refs[i]
        n4, n16 = rows // DILATIONS[1], rows // DILATIONS[2]
        out4, out16 = qkv_refs[3 + i], qkv_refs[6 + i]
        for p in range(ATTN_WIDTH // LANES):
            cs = slice(p * LANES, (p + 1) * LANES)
            stage1[p] = val[:, cs]
            for r4 in range(DILATION_STEP):
                part = stage1[p, pl.ds(r4, n4, stride=DILATION_STEP), :]
                out4[r4, sub * n4:(sub + 1) * n4, cs] = part.astype(BF16)
                stage2[p, r4] = part
                for c in range(DILATION_STEP):
                    out16[r4 + DILATION_STEP * c, sub * n16:(sub + 1) * n16, cs] = (
                        stage2[p, r4, pl.ds(c, n16, stride=DILATION_STEP), :].astype(BF16))
    ga = _sigmoid(proj(c0, d_model) + bg_ref[:, :d_model]); c0 += d_model
    gb = _sigmoid(proj(c0, d_model) + bg_ref[:, d_model:])
    gb_ref[rs_tile, :] = gb.astype(BF16)

    vn = _rms(v, g2_ref[...]).astype(BF16)
    lane_lo = lax.broadcasted_iota(I32, (CHUNK, LANES), 1) < HEAD_DIM
    bsp = bsp_ref[...]
    n_chunk = rows // CHUNK
    mixed_slabs = []
    for p in range(GMLP_WIDTH // LANES):
        slab = jnp.concatenate(
            [vn[c * CHUNK:(c + 1) * CHUNK, p * LANES:(p + 1) * LANES] for c in range(n_chunk)],
            axis=1)
        r = _dot(ws_ref[p], slab)
        mixed_slabs.append([jnp.where(lane_lo, r[:CHUNK, c * LANES:(c + 1) * LANES],
                                      r[CHUNK:, c * LANES:(c + 1) * LANES])
                            for c in range(n_chunk)])
    a_chunks = []
    for c in range(n_chunk):
        rs = slice(c * CHUNK, (c + 1) * CHUNK)
        mixed = jnp.concatenate([slabs[c] for slabs in mixed_slabs], axis=1) + bsp
        a_chunks.append((u[rs] * mixed).astype(BF16))
    a = jnp.concatenate(a_chunks, axis=0)
    ta_ref[rs_tile, :] = (ga * _dot(a, pa_ref[...])).astype(BF16)


def _mix_in(x, g, w_in, b_gate, g2, ws_pairs, bsp, w_pa):
    b, s, d = x.shape
    const = lambda shape: pl.BlockSpec(shape, lambda bi, t: (0,) * len(shape))
    row = lambda w: pl.BlockSpec((None, ROW_TILE, w), lambda bi, t: (bi, t, 0))
    qkv_specs, qkv_shapes = [], []
    for dil in DILATIONS:
        spec = pl.BlockSpec((None, dil, ROW_TILE // dil, ATTN_WIDTH), lambda bi, t: (bi, 0, t, 0))
        qkv_specs += [spec] * 3
        qkv_shapes += [jax.ShapeDtypeStruct((b, dil, s // dil, ATTN_WIDTH), BF16)] * 3
    outs = pl.pallas_call(
        _mix_in_kernel,
        grid=(b, s // ROW_TILE),
        in_specs=[row(d), const(g.shape),
                  pl.BlockSpec(w_in.shape, lambda bi, t: (0, 0), pipeline_mode=pl.Buffered(1)),
                  const(b_gate.shape), const(g2.shape),
                  const(ws_pairs.shape), const(bsp.shape), const(w_pa.shape)],
        out_specs=qkv_specs + [row(d), row(d)],
        out_shape=qkv_shapes + [jax.ShapeDtypeStruct((b, s, d), BF16)] * 2,
        scratch_shapes=[pltpu.VMEM(w_in.shape, BF16)]
                       + [pltpu.VMEM((ATTN_WIDTH // LANES, SUB_ROWS, LANES), F32),
                          pltpu.VMEM((ATTN_WIDTH // LANES, DILATION_STEP,
                                      SUB_ROWS // DILATION_STEP, LANES), F32)]
                       * (3 * (ROW_TILE // SUB_ROWS)),
        compiler_params=_cparams(("arbitrary", "arbitrary"), 60),
        name="mix_in",
    )(x, g, w_in, b_gate, g2, ws_pairs, bsp, w_pa)
    n_qkv = 3 * len(DILATIONS)
    qkv = [outs[3 * i:3 * i + 3] for i in range(len(DILATIONS))]
    return qkv, outs[n_qkv], outs[n_qkv + 1]


def _attn_kernel(q_ref, k_ref, v_ref, o_ref, l_ref, bias_ref, *, dil):
    n_res, rows, _ = q_ref.shape
    seq = k_ref.shape[1]
    t = pl.program_id(2)
    first = jnp.logical_and(jnp.logical_and(pl.program_id(0) == 0, pl.program_id(1) == 0), t == 0)

    @pl.when(first)
    def _():
        ii = lax.broadcasted_iota(I32, (Q_TILE, KEY_TILE), 0)
        jj = lax.broadcasted_iota(I32, (Q_TILE, KEY_TILE), 1)
        for var in range(3):
            absd = jnp.abs(jj - ii - var * HALF_WINDOW)
            valid = absd <= HALF_WINDOW
            absf = absd.astype(F32)
            for h in range(N_HEADS):
                slope = 2.0 ** (-8.0 * (h + 1) / N_HEADS)
                bias_ref[var, h] = jnp.where(valid, -(slope * dil) * absf, -jnp.inf)

    lane = lax.broadcasted_iota(I32, (Q_TILE, LANES), 1)
    lane_lo = lane < HEAD_DIM
    mask_lo = jnp.where(lane_lo, 1.0, 0.0).astype(BF16)
    mask_hi = jnp.where(lane_lo, 0.0, 1.0).astype(BF16)
    for rr in range(n_res):
        for qi in range(rows // Q_TILE):
            rs = slice(qi * Q_TILE, (qi + 1) * Q_TILE)
            i0 = t * rows + qi * Q_TILE
            start = pl.multiple_of(jnp.clip(i0 - HALF_WINDOW, 0, seq - KEY_TILE), HALF_WINDOW)
            var = (i0 - start) // HALF_WINDOW
            for p in range(ATTN_WIDTH // LANES):
                cs = slice(p * LANES, (p + 1) * LANES)
                qp = q_ref[rr, rs, cs]
                kp = k_ref[rr, pl.ds(start, KEY_TILE), cs]
                vp = v_ref[rr, pl.ds(start, KEY_TILE), cs]
                q2 = jnp.concatenate([qp * mask_lo, qp * mask_hi], axis=0)
                s2 = _dot_nt(q2, kp)
                probs = []
                for hh in range(2):
                    h = 2 * p + hh
                    s = s2[hh * Q_TILE:(hh + 1) * Q_TILE] + bias_ref[var, h]
                    m = jnp.max(s, axis=-1, keepdims=True)
                    e = jnp.exp(s - m)
                    den = jnp.sum(e, axis=-1, keepdims=True)
                    probs.append(e)
                    lo = h * STAT_LANES
                    l_ref[rr, rs, lo:lo + DEN_SHIFT] = jnp.broadcast_to(m, (Q_TILE, DEN_SHIFT))
                    l_ref[rr, rs, lo + DEN_SHIFT:lo + STAT_LANES] = jnp.broadcast_to(
                        den, (Q_TILE, DEN_SHIFT))
                o2 = _dot(jnp.concatenate(probs, axis=0).astype(BF16), vp)
                o_ref[rr, rs, cs] = jnp.where(lane_lo, o2[:Q_TILE], o2[Q_TILE:]).astype(BF16)


def _attn_pattern(q, k, v, dil):
    b, _, seq, w = q.shape
    rows = min(seq, ATTN_STEP_ROWS)
    n_res = ATTN_STEP_ROWS // rows
    qspec = lambda width: pl.BlockSpec((None, n_res, rows, width), lambda bi, r, t: (bi, r, t, 0))
    kspec = pl.BlockSpec((None, n_res, seq, w), lambda bi, r, t: (bi, r, 0, 0))
    return pl.pallas_call(
        functools.partial(_attn_kernel, dil=dil),
        grid=(b, dil // n_res, seq // rows),
        in_specs=[qspec(w), kspec, kspec],
        out_specs=[qspec(w), qspec(LANES)],
        out_shape=[jax.ShapeDtypeStruct(q.shape, BF16),
                   jax.ShapeDtypeStruct((b, dil, seq, LANES), F32)],
        scratch_shapes=[pltpu.VMEM((3, N_HEADS, Q_TILE, KEY_TILE), F32)],
        compiler_params=_cparams(("arbitrary", "arbitrary", "arbitrary"), 48),
        name=f"attn_d{dil}",
    )(q, k, v)


def _to_natural(src_ref, nat_ref, tmp_ref, rows):
    n_slab = nat_ref.shape[0]
    step = DILATION_STEP
    for p in range(n_slab):
        cs = slice(p * LANES, (p + 1) * LANES)
        for r4 in range(step):
            if tmp_ref is None:
                quarter = src_ref[r4][:, cs].astype(F32)
            else:
                for c in range(step):
                    tmp_ref[p, r4, pl.ds(c, rows // (step * step), stride=step), :] = (
                        src_ref[r4 + step * c][:, cs].astype(F32))
                quarter = tmp_ref[p, r4]
            nat_ref[p, pl.ds(r4, rows // step, stride=step), :] = quarter
    return jnp.concatenate([nat_ref[p] for p in range(n_slab)], axis=1)


def _mix_out_kernel(x_ref, ta_ref, gb_ref, *refs):
    n_pat = len(DILATIONS)
    o_refs = refs[:n_pat]
    l_refs = refs[n_pat:2 * n_pat]
    pb_f32, wo_f32, g_ref, wr_ref, x1_ref, h2_ref, aff_ref = refs[2 * n_pat:2 * n_pat + 7]
    pb_ref, wo_ref = refs[2 * n_pat + 7:2 * n_pat + 9]
    stage_refs = refs[2 * n_pat + 9:]
    rows = x_ref.shape[0]

    @pl.when(jnp.logical_and(pl.program_id(0) == 0, pl.program_id(1) == 0))
    def _():
        pb_ref[...] = pb_f32[...].astype(BF16)
        wo_ref[...] = wo_f32[...].astype(BF16)

    stage_refs = list(stage_refs)
    outs = [o_refs[0][0].astype(F32)]
    lses = [l_refs[0][0]]
    for di in (1, 2):
        for src, dest in ((o_refs[di], outs), (l_refs[di], lses)):
            nat = stage_refs.pop(0)
            tmp = stage_refs.pop(0) if di == 2 else None
            dest.append(_to_natural(src, nat, tmp, rows))

    dens = [pltpu.roll(st, LANES - DEN_SHIFT, 1) for st in lses]
    lses = [st + jnp.log(den) for st, den in zip(lses, dens)]
    m = functools.reduce(jnp.maximum, lses)
    ws = [jnp.exp(l - m) for l in lses]
    inv = 1.0 / functools.reduce(lambda a, c: a + c, ws)
    lane = lax.broadcasted_iota(I32, (rows, LANES), 1)
    used = lane % STAT_LANES < DEN_SHIFT
    ws = [jnp.where(used, w * inv / den, 0.0) for w, den in zip(ws, dens)]
    k_i = lax.broadcasted_iota(I32, (LANES, ATTN_WIDTH), 0)
    c_i = lax.broadcasted_iota(I32, (LANES, ATTN_WIDTH), 1)
    spread = jnp.where(k_i == (c_i // HEAD_DIM) * STAT_LANES, 1.0, 0.0).astype(BF16)
    o = None
    for w, o_p in zip(ws, outs):
        w_hi = w.astype(BF16)
        w_lo = (w - w_hi.astype(F32)).astype(BF16)
        term = (_dot(w_hi, spread) + _dot(w_lo, spread)) * o_p
        o = term if o is None else o + term

    ob = _dot(o.astype(BF16), pb_ref[...])
    merged = (ta_ref[...].astype(F32) + gb_ref[...].astype(F32) * ob).astype(BF16)
    x1 = x_ref[...] + _dot(merged, wo_ref[...])
    x1_ref[...] = x1
    h2 = _rms(x1, g_ref[...])
    h2_ref[...] = h2.astype(BF16)
    h_hi = h2.astype(BF16)
    h_lo = (h2 - h_hi.astype(F32)).astype(BF16)
    wr = wr_ref[...]
    w_hi = wr.astype(BF16)
    w_lo = (wr - w_hi.astype(F32)).astype(BF16)
    n_e = wr.shape[0]
    by_hi = _dot_nt(jnp.concatenate([w_hi, w_lo], axis=0), h_hi)
    logits = by_hi[:n_e] + (_dot_nt(w_hi, h_lo) + by_hi[n_e:])
    e = jnp.exp(logits - jnp.max(logits, axis=0, keepdims=True))
    aff_ref[...] = e / jnp.sum(e, axis=0, keepdims=True)


def _mix_out(x, ta, gb, os_, ls_, w_pb, w_out, g, w_router_t):
    b, s, d = x.shape
    n_e = w_router_t.shape[0]
    const = lambda shape: pl.BlockSpec(shape, lambda bi, t: (0,) * len(shape))
    once = lambda shape: pl.BlockSpec(shape, lambda bi, t: (0,) * len(shape),
                                      pipeline_mode=pl.Buffered(1))
    row = lambda w: pl.BlockSpec((None, ROW_TILE, w), lambda bi, t: (bi, t, 0))
    res = lambda dil, w: pl.BlockSpec((None, dil, ROW_TILE // dil, w), lambda bi, t: (bi, 0, t, 0))
    stage = []
    for di in (1, 2):
        for slabs in (ATTN_WIDTH // LANES, 1):
            stage.append(pltpu.VMEM((slabs, ROW_TILE, LANES), F32))
            if di == 2:
                stage.append(pltpu.VMEM((slabs, DILATION_STEP, ROW_TILE // DILATION_STEP, LANES),
                                        F32))
    return pl.pallas_call(
        _mix_out_kernel,
        grid=(b, s // ROW_TILE),
        in_specs=[row(d), row(d), row(d)]
                 + [res(dil, ATTN_WIDTH) for dil in DILATIONS]
                 + [res(dil, LANES) for dil in DILATIONS]
                 + [once(w_pb.shape), once(w_out.shape), const(g.shape), const(w_router_t.shape)],
        out_specs=[row(d), row(d), pl.BlockSpec((None, n_e, ROW_TILE), lambda bi, t: (bi, 0, t))],
        out_shape=[jax.ShapeDtypeStruct((b, s, d), F32), jax.ShapeDtypeStruct((b, s, d), BF16),
                   jax.ShapeDtypeStruct((b, n_e, s), F32)],
        scratch_shapes=[pltpu.VMEM(w_pb.shape, BF16), pltpu.VMEM(w_out.shape, BF16)] + stage,
        compiler_params=_cparams(("arbitrary", "arbitrary"), 48),
        name="mix_out",
    )(x, ta, gb, *os_, *ls_, w_pb, w_out, g, w_router_t)


class _Plan(NamedTuple):
    tiles: int
    fast: int
    slow: int
    group: int


GATHER_PLAN = _Plan(tiles=GATHER_TOKENS // SLOT_TILE, fast=80, slow=GATHER_TOKENS + SLOT_ALIGN,
                    group=N_EXPERTS // 2)
COMBINE_PLAN = _Plan(tiles=1, fast=48, slow=SLOT_TILE + SLOT_ALIGN, group=N_EXPERTS)
FAST_WINDOW, SLOT_WINDOW = COMBINE_PLAN.fast, COMBINE_PLAN.slow


def _slot_plan(cum, plan, cap):
    rows = cum.shape[0]
    nxt = pltpu.roll(cum, LANES - plan.tiles, 1)
    lo = jnp.floor(cum * (1.0 / SLOT_ALIGN)) * SLOT_ALIGN
    ok = jnp.where(nxt - lo <= plan.fast, 1.0, 0.0)
    fit = jnp.min(ok.reshape(rows // plan.group, plan.group, LANES), axis=1)
    assert (cap - plan.fast) % SLOT_ALIGN == 0 and (cap - plan.slow) % SLOT_ALIGN == 0
    return (jnp.minimum(lo, cap - plan.fast).astype(I32),
            jnp.minimum(lo, cap - plan.slow).astype(I32), fit.astype(I32))


def _topk_kernel(aff_ref, rank_ref, *plan_refs, cap, plans):
    n_e, s = aff_ref.shape
    n_blk = s // SLOT_TILE
    aff = aff_ref[...]
    thr = jnp.zeros((n_e, 1), I32)
    for bit in range(30, -1, -1):
        cand = thr | (1 << bit)
        cnt = jnp.sum((aff >= pltpu.bitcast(cand, F32)).astype(I32), axis=1, keepdims=True)
        thr = jnp.where(cnt >= cap, cand, thr)
    above = aff >= pltpu.bitcast(thr + 1, F32)
    tie = jnp.logical_and(aff >= pltpu.bitcast(thr, F32), jnp.logical_not(above))
    need = (cap - jnp.sum(above.astype(I32), axis=1, keepdims=True)).astype(F32)
    r_i = lax.broadcasted_iota(I32, (SLOT_TILE, SLOT_TILE), 0)
    c_i = lax.broadcasted_iota(I32, (SLOT_TILE, SLOT_TILE), 1)
    tri = jnp.where(r_i < c_i, 1.0, 0.0).astype(BF16)
    lane = lax.broadcasted_iota(I32, (n_e, LANES), 1)
    run_tie = jnp.zeros((n_e, 1), F32)
    run_sel = jnp.zeros((n_e, 1), F32)
    cum = jnp.zeros((n_e, LANES), F32)
    for j in range(n_blk):
        cs = slice(j * SLOT_TILE, (j + 1) * SLOT_TILE)
        tie_f = jnp.where(tie[:, cs], 1.0, 0.0)
        tie_rank = _dot(tie_f.astype(BF16), tri) + run_tie
        run_tie = run_tie + jnp.sum(tie_f, axis=1, keepdims=True)
        sel_f = jnp.where(above[:, cs], 1.0, jnp.where(tie_rank < need, tie_f, 0.0))
        rank = _dot(sel_f.astype(BF16), tri) + run_sel
        rank_ref[:, cs] = jnp.where(sel_f > 0.0, rank, -1.0).astype(I32)
        cum = jnp.where(lane == j, run_sel, cum)
        run_sel = run_sel + jnp.sum(sel_f, axis=1, keepdims=True)
    cum = jnp.where(lane == n_blk, run_sel, cum)
    for i, plan in enumerate(plans):
        for ref, val in zip(plan_refs[3 * i:3 * i + 3], _slot_plan(cum, plan, cap)):
            ref[...] = val


def _topk(aff, cap, plans):
    b, n_e, s = aff.shape
    rows = b * n_e
    full = lambda r, w: pl.BlockSpec((r, w), lambda i: (0, 0))
    plan_specs, plan_shapes = [], []
    for plan in plans:
        for r in (rows, rows, rows // plan.group):
            plan_specs.append(full(r, LANES))
            plan_shapes.append(jax.ShapeDtypeStruct((r, LANES), I32))
    rank, *tables = pl.pallas_call(
        functools.partial(_topk_kernel, cap=cap, plans=plans),
        grid=(1,),
        in_specs=[full(rows, s)],
        out_specs=[full(rows, s)] + plan_specs,
        out_shape=[jax.ShapeDtypeStruct((rows, s), I32)] + plan_shapes,
        compiler_params=_cparams(("arbitrary",), 32),
        name="topk",
    )(aff.reshape(rows, s))
    tables = [t.reshape(-1) for t in tables]
    return rank.reshape(b, n_e, s), [tables[3 * i:3 * i + 3] for i in range(len(plans))]


def _window_start(table_ref, expert_row, tile):
    return pl.multiple_of(table_ref[expert_row * LANES + tile], SLOT_ALIGN)


def _run_blocks(fits, fast, slow):
    all_fit = functools.reduce(jnp.logical_and, fits)

    @pl.when(all_fit)
    def _():
        for jj in range(len(fits)):
            fast(jj)

    @pl.when(jnp.logical_not(all_fit))
    def _():
        for jj, fit in enumerate(fits):
            pl.when(fit)(functools.partial(fast, jj))
            pl.when(jnp.logical_not(fit))(functools.partial(slow, jj))


def _gather_kernel(fast_ref, slow_ref, fit_ref, rank_ref, aff_ref, h2_ref, xe_ref, gate_ref):
    n_e = rank_ref.shape[0]
    rows = h2_ref.shape[0]
    group = pl.program_id(0) * pl.num_programs(1) + pl.program_id(1)
    t = pl.program_id(2)

    @pl.when(t == 0)
    def _():
        xe_ref[...] = jnp.zeros_like(xe_ref)
        gate_ref[...] = jnp.zeros_like(gate_ref)

    row_fast = lax.broadcasted_iota(I32, (GATHER_PLAN.fast, GATHER_TOKENS), 0)
    row_slow = lax.broadcasted_iota(I32, (GATHER_PLAN.slow, GATHER_TOKENS), 0)
    n_blk = rows // GATHER_TOKENS
    tile = lambda jj: (t * n_blk + jj) * GATHER_PLAN.tiles

    def hit(jj, e, row_i):
        toks = slice(jj * GATHER_TOKENS, (jj + 1) * GATHER_TOKENS)
        table = fast_ref if row_i is row_fast else slow_ref
        lo = _window_start(table, group * n_e + e, tile(jj))
        return lo, (row_i + lo) == rank_ref[e:e + 1, toks]

    def add_window(jj, e, lo, hit_e, rows_e):
        toks = slice(jj * GATHER_TOKENS, (jj + 1) * GATHER_TOKENS)
        win = pl.ds(lo, hit_e.shape[0])
        xe_ref[e, win, :] += rows_e.astype(BF16)
        gate_ref[e, win, :] += jnp.sum(jnp.where(hit_e, aff_ref[e:e + 1, toks], 0.0), axis=1,
                                       keepdims=True)

    def fast(jj):
        hs = [hit(jj, e, row_fast) for e in range(n_e)]
        stack = jnp.concatenate([jnp.where(h, 1.0, 0.0).astype(BF16) for _, h in hs], axis=0)
        res = _dot(stack, h2_ref[jj * GATHER_TOKENS:(jj + 1) * GATHER_TOKENS, :])
        for e, (lo, h) in enumerate(hs):
            add_window(jj, e, lo, h, res[e * GATHER_PLAN.fast:(e + 1) * GATHER_PLAN.fast])

    def slow(jj):
        for e in range(n_e):
            lo, h = hit(jj, e, row_slow)
            add_window(jj, e, lo, h,
                       _dot(jnp.where(h, 1.0, 0.0).astype(BF16),
                            h2_ref[jj * GATHER_TOKENS:(jj + 1) * GATHER_TOKENS, :]))

    _run_blocks([fit_ref[group * LANES + tile(jj)] != 0 for jj in range(n_blk)], fast, slow)


def _gather(plan_tables, rank, aff, h2, cap):
    b, s, d = h2.shape
    n_e = rank.shape[1]
    rows = cap
    grp = GATHER_PLAN.group
    per_tok = pl.BlockSpec((None, grp, GATHER_STEP_TOKENS), lambda bi, eg, t, *_: (bi, eg, t))
    whole = lambda w: pl.BlockSpec((None, grp, rows, w), lambda bi, eg, t, *_: (bi, eg, 0, 0))
    grid_spec = pltpu.PrefetchScalarGridSpec(
        num_scalar_prefetch=3,
        grid=(b, n_e // grp, s // GATHER_STEP_TOKENS),
        in_specs=[per_tok, per_tok,
                  pl.BlockSpec((None, GATHER_STEP_TOKENS, d), lambda bi, eg, t, *_: (bi, t, 0))],
        out_specs=[whole(d), whole(LANES)],
    )
    return pl.pallas_call(
        _gather_kernel,
        grid_spec=grid_spec,
        out_shape=[jax.ShapeDtypeStruct((b, n_e, rows, d), BF16),
                   jax.ShapeDtypeStruct((b, n_e, rows, LANES), F32)],
        compiler_params=_cparams(("arbitrary", "arbitrary", "arbitrary"), 56),
        name="gather",
    )(*plan_tables, rank, aff, h2)


def _moe_ffn_kernel(xe_ref, gate_ref, wg_ref, wu_ref, wd_ref, ye_ref, wg_bf, wu_bf, wd_bf):
    n_seq, cap, d = xe_ref.shape

    @pl.when(pl.program_id(1) == 0)
    def _():
        wg_bf[...] = wg_ref[...].astype(BF16)
        wu_bf[...] = wu_ref[...].astype(BF16)
        wd_bf[...] = wd_ref[...].astype(BF16)

    xe = xe_ref[...].reshape(n_seq * cap, d)
    gate_h = _dot(xe, wg_bf[...])
    up_h = _dot(xe, wu_bf[...])
    hidden = (gate_h * _sigmoid(gate_h) * up_h).astype(BF16)
    ye = _dot(hidden, wd_bf[...]) * gate_ref[...].reshape(n_seq * cap, LANES)[:, 0:1]
    ye_ref[...] = ye.astype(BF16).reshape(n_seq, cap, d)


def _moe_ffn(xe, gate, wg, wu, wd, cap):
    b, n_e, rows, d = xe.shape
    hid = wg.shape[2]
    return pl.pallas_call(
        _moe_ffn_kernel,
        grid=(n_e, b // FFN_SEQS),
        in_specs=[
            pl.BlockSpec((FFN_SEQS, None, cap, d), lambda e, bi: (bi, e, 0, 0)),
            pl.BlockSpec((FFN_SEQS, None, cap, LANES), lambda e, bi: (bi, e, 0, 0)),
            pl.BlockSpec((None, d, hid), lambda e, bi: (e, 0, 0)),
            pl.BlockSpec((None, d, hid), lambda e, bi: (e, 0, 0)),
            pl.BlockSpec((None, hid, d), lambda e, bi: (e, 0, 0)),
        ],
        out_specs=pl.BlockSpec((FFN_SEQS, None, rows, d), lambda e, bi: (bi, e, 0, 0)),
        out_shape=jax.ShapeDtypeStruct((b, n_e, rows, d), BF16),
        scratch_shapes=[pltpu.VMEM((d, hid), BF16), pltpu.VMEM((d, hid), BF16),
                        pltpu.VMEM((hid, d), BF16)],
        compiler_params=_cparams(("arbitrary", "arbitrary"), 56),
        name="moe_ffn",
    )(xe, gate, wg, wu, wd)


def _combine_kernel(fast_ref, slow_ref, fit_ref, rank_ref, ye_ref, x1_ref, g_ref, y_ref, rhs_ref):
    n_e = rank_ref.shape[0]
    n_blk = x1_ref.shape[0] // SLOT_TILE
    bi = pl.program_id(0)
    t = pl.program_id(1)
    row_fast = lax.broadcasted_iota(I32, (FAST_WINDOW, SLOT_TILE), 0)
    row_slow = lax.broadcasted_iota(I32, (SLOT_WINDOW, SLOT_TILE), 0)

    def hits(jj, e, row_i):
        toks = slice(jj * SLOT_TILE, (jj + 1) * SLOT_TILE)
        table = fast_ref if row_i is row_fast else slow_ref
        lo = _window_start(table, bi * n_e + e, t * n_blk + jj)
        hit = (row_i + lo) == rank_ref[e:e + 1, toks]
        return pl.ds(lo, row_i.shape[0]), jnp.where(hit, 1.0, 0.0).astype(BF16)

    def finish(jj, moe):
        toks = slice(jj * SLOT_TILE, (jj + 1) * SLOT_TILE)
        y_ref[toks, :] = _rms(x1_ref[toks, :] + moe, g_ref[...])

    def fast(jj):
        rhs = rhs_ref.at[jj % rhs_ref.shape[0]]
        stack = []
        for e in range(n_e):
            win, hit = hits(jj, e, row_fast)
            rhs[e * FAST_WINDOW:(e + 1) * FAST_WINDOW, :] = ye_ref[e, win, :]
            stack.append(hit)
        finish(jj, _dot_tn(jnp.concatenate(stack, axis=0), rhs[...]))

    def slow(jj):
        acc = None
        for e in range(n_e):
            win, hit = hits(jj, e, row_slow)
            part = _dot_tn(hit, ye_ref[e, win, :])
            acc = part if acc is None else acc + part
        finish(jj, acc)

    _run_blocks([fit_ref[bi * LANES + t * n_blk + jj] != 0 for jj in range(n_blk)], fast, slow)


def _combine(plan_tables, rank, ye, x1, g):
    b, s, d = x1.shape
    n_e, ye_rows = ye.shape[1], ye.shape[2]
    row = pl.BlockSpec((None, COMBINE_ROWS, d), lambda bi, t, *_: (bi, t, 0))
    grid_spec = pltpu.PrefetchScalarGridSpec(
        num_scalar_prefetch=3,
        grid=(b, s // COMBINE_ROWS),
        in_specs=[
            pl.BlockSpec((None, n_e, COMBINE_ROWS), lambda bi, t, *_: (bi, 0, t)),
            pl.BlockSpec((None, n_e, ye_rows, d), lambda bi, t, *_: (bi, 0, 0, 0)),
            row,
            pl.BlockSpec(g.shape, lambda bi, t, *_: (0, 0)),
        ],
        out_specs=row,
        scratch_shapes=[pltpu.VMEM((2, n_e * FAST_WINDOW, d), BF16)],
    )
    return pl.pallas_call(
        _combine_kernel,
        grid_spec=grid_spec,
        out_shape=jax.ShapeDtypeStruct((b, s, d), F32),
        compiler_params=_cparams(("arbitrary", "arbitrary"), 60),
        name="combine",
    )(*plan_tables, rank, ye, x1, g)


def _moe_stages(aff, h2, x1, wg, wu, wd, g_final, cap):
    rank, (gather_tables, combine_tables) = _topk(aff, cap, (GATHER_PLAN, COMBINE_PLAN))
    xe, gate = _gather(gather_tables, rank, aff, h2, cap)
    ye = _moe_ffn(xe, gate, wg, wu, wd, cap)
    return _combine(combine_tables, rank, ye, x1, g_final)


def kernel(x, norm_mix_g, w_in, b_gate, gmlp_norm_g, w_spatial, b_spatial, w_proj_a, w_proj_b,
           w_out, norm_ffn_g, w_router, w_e_gate, w_e_up, w_e_down, norm_final_g):
    b, s, d = x.shape
    assert w_in.shape[0] == 1, "single-layer block"
    cap = CAPACITY_FACTOR * s // N_EXPERTS
    group_width = GMLP_WIDTH // GMLP_GROUPS
    ws_pairs = w_spatial[0].astype(BF16).reshape(GMLP_GROUPS // 2, 2 * CHUNK, CHUNK)
    bsp = jnp.repeat(b_spatial[0].T, group_width, axis=1)
    qkv, ta, gb = _mix_in(x, norm_mix_g, w_in[0], b_gate, gmlp_norm_g, ws_pairs, bsp,
                          w_proj_a[0].astype(BF16))
    os_, ls_ = [], []
    for (q, k, v), dil in zip(qkv, DILATIONS):
        o, lse = _attn_pattern(q, k, v, dil)
        os_.append(o)
        ls_.append(lse)
    x1, h2, aff = _mix_out(x, ta, gb, os_, ls_, w_proj_b[0], w_out[0], norm_ffn_g, w_router[0].T)
    return _moe_stages(aff, h2, x1, w_e_gate[0], w_e_up[0], w_e_down[0], norm_final_g[None], cap)
```

```python
import functools

from typing import NamedTuple

import jax
import jax.numpy as jnp
from jax import lax
from jax.experimental import pallas as pl
from jax.experimental.pallas import tpu as pltpu

F32 = jnp.float32
BF16 = jnp.bfloat16
I32 = jnp.int32

EPS = 1e-6
GMLP_WIDTH = 512
GMLP_GROUPS = 8
CHUNK = 128
N_HEADS = 8
HEAD_DIM = 64
ATTN_WIDTH = N_HEADS * HEAD_DIM
DILATIONS = (1, 4, 16)
DILATION_STEP = 4
assert all(b == a * DILATION_STEP for a, b in zip(DILATIONS, DILATIONS[1:]))
HALF_WINDOW = 64
N_EXPERTS = 16
CAPACITY_FACTOR = 2

LANES = 128
Q_TILE = 128
KEY_TILE = 2 * Q_TILE
ATTN_STEP_ROWS = 2048
STAT_LANES = LANES // N_HEADS
DEN_SHIFT = STAT_LANES // 2
SLOT_TILE = 128
SLOT_ALIGN = 16
COMBINE_ROWS = 1024
FFN_SEQS = 2
GATHER_STEP_TOKENS = 2048
GATHER_TOKENS = 256
ROW_TILE = 512
SUB_ROWS = 512
MIB = 1024 * 1024


def _cparams(sem, vmem_mib):
    return pltpu.CompilerParams(dimension_semantics=sem, vmem_limit_bytes=vmem_mib * MIB)


def _gelu_tanh(x):
    return 0.5 * x * (1.0 + jnp.tanh(0.7978845608028654 * (x + 0.044715 * (x * x * x))))


def _sigmoid(x):
    return 1.0 / (1.0 + jnp.exp(-x))


def _rms(x, g):
    return x * lax.rsqrt(jnp.mean(x * x, axis=-1, keepdims=True) + EPS) * g


def _dot(a, b):
    return jnp.dot(a, b, preferred_element_type=F32)


def _dot_nt(a, b):
    return lax.dot_general(a, b, (((1,), (1,)), ((), ())), preferred_element_type=F32)


def _dot_tn(a, b):
    return lax.dot_general(a, b, (((0,), (0,)), ((), ())), preferred_element_type=F32)


def _mix_in_kernel(x_ref, g_ref, win_ref, bg_ref, g2_ref, ws_ref, bsp_ref, pa_ref, *refs):
    n_qkv = 3 * len(DILATIONS)
    qkv_refs = refs[:n_qkv]
    ta_ref, gb_ref = refs[n_qkv:n_qkv + 2]
    win_bf = refs[n_qkv + 2]
    stage_refs = refs[n_qkv + 3:]

    @pl.when(jnp.logical_and(pl.program_id(0) == 0, pl.program_id(1) == 0))
    def _():
        for c in range(0, win_ref.shape[1], GMLP_WIDTH):
            win_bf[:, c:c + GMLP_WIDTH] = win_ref[:, c:c + GMLP_WIDTH].astype(BF16)

    for sub in range(x_ref.shape[0] // SUB_ROWS):
        mine = stage_refs[6 * sub:6 * sub + 6]
        _mix_in_rows(sub, x_ref, g_ref, win_bf, bg_ref, g2_ref, ws_ref, bsp_ref, pa_ref,
                     qkv_refs, ta_ref, gb_ref, list(zip(mine[0::2], mine[1::2])))


def _mix_in_rows(sub, x_ref, g_ref, win_ref, bg_ref, g2_ref, ws_ref, bsp_ref, pa_ref,
                 qkv_refs, ta_ref, gb_ref, stage_refs):
    rows, d_model = SUB_ROWS, x_ref.shape[1]
    rs_tile = slice(sub * rows, (sub + 1) * rows)
    h = _rms(x_ref[rs_tile, :], g_ref[...]).astype(BF16)

    def proj(lo, width):
        return _dot(h, win_ref[:, lo:lo + width])

    c0 = 0
    u = _gelu_tanh(proj(c0, GMLP_WIDTH)); c0 += GMLP_WIDTH
    v = _gelu_tanh(proj(c0, GMLP_WIDTH)); c0 += GMLP_WIDTH
    for i in range(3):
        val = proj(c0, ATTN_WIDTH); c0 += ATTN_WIDTH
        if i == 0:
            val = val * (HEAD_DIM ** -0.5)
        qkv_refs[i][0, rs_tile, :] = val.astype(BF16)
        stage1, stage2 = stage_refs[i]
        n4, n16 = rows // DILATIONS[1], rows // DILATIONS[2]
        out4, out16 = qkv_refs[3 + i], qkv_refs[6 + i]
        for p in range(ATTN_WIDTH // LANES):
            cs = slice(p * LANES, (p + 1) * LANES)
            stage1[p] = val[:, cs]
            for r4 in range(DILATION_STEP):
                part = stage1[p, pl.ds(r4, n4, stride=DILATION_STEP), :]
                out4[r4, sub * n4:(sub + 1) * n4, cs] = part.astype(BF16)
                stage2[p, r4] = part
                for c in range(DILATION_STEP):
                    out16[r4 + DILATION_STEP * c, sub * n16:(sub + 1) * n16, cs] = (
                        stage2[p, r4, pl.ds(c, n16, stride=DILATION_STEP), :].astype(BF16))
    ga = _sigmoid(proj(c0, d_model) + bg_ref[:, :d_model]); c0 += d_model
    gb = _sigmoid(proj(c0, d_model) + bg_ref[:, d_model:])
    gb_ref[rs_tile, :] = gb.astype(BF16)

    vn = _rms(v, g2_ref[...]).astype(BF16)
    lane_lo = lax.broadcasted_iota(I32, (CHUNK, LANES), 1) < HEAD_DIM
    bsp = bsp_ref[...]
    n_chunk = rows // CHUNK
    mixed_slabs = []
    for p in range(GMLP_WIDTH // LANES):
        slab = jnp.concatenate(
            [vn[c * CHUNK:(c + 1) * CHUNK, p * LANES:(p + 1) * LANES] for c in range(n_chunk)],
            axis=1)
        r = _dot(ws_ref[p], slab)
        mixed_slabs.append([jnp.where(lane_lo, r[:CHUNK, c * LANES:(c + 1) * LANES],
                                      r[CHUNK:, c * LANES:(c + 1) * LANES])
                            for c in range(n_chunk)])
    a_chunks = []
    for c in range(n_chunk):
        rs = slice(c * CHUNK, (c + 1) * CHUNK)
        mixed = jnp.concatenate([slabs[c] for slabs in mixed_slabs], axis=1) + bsp
        a_chunks.append((u[rs] * mixed).astype(BF16))
    a = jnp.concatenate(a_chunks, axis=0)
    ta_ref[rs_tile, :] = (ga * _dot(a, pa_ref[...])).astype(BF16)


def _mix_in(x, g, w_in, b_gate, g2, ws_pairs, bsp, w_pa):
    b, s, d = x.shape
    const = lambda shape: pl.BlockSpec(shape, lambda bi, t: (0,) * len(shape))
    row = lambda w: pl.BlockSpec((None, ROW_TILE, w), lambda bi, t: (bi, t, 0))
    qkv_specs, qkv_shapes = [], []
    for dil in DILATIONS:
        spec = pl.BlockSpec((None, dil, ROW_TILE // dil, ATTN_WIDTH), lambda bi, t: (bi, 0, t, 0))
        qkv_specs += [spec] * 3
        qkv_shapes += [jax.ShapeDtypeStruct((b, dil, s // dil, ATTN_WIDTH), BF16)] * 3
    outs = pl.pallas_call(
        _mix_in_kernel,
        grid=(b, s // ROW_TILE),
        in_specs=[row(d), const(g.shape),
                  pl.BlockSpec(w_in.shape, lambda bi, t: (0, 0), pipeline_mode=pl.Buffered(1)),
                  const(b_gate.shape), const(g2.shape),
                  const(ws_pairs.shape), const(bsp.shape), const(w_pa.shape)],
        out_specs=qkv_specs + [row(d), row(d)],
        out_shape=qkv_shapes + [jax.ShapeDtypeStruct((b, s, d), BF16)] * 2,
        scratch_shapes=[pltpu.VMEM(w_in.shape, BF16)]
                       + [pltpu.VMEM((ATTN_WIDTH // LANES, SUB_ROWS, LANES), F32),
                          pltpu.VMEM((ATTN_WIDTH // LANES, DILATION_STEP,
                                      SUB_ROWS // DILATION_STEP, LANES), F32)]
                       * (3 * (ROW_TILE // SUB_ROWS)),
        compiler_params=_cparams(("arbitrary", "arbitrary"), 60),
        name="mix_in",
    )(x, g, w_in, b_gate, g2, ws_pairs, bsp, w_pa)
    n_qkv = 3 * len(DILATIONS)
    qkv = [outs[3 * i:3 * i + 3] for i in range(len(DILATIONS))]
    return qkv, outs[n_qkv], outs[n_qkv + 1]


def _attn_kernel(q_ref, k_ref, v_ref, o_ref, l_ref, bias_ref, *, dil):
    n_res, rows, _ = q_ref.shape
    seq = k_ref.shape[1]
    t = pl.program_id(2)
    first = jnp.logical_and(jnp.logical_and(pl.program_id(0) == 0, pl.program_id(1) == 0), t == 0)

    @pl.when(first)
    def _():
        ii = lax.broadcasted_iota(I32, (Q_TILE, KEY_TILE), 0)
        jj = lax.broadcasted_iota(I32, (Q_TILE, KEY_TILE), 1)
        for var in range(3):
            absd = jnp.abs(jj - ii - var * HALF_WINDOW)
            valid = absd <= HALF_WINDOW
            absf = absd.astype(F32)
            for h in range(N_HEADS):
                slope = 2.0 ** (-8.0 * (h + 1) / N_HEADS)
                bias_ref[var, h] = jnp.where(valid, -(slope * dil) * absf, -jnp.inf)

    lane = lax.broadcasted_iota(I32, (Q_TILE, LANES), 1)
    lane_lo = lane < HEAD_DIM
    mask_lo = jnp.where(lane_lo, 1.0, 0.0).astype(BF16)
    mask_hi = jnp.where(lane_lo, 0.0, 1.0).astype(BF16)
    for rr in range(n_res):
        for qi in range(rows // Q_TILE):
            rs = slice(qi * Q_TILE, (qi + 1) * Q_TILE)
            i0 = t * rows + qi * Q_TILE
            start = pl.multiple_of(jnp.clip(i0 - HALF_WINDOW, 0, seq - KEY_TILE), HALF_WINDOW)
            var = (i0 - start) // HALF_WINDOW
            for p in range(ATTN_WIDTH // LANES):
                cs = slice(p * LANES, (p + 1) * LANES)
                qp = q_ref[rr, rs, cs]
                kp = k_ref[rr, pl.ds(start, KEY_TILE), cs]
                vp = v_ref[rr, pl.ds(start, KEY_TILE), cs]
                q2 = jnp.concatenate([qp * mask_lo, qp * mask_hi], axis=0)
                s2 = _dot_nt(q2, kp)
                probs = []
                for hh in range(2):
                    h = 2 * p + hh
                    s = s2[hh * Q_TILE:(hh + 1) * Q_TILE] + bias_ref[var, h]
                    m = jnp.max(s, axis=-1, keepdims=True)
                    e = jnp.exp(s - m)
                    den = jnp.sum(e, axis=-1, keepdims=True)
                    probs.append(e)
                    lo = h * STAT_LANES
                    l_ref[rr, rs, lo:lo + DEN_SHIFT] = jnp.broadcast_to(m, (Q_TILE, DEN_SHIFT))
                    l_ref[rr, rs, lo + DEN_SHIFT:lo + STAT_LANES] = jnp.broadcast_to(
                        den, (Q_TILE, DEN_SHIFT))
                o2 = _dot(jnp.concatenate(probs, axis=0).astype(BF16), vp)
                o_ref[rr, rs, cs] = jnp.where(lane_lo, o2[:Q_TILE], o2[Q_TILE:]).astype(BF16)


def _attn_pattern(q, k, v, dil):
    b, _, seq, w = q.shape
    rows = min(seq, ATTN_STEP_ROWS)
    n_res = ATTN_STEP_ROWS // rows
    qspec = lambda width: pl.BlockSpec((None, n_res, rows, width), lambda bi, r, t: (bi, r, t, 0))
    kspec = pl.BlockSpec((None, n_res, seq, w), lambda bi, r, t: (bi, r, 0, 0))
    return pl.pallas_call(
        functools.partial(_attn_kernel, dil=dil),
        grid=(b, dil // n_res, seq // rows),
        in_specs=[qspec(w), kspec, kspec],
        out_specs=[qspec(w), qspec(LANES)],
        out_shape=[jax.ShapeDtypeStruct(q.shape, BF16),
                   jax.ShapeDtypeStruct((b, dil, seq, LANES), F32)],
        scratch_shapes=[pltpu.VMEM((3, N_HEADS, Q_TILE, KEY_TILE), F32)],
        compiler_params=_cparams(("arbitrary", "arbitrary", "arbitrary"), 48),
        name=f"attn_d{dil}",
    )(q, k, v)


def _to_natural(src_ref, nat_ref, tmp_ref, rows):
    n_slab = nat_ref.shape[0]
    step = DILATION_STEP
    for p in range(n_slab):
        cs = slice(p * LANES, (p + 1) * LANES)
        for r4 in range(step):
            if tmp_ref is None:
                quarter = src_ref[r4][:, cs].astype(F32)
            else:
                for c in range(step):
                    tmp_ref[p, r4, pl.ds(c, rows // (step * step), stride=step), :] = (
                        src_ref[r4 + step * c][:, cs].astype(F32))
                quarter = tmp_ref[p, r4]
            nat_ref[p, pl.ds(r4, rows // step, stride=step), :] = quarter
    return jnp.concatenate([nat_ref[p] for p in range(n_slab)], axis=1)


def _mix_out_kernel(x_ref, ta_ref, gb_ref, *refs):
    n_pat = len(DILATIONS)
    o_refs = refs[:n_pat]
    l_refs = refs[n_pat:2 * n_pat]
    pb_f32, wo_f32, g_ref, wr_ref, x1_ref, h2_ref, aff_ref = refs[2 * n_pat:2 * n_pat + 7]
    pb_ref, wo_ref = refs[2 * n_pat + 7:2 * n_pat + 9]
    stage_refs = refs[2 * n_pat + 9:]
    rows = x_ref.shape[0]

    @pl.when(jnp.logical_and(pl.program_id(0) == 0, pl.program_id(1) == 0))
    def _():
        pb_ref[...] = pb_f32[...].astype(BF16)
        wo_ref[...] = wo_f32[...].astype(BF16)

    stage_refs = list(stage_refs)
    outs = [o_refs[0][0].astype(F32)]
    lses = [l_refs[0][0]]
    for di in (1, 2):
        for src, dest in ((o_refs[di], outs), (l_refs[di], lses)):
            nat = stage_refs.pop(0)
            tmp = stage_refs.pop(0) if di == 2 else None
            dest.append(_to_natural(src, nat, tmp, rows))

    dens = [pltpu.roll(st, LANES - DEN_SHIFT, 1) for st in lses]
    lses = [st + jnp.log(den) for st, den in zip(lses, dens)]
    m = functools.reduce(jnp.maximum, lses)
    ws = [jnp.exp(l - m) for l in lses]
    inv = 1.0 / functools.reduce(lambda a, c: a + c, ws)
    lane = lax.broadcasted_iota(I32, (rows, LANES), 1)
    used = lane % STAT_LANES < DEN_SHIFT
    ws = [jnp.where(used, w * inv / den, 0.0) for w, den in zip(ws, dens)]
    k_i = lax.broadcasted_iota(I32, (LANES, ATTN_WIDTH), 0)
    c_i = lax.broadcasted_iota(I32, (LANES, ATTN_WIDTH), 1)
    spread = jnp.where(k_i == (c_i // HEAD_DIM) * STAT_LANES, 1.0, 0.0).astype(BF16)
    o = None
    for w, o_p in zip(ws, outs):
        w_hi = w.astype(BF16)
        w_lo = (w - w_hi.astype(F32)).astype(BF16)
        term = (_dot(w_hi, spread) + _dot(w_lo, spread)) * o_p
        o = term if o is None else o + term

    ob = _dot(o.astype(BF16), pb_ref[...])
    merged = (ta_ref[...].astype(F32) + gb_ref[...].astype(F32) * ob).astype(BF16)
    x1 = x_ref[...] + _dot(merged, wo_ref[...])
    x1_ref[...] = x1
    h2 = _rms(x1, g_ref[...])
    h2_ref[...] = h2.astype(BF16)
    h_hi = h2.astype(BF16)
    h_lo = (h2 - h_hi.astype(F32)).astype(BF16)
    wr = wr_ref[...]
    w_hi = wr.astype(BF16)
    w_lo = (wr - w_hi.astype(F32)).astype(BF16)
    n_e = wr.shape[0]
    by_hi = _dot_nt(jnp.concatenate([w_hi, w_lo], axis=0), h_hi)
    logits = by_hi[:n_e] + (_dot_nt(w_hi, h_lo) + by_hi[n_e:])
    e = jnp.exp(logits - jnp.max(logits, axis=0, keepdims=True))
    aff_ref[...] = e / jnp.sum(e, axis=0, keepdims=True)


def _mix_out(x, ta, gb, os_, ls_, w_pb, w_out, g, w_router_t):
    b, s, d = x.shape
    n_e = w_router_t.shape[0]
    const = lambda shape: pl.BlockSpec(shape, lambda bi, t: (0,) * len(shape))
    once = lambda shape: pl.BlockSpec(shape, lambda bi, t: (0,) * len(shape),
                                      pipeline_mode=pl.Buffered(1))
    row = lambda w: pl.BlockSpec((None, ROW_TILE, w), lambda bi, t: (bi, t, 0))
    res = lambda dil, w: pl.BlockSpec((None, dil, ROW_TILE // dil, w), lambda bi, t: (bi, 0, t, 0))
    stage = []
    for di in (1, 2):
        for slabs in (ATTN_WIDTH // LANES, 1):
            stage.append(pltpu.VMEM((slabs, ROW_TILE, LANES), F32))
            if di == 2:
                stage.append(pltpu.VMEM((slabs, DILATION_STEP, ROW_TILE // DILATION_STEP, LANES),
                                        F32))
    return pl.pallas_call(
        _mix_out_kernel,
        grid=(b, s // ROW_TILE),
        in_specs=[row(d), row(d), row(d)]
                 + [res(dil, ATTN_WIDTH) for dil in DILATIONS]
                 + [res(dil, LANES) for dil in DILATIONS]
                 + [once(w_pb.shape), once(w_out.shape), const(g.shape), const(w_router_t.shape)],
        out_specs=[row(d), row(d), pl.BlockSpec((None, n_e, ROW_TILE), lambda bi, t: (bi, 0, t))],
        out_shape=[jax.ShapeDtypeStruct((b, s, d), F32), jax.ShapeDtypeStruct((b, s, d), BF16),
                   jax.ShapeDtypeStruct((b, n_e, s), F32)],
        scratch_shapes=[pltpu.VMEM(w_pb.shape, BF16), pltpu.VMEM(w_out.shape, BF16)] + stage,
        compiler_params=_cparams(("arbitrary", "arbitrary"), 48),
        name="mix_out",
    )(x, ta, gb, *os_, *ls_, w_pb, w_out, g, w_router_t)


class _Plan(NamedTuple):
    tiles: int
    fast: int
    slow: int
    group: int


GATHER_PLAN = _Plan(tiles=GATHER_TOKENS // SLOT_TILE, fast=80, slow=GATHER_TOKENS + SLOT_ALIGN,
                    group=N_EXPERTS // 2)
COMBINE_PLAN = _Plan(tiles=1, fast=48, slow=SLOT_TILE + SLOT_ALIGN, group=N_EXPERTS)
FAST_WINDOW, SLOT_WINDOW = COMBINE_PLAN.fast, COMBINE_PLAN.slow


def _slot_plan(cum, plan, cap):
    rows = cum.shape[0]
    nxt = pltpu.roll(cum, LANES - plan.tiles, 1)
    lo = jnp.floor(cum * (1.0 / SLOT_ALIGN)) * SLOT_ALIGN
    ok = jnp.where(nxt - lo <= plan.fast, 1.0, 0.0)
    fit = jnp.min(ok.reshape(rows // plan.group, plan.group, LANES), axis=1)
    assert (cap - plan.fast) % SLOT_ALIGN == 0 and (cap - plan.slow) % SLOT_ALIGN == 0
    return (jnp.minimum(lo, cap - plan.fast).astype(I32),
            jnp.minimum(lo, cap - plan.slow).astype(I32), fit.astype(I32))


def _topk_kernel(aff_ref, rank_ref, *plan_refs, cap, plans):
    n_e, s = aff_ref.shape
    n_blk = s // SLOT_TILE
    aff = aff_ref[...]
    thr = jnp.zeros((n_e, 1), I32)
    for bit in range(30, -1, -1):
        cand = thr | (1 << bit)
        cnt = jnp.sum((aff >= pltpu.bitcast(cand, F32)).astype(I32), axis=1, keepdims=True)
        thr = jnp.where(cnt >= cap, cand, thr)
    above = aff >= pltpu.bitcast(thr + 1, F32)
    tie = jnp.logical_and(aff >= pltpu.bitcast(thr, F32), jnp.logical_not(above))
    need = (cap - jnp.sum(above.astype(I32), axis=1, keepdims=True)).astype(F32)
    r_i = lax.broadcasted_iota(I32, (SLOT_TILE, SLOT_TILE), 0)
    c_i = lax.broadcasted_iota(I32, (SLOT_TILE, SLOT_TILE), 1)
    tri = jnp.where(r_i < c_i, 1.0, 0.0).astype(BF16)
    lane = lax.broadcasted_iota(I32, (n_e, LANES), 1)
    run_tie = jnp.zeros((n_e, 1), F32)
    run_sel = jnp.zeros((n_e, 1), F32)
    cum = jnp.zeros((n_e, LANES), F32)
    for j in range(n_blk):
        cs = slice(j * SLOT_TILE, (j + 1) * SLOT_TILE)
        tie_f = jnp.where(tie[:, cs], 1.0, 0.0)
        tie_rank = _dot(tie_f.astype(BF16), tri) + run_tie
        run_tie = run_tie + jnp.sum(tie_f, axis=1, keepdims=True)
        sel_f = jnp.where(above[:, cs], 1.0, jnp.where(tie_rank < need, tie_f, 0.0))
        rank = _dot(sel_f.astype(BF16), tri) + run_sel
        rank_ref[:, cs] = jnp.where(sel_f > 0.0, rank, -1.0).astype(I32)
        cum = jnp.where(lane == j, run_sel, cum)
        run_sel = run_sel + jnp.sum(sel_f, axis=1, keepdims=True)
    cum = jnp.where(lane == n_blk, run_sel, cum)
    for i, plan in enumerate(plans):
        for ref, val in zip(plan_refs[3 * i:3 * i + 3], _slot_plan(cum, plan, cap)):
            ref[...] = val


def _topk(aff, cap, plans):
    b, n_e, s = aff.shape
    rows = b * n_e
    full = lambda r, w: pl.BlockSpec((r, w), lambda i: (0, 0))
    plan_specs, plan_shapes = [], []
    for plan in plans:
        for r in (rows, rows, rows // plan.group):
            plan_specs.append(full(r, LANES))
            plan_shapes.append(jax.ShapeDtypeStruct((r, LANES), I32))
    rank, *tables = pl.pallas_call(
        functools.partial(_topk_kernel, cap=cap, plans=plans),
        grid=(1,),
        in_specs=[full(rows, s)],
        out_specs=[full(rows, s)] + plan_specs,
        out_shape=[jax.ShapeDtypeStruct((rows, s), I32)] + plan_shapes,
        compiler_params=_cparams(("arbitrary",), 32),
        name="topk",
    )(aff.reshape(rows, s))
    tables = [t.reshape(-1) for t in tables]
    return rank.reshape(b, n_e, s), [tables[3 * i:3 * i + 3] for i in range(len(plans))]


def _window_start(table_ref, expert_row, tile):
    return pl.multiple_of(table_ref[expert_row * LANES + tile], SLOT_ALIGN)


def _run_blocks(fits, fast, slow):
    all_fit = functools.reduce(jnp.logical_and, fits)

    @pl.when(all_fit)
    def _():
        for jj in range(len(fits)):
            fast(jj)

    @pl.when(jnp.logical_not(all_fit))
    def _():
        for jj, fit in enumerate(fits):
            pl.when(fit)(functools.partial(fast, jj))
            pl.when(jnp.logical_not(fit))(functools.partial(slow, jj))


def _gather_kernel(fast_ref, slow_ref, fit_ref, rank_ref, aff_ref, h2_ref, xe_ref, gate_ref):
    n_e = rank_ref.shape[0]
    rows = h2_ref.shape[0]
    group = pl.program_id(0) * pl.num_programs(1) + pl.program_id(1)
    t = pl.program_id(2)

    @pl.when(t == 0)
    def _():
        xe_ref[...] = jnp.zeros_like(xe_ref)
        gate_ref[...] = jnp.zeros_like(gate_ref)

    row_fast = lax.broadcasted_iota(I32, (GATHER_PLAN.fast, GATHER_TOKENS), 0)
    row_slow = lax.broadcasted_iota(I32, (GATHER_PLAN.slow, GATHER_TOKENS), 0)
    n_blk = rows // GATHER_TOKENS
    tile = lambda jj: (t * n_blk + jj) * GATHER_PLAN.tiles

    def hit(jj, e, row_i):
        toks = slice(jj * GATHER_TOKENS, (jj + 1) * GATHER_TOKENS)
        table = fast_ref if row_i is row_fast else slow_ref
        lo = _window_start(table, group * n_e + e, tile(jj))
        return lo, (row_i + lo) == rank_ref[e:e + 1, toks]

    def add_window(jj, e, lo, hit_e, rows_e):
        toks = slice(jj * GATHER_TOKENS, (jj + 1) * GATHER_TOKENS)
        win = pl.ds(lo, hit_e.shape[0])
        xe_ref[e, win, :] += rows_e.astype(BF16)
        gate_ref[e, win, :] += jnp.sum(jnp.where(hit_e, aff_ref[e:e + 1, toks], 0.0), axis=1,
                                       keepdims=True)

    def fast(jj):
        hs = [hit(jj, e, row_fast) for e in range(n_e)]
        stack = jnp.concatenate([jnp.where(h, 1.0, 0.0).astype(BF16) for _, h in hs], axis=0)
        res = _dot(stack, h2_ref[jj * GATHER_TOKENS:(jj + 1) * GATHER_TOKENS, :])
        for e, (lo, h) in enumerate(hs):
            add_window(jj, e, lo, h, res[e * GATHER_PLAN.fast:(e + 1) * GATHER_PLAN.fast])

    def slow(jj):
        for e in range(n_e):
            lo, h = hit(jj, e, row_slow)
            add_window(jj, e, lo, h,
                       _dot(jnp.where(h, 1.0, 0.0).astype(BF16),
                            h2_ref[jj * GATHER_TOKENS:(jj + 1) * GATHER_TOKENS, :]))

    _run_blocks([fit_ref[group * LANES + tile(jj)] != 0 for jj in range(n_blk)], fast, slow)


def _gather(plan_tables, rank, aff, h2, cap):
    b, s, d = h2.shape
    n_e = rank.shape[1]
    rows = cap
    grp = GATHER_PLAN.group
    per_tok = pl.BlockSpec((None, grp, GATHER_STEP_TOKENS), lambda bi, eg, t, *_: (bi, eg, t))
    whole = lambda w: pl.BlockSpec((None, grp, rows, w), lambda bi, eg, t, *_: (bi, eg, 0, 0))
    grid_spec = pltpu.PrefetchScalarGridSpec(
        num_scalar_prefetch=3,
        grid=(b, n_e // grp, s // GATHER_STEP_TOKENS),
        in_specs=[per_tok, per_tok,
                  pl.BlockSpec((None, GATHER_STEP_TOKENS, d), lambda bi, eg, t, *_: (bi, t, 0))],
        out_specs=[whole(d), whole(LANES)],
    )
    return pl.pallas_call(
        _gather_kernel,
        grid_spec=grid_spec,
        out_shape=[jax.ShapeDtypeStruct((b, n_e, rows, d), BF16),
                   jax.ShapeDtypeStruct((b, n_e, rows, LANES), F32)],
        compiler_params=_cparams(("arbitrary", "arbitrary", "arbitrary"), 56),
        name="gather",
    )(*plan_tables, rank, aff, h2)


def _moe_ffn_kernel(xe_ref, gate_ref, wg_ref, wu_ref, wd_ref, ye_ref, wg_bf, wu_bf, wd_bf):
    n_seq, cap, d = xe_ref.shape

    @pl.when(pl.program_id(1) == 0)
    def _():
        wg_bf[...] = wg_ref[...].astype(BF16)
        wu_bf[...] = wu_ref[...].astype(BF16)
        wd_bf[...] = wd_ref[...].astype(BF16)

    xe = xe_ref[...].reshape(n_seq * cap, d)
    gate_h = _dot(xe, wg_bf[...])
    up_h = _dot(xe, wu_bf[...])
    hidden = (gate_h * _sigmoid(gate_h) * up_h).astype(BF16)
    ye = _dot(hidden, wd_bf[...]) * gate_ref[...].reshape(n_seq * cap, LANES)[:, 0:1]
    ye_ref[...] = ye.astype(BF16).reshape(n_seq, cap, d)


def _moe_ffn(xe, gate, wg, wu, wd, cap):
    b, n_e, rows, d = xe.shape
    hid = wg.shape[2]
    return pl.pallas_call(
        _moe_ffn_kernel,
        grid=(n_e, b // FFN_SEQS),
        in_specs=[
            pl.BlockSpec((FFN_SEQS, None, cap, d), lambda e, bi: (bi, e, 0, 0)),
            pl.BlockSpec((FFN_SEQS, None, cap, LANES), lambda e, bi: (bi, e, 0, 0)),
            pl.BlockSpec((None, d, hid), lambda e, bi: (e, 0, 0)),
            pl.BlockSpec((None, d, hid), lambda e, bi: (e, 0, 0)),
            pl.BlockSpec((None, hid, d), lambda e, bi: (e, 0, 0)),
        ],
        out_specs=pl.BlockSpec((FFN_SEQS, None, rows, d), lambda e, bi: (bi, e, 0, 0)),
        out_shape=jax.ShapeDtypeStruct((b, n_e, rows, d), BF16),
        scratch_shapes=[pltpu.VMEM((d, hid), BF16), pltpu.VMEM((d, hid), BF16),
                        pltpu.VMEM((hid, d), BF16)],
        compiler_params=_cparams(("arbitrary", "arbitrary"), 56),
        name="moe_ffn",
    )(xe, gate, wg, wu, wd)


def _combine_kernel(fast_ref, slow_ref, fit_ref, rank_ref, ye_ref, x1_ref, g_ref, y_ref, rhs_ref):
    n_e = rank_ref.shape[0]
    n_blk = x1_ref.shape[0] // SLOT_TILE
    bi = pl.program_id(0)
    t = pl.program_id(1)
    row_fast = lax.broadcasted_iota(I32, (FAST_WINDOW, SLOT_TILE), 0)
    row_slow = lax.broadcasted_iota(I32, (SLOT_WINDOW, SLOT_TILE), 0)

    def hits(jj, e, row_i):
        toks = slice(jj * SLOT_TILE, (jj + 1) * SLOT_TILE)
        table = fast_ref if row_i is row_fast else slow_ref
        lo = _window_start(table, bi * n_e + e, t * n_blk + jj)
        hit = (row_i + lo) == rank_ref[e:e + 1, toks]
        return pl.ds(lo, row_i.shape[0]), jnp.where(hit, 1.0, 0.0).astype(BF16)

    def finish(jj, moe):
        toks = slice(jj * SLOT_TILE, (jj + 1) * SLOT_TILE)
        y_ref[toks, :] = _rms(x1_ref[toks, :] + moe, g_ref[...])

    def fast(jj):
        rhs = rhs_ref.at[jj % rhs_ref.shape[0]]
        stack = []
        for e in range(n_e):
            win, hit = hits(jj, e, row_fast)
            rhs[e * FAST_WINDOW:(e + 1) * FAST_WINDOW, :] = ye_ref[e, win, :]
            stack.append(hit)
        finish(jj, _dot_tn(jnp.concatenate(stack, axis=0), rhs[...]))

    def slow(jj):
        acc = None
        for e in range(n_e):
            win, hit = hits(jj, e, row_slow)
            part = _dot_tn(hit, ye_ref[e, win, :])
            acc = part if acc is None else acc + part
        finish(jj, acc)

    _run_blocks([fit_ref[bi * LANES + t * n_blk + jj] != 0 for jj in range(n_blk)], fast, slow)


def _combine(plan_tables, rank, ye, x1, g):
    b, s, d = x1.shape
    n_e, ye_rows = ye.shape[1], ye.shape[2]
    row = pl.BlockSpec((None, COMBINE_ROWS, d), lambda bi, t, *_: (bi, t, 0))
    grid_spec = pltpu.PrefetchScalarGridSpec(
        num_scalar_prefetch=3,
        grid=(b, s // COMBINE_ROWS),
        in_specs=[
            pl.BlockSpec((None, n_e, COMBINE_ROWS), lambda bi, t, *_: (bi, 0, t)),
            pl.BlockSpec((None, n_e, ye_rows, d), lambda bi, t, *_: (bi, 0, 0, 0)),
            row,
            pl.BlockSpec(g.shape, lambda bi, t, *_: (0, 0)),
        ],
        out_specs=row,
        scratch_shapes=[pltpu.VMEM((2, n_e * FAST_WINDOW, d), BF16)],
    )
    return pl.pallas_call(
        _combine_kernel,
        grid_spec=grid_spec,
        out_shape=jax.ShapeDtypeStruct((b, s, d), F32),
        compiler_params=_cparams(("arbitrary", "arbitrary"), 60),
        name="combine",
    )(*plan_tables, rank, ye, x1, g)


def _moe_stages(aff, h2, x1, wg, wu, wd, g_final, cap):
    rank, (gather_tables, combine_tables) = _topk(aff, cap, (GATHER_PLAN, COMBINE_PLAN))
    xe, gate = _gather(gather_tables, rank, aff, h2, cap)
    ye = _moe_ffn(xe, gate, wg, wu, wd, cap)
    return _combine(combine_tables, rank, ye, x1, g_final)


def kernel(x, norm_mix_g, w_in, b_gate, gmlp_norm_g, w_spatial, b_spatial, w_proj_a, w_proj_b,
           w_out, norm_ffn_g, w_router, w_e_gate, w_e_up, w_e_down, norm_final_g):
    b, s, d = x.shape
    assert w_in.shape[0] == 1, "single-layer block"
    cap = CAPACITY_FACTOR * s // N_EXPERTS
    group_width = GMLP_WIDTH // GMLP_GROUPS
    ws_pairs = w_spatial[0].astype(BF16).reshape(GMLP_GROUPS // 2, 2 * CHUNK, CHUNK)
    bsp = jnp.repeat(b_spatial[0].T, group_width, axis=1)
    qkv, ta, gb = _mix_in(x, norm_mix_g, w_in[0], b_gate, gmlp_norm_g, ws_pairs, bsp,
                          w_proj_a[0].astype(BF16))
    os_, ls_ = [], []
    for (q, k, v), dil in zip(qkv, DILATIONS):
        o, lse = _attn_pattern(q, k, v, dil)
        os_.append(o)
        ls_.append(lse)
    x1, h2, aff = _mix_out(x, ta, gb, os_, ls_, w_proj_b[0], w_out[0], norm_ffn_g, w_router[0].T)
    return _moe_stages(aff, h2, x1, w_e_gate[0], w_e_up[0], w_e_down[0], norm_final_g[None], cap)
```

```python
import functools

from typing import NamedTuple

import jax
import jax.numpy as jnp
from jax import lax
from jax.experimental import pallas as pl
from jax.experimental.pallas import tpu as pltpu

F32 = jnp.float32
BF16 = jnp.bfloat16
I32 = jnp.int32

EPS = 1e-6
GMLP_WIDTH = 512
GMLP_GROUPS = 8
CHUNK = 128
N_HEADS = 8
HEAD_DIM = 64
ATTN_WIDTH = N_HEADS * HEAD_DIM
DILATIONS = (1, 4, 16)
DILATION_STEP = 4
assert all(b == a * DILATION_STEP for a, b in zip(DILATIONS, DILATIONS[1:]))
HALF_WINDOW = 64
N_EXPERTS = 16
CAPACITY_FACTOR = 2

LANES = 128
Q_TILE = 128
KEY_TILE = 2 * Q_TILE
ATTN_STEP_ROWS = 2048
STAT_LANES = LANES // N_HEADS
DEN_SHIFT = STAT_LANES // 2
SLOT_TILE = 128
SLOT_ALIGN = 16
COMBINE_ROWS = 1024
FFN_SEQS = 2
GATHER_STEP_TOKENS = 2048
GATHER_TOKENS = 256
ROW_TILE = 512
MIB = 1024 * 1024


def _cparams(sem, vmem_mib):
    return pltpu.CompilerParams(dimension_semantics=sem, vmem_limit_bytes=vmem_mib * MIB)


def _gelu_tanh(x):
    return 0.5 * x * (1.0 + jnp.tanh(0.7978845608028654 * (x + 0.044715 * (x * x * x))))


def _sigmoid(x):
    return 1.0 / (1.0 + jnp.exp(-x))


def _rms(x, g):
    return x * lax.rsqrt(jnp.mean(x * x, axis=-1, keepdims=True) + EPS) * g


def _dot(a, b):
    return jnp.dot(a, b, preferred_element_type=F32)


def _dot_nt(a, b):
    return lax.dot_general(a, b, (((1,), (1,)), ((), ())), preferred_element_type=F32)


def _dot_tn(a, b):
    return lax.dot_general(a, b, (((0,), (0,)), ((), ())), preferred_element_type=F32)


def _mix_in_kernel(x_ref, g_ref, win_ref, bg_ref, g2_ref, ws_ref, bsp_ref, pa_ref, *refs):
    n_qkv = 3 * len(DILATIONS)
    qkv_refs = refs[:n_qkv]
    ta_ref, gb_ref = refs[n_qkv:n_qkv + 2]
    win_bf = refs[n_qkv + 2]
    stage_refs = refs[n_qkv + 3:]
    stage_refs = list(zip(stage_refs[0::2], stage_refs[1::2]))
    rows, d_model = x_ref.shape

    @pl.when(jnp.logical_and(pl.program_id(0) == 0, pl.program_id(1) == 0))
    def _():
        for c in range(0, win_ref.shape[1], GMLP_WIDTH):
            win_bf[:, c:c + GMLP_WIDTH] = win_ref[:, c:c + GMLP_WIDTH].astype(BF16)

    h = _rms(x_ref[...], g_ref[...]).astype(BF16)

    def proj(lo, width):
        return _dot(h, win_bf[:, lo:lo + width])

    c0 = 0
    u = _gelu_tanh(proj(c0, GMLP_WIDTH)); c0 += GMLP_WIDTH
    v = _gelu_tanh(proj(c0, GMLP_WIDTH)); c0 += GMLP_WIDTH
    for i in range(3):
        val = proj(c0, ATTN_WIDTH); c0 += ATTN_WIDTH
        if i == 0:
            val = val * (HEAD_DIM ** -0.5)
        qkv_refs[i][0] = val.astype(BF16)
        stage1, stage2 = stage_refs[i]
        n4, n16 = rows // DILATIONS[1], rows // DILATIONS[2]
        out4, out16 = qkv_refs[3 + i], qkv_refs[6 + i]
        for p in range(ATTN_WIDTH // LANES):
            cs = slice(p * LANES, (p + 1) * LANES)
            stage1[p] = val[:, cs]
            for r4 in range(DILATION_STEP):
                part = stage1[p, pl.ds(r4, n4, stride=DILATION_STEP), :]
                out4[r4, :, cs] = part.astype(BF16)
                stage2[p, r4] = part
                for c in range(DILATION_STEP):
                    out16[r4 + DILATION_STEP * c, :, cs] = (
                        stage2[p, r4, pl.ds(c, n16, stride=DILATION_STEP), :].astype(BF16))
    ga = _sigmoid(proj(c0, d_model) + bg_ref[:, :d_model]); c0 += d_model
    gb = _sigmoid(proj(c0, d_model) + bg_ref[:, d_model:])
    gb_ref[...] = gb.astype(BF16)

    vn = _rms(v, g2_ref[...]).astype(BF16)
    lane_lo = lax.broadcasted_iota(I32, (CHUNK, LANES), 1) < HEAD_DIM
    bsp = bsp_ref[...]
    n_chunk = rows // CHUNK
    mixed_slabs = []
    for p in range(GMLP_WIDTH // LANES):
        slab = jnp.concatenate(
            [vn[c * CHUNK:(c + 1) * CHUNK, p * LANES:(p + 1) * LANES] for c in range(n_chunk)],
            axis=1)
        r = _dot(ws_ref[p], slab)
        mixed_slabs.append([jnp.where(lane_lo, r[:CHUNK, c * LANES:(c + 1) * LANES],
                                      r[CHUNK:, c * LANES:(c + 1) * LANES])
                            for c in range(n_chunk)])
    a_chunks = []
    for c in range(n_chunk):
        rs = slice(c * CHUNK, (c + 1) * CHUNK)
        mixed = jnp.concatenate([slabs[c] for slabs in mixed_slabs], axis=1) + bsp
        a_chunks.append((u[rs] * mixed).astype(BF16))
    a = jnp.concatenate(a_chunks, axis=0)
    ta_ref[...] = (ga * _dot(a, pa_ref[...])).astype(BF16)


def _mix_in(x, g, w_in, b_gate, g2, ws_pairs, bsp, w_pa):
    b, s, d = x.shape
    const = lambda shape: pl.BlockSpec(shape, lambda bi, t: (0,) * len(shape))
    row = lambda w: pl.BlockSpec((None, ROW_TILE, w), lambda bi, t: (bi, t, 0))
    qkv_specs, qkv_shapes = [], []
    for dil in DILATIONS:
        spec = pl.BlockSpec((None, dil, ROW_TILE // dil, ATTN_WIDTH), lambda bi, t: (bi, 0, t, 0))
        qkv_specs += [spec] * 3
        qkv_shapes += [jax.ShapeDtypeStruct((b, dil, s // dil, ATTN_WIDTH), BF16)] * 3
    outs = pl.pallas_call(
        _mix_in_kernel,
        grid=(b, s // ROW_TILE),
        in_specs=[row(d), const(g.shape),
                  pl.BlockSpec(w_in.shape, lambda bi, t: (0, 0), pipeline_mode=pl.Buffered(1)),
                  const(b_gate.shape), const(g2.shape),
                  const(ws_pairs.shape), const(bsp.shape), const(w_pa.shape)],
        out_specs=qkv_specs + [row(d), row(d)],
        out_shape=qkv_shapes + [jax.ShapeDtypeStruct((b, s, d), BF16)] * 2,
        scratch_shapes=[pltpu.VMEM(w_in.shape, BF16)]
                       + [pltpu.VMEM((ATTN_WIDTH // LANES, ROW_TILE, LANES), F32),
                          pltpu.VMEM((ATTN_WIDTH // LANES, DILATION_STEP,
                                      ROW_TILE // DILATION_STEP, LANES), F32)] * 3,
        compiler_params=_cparams(("arbitrary", "arbitrary"), 60),
        name="mix_in",
    )(x, g, w_in, b_gate, g2, ws_pairs, bsp, w_pa)
    n_qkv = 3 * len(DILATIONS)
    qkv = [outs[3 * i:3 * i + 3] for i in range(len(DILATIONS))]
    return qkv, outs[n_qkv], outs[n_qkv + 1]


def _attn_kernel(q_ref, k_ref, v_ref, o_ref, l_ref, bias_ref, *, dil):
    n_res, rows, _ = q_ref.shape
    seq = k_ref.shape[1]
    t = pl.program_id(2)
    first = jnp.logical_and(jnp.logical_and(pl.program_id(0) == 0, pl.program_id(1) == 0), t == 0)

    @pl.when(first)
    def _():
        ii = lax.broadcasted_iota(I32, (Q_TILE, KEY_TILE), 0)
        jj = lax.broadcasted_iota(I32, (Q_TILE, KEY_TILE), 1)
        for var in range(3):
            absd = jnp.abs(jj - ii - var * HALF_WINDOW)
            valid = absd <= HALF_WINDOW
            absf = absd.astype(F32)
            for h in range(N_HEADS):
                slope = 2.0 ** (-8.0 * (h + 1) / N_HEADS)
                bias_ref[var, h] = jnp.where(valid, -(slope * dil) * absf, -jnp.inf)

    lane = lax.broadcasted_iota(I32, (Q_TILE, LANES), 1)
    lane_lo = lane < HEAD_DIM
    mask_lo = jnp.where(lane_lo, 1.0, 0.0).astype(BF16)
    mask_hi = jnp.where(lane_lo, 0.0, 1.0).astype(BF16)
    for rr in range(n_res):
        for qi in range(rows // Q_TILE):
            rs = slice(qi * Q_TILE, (qi + 1) * Q_TILE)
            i0 = t * rows + qi * Q_TILE
            start = pl.multiple_of(jnp.clip(i0 - HALF_WINDOW, 0, seq - KEY_TILE), HALF_WINDOW)
            var = (i0 - start) // HALF_WINDOW
            for p in range(ATTN_WIDTH // LANES):
                cs = slice(p * LANES, (p + 1) * LANES)
                qp = q_ref[rr, rs, cs]
                kp = k_ref[rr, pl.ds(start, KEY_TILE), cs]
                vp = v_ref[rr, pl.ds(start, KEY_TILE), cs]
                q2 = jnp.concatenate([qp * mask_lo, qp * mask_hi], axis=0)
                s2 = _dot_nt(q2, kp)
                probs = []
                for hh in range(2):
                    h = 2 * p + hh
                    s = s2[hh * Q_TILE:(hh + 1) * Q_TILE] + bias_ref[var, h]
                    m = jnp.max(s, axis=-1, keepdims=True)
                    e = jnp.exp(s - m)
                    den = jnp.sum(e, axis=-1, keepdims=True)
                    probs.append(e)
                    lo = h * STAT_LANES
                    l_ref[rr, rs, lo:lo + DEN_SHIFT] = jnp.broadcast_to(m, (Q_TILE, DEN_SHIFT))
                    l_ref[rr, rs, lo + DEN_SHIFT:lo + STAT_LANES] = jnp.broadcast_to(
                        den, (Q_TILE, DEN_SHIFT))
                o2 = _dot(jnp.concatenate(probs, axis=0).astype(BF16), vp)
                o_ref[rr, rs, cs] = jnp.where(lane_lo, o2[:Q_TILE], o2[Q_TILE:]).astype(BF16)


def _attn_pattern(q, k, v, dil):
    b, _, seq, w = q.shape
    rows = min(seq, ATTN_STEP_ROWS)
    n_res = ATTN_STEP_ROWS // rows
    qspec = lambda width: pl.BlockSpec((None, n_res, rows, width), lambda bi, r, t: (bi, r, t, 0))
    kspec = pl.BlockSpec((None, n_res, seq, w), lambda bi, r, t: (bi, r, 0, 0))
    return pl.pallas_call(
        functools.partial(_attn_kernel, dil=dil),
        grid=(b, dil // n_res, seq // rows),
        in_specs=[qspec(w), kspec, kspec],
        out_specs=[qspec(w), qspec(LANES)],
        out_shape=[jax.ShapeDtypeStruct(q.shape, BF16),
                   jax.ShapeDtypeStruct((b, dil, seq, LANES), F32)],
        scratch_shapes=[pltpu.VMEM((3, N_HEADS, Q_TILE, KEY_TILE), F32)],
        compiler_params=_cparams(("arbitrary", "arbitrary", "arbitrary"), 48),
        name=f"attn_d{dil}",
    )(q, k, v)


def _to_natural(src_ref, nat_ref, tmp_ref, rows):
    n_slab = nat_ref.shape[0]
    step = DILATION_STEP
    for p in range(n_slab):
        cs = slice(p * LANES, (p + 1) * LANES)
        for r4 in range(step):
            if tmp_ref is None:
                quarter = src_ref[r4][:, cs].astype(F32)
            else:
                for c in range(step):
                    tmp_ref[p, r4, pl.ds(c, rows // (step * step), stride=step), :] = (
                        src_ref[r4 + step * c][:, cs].astype(F32))
                quarter = tmp_ref[p, r4]
            nat_ref[p, pl.ds(r4, rows // step, stride=step), :] = quarter
    return jnp.concatenate([nat_ref[p] for p in range(n_slab)], axis=1)


def _mix_out_kernel(x_ref, ta_ref, gb_ref, *refs):
    n_pat = len(DILATIONS)
    o_refs = refs[:n_pat]
    l_refs = refs[n_pat:2 * n_pat]
    pb_f32, wo_f32, g_ref, wr_ref, x1_ref, h2_ref, aff_ref = refs[2 * n_pat:2 * n_pat + 7]
    pb_ref, wo_ref = refs[2 * n_pat + 7:2 * n_pat + 9]
    stage_refs = refs[2 * n_pat + 9:]
    rows = x_ref.shape[0]

    @pl.when(jnp.logical_and(pl.program_id(0) == 0, pl.program_id(1) == 0))
    def _():
        pb_ref[...] = pb_f32[...].astype(BF16)
        wo_ref[...] = wo_f32[...].astype(BF16)

    stage_refs = list(stage_refs)
    outs = [o_refs[0][0].astype(F32)]
    lses = [l_refs[0][0]]
    for di in (1, 2):
        for src, dest in ((o_refs[di], outs), (l_refs[di], lses)):
            nat = stage_refs.pop(0)
            tmp = stage_refs.pop(0) if di == 2 else None
            dest.append(_to_natural(src, nat, tmp, rows))

    dens = [pltpu.roll(st, LANES - DEN_SHIFT, 1) for st in lses]
    lses = [st + jnp.log(den) for st, den in zip(lses, dens)]
    m = functools.reduce(jnp.maximum, lses)
    ws = [jnp.exp(l - m) for l in lses]
    inv = 1.0 / functools.reduce(lambda a, c: a + c, ws)
    lane = lax.broadcasted_iota(I32, (rows, LANES), 1)
    used = lane % STAT_LANES < DEN_SHIFT
    ws = [jnp.where(used, w * inv / den, 0.0) for w, den in zip(ws, dens)]
    k_i = lax.broadcasted_iota(I32, (LANES, ATTN_WIDTH), 0)
    c_i = lax.broadcasted_iota(I32, (LANES, ATTN_WIDTH), 1)
    spread = jnp.where(k_i == (c_i // HEAD_DIM) * STAT_LANES, 1.0, 0.0).astype(BF16)
    o = None
    for w, o_p in zip(ws, outs):
        w_hi = w.astype(BF16)
        w_lo = (w - w_hi.astype(F32)).astype(BF16)
        term = (_dot(w_hi, spread) + _dot(w_lo, spread)) * o_p
        o = term if o is None else o + term

    ob = _dot(o.astype(BF16), pb_ref[...])
    merged = (ta_ref[...].astype(F32) + gb_ref[...].astype(F32) * ob).astype(BF16)
    x1 = x_ref[...] + _dot(merged, wo_ref[...])
    x1_ref[...] = x1
    h2 = _rms(x1, g_ref[...])
    h2_ref[...] = h2.astype(BF16)
    h_hi = h2.astype(BF16)
    h_lo = (h2 - h_hi.astype(F32)).astype(BF16)
    wr = wr_ref[...]
    w_hi = wr.astype(BF16)
    w_lo = (wr - w_hi.astype(F32)).astype(BF16)
    n_e = wr.shape[0]
    by_hi = _dot_nt(jnp.concatenate([w_hi, w_lo], axis=0), h_hi)
    logits = by_hi[:n_e] + (_dot_nt(w_hi, h_lo) + by_hi[n_e:])
    e = jnp.exp(logits - jnp.max(logits, axis=0, keepdims=True))
    aff_ref[...] = e / jnp.sum(e, axis=0, keepdims=True)


def _mix_out(x, ta, gb, os_, ls_, w_pb, w_out, g, w_router_t):
    b, s, d = x.shape
    n_e = w_router_t.shape[0]
    const = lambda shape: pl.BlockSpec(shape, lambda bi, t: (0,) * len(shape))
    once = lambda shape: pl.BlockSpec(shape, lambda bi, t: (0,) * len(shape),
                                      pipeline_mode=pl.Buffered(1))
    row = lambda w: pl.BlockSpec((None, ROW_TILE, w), lambda bi, t: (bi, t, 0))
    res = lambda dil, w: pl.BlockSpec((None, dil, ROW_TILE // dil, w), lambda bi, t: (bi, 0, t, 0))
    stage = []
    for di in (1, 2):
        for slabs in (ATTN_WIDTH // LANES, 1):
            stage.append(pltpu.VMEM((slabs, ROW_TILE, LANES), F32))
            if di == 2:
                stage.append(pltpu.VMEM((slabs, DILATION_STEP, ROW_TILE // DILATION_STEP, LANES),
                                        F32))
    return pl.pallas_call(
        _mix_out_kernel,
        grid=(b, s // ROW_TILE),
        in_specs=[row(d), row(d), row(d)]
                 + [res(dil, ATTN_WIDTH) for dil in DILATIONS]
                 + [res(dil, LANES) for dil in DILATIONS]
                 + [once(w_pb.shape), once(w_out.shape), const(g.shape), const(w_router_t.shape)],
        out_specs=[row(d), row(d), pl.BlockSpec((None, n_e, ROW_TILE), lambda bi, t: (bi, 0, t))],
        out_shape=[jax.ShapeDtypeStruct((b, s, d), F32), jax.ShapeDtypeStruct((b, s, d), BF16),
                   jax.ShapeDtypeStruct((b, n_e, s), F32)],
        scratch_shapes=[pltpu.VMEM(w_pb.shape, BF16), pltpu.VMEM(w_out.shape, BF16)] + stage,
        compiler_params=_cparams(("arbitrary", "arbitrary"), 48),
        name="mix_out",
    )(x, ta, gb, *os_, *ls_, w_pb, w_out, g, w_router_t)


class _Plan(NamedTuple):
    tiles: int
    fast: int
    slow: int
    group: int


GATHER_PLAN = _Plan(tiles=GATHER_TOKENS // SLOT_TILE, fast=80, slow=GATHER_TOKENS + SLOT_ALIGN,
                    group=N_EXPERTS // 2)
COMBINE_PLAN = _Plan(tiles=1, fast=48, slow=SLOT_TILE + SLOT_ALIGN, group=N_EXPERTS)
FAST_WINDOW, SLOT_WINDOW = COMBINE_PLAN.fast, COMBINE_PLAN.slow


def _slot_plan(cum, plan, cap):
    rows = cum.shape[0]
    nxt = pltpu.roll(cum, LANES - plan.tiles, 1)
    lo = jnp.floor(cum * (1.0 / SLOT_ALIGN)) * SLOT_ALIGN
    ok = jnp.where(nxt - lo <= plan.fast, 1.0, 0.0)
    fit = jnp.min(ok.reshape(rows // plan.group, plan.group, LANES), axis=1)
    assert (cap - plan.fast) % SLOT_ALIGN == 0 and (cap - plan.slow) % SLOT_ALIGN == 0
    return (jnp.minimum(lo, cap - plan.fast).astype(I32),
            jnp.minimum(lo, cap - plan.slow).astype(I32), fit.astype(I32))


def _topk_kernel(aff_ref, rank_ref, *plan_refs, cap, plans):
    n_e, s = aff_ref.shape
    n_blk = s // SLOT_TILE
    aff = aff_ref[...]
    thr = jnp.zeros((n_e, 1), I32)
    for bit in range(30, -1, -1):
        cand = thr | (1 << bit)
        cnt = jnp.sum((aff >= pltpu.bitcast(cand, F32)).astype(I32), axis=1, keepdims=True)
        thr = jnp.where(cnt >= cap, cand, thr)
    above = aff >= pltpu.bitcast(thr + 1, F32)
    tie = jnp.logical_and(aff >= pltpu.bitcast(thr, F32), jnp.logical_not(above))
    need = (cap - jnp.sum(above.astype(I32), axis=1, keepdims=True)).astype(F32)
    r_i = lax.broadcasted_iota(I32, (SLOT_TILE, SLOT_TILE), 0)
    c_i = lax.broadcasted_iota(I32, (SLOT_TILE, SLOT_TILE), 1)
    tri = jnp.where(r_i < c_i, 1.0, 0.0).astype(BF16)
    lane = lax.broadcasted_iota(I32, (n_e, LANES), 1)
    run_tie = jnp.zeros((n_e, 1), F32)
    run_sel = jnp.zeros((n_e, 1), F32)
    cum = jnp.zeros((n_e, LANES), F32)
    for j in range(n_blk):
        cs = slice(j * SLOT_TILE, (j + 1) * SLOT_TILE)
        tie_f = jnp.where(tie[:, cs], 1.0, 0.0)
        tie_rank = _dot(tie_f.astype(BF16), tri) + run_tie
        run_tie = run_tie + jnp.sum(tie_f, axis=1, keepdims=True)
        sel_f = jnp.where(above[:, cs], 1.0, jnp.where(tie_rank < need, tie_f, 0.0))
        rank = _dot(sel_f.astype(BF16), tri) + run_sel
        rank_ref[:, cs] = jnp.where(sel_f > 0.0, rank, -1.0).astype(I32)
        cum = jnp.where(lane == j, run_sel, cum)
        run_sel = run_sel + jnp.sum(sel_f, axis=1, keepdims=True)
    cum = jnp.where(lane == n_blk, run_sel, cum)
    for i, plan in enumerate(plans):
        for ref, val in zip(plan_refs[3 * i:3 * i + 3], _slot_plan(cum, plan, cap)):
            ref[...] = val


def _topk(aff, cap, plans):
    b, n_e, s = aff.shape
    rows = b * n_e
    full = lambda r, w: pl.BlockSpec((r, w), lambda i: (0, 0))
    plan_specs, plan_shapes = [], []
    for plan in plans:
        for r in (rows, rows, rows // plan.group):
            plan_specs.append(full(r, LANES))
            plan_shapes.append(jax.ShapeDtypeStruct((r, LANES), I32))
    rank, *tables = pl.pallas_call(
        functools.partial(_topk_kernel, cap=cap, plans=plans),
        grid=(1,),
        in_specs=[full(rows, s)],
        out_specs=[full(rows, s)] + plan_specs,
        out_shape=[jax.ShapeDtypeStruct((rows, s), I32)] + plan_shapes,
        compiler_params=_cparams(("arbitrary",), 32),
        name="topk",
    )(aff.reshape(rows, s))
    tables = [t.reshape(-1) for t in tables]
    return rank.reshape(b, n_e, s), [tables[3 * i:3 * i + 3] for i in range(len(plans))]


def _window_start(table_ref, expert_row, tile):
    return pl.multiple_of(table_ref[expert_row * LANES + tile], SLOT_ALIGN)


def _run_blocks(fits, fast, slow):
    all_fit = functools.reduce(jnp.logical_and, fits)

    @pl.when(all_fit)
    def _():
        for jj in range(len(fits)):
            fast(jj)

    @pl.when(jnp.logical_not(all_fit))
    def _():
        for jj, fit in enumerate(fits):
            pl.when(fit)(functools.partial(fast, jj))
            pl.when(jnp.logical_not(fit))(functools.partial(slow, jj))


def _gather_kernel(fast_ref, slow_ref, fit_ref, rank_ref, aff_ref, h2_ref, xe_ref, gate_ref):
    n_e = rank_ref.shape[0]
    rows = h2_ref.shape[0]
    group = pl.program_id(0) * pl.num_programs(1) + pl.program_id(1)
    t = pl.program_id(2)

    @pl.when(t == 0)
    def _():
        xe_ref[...] = jnp.zeros_like(xe_ref)
        gate_ref[...] = jnp.zeros_like(gate_ref)

    row_fast = lax.broadcasted_iota(I32, (GATHER_PLAN.fast, GATHER_TOKENS), 0)
    row_slow = lax.broadcasted_iota(I32, (GATHER_PLAN.slow, GATHER_TOKENS), 0)
    n_blk = rows // GATHER_TOKENS
    tile = lambda jj: (t * n_blk + jj) * GATHER_PLAN.tiles

    def hit(jj, e, row_i):
        toks = slice(jj * GATHER_TOKENS, (jj + 1) * GATHER_TOKENS)
        table = fast_ref if row_i is row_fast else slow_ref
        lo = _window_start(table, group * n_e + e, tile(jj))
        return lo, (row_i + lo) == rank_ref[e:e + 1, toks]

    def add_window(jj, e, lo, hit_e, rows_e):
        toks = slice(jj * GATHER_TOKENS, (jj + 1) * GATHER_TOKENS)
        win = pl.ds(lo, hit_e.shape[0])
        xe_ref[e, win, :] += rows_e.astype(BF16)
        gate_ref[e, win, :] += jnp.sum(jnp.where(hit_e, aff_ref[e:e + 1, toks], 0.0), axis=1,
                                       keepdims=True)

    def fast(jj):
        hs = [hit(jj, e, row_fast) for e in range(n_e)]
        stack = jnp.concatenate([jnp.where(h, 1.0, 0.0).astype(BF16) for _, h in hs], axis=0)
        res = _dot(stack, h2_ref[jj * GATHER_TOKENS:(jj + 1) * GATHER_TOKENS, :])
        for e, (lo, h) in enumerate(hs):
            add_window(jj, e, lo, h, res[e * GATHER_PLAN.fast:(e + 1) * GATHER_PLAN.fast])

    def slow(jj):
        for e in range(n_e):
            lo, h = hit(jj, e, row_slow)
            add_window(jj, e, lo, h,
                       _dot(jnp.where(h, 1.0, 0.0).astype(BF16),
                            h2_ref[jj * GATHER_TOKENS:(jj + 1) * GATHER_TOKENS, :]))

    _run_blocks([fit_ref[group * LANES + tile(jj)] != 0 for jj in range(n_blk)], fast, slow)


def _gather(plan_tables, rank, aff, h2, cap):
    b, s, d = h2.shape
    n_e = rank.shape[1]
    rows = cap
    grp = GATHER_PLAN.group
    per_tok = pl.BlockSpec((None, grp, GATHER_STEP_TOKENS), lambda bi, eg, t, *_: (bi, eg, t))
    whole = lambda w: pl.BlockSpec((None, grp, rows, w), lambda bi, eg, t, *_: (bi, eg, 0, 0))
    grid_spec = pltpu.PrefetchScalarGridSpec(
        num_scalar_prefetch=3,
        grid=(b, n_e // grp, s // GATHER_STEP_TOKENS),
        in_specs=[per_tok, per_tok,
                  pl.BlockSpec((None, GATHER_STEP_TOKENS, d), lambda bi, eg, t, *_: (bi, t, 0))],
        out_specs=[whole(d), whole(LANES)],
    )
    return pl.pallas_call(
        _gather_kernel,
        grid_spec=grid_spec,
        out_shape=[jax.ShapeDtypeStruct((b, n_e, rows, d), BF16),
                   jax.ShapeDtypeStruct((b, n_e, rows, LANES), F32)],
        compiler_params=_cparams(("arbitrary", "arbitrary", "arbitrary"), 56),
        name="gather",
    )(*plan_tables, rank, aff, h2)


def _moe_ffn_kernel(xe_ref, gate_ref, wg_ref, wu_ref, wd_ref, ye_ref, wg_bf, wu_bf, wd_bf):
    n_seq, cap, d = xe_ref.shape

    @pl.when(pl.program_id(1) == 0)
    def _():
        wg_bf[...] = wg_ref[...].astype(BF16)
        wu_bf[...] = wu_ref[...].astype(BF16)
        wd_bf[...] = wd_ref[...].astype(BF16)

    xe = xe_ref[...].reshape(n_seq * cap, d)
    gate_h = _dot(xe, wg_bf[...])
    up_h = _dot(xe, wu_bf[...])
    hidden = (gate_h * _sigmoid(gate_h) * up_h).astype(BF16)
    ye = _dot(hidden, wd_bf[...]) * gate_ref[...].reshape(n_seq * cap, LANES)[:, 0:1]
    ye_ref[...] = ye.astype(BF16).reshape(n_seq, cap, d)


def _moe_ffn(xe, gate, wg, wu, wd, cap):
    b, n_e, rows, d = xe.shape
    hid = wg.shape[2]
    return pl.pallas_call(
        _moe_ffn_kernel,
        grid=(n_e, b // FFN_SEQS),
        in_specs=[
            pl.BlockSpec((FFN_SEQS, None, cap, d), lambda e, bi: (bi, e, 0, 0)),
            pl.BlockSpec((FFN_SEQS, None, cap, LANES), lambda e, bi: (bi, e, 0, 0)),
            pl.BlockSpec((None, d, hid), lambda e, bi: (e, 0, 0)),
            pl.BlockSpec((None, d, hid), lambda e, bi: (e, 0, 0)),
            pl.BlockSpec((None, hid, d), lambda e, bi: (e, 0, 0)),
        ],
        out_specs=pl.BlockSpec((FFN_SEQS, None, rows, d), lambda e, bi: (bi, e, 0, 0)),
        out_shape=jax.ShapeDtypeStruct((b, n_e, rows, d), BF16),
        scratch_shapes=[pltpu.VMEM((d, hid), BF16), pltpu.VMEM((d, hid), BF16),
                        pltpu.VMEM((hid, d), BF16)],
        compiler_params=_cparams(("arbitrary", "arbitrary"), 56),
        name="moe_ffn",
    )(xe, gate, wg, wu, wd)


def _combine_kernel(fast_ref, slow_ref, fit_ref, rank_ref, ye_ref, x1_ref, g_ref, y_ref, rhs_ref):
    n_e = rank_ref.shape[0]
    n_blk = x1_ref.shape[0] // SLOT_TILE
    bi = pl.program_id(0)
    t = pl.program_id(1)
    row_fast = lax.broadcasted_iota(I32, (FAST_WINDOW, SLOT_TILE), 0)
    row_slow = lax.broadcasted_iota(I32, (SLOT_WINDOW, SLOT_TILE), 0)

    def hits(jj, e, row_i):
        toks = slice(jj * SLOT_TILE, (jj + 1) * SLOT_TILE)
        table = fast_ref if row_i is row_fast else slow_ref
        lo = _window_start(table, bi * n_e + e, t * n_blk + jj)
        hit = (row_i + lo) == rank_ref[e:e + 1, toks]
        return pl.ds(lo, row_i.shape[0]), jnp.where(hit, 1.0, 0.0).astype(BF16)

    def finish(jj, moe):
        toks = slice(jj * SLOT_TILE, (jj + 1) * SLOT_TILE)
        y_ref[toks, :] = _rms(x1_ref[toks, :] + moe, g_ref[...])

    def fast(jj):
        rhs = rhs_ref.at[jj % rhs_ref.shape[0]]
        stack = []
        for e in range(n_e):
            win, hit = hits(jj, e, row_fast)
            rhs[e * FAST_WINDOW:(e + 1) * FAST_WINDOW, :] = ye_ref[e, win, :]
            stack.append(hit)
        finish(jj, _dot_tn(jnp.concatenate(stack, axis=0), rhs[...]))

    def slow(jj):
        acc = None
        for e in range(n_e):
            win, hit = hits(jj, e, row_slow)
            part = _dot_tn(hit, ye_ref[e, win, :])
            acc = part if acc is None else acc + part
        finish(jj, acc)

    _run_blocks([fit_ref[bi * LANES + t * n_blk + jj] != 0 for jj in range(n_blk)], fast, slow)


def _combine(plan_tables, rank, ye, x1, g):
    b, s, d = x1.shape
    n_e, ye_rows = ye.shape[1], ye.shape[2]
    row = pl.BlockSpec((None, COMBINE_ROWS, d), lambda bi, t, *_: (bi, t, 0))
    grid_spec = pltpu.PrefetchScalarGridSpec(
        num_scalar_prefetch=3,
        grid=(b, s // COMBINE_ROWS),
        in_specs=[
            pl.BlockSpec((None, n_e, COMBINE_ROWS), lambda bi, t, *_: (bi, 0, t)),
            pl.BlockSpec((None, n_e, ye_rows, d), lambda bi, t, *_: (bi, 0, 0, 0)),
            row,
            pl.BlockSpec(g.shape, lambda bi, t, *_: (0, 0)),
        ],
        out_specs=row,
        scratch_shapes=[pltpu.VMEM((2, n_e * FAST_WINDOW, d), BF16)],
    )
    return pl.pallas_call(
        _combine_kernel,
        grid_spec=grid_spec,
        out_shape=jax.ShapeDtypeStruct((b, s, d), F32),
        compiler_params=_cparams(("arbitrary", "arbitrary"), 60),
        name="combine",
    )(*plan_tables, rank, ye, x1, g)


def _moe_stages(aff, h2, x1, wg, wu, wd, g_final, cap):
    rank, (gather_tables, combine_tables) = _topk(aff, cap, (GATHER_PLAN, COMBINE_PLAN))
    xe, gate = _gather(gather_tables, rank, aff, h2, cap)
    ye = _moe_ffn(xe, gate, wg, wu, wd, cap)
    return _combine(combine_tables, rank, ye, x1, g_final)


def kernel(x, norm_mix_g, w_in, b_gate, gmlp_norm_g, w_spatial, b_spatial, w_proj_a, w_proj_b,
           w_out, norm_ffn_g, w_router, w_e_gate, w_e_up, w_e_down, norm_final_g):
    b, s, d = x.shape
    assert w_in.shape[0] == 1, "single-layer block"
    cap = CAPACITY_FACTOR * s // N_EXPERTS
    group_width = GMLP_WIDTH // GMLP_GROUPS
    ws_pairs = w_spatial[0].astype(BF16).reshape(GMLP_GROUPS // 2, 2 * CHUNK, CHUNK)
    bsp = jnp.repeat(b_spatial[0].T, group_width, axis=1)
    qkv, ta, gb = _mix_in(x, norm_mix_g, w_in[0], b_gate, gmlp_norm_g, ws_pairs, bsp,
                          w_proj_a[0].astype(BF16))
    os_, ls_ = [], []
    for (q, k, v), dil in zip(qkv, DILATIONS):
        o, lse = _attn_pattern(q, k, v, dil)
        os_.append(o)
        ls_.append(lse)
    x1, h2, aff = _mix_out(x, ta, gb, os_, ls_, w_proj_b[0], w_out[0], norm_ffn_g, w_router[0].T)
    return _moe_stages(aff, h2, x1, w_e_gate[0], w_e_up[0], w_e_down[0], norm_final_g[None], cap)
```

```python
import functools

from typing import NamedTuple

import jax
import jax.numpy as jnp
from jax import lax
from jax.experimental import pallas as pl
from jax.experimental.pallas import tpu as pltpu

F32 = jnp.float32
BF16 = jnp.bfloat16
I32 = jnp.int32

EPS = 1e-6
GMLP_WIDTH = 512
GMLP_GROUPS = 8
CHUNK = 128
N_HEADS = 8
HEAD_DIM = 64
ATTN_WIDTH = N_HEADS * HEAD_DIM
DILATIONS = (1, 4, 16)
DILATION_STEP = 4
assert all(b == a * DILATION_STEP for a, b in zip(DILATIONS, DILATIONS[1:]))
HALF_WINDOW = 64
N_EXPERTS = 16
CAPACITY_FACTOR = 2

LANES = 128
Q_TILE = 128
KEY_TILE = 2 * Q_TILE
ATTN_STEP_ROWS = 4096
STAT_LANES = LANES // N_HEADS
DEN_SHIFT = STAT_LANES // 2
SLOT_TILE = 128
SLOT_ALIGN = 16
COMBINE_ROWS = 1024
FFN_SEQS = 2
GATHER_STEP_TOKENS = 2048
GATHER_TOKENS = 256
ROW_TILE = 512
MIB = 1024 * 1024


def _cparams(sem, vmem_mib):
    return pltpu.CompilerParams(dimension_semantics=sem, vmem_limit_bytes=vmem_mib * MIB)


def _gelu_tanh(x):
    return 0.5 * x * (1.0 + jnp.tanh(0.7978845608028654 * (x + 0.044715 * (x * x * x))))


def _sigmoid(x):
    return 1.0 / (1.0 + jnp.exp(-x))


def _rms(x, g):
    return x * lax.rsqrt(jnp.mean(x * x, axis=-1, keepdims=True) + EPS) * g


def _dot(a, b):
    return jnp.dot(a, b, preferred_element_type=F32)


def _dot_nt(a, b):
    return lax.dot_general(a, b, (((1,), (1,)), ((), ())), preferred_element_type=F32)


def _dot_tn(a, b):
    return lax.dot_general(a, b, (((0,), (0,)), ((), ())), preferred_element_type=F32)


def _mix_in_kernel(x_ref, g_ref, win_ref, bg_ref, g2_ref, ws_ref, bsp_ref, pa_ref, *refs):
    n_qkv = 3 * len(DILATIONS)
    qkv_refs = refs[:n_qkv]
    ta_ref, gb_ref = refs[n_qkv:n_qkv + 2]
    win_bf = refs[n_qkv + 2]
    stage_refs = refs[n_qkv + 3:]
    stage_refs = list(zip(stage_refs[0::2], stage_refs[1::2]))
    rows, d_model = x_ref.shape

    @pl.when(jnp.logical_and(pl.program_id(0) == 0, pl.program_id(1) == 0))
    def _():
        for c in range(0, win_ref.shape[1], GMLP_WIDTH):
            win_bf[:, c:c + GMLP_WIDTH] = win_ref[:, c:c + GMLP_WIDTH].astype(BF16)

    h = _rms(x_ref[...], g_ref[...]).astype(BF16)

    def proj(lo, width):
        return _dot(h, win_bf[:, lo:lo + width])

    c0 = 0
    u = _gelu_tanh(proj(c0, GMLP_WIDTH)); c0 += GMLP_WIDTH
    v = _gelu_tanh(proj(c0, GMLP_WIDTH)); c0 += GMLP_WIDTH
    for i in range(3):
        val = proj(c0, ATTN_WIDTH); c0 += ATTN_WIDTH
        if i == 0:
            val = val * (HEAD_DIM ** -0.5)
        qkv_refs[i][0] = val.astype(BF16)
        stage1, stage2 = stage_refs[i]
        n4, n16 = rows // DILATIONS[1], rows // DILATIONS[2]
        out4, out16 = qkv_refs[3 + i], qkv_refs[6 + i]
        for p in range(ATTN_WIDTH // LANES):
            cs = slice(p * LANES, (p + 1) * LANES)
            stage1[p] = val[:, cs]
            for r4 in range(DILATION_STEP):
                part = stage1[p, pl.ds(r4, n4, stride=DILATION_STEP), :]
                out4[r4, :, cs] = part.astype(BF16)
                stage2[p, r4] = part
                for c in range(DILATION_STEP):
                    out16[r4 + DILATION_STEP * c, :, cs] = (
                        stage2[p, r4, pl.ds(c, n16, stride=DILATION_STEP), :].astype(BF16))
    ga = _sigmoid(proj(c0, d_model) + bg_ref[:, :d_model]); c0 += d_model
    gb = _sigmoid(proj(c0, d_model) + bg_ref[:, d_model:])
    gb_ref[...] = gb.astype(BF16)

    vn = _rms(v, g2_ref[...]).astype(BF16)
    lane_lo = lax.broadcasted_iota(I32, (CHUNK, LANES), 1) < HEAD_DIM
    bsp = bsp_ref[...]
    n_chunk = rows // CHUNK
    mixed_slabs = []
    for p in range(GMLP_WIDTH // LANES):
        slab = jnp.concatenate(
            [vn[c * CHUNK:(c + 1) * CHUNK, p * LANES:(p + 1) * LANES] for c in range(n_chunk)],
            axis=1)
        r = _dot(ws_ref[p], slab)
        mixed_slabs.append([jnp.where(lane_lo, r[:CHUNK, c * LANES:(c + 1) * LANES],
                                      r[CHUNK:, c * LANES:(c + 1) * LANES])
                            for c in range(n_chunk)])
    a_chunks = []
    for c in range(n_chunk):
        rs = slice(c * CHUNK, (c + 1) * CHUNK)
        mixed = jnp.concatenate([slabs[c] for slabs in mixed_slabs], axis=1) + bsp
        a_chunks.append((u[rs] * mixed).astype(BF16))
    a = jnp.concatenate(a_chunks, axis=0)
    ta_ref[...] = (ga * _dot(a, pa_ref[...])).astype(BF16)


def _mix_in(x, g, w_in, b_gate, g2, ws_pairs, bsp, w_pa):
    b, s, d = x.shape
    const = lambda shape: pl.BlockSpec(shape, lambda bi, t: (0,) * len(shape))
    row = lambda w: pl.BlockSpec((None, ROW_TILE, w), lambda bi, t: (bi, t, 0))
    qkv_specs, qkv_shapes = [], []
    for dil in DILATIONS:
        spec = pl.BlockSpec((None, dil, ROW_TILE // dil, ATTN_WIDTH), lambda bi, t: (bi, 0, t, 0))
        qkv_specs += [spec] * 3
        qkv_shapes += [jax.ShapeDtypeStruct((b, dil, s // dil, ATTN_WIDTH), BF16)] * 3
    outs = pl.pallas_call(
        _mix_in_kernel,
        grid=(b, s // ROW_TILE),
        in_specs=[row(d), const(g.shape),
                  pl.BlockSpec(w_in.shape, lambda bi, t: (0, 0), pipeline_mode=pl.Buffered(1)),
                  const(b_gate.shape), const(g2.shape),
                  const(ws_pairs.shape), const(bsp.shape), const(w_pa.shape)],
        out_specs=qkv_specs + [row(d), row(d)],
        out_shape=qkv_shapes + [jax.ShapeDtypeStruct((b, s, d), BF16)] * 2,
        scratch_shapes=[pltpu.VMEM(w_in.shape, BF16)]
                       + [pltpu.VMEM((ATTN_WIDTH // LANES, ROW_TILE, LANES), F32),
                          pltpu.VMEM((ATTN_WIDTH // LANES, DILATION_STEP,
                                      ROW_TILE // DILATION_STEP, LANES), F32)] * 3,
        compiler_params=_cparams(("arbitrary", "arbitrary"), 60),
        name="mix_in",
    )(x, g, w_in, b_gate, g2, ws_pairs, bsp, w_pa)
    n_qkv = 3 * len(DILATIONS)
    qkv = [outs[3 * i:3 * i + 3] for i in range(len(DILATIONS))]
    return qkv, outs[n_qkv], outs[n_qkv + 1]


def _attn_kernel(q_ref, k_ref, v_ref, o_ref, l_ref, bias_ref, *, dil):
    n_res, rows, _ = q_ref.shape
    seq = k_ref.shape[1]
    t = pl.program_id(2)
    first = jnp.logical_and(jnp.logical_and(pl.program_id(0) == 0, pl.program_id(1) == 0), t == 0)

    @pl.when(first)
    def _():
        ii = lax.broadcasted_iota(I32, (Q_TILE, KEY_TILE), 0)
        jj = lax.broadcasted_iota(I32, (Q_TILE, KEY_TILE), 1)
        for var in range(3):
            absd = jnp.abs(jj - ii - var * HALF_WINDOW)
            valid = absd <= HALF_WINDOW
            absf = absd.astype(F32)
            for h in range(N_HEADS):
                slope = 2.0 ** (-8.0 * (h + 1) / N_HEADS)
                bias_ref[var, h] = jnp.where(valid, -(slope * dil) * absf, -jnp.inf)

    lane = lax.broadcasted_iota(I32, (Q_TILE, LANES), 1)
    lane_lo = lane < HEAD_DIM
    mask_lo = jnp.where(lane_lo, 1.0, 0.0).astype(BF16)
    mask_hi = jnp.where(lane_lo, 0.0, 1.0).astype(BF16)
    for rr in range(n_res):
        for qi in range(rows // Q_TILE):
            rs = slice(qi * Q_TILE, (qi + 1) * Q_TILE)
            i0 = t * rows + qi * Q_TILE
            start = pl.multiple_of(jnp.clip(i0 - HALF_WINDOW, 0, seq - KEY_TILE), HALF_WINDOW)
            var = (i0 - start) // HALF_WINDOW
            for p in range(ATTN_WIDTH // LANES):
                cs = slice(p * LANES, (p + 1) * LANES)
                qp = q_ref[rr, rs, cs]
                kp = k_ref[rr, pl.ds(start, KEY_TILE), cs]
                vp = v_ref[rr, pl.ds(start, KEY_TILE), cs]
                q2 = jnp.concatenate([qp * mask_lo, qp * mask_hi], axis=0)
                s2 = _dot_nt(q2, kp)
                probs = []
                for hh in range(2):
                    h = 2 * p + hh
                    s = s2[hh * Q_TILE:(hh + 1) * Q_TILE] + bias_ref[var, h]
                    m = jnp.max(s, axis=-1, keepdims=True)
                    e = jnp.exp(s - m)
                    den = jnp.sum(e, axis=-1, keepdims=True)
                    probs.append(e)
                    lo = h * STAT_LANES
                    l_ref[rr, rs, lo:lo + DEN_SHIFT] = jnp.broadcast_to(m, (Q_TILE, DEN_SHIFT))
                    l_ref[rr, rs, lo + DEN_SHIFT:lo + STAT_LANES] = jnp.broadcast_to(
                        den, (Q_TILE, DEN_SHIFT))
                o2 = _dot(jnp.concatenate(probs, axis=0).astype(BF16), vp)
                o_ref[rr, rs, cs] = jnp.where(lane_lo, o2[:Q_TILE], o2[Q_TILE:]).astype(BF16)


def _attn_pattern(q, k, v, dil):
    b, _, seq, w = q.shape
    rows = min(seq, ATTN_STEP_ROWS)
    n_res = ATTN_STEP_ROWS // rows
    qspec = lambda width: pl.BlockSpec((None, n_res, rows, width), lambda bi, r, t: (bi, r, t, 0))
    kspec = pl.BlockSpec((None, n_res, seq, w), lambda bi, r, t: (bi, r, 0, 0))
    return pl.pallas_call(
        functools.partial(_attn_kernel, dil=dil),
        grid=(b, dil // n_res, seq // rows),
        in_specs=[qspec(w), kspec, kspec],
        out_specs=[qspec(w), qspec(LANES)],
        out_shape=[jax.ShapeDtypeStruct(q.shape, BF16),
                   jax.ShapeDtypeStruct((b, dil, seq, LANES), F32)],
        scratch_shapes=[pltpu.VMEM((3, N_HEADS, Q_TILE, KEY_TILE), F32)],
        compiler_params=_cparams(("arbitrary", "arbitrary", "arbitrary"), 48),
        name=f"attn_d{dil}",
    )(q, k, v)


def _to_natural(src_ref, nat_ref, tmp_ref, rows):
    n_slab = nat_ref.shape[0]
    step = DILATION_STEP
    for p in range(n_slab):
        cs = slice(p * LANES, (p + 1) * LANES)
        for r4 in range(step):
            if tmp_ref is None:
                quarter = src_ref[r4][:, cs].astype(F32)
            else:
                for c in range(step):
                    tmp_ref[p, r4, pl.ds(c, rows // (step * step), stride=step), :] = (
                        src_ref[r4 + step * c][:, cs].astype(F32))
                quarter = tmp_ref[p, r4]
            nat_ref[p, pl.ds(r4, rows // step, stride=step), :] = quarter
    return jnp.concatenate([nat_ref[p] for p in range(n_slab)], axis=1)


def _mix_out_kernel(x_ref, ta_ref, gb_ref, *refs):
    n_pat = len(DILATIONS)
    o_refs = refs[:n_pat]
    l_refs = refs[n_pat:2 * n_pat]
    pb_f32, wo_f32, g_ref, wr_ref, x1_ref, h2_ref, aff_ref = refs[2 * n_pat:2 * n_pat + 7]
    pb_ref, wo_ref = refs[2 * n_pat + 7:2 * n_pat + 9]
    stage_refs = refs[2 * n_pat + 9:]
    rows = x_ref.shape[0]

    @pl.when(jnp.logical_and(pl.program_id(0) == 0, pl.program_id(1) == 0))
    def _():
        pb_ref[...] = pb_f32[...].astype(BF16)
        wo_ref[...] = wo_f32[...].astype(BF16)

    stage_refs = list(stage_refs)
    outs = [o_refs[0][0].astype(F32)]
    lses = [l_refs[0][0]]
    for di in (1, 2):
        for src, dest in ((o_refs[di], outs), (l_refs[di], lses)):
            nat = stage_refs.pop(0)
            tmp = stage_refs.pop(0) if di == 2 else None
            dest.append(_to_natural(src, nat, tmp, rows))

    dens = [pltpu.roll(st, LANES - DEN_SHIFT, 1) for st in lses]
    lses = [st + jnp.log(den) for st, den in zip(lses, dens)]
    m = functools.reduce(jnp.maximum, lses)
    ws = [jnp.exp(l - m) for l in lses]
    inv = 1.0 / functools.reduce(lambda a, c: a + c, ws)
    lane = lax.broadcasted_iota(I32, (rows, LANES), 1)
    used = lane % STAT_LANES < DEN_SHIFT
    ws = [jnp.where(used, w * inv / den, 0.0) for w, den in zip(ws, dens)]
    k_i = lax.broadcasted_iota(I32, (LANES, ATTN_WIDTH), 0)
    c_i = lax.broadcasted_iota(I32, (LANES, ATTN_WIDTH), 1)
    spread = jnp.where(k_i == (c_i // HEAD_DIM) * STAT_LANES, 1.0, 0.0).astype(BF16)
    o = None
    for w, o_p in zip(ws, outs):
        w_hi = w.astype(BF16)
        w_lo = (w - w_hi.astype(F32)).astype(BF16)
        term = (_dot(w_hi, spread) + _dot(w_lo, spread)) * o_p
        o = term if o is None else o + term

    ob = _dot(o.astype(BF16), pb_ref[...])
    merged = (ta_ref[...].astype(F32) + gb_ref[...].astype(F32) * ob).astype(BF16)
    x1 = x_ref[...] + _dot(merged, wo_ref[...])
    x1_ref[...] = x1
    h2 = _rms(x1, g_ref[...])
    h2_ref[...] = h2.astype(BF16)
    h_hi = h2.astype(BF16)
    h_lo = (h2 - h_hi.astype(F32)).astype(BF16)
    wr = wr_ref[...]
    w_hi = wr.astype(BF16)
    w_lo = (wr - w_hi.astype(F32)).astype(BF16)
    n_e = wr.shape[0]
    by_hi = _dot_nt(jnp.concatenate([w_hi, w_lo], axis=0), h_hi)
    logits = by_hi[:n_e] + (_dot_nt(w_hi, h_lo) + by_hi[n_e:])
    e = jnp.exp(logits - jnp.max(logits, axis=0, keepdims=True))
    aff_ref[...] = e / jnp.sum(e, axis=0, keepdims=True)


def _mix_out(x, ta, gb, os_, ls_, w_pb, w_out, g, w_router_t):
    b, s, d = x.shape
    n_e = w_router_t.shape[0]
    const = lambda shape: pl.BlockSpec(shape, lambda bi, t: (0,) * len(shape))
    once = lambda shape: pl.BlockSpec(shape, lambda bi, t: (0,) * len(shape),
                                      pipeline_mode=pl.Buffered(1))
    row = lambda w: pl.BlockSpec((None, ROW_TILE, w), lambda bi, t: (bi, t, 0))
    res = lambda dil, w: pl.BlockSpec((None, dil, ROW_TILE // dil, w), lambda bi, t: (bi, 0, t, 0))
    stage = []
    for di in (1, 2):
        for slabs in (ATTN_WIDTH // LANES, 1):
            stage.append(pltpu.VMEM((slabs, ROW_TILE, LANES), F32))
            if di == 2:
                stage.append(pltpu.VMEM((slabs, DILATION_STEP, ROW_TILE // DILATION_STEP, LANES),
                                        F32))
    return pl.pallas_call(
        _mix_out_kernel,
        grid=(b, s // ROW_TILE),
        in_specs=[row(d), row(d), row(d)]
                 + [res(dil, ATTN_WIDTH) for dil in DILATIONS]
                 + [res(dil, LANES) for dil in DILATIONS]
                 + [once(w_pb.shape), once(w_out.shape), const(g.shape), const(w_router_t.shape)],
        out_specs=[row(d), row(d), pl.BlockSpec((None, n_e, ROW_TILE), lambda bi, t: (bi, 0, t))],
        out_shape=[jax.ShapeDtypeStruct((b, s, d), F32), jax.ShapeDtypeStruct((b, s, d), BF16),
                   jax.ShapeDtypeStruct((b, n_e, s), F32)],
        scratch_shapes=[pltpu.VMEM(w_pb.shape, BF16), pltpu.VMEM(w_out.shape, BF16)] + stage,
        compiler_params=_cparams(("arbitrary", "arbitrary"), 48),
        name="mix_out",
    )(x, ta, gb, *os_, *ls_, w_pb, w_out, g, w_router_t)


class _Plan(NamedTuple):
    tiles: int
    fast: int
    slow: int
    group: int


GATHER_PLAN = _Plan(tiles=GATHER_TOKENS // SLOT_TILE, fast=80, slow=GATHER_TOKENS + SLOT_ALIGN,
                    group=N_EXPERTS // 2)
COMBINE_PLAN = _Plan(tiles=1, fast=48, slow=SLOT_TILE + SLOT_ALIGN, group=N_EXPERTS)
FAST_WINDOW, SLOT_WINDOW = COMBINE_PLAN.fast, COMBINE_PLAN.slow


def _slot_plan(cum, plan, cap):
    rows = cum.shape[0]
    nxt = pltpu.roll(cum, LANES - plan.tiles, 1)
    lo = jnp.floor(cum * (1.0 / SLOT_ALIGN)) * SLOT_ALIGN
    ok = jnp.where(nxt - lo <= plan.fast, 1.0, 0.0)
    fit = jnp.min(ok.reshape(rows // plan.group, plan.group, LANES), axis=1)
    assert (cap - plan.fast) % SLOT_ALIGN == 0 and (cap - plan.slow) % SLOT_ALIGN == 0
    return (jnp.minimum(lo, cap - plan.fast).astype(I32),
            jnp.minimum(lo, cap - plan.slow).astype(I32), fit.astype(I32))


def _topk_kernel(aff_ref, rank_ref, *plan_refs, cap, plans):
    n_e, s = aff_ref.shape
    n_blk = s // SLOT_TILE
    aff = aff_ref[...]
    thr = jnp.zeros((n_e, 1), I32)
    for bit in range(30, -1, -1):
        cand = thr | (1 << bit)
        cnt = jnp.sum((aff >= pltpu.bitcast(cand, F32)).astype(I32), axis=1, keepdims=True)
        thr = jnp.where(cnt >= cap, cand, thr)
    above = aff >= pltpu.bitcast(thr + 1, F32)
    tie = jnp.logical_and(aff >= pltpu.bitcast(thr, F32), jnp.logical_not(above))
    need = (cap - jnp.sum(above.astype(I32), axis=1, keepdims=True)).astype(F32)
    r_i = lax.broadcasted_iota(I32, (SLOT_TILE, SLOT_TILE), 0)
    c_i = lax.broadcasted_iota(I32, (SLOT_TILE, SLOT_TILE), 1)
    tri = jnp.where(r_i < c_i, 1.0, 0.0).astype(BF16)
    lane = lax.broadcasted_iota(I32, (n_e, LANES), 1)
    run_tie = jnp.zeros((n_e, 1), F32)
    run_sel = jnp.zeros((n_e, 1), F32)
    cum = jnp.zeros((n_e, LANES), F32)
    for j in range(n_blk):
        cs = slice(j * SLOT_TILE, (j + 1) * SLOT_TILE)
        tie_f = jnp.where(tie[:, cs], 1.0, 0.0)
        tie_rank = _dot(tie_f.astype(BF16), tri) + run_tie
        run_tie = run_tie + jnp.sum(tie_f, axis=1, keepdims=True)
        sel_f = jnp.where(above[:, cs], 1.0, jnp.where(tie_rank < need, tie_f, 0.0))
        rank = _dot(sel_f.astype(BF16), tri) + run_sel
        rank_ref[:, cs] = jnp.where(sel_f > 0.0, rank, -1.0).astype(I32)
        cum = jnp.where(lane == j, run_sel, cum)
        run_sel = run_sel + jnp.sum(sel_f, axis=1, keepdims=True)
    cum = jnp.where(lane == n_blk, run_sel, cum)
    for i, plan in enumerate(plans):
        for ref, val in zip(plan_refs[3 * i:3 * i + 3], _slot_plan(cum, plan, cap)):
            ref[...] = val


def _topk(aff, cap, plans):
    b, n_e, s = aff.shape
    rows = b * n_e
    full = lambda r, w: pl.BlockSpec((r, w), lambda i: (0, 0))
    plan_specs, plan_shapes = [], []
    for plan in plans:
        for r in (rows, rows, rows // plan.group):
            plan_specs.append(full(r, LANES))
            plan_shapes.append(jax.ShapeDtypeStruct((r, LANES), I32))
    rank, *tables = pl.pallas_call(
        functools.partial(_topk_kernel, cap=cap, plans=plans),
        grid=(1,),
        in_specs=[full(rows, s)],
        out_specs=[full(rows, s)] + plan_specs,
        out_shape=[jax.ShapeDtypeStruct((rows, s), I32)] + plan_shapes,
        compiler_params=_cparams(("arbitrary",), 32),
        name="topk",
    )(aff.reshape(rows, s))
    tables = [t.reshape(-1) for t in tables]
    return rank.reshape(b, n_e, s), [tables[3 * i:3 * i + 3] for i in range(len(plans))]


def _window_start(table_ref, expert_row, tile):
    return pl.multiple_of(table_ref[expert_row * LANES + tile], SLOT_ALIGN)


def _run_blocks(fits, fast, slow):
    all_fit = functools.reduce(jnp.logical_and, fits)

    @pl.when(all_fit)
    def _():
        for jj in range(len(fits)):
            fast(jj)

    @pl.when(jnp.logical_not(all_fit))
    def _():
        for jj, fit in enumerate(fits):
            pl.when(fit)(functools.partial(fast, jj))
            pl.when(jnp.logical_not(fit))(functools.partial(slow, jj))


def _gather_kernel(fast_ref, slow_ref, fit_ref, rank_ref, aff_ref, h2_ref, xe_ref, gate_ref):
    n_e = rank_ref.shape[0]
    rows = h2_ref.shape[0]
    group = pl.program_id(0) * pl.num_programs(1) + pl.program_id(1)
    t = pl.program_id(2)

    @pl.when(t == 0)
    def _():
        xe_ref[...] = jnp.zeros_like(xe_ref)
        gate_ref[...] = jnp.zeros_like(gate_ref)

    row_fast = lax.broadcasted_iota(I32, (GATHER_PLAN.fast, GATHER_TOKENS), 0)
    row_slow = lax.broadcasted_iota(I32, (GATHER_PLAN.slow, GATHER_TOKENS), 0)
    n_blk = rows // GATHER_TOKENS
    tile = lambda jj: (t * n_blk + jj) * GATHER_PLAN.tiles

    def hit(jj, e, row_i):
        toks = slice(jj * GATHER_TOKENS, (jj + 1) * GATHER_TOKENS)
        table = fast_ref if row_i is row_fast else slow_ref
        lo = _window_start(table, group * n_e + e, tile(jj))
        return lo, (row_i + lo) == rank_ref[e:e + 1, toks]

    def add_window(jj, e, lo, hit_e, rows_e):
        toks = slice(jj * GATHER_TOKENS, (jj + 1) * GATHER_TOKENS)
        win = pl.ds(lo, hit_e.shape[0])
        xe_ref[e, win, :] += rows_e.astype(BF16)
        gate_ref[e, win, :] += jnp.sum(jnp.where(hit_e, aff_ref[e:e + 1, toks], 0.0), axis=1,
                                       keepdims=True)

    def fast(jj):
        hs = [hit(jj, e, row_fast) for e in range(n_e)]
        stack = jnp.concatenate([jnp.where(h, 1.0, 0.0).astype(BF16) for _, h in hs], axis=0)
        res = _dot(stack, h2_ref[jj * GATHER_TOKENS:(jj + 1) * GATHER_TOKENS, :])
        for e, (lo, h) in enumerate(hs):
            add_window(jj, e, lo, h, res[e * GATHER_PLAN.fast:(e + 1) * GATHER_PLAN.fast])

    def slow(jj):
        for e in range(n_e):
            lo, h = hit(jj, e, row_slow)
            add_window(jj, e, lo, h,
                       _dot(jnp.where(h, 1.0, 0.0).astype(BF16),
                            h2_ref[jj * GATHER_TOKENS:(jj + 1) * GATHER_TOKENS, :]))

    _run_blocks([fit_ref[group * LANES + tile(jj)] != 0 for jj in range(n_blk)], fast, slow)


def _gather(plan_tables, rank, aff, h2, cap):
    b, s, d = h2.shape
    n_e = rank.shape[1]
    rows = cap
    grp = GATHER_PLAN.group
    per_tok = pl.BlockSpec((None, grp, GATHER_STEP_TOKENS), lambda bi, eg, t, *_: (bi, eg, t))
    whole = lambda w: pl.BlockSpec((None, grp, rows, w), lambda bi, eg, t, *_: (bi, eg, 0, 0))
    grid_spec = pltpu.PrefetchScalarGridSpec(
        num_scalar_prefetch=3,
        grid=(b, n_e // grp, s // GATHER_STEP_TOKENS),
        in_specs=[per_tok, per_tok,
                  pl.BlockSpec((None, GATHER_STEP_TOKENS, d), lambda bi, eg, t, *_: (bi, t, 0))],
        out_specs=[whole(d), whole(LANES)],
    )
    return pl.pallas_call(
        _gather_kernel,
        grid_spec=grid_spec,
        out_shape=[jax.ShapeDtypeStruct((b, n_e, rows, d), BF16),
                   jax.ShapeDtypeStruct((b, n_e, rows, LANES), F32)],
        compiler_params=_cparams(("arbitrary", "arbitrary", "arbitrary"), 56),
        name="gather",
    )(*plan_tables, rank, aff, h2)


def _moe_ffn_kernel(xe_ref, gate_ref, wg_ref, wu_ref, wd_ref, ye_ref, wg_bf, wu_bf, wd_bf):
    n_seq, cap, d = xe_ref.shape

    @pl.when(pl.program_id(1) == 0)
    def _():
        wg_bf[...] = wg_ref[...].astype(BF16)
        wu_bf[...] = wu_ref[...].astype(BF16)
        wd_bf[...] = wd_ref[...].astype(BF16)

    xe = xe_ref[...].reshape(n_seq * cap, d)
    gate_h = _dot(xe, wg_bf[...])
    up_h = _dot(xe, wu_bf[...])
    hidden = (gate_h * _sigmoid(gate_h) * up_h).astype(BF16)
    ye = _dot(hidden, wd_bf[...]) * gate_ref[...].reshape(n_seq * cap, LANES)[:, 0:1]
    ye_ref[...] = ye.astype(BF16).reshape(n_seq, cap, d)


def _moe_ffn(xe, gate, wg, wu, wd, cap):
    b, n_e, rows, d = xe.shape
    hid = wg.shape[2]
    return pl.pallas_call(
        _moe_ffn_kernel,
        grid=(n_e, b // FFN_SEQS),
        in_specs=[
            pl.BlockSpec((FFN_SEQS, None, cap, d), lambda e, bi: (bi, e, 0, 0)),
            pl.BlockSpec((FFN_SEQS, None, cap, LANES), lambda e, bi: (bi, e, 0, 0)),
            pl.BlockSpec((None, d, hid), lambda e, bi: (e, 0, 0)),
            pl.BlockSpec((None, d, hid), lambda e, bi: (e, 0, 0)),
            pl.BlockSpec((None, hid, d), lambda e, bi: (e, 0, 0)),
        ],
        out_specs=pl.BlockSpec((FFN_SEQS, None, rows, d), lambda e, bi: (bi, e, 0, 0)),
        out_shape=jax.ShapeDtypeStruct((b, n_e, rows, d), BF16),
        scratch_shapes=[pltpu.VMEM((d, hid), BF16), pltpu.VMEM((d, hid), BF16),
                        pltpu.VMEM((hid, d), BF16)],
        compiler_params=_cparams(("arbitrary", "arbitrary"), 56),
        name="moe_ffn",
    )(xe, gate, wg, wu, wd)


def _combine_kernel(fast_ref, slow_ref, fit_ref, rank_ref, ye_ref, x1_ref, g_ref, y_ref, rhs_ref):
    n_e = rank_ref.shape[0]
    n_blk = x1_ref.shape[0] // SLOT_TILE
    bi = pl.program_id(0)
    t = pl.program_id(1)
    row_fast = lax.broadcasted_iota(I32, (FAST_WINDOW, SLOT_TILE), 0)
    row_slow = lax.broadcasted_iota(I32, (SLOT_WINDOW, SLOT_TILE), 0)

    def hits(jj, e, row_i):
        toks = slice(jj * SLOT_TILE, (jj + 1) * SLOT_TILE)
        table = fast_ref if row_i is row_fast else slow_ref
        lo = _window_start(table, bi * n_e + e, t * n_blk + jj)
        hit = (row_i + lo) == rank_ref[e:e + 1, toks]
        return pl.ds(lo, row_i.shape[0]), jnp.where(hit, 1.0, 0.0).astype(BF16)

    def finish(jj, moe):
        toks = slice(jj * SLOT_TILE, (jj + 1) * SLOT_TILE)
        y_ref[toks, :] = _rms(x1_ref[toks, :] + moe, g_ref[...])

    def fast(jj):
        rhs = rhs_ref.at[jj % rhs_ref.shape[0]]
        stack = []
        for e in range(n_e):
            win, hit = hits(jj, e, row_fast)
            rhs[e * FAST_WINDOW:(e + 1) * FAST_WINDOW, :] = ye_ref[e, win, :]
            stack.append(hit)
        finish(jj, _dot_tn(jnp.concatenate(stack, axis=0), rhs[...]))

    def slow(jj):
        acc = None
        for e in range(n_e):
            win, hit = hits(jj, e, row_slow)
            part = _dot_tn(hit, ye_ref[e, win, :])
            acc = part if acc is None else acc + part
        finish(jj, acc)

    _run_blocks([fit_ref[bi * LANES + t * n_blk + jj] != 0 for jj in range(n_blk)], fast, slow)


def _combine(plan_tables, rank, ye, x1, g):
    b, s, d = x1.shape
    n_e, ye_rows = ye.shape[1], ye.shape[2]
    row = pl.BlockSpec((None, COMBINE_ROWS, d), lambda bi, t, *_: (bi, t, 0))
    grid_spec = pltpu.PrefetchScalarGridSpec(
        num_scalar_prefetch=3,
        grid=(b, s // COMBINE_ROWS),
        in_specs=[
            pl.BlockSpec((None, n_e, COMBINE_ROWS), lambda bi, t, *_: (bi, 0, t)),
            pl.BlockSpec((None, n_e, ye_rows, d), lambda bi, t, *_: (bi, 0, 0, 0)),
            row,
            pl.BlockSpec(g.shape, lambda bi, t, *_: (0, 0)),
        ],
        out_specs=row,
        scratch_shapes=[pltpu.VMEM((2, n_e * FAST_WINDOW, d), BF16)],
    )
    return pl.pallas_call(
        _combine_kernel,
        grid_spec=grid_spec,
        out_shape=jax.ShapeDtypeStruct((b, s, d), F32),
        compiler_params=_cparams(("arbitrary", "arbitrary"), 60),
        name="combine",
    )(*plan_tables, rank, ye, x1, g)


def _moe_stages(aff, h2, x1, wg, wu, wd, g_final, cap):
    rank, (gather_tables, combine_tables) = _topk(aff, cap, (GATHER_PLAN, COMBINE_PLAN))
    xe, gate = _gather(gather_tables, rank, aff, h2, cap)
    ye = _moe_ffn(xe, gate, wg, wu, wd, cap)
    return _combine(combine_tables, rank, ye, x1, g_final)


def kernel(x, norm_mix_g, w_in, b_gate, gmlp_norm_g, w_spatial, b_spatial, w_proj_a, w_proj_b,
           w_out, norm_ffn_g, w_router, w_e_gate, w_e_up, w_e_down, norm_final_g):
    b, s, d = x.shape
    assert w_in.shape[0] == 1, "single-layer block"
    cap = CAPACITY_FACTOR * s // N_EXPERTS
    group_width = GMLP_WIDTH // GMLP_GROUPS
    ws_pairs = w_spatial[0].astype(BF16).reshape(GMLP_GROUPS // 2, 2 * CHUNK, CHUNK)
    bsp = jnp.repeat(b_spatial[0].T, group_width, axis=1)
    qkv, ta, gb = _mix_in(x, norm_mix_g, w_in[0], b_gate, gmlp_norm_g, ws_pairs, bsp,
                          w_proj_a[0].astype(BF16))
    os_, ls_ = [], []
    for (q, k, v), dil in zip(qkv, DILATIONS):
        o, lse = _attn_pattern(q, k, v, dil)
        os_.append(o)
        ls_.append(lse)
    x1, h2, aff = _mix_out(x, ta, gb, os_, ls_, w_proj_b[0], w_out[0], norm_ffn_g, w_router[0].T)
    return _moe_stages(aff, h2, x1, w_e_gate[0], w_e_up[0], w_e_down[0], norm_final_g[None], cap)
```

```python
import functools

from typing import NamedTuple

import jax
import jax.numpy as jnp
from jax import lax
from jax.experimental import pallas as pl
from jax.experimental.pallas import tpu as pltpu

F32 = jnp.float32
BF16 = jnp.bfloat16
I32 = jnp.int32

EPS = 1e-6
GMLP_WIDTH = 512
GMLP_GROUPS = 8
CHUNK = 128
N_HEADS = 8
HEAD_DIM = 64
ATTN_WIDTH = N_HEADS * HEAD_DIM
DILATIONS = (1, 4, 16)
DILATION_STEP = 4
assert all(b == a * DILATION_STEP for a, b in zip(DILATIONS, DILATIONS[1:]))
HALF_WINDOW = 64
N_EXPERTS = 16
CAPACITY_FACTOR = 2

LANES = 128
Q_TILE = 128
KEY_TILE = 2 * Q_TILE
ATTN_STEP_ROWS = 2048
STAT_LANES = LANES // N_HEADS
DEN_SHIFT = STAT_LANES // 2
SLOT_TILE = 128
SLOT_ALIGN = 16
COMBINE_ROWS = 1024
FFN_SEQS = 2
GATHER_STEP_TOKENS = 2048
GATHER_TOKENS = 256
ROW_TILE = 512
MIB = 1024 * 1024


def _cparams(sem, vmem_mib):
    return pltpu.CompilerParams(dimension_semantics=sem, vmem_limit_bytes=vmem_mib * MIB)


def _gelu_tanh(x):
    return 0.5 * x * (1.0 + jnp.tanh(0.7978845608028654 * (x + 0.044715 * (x * x * x))))


def _sigmoid(x):
    return 1.0 / (1.0 + jnp.exp(-x))


def _rms(x, g):
    return x * lax.rsqrt(jnp.mean(x * x, axis=-1, keepdims=True) + EPS) * g


def _dot(a, b):
    return jnp.dot(a, b, preferred_element_type=F32)


def _dot_nt(a, b):
    return lax.dot_general(a, b, (((1,), (1,)), ((), ())), preferred_element_type=F32)


def _dot_tn(a, b):
    return lax.dot_general(a, b, (((0,), (0,)), ((), ())), preferred_element_type=F32)


def _mix_in_kernel(x_ref, g_ref, win_ref, bg_ref, g2_ref, ws_ref, bsp_ref, pa_ref, *refs):
    n_qkv = 3 * len(DILATIONS)
    qkv_refs = refs[:n_qkv]
    ta_ref, gb_ref = refs[n_qkv:n_qkv + 2]
    win_bf = refs[n_qkv + 2]
    stage_refs = refs[n_qkv + 3:]
    stage_refs = list(zip(stage_refs[0::2], stage_refs[1::2]))
    rows, d_model = x_ref.shape

    @pl.when(jnp.logical_and(pl.program_id(0) == 0, pl.program_id(1) == 0))
    def _():
        for c in range(0, win_ref.shape[1], GMLP_WIDTH):
            win_bf[:, c:c + GMLP_WIDTH] = win_ref[:, c:c + GMLP_WIDTH].astype(BF16)

    h = _rms(x_ref[...], g_ref[...]).astype(BF16)

    def proj(lo, width):
        return _dot(h, win_bf[:, lo:lo + width])

    c0 = 0
    u = _gelu_tanh(proj(c0, GMLP_WIDTH)); c0 += GMLP_WIDTH
    v = _gelu_tanh(proj(c0, GMLP_WIDTH)); c0 += GMLP_WIDTH
    for i in range(3):
        val = proj(c0, ATTN_WIDTH); c0 += ATTN_WIDTH
        if i == 0:
            val = val * (HEAD_DIM ** -0.5)
        qkv_refs[i][0] = val.astype(BF16)
        stage1, stage2 = stage_refs[i]
        n4, n16 = rows // DILATIONS[1], rows // DILATIONS[2]
        out4, out16 = qkv_refs[3 + i], qkv_refs[6 + i]
        for p in range(ATTN_WIDTH // LANES):
            cs = slice(p * LANES, (p + 1) * LANES)
            stage1[p] = val[:, cs]
            for r4 in range(DILATION_STEP):
                part = stage1[p, pl.ds(r4, n4, stride=DILATION_STEP), :]
                out4[r4, :, cs] = part.astype(BF16)
                stage2[p, r4] = part
                for c in range(DILATION_STEP):
                    out16[r4 + DILATION_STEP * c, :, cs] = (
                        stage2[p, r4, pl.ds(c, n16, stride=DILATION_STEP), :].astype(BF16))
    ga = _sigmoid(proj(c0, d_model) + bg_ref[:, :d_model]); c0 += d_model
    gb = _sigmoid(proj(c0, d_model) + bg_ref[:, d_model:])
    gb_ref[...] = gb.astype(BF16)

    vn = _rms(v, g2_ref[...]).astype(BF16)
    lane_lo = lax.broadcasted_iota(I32, (CHUNK, LANES), 1) < HEAD_DIM
    bsp = bsp_ref[...]
    n_chunk = rows // CHUNK
    mixed_slabs = []
    for p in range(GMLP_WIDTH // LANES):
        slab = jnp.concatenate(
            [vn[c * CHUNK:(c + 1) * CHUNK, p * LANES:(p + 1) * LANES] for c in range(n_chunk)],
            axis=1)
        r = _dot(ws_ref[p], slab)
        mixed_slabs.append([jnp.where(lane_lo, r[:CHUNK, c * LANES:(c + 1) * LANES],
                                      r[CHUNK:, c * LANES:(c + 1) * LANES])
                            for c in range(n_chunk)])
    a_chunks = []
    for c in range(n_chunk):
        rs = slice(c * CHUNK, (c + 1) * CHUNK)
        mixed = jnp.concatenate([slabs[c] for slabs in mixed_slabs], axis=1) + bsp
        a_chunks.append((u[rs] * mixed).astype(BF16))
    a = jnp.concatenate(a_chunks, axis=0)
    ta_ref[...] = (ga * _dot(a, pa_ref[...])).astype(BF16)


def _mix_in(x, g, w_in, b_gate, g2, ws_pairs, bsp, w_pa):
    b, s, d = x.shape
    const = lambda shape: pl.BlockSpec(shape, lambda bi, t: (0,) * len(shape))
    row = lambda w: pl.BlockSpec((None, ROW_TILE, w), lambda bi, t: (bi, t, 0))
    qkv_specs, qkv_shapes = [], []
    for dil in DILATIONS:
        spec = pl.BlockSpec((None, dil, ROW_TILE // dil, ATTN_WIDTH), lambda bi, t: (bi, 0, t, 0))
        qkv_specs += [spec] * 3
        qkv_shapes += [jax.ShapeDtypeStruct((b, dil, s // dil, ATTN_WIDTH), BF16)] * 3
    outs = pl.pallas_call(
        _mix_in_kernel,
        grid=(b, s // ROW_TILE),
        in_specs=[row(d), const(g.shape),
                  pl.BlockSpec(w_in.shape, lambda bi, t: (0, 0), pipeline_mode=pl.Buffered(1)),
                  const(b_gate.shape), const(g2.shape),
                  const(ws_pairs.shape), const(bsp.shape), const(w_pa.shape)],
        out_specs=qkv_specs + [row(d), row(d)],
        out_shape=qkv_shapes + [jax.ShapeDtypeStruct((b, s, d), BF16)] * 2,
        scratch_shapes=[pltpu.VMEM(w_in.shape, BF16)]
                       + [pltpu.VMEM((ATTN_WIDTH // LANES, ROW_TILE, LANES), F32),
                          pltpu.VMEM((ATTN_WIDTH // LANES, DILATION_STEP,
                                      ROW_TILE // DILATION_STEP, LANES), F32)] * 3,
        compiler_params=_cparams(("arbitrary", "arbitrary"), 60),
        name="mix_in",
    )(x, g, w_in, b_gate, g2, ws_pairs, bsp, w_pa)
    n_qkv = 3 * len(DILATIONS)
    qkv = [outs[3 * i:3 * i + 3] for i in range(len(DILATIONS))]
    return qkv, outs[n_qkv], outs[n_qkv + 1]


def _attn_kernel(q_ref, k_ref, v_ref, o_ref, l_ref, bias_ref, *, dil):
    n_res, rows, _ = q_ref.shape
    seq = k_ref.shape[1]
    t = pl.program_id(2)
    first = jnp.logical_and(jnp.logical_and(pl.program_id(0) == 0, pl.program_id(1) == 0), t == 0)

    @pl.when(first)
    def _():
        ii = lax.broadcasted_iota(I32, (Q_TILE, KEY_TILE), 0)
        jj = lax.broadcasted_iota(I32, (Q_TILE, KEY_TILE), 1)
        for var in range(3):
            absd = jnp.abs(jj - ii - var * HALF_WINDOW)
            valid = absd <= HALF_WINDOW
            absf = absd.astype(F32)
            for h in range(N_HEADS):
                slope = 2.0 ** (-8.0 * (h + 1) / N_HEADS)
                bias_ref[var, h] = jnp.where(valid, -(slope * dil) * absf, -jnp.inf)

    lane = lax.broadcasted_iota(I32, (Q_TILE, LANES), 1)
    lane_lo = lane < HEAD_DIM
    mask_lo = jnp.where(lane_lo, 1.0, 0.0).astype(BF16)
    mask_hi = jnp.where(lane_lo, 0.0, 1.0).astype(BF16)
    for rr in range(n_res):
        for qi in range(rows // Q_TILE):
            rs = slice(qi * Q_TILE, (qi + 1) * Q_TILE)
            i0 = t * rows + qi * Q_TILE
            start = pl.multiple_of(jnp.clip(i0 - HALF_WINDOW, 0, seq - KEY_TILE), HALF_WINDOW)
            var = (i0 - start) // HALF_WINDOW
            for p in range(ATTN_WIDTH // LANES):
                cs = slice(p * LANES, (p + 1) * LANES)
                qp = q_ref[rr, rs, cs]
                kp = k_ref[rr, pl.ds(start, KEY_TILE), cs]
                vp = v_ref[rr, pl.ds(start, KEY_TILE), cs]
                q2 = jnp.concatenate([qp * mask_lo, qp * mask_hi], axis=0)
                s2 = _dot_nt(q2, kp)
                probs = []
                for hh in range(2):
                    h = 2 * p + hh
                    s = s2[hh * Q_TILE:(hh + 1) * Q_TILE] + bias_ref[var, h]
                    m = jnp.max(s, axis=-1, keepdims=True)
                    e = jnp.exp(s - m)
                    den = jnp.sum(e, axis=-1, keepdims=True)
                    probs.append(e)
                    lo = h * STAT_LANES
                    l_ref[rr, rs, lo:lo + DEN_SHIFT] = jnp.broadcast_to(m, (Q_TILE, DEN_SHIFT))
                    l_ref[rr, rs, lo + DEN_SHIFT:lo + STAT_LANES] = jnp.broadcast_to(
                        den, (Q_TILE, DEN_SHIFT))
                o2 = _dot(jnp.concatenate(probs, axis=0).astype(BF16), vp)
                o_ref[rr, rs, cs] = jnp.where(lane_lo, o2[:Q_TILE], o2[Q_TILE:]).astype(BF16)


def _attn_pattern(q, k, v, dil):
    b, _, seq, w = q.shape
    rows = min(seq, ATTN_STEP_ROWS)
    n_res = ATTN_STEP_ROWS // rows
    qspec = lambda width: pl.BlockSpec((None, n_res, rows, width), lambda bi, r, t: (bi, r, t, 0))
    kspec = pl.BlockSpec((None, n_res, seq, w), lambda bi, r, t: (bi, r, 0, 0))
    return pl.pallas_call(
        functools.partial(_attn_kernel, dil=dil),
        grid=(b, dil // n_res, seq // rows),
        in_specs=[qspec(w), kspec, kspec],
        out_specs=[qspec(w), qspec(LANES)],
        out_shape=[jax.ShapeDtypeStruct(q.shape, BF16),
                   jax.ShapeDtypeStruct((b, dil, seq, LANES), F32)],
        scratch_shapes=[pltpu.VMEM((3, N_HEADS, Q_TILE, KEY_TILE), F32)],
        compiler_params=_cparams(("arbitrary", "arbitrary", "arbitrary"), 48),
        name=f"attn_d{dil}",
    )(q, k, v)


def _to_natural(src_ref, nat_ref, tmp_ref, rows):
    n_slab = nat_ref.shape[0]
    step = DILATION_STEP
    for p in range(n_slab):
        cs = slice(p * LANES, (p + 1) * LANES)
        for r4 in range(step):
            if tmp_ref is None:
                quarter = src_ref[r4][:, cs].astype(F32)
            else:
                for c in range(step):
                    tmp_ref[p, r4, pl.ds(c, rows // (step * step), stride=step), :] = (
                        src_ref[r4 + step * c][:, cs].astype(F32))
                quarter = tmp_ref[p, r4]
            nat_ref[p, pl.ds(r4, rows // step, stride=step), :] = quarter
    return jnp.concatenate([nat_ref[p] for p in range(n_slab)], axis=1)


def _mix_out_kernel(x_ref, ta_ref, gb_ref, *refs):
    n_pat = len(DILATIONS)
    o_refs = refs[:n_pat]
    l_refs = refs[n_pat:2 * n_pat]
    pb_f32, wo_f32, g_ref, wr_ref, x1_ref, h2_ref, aff_ref = refs[2 * n_pat:2 * n_pat + 7]
    pb_ref, wo_ref = refs[2 * n_pat + 7:2 * n_pat + 9]
    stage_refs = refs[2 * n_pat + 9:]
    rows = x_ref.shape[0]

    @pl.when(jnp.logical_and(pl.program_id(0) == 0, pl.program_id(1) == 0))
    def _():
        pb_ref[...] = pb_f32[...].astype(BF16)
        wo_ref[...] = wo_f32[...].astype(BF16)

    stage_refs = list(stage_refs)
    outs = [o_refs[0][0].astype(F32)]
    lses = [l_refs[0][0]]
    for di in (1, 2):
        for src, dest in ((o_refs[di], outs), (l_refs[di], lses)):
            nat = stage_refs.pop(0)
            tmp = stage_refs.pop(0) if di == 2 else None
            dest.append(_to_natural(src, nat, tmp, rows))

    dens = [pltpu.roll(st, LANES - DEN_SHIFT, 1) for st in lses]
    lses = [st + jnp.log(den) for st, den in zip(lses, dens)]
    m = functools.reduce(jnp.maximum, lses)
    ws = [jnp.exp(l - m) for l in lses]
    inv = 1.0 / functools.reduce(lambda a, c: a + c, ws)
    lane = lax.broadcasted_iota(I32, (rows, LANES), 1)
    used = lane % STAT_LANES < DEN_SHIFT
    ws = [jnp.where(used, w * inv / den, 0.0) for w, den in zip(ws, dens)]
    k_i = lax.broadcasted_iota(I32, (2 * LANES, ATTN_WIDTH), 0) % LANES
    c_i = lax.broadcasted_iota(I32, (2 * LANES, ATTN_WIDTH), 1)
    spread = jnp.where(k_i == (c_i // HEAD_DIM) * STAT_LANES, 1.0, 0.0).astype(BF16)
    parts = []
    for w in ws:
        w_hi = w.astype(BF16)
        w_lo = (w - w_hi.astype(F32)).astype(BF16)
        parts.append(jnp.concatenate([w_hi, w_lo], axis=1))
    factors = _dot(jnp.concatenate(parts, axis=0), spread)
    o = None
    for i, o_p in enumerate(outs):
        term = factors[i * rows:(i + 1) * rows] * o_p
        o = term if o is None else o + term

    ob = _dot(o.astype(BF16), pb_ref[...])
    merged = (ta_ref[...].astype(F32) + gb_ref[...].astype(F32) * ob).astype(BF16)
    x1 = x_ref[...] + _dot(merged, wo_ref[...])
    x1_ref[...] = x1
    h2 = _rms(x1, g_ref[...])
    h2_ref[...] = h2.astype(BF16)
    h_hi = h2.astype(BF16)
    h_lo = (h2 - h_hi.astype(F32)).astype(BF16)
    wr = wr_ref[...]
    w_hi = wr.astype(BF16)
    w_lo = (wr - w_hi.astype(F32)).astype(BF16)
    n_e = wr.shape[0]
    by_hi = _dot_nt(jnp.concatenate([w_hi, w_lo], axis=0), h_hi)
    logits = by_hi[:n_e] + (_dot_nt(w_hi, h_lo) + by_hi[n_e:])
    e = jnp.exp(logits - jnp.max(logits, axis=0, keepdims=True))
    aff_ref[...] = e / jnp.sum(e, axis=0, keepdims=True)


def _mix_out(x, ta, gb, os_, ls_, w_pb, w_out, g, w_router_t):
    b, s, d = x.shape
    n_e = w_router_t.shape[0]
    const = lambda shape: pl.BlockSpec(shape, lambda bi, t: (0,) * len(shape))
    once = lambda shape: pl.BlockSpec(shape, lambda bi, t: (0,) * len(shape),
                                      pipeline_mode=pl.Buffered(1))
    row = lambda w: pl.BlockSpec((None, ROW_TILE, w), lambda bi, t: (bi, t, 0))
    res = lambda dil, w: pl.BlockSpec((None, dil, ROW_TILE // dil, w), lambda bi, t: (bi, 0, t, 0))
    stage = []
    for di in (1, 2):
        for slabs in (ATTN_WIDTH // LANES, 1):
            stage.append(pltpu.VMEM((slabs, ROW_TILE, LANES), F32))
            if di == 2:
                stage.append(pltpu.VMEM((slabs, DILATION_STEP, ROW_TILE // DILATION_STEP, LANES),
                                        F32))
    return pl.pallas_call(
        _mix_out_kernel,
        grid=(b, s // ROW_TILE),
        in_specs=[row(d), row(d), row(d)]
                 + [res(dil, ATTN_WIDTH) for dil in DILATIONS]
                 + [res(dil, LANES) for dil in DILATIONS]
                 + [once(w_pb.shape), once(w_out.shape), const(g.shape), const(w_router_t.shape)],
        out_specs=[row(d), row(d), pl.BlockSpec((None, n_e, ROW_TILE), lambda bi, t: (bi, 0, t))],
        out_shape=[jax.ShapeDtypeStruct((b, s, d), F32), jax.ShapeDtypeStruct((b, s, d), BF16),
                   jax.ShapeDtypeStruct((b, n_e, s), F32)],
        scratch_shapes=[pltpu.VMEM(w_pb.shape, BF16), pltpu.VMEM(w_out.shape, BF16)] + stage,
        compiler_params=_cparams(("arbitrary", "arbitrary"), 48),
        name="mix_out",
    )(x, ta, gb, *os_, *ls_, w_pb, w_out, g, w_router_t)


class _Plan(NamedTuple):
    tiles: int
    fast: int
    slow: int
    group: int


GATHER_PLAN = _Plan(tiles=GATHER_TOKENS // SLOT_TILE, fast=80, slow=GATHER_TOKENS + SLOT_ALIGN,
                    group=N_EXPERTS // 2)
COMBINE_PLAN = _Plan(tiles=1, fast=48, slow=SLOT_TILE + SLOT_ALIGN, group=N_EXPERTS)
FAST_WINDOW, SLOT_WINDOW = COMBINE_PLAN.fast, COMBINE_PLAN.slow


def _slot_plan(cum, plan, cap):
    rows = cum.shape[0]
    nxt = pltpu.roll(cum, LANES - plan.tiles, 1)
    lo = jnp.floor(cum * (1.0 / SLOT_ALIGN)) * SLOT_ALIGN
    ok = jnp.where(nxt - lo <= plan.fast, 1.0, 0.0)
    fit = jnp.min(ok.reshape(rows // plan.group, plan.group, LANES), axis=1)
    assert (cap - plan.fast) % SLOT_ALIGN == 0 and (cap - plan.slow) % SLOT_ALIGN == 0
    return (jnp.minimum(lo, cap - plan.fast).astype(I32),
            jnp.minimum(lo, cap - plan.slow).astype(I32), fit.astype(I32))


def _topk_kernel(aff_ref, rank_ref, *plan_refs, cap, plans):
    n_e, s = aff_ref.shape
    n_blk = s // SLOT_TILE
    aff = aff_ref[...]
    thr = jnp.zeros((n_e, 1), I32)
    for bit in range(30, -1, -1):
        cand = thr | (1 << bit)
        cnt = jnp.sum((aff >= pltpu.bitcast(cand, F32)).astype(I32), axis=1, keepdims=True)
        thr = jnp.where(cnt >= cap, cand, thr)
    above = aff >= pltpu.bitcast(thr + 1, F32)
    tie = jnp.logical_and(aff >= pltpu.bitcast(thr, F32), jnp.logical_not(above))
    need = (cap - jnp.sum(above.astype(I32), axis=1, keepdims=True)).astype(F32)
    r_i = lax.broadcasted_iota(I32, (SLOT_TILE, SLOT_TILE), 0)
    c_i = lax.broadcasted_iota(I32, (SLOT_TILE, SLOT_TILE), 1)
    tri = jnp.where(r_i < c_i, 1.0, 0.0).astype(BF16)
    lane = lax.broadcasted_iota(I32, (n_e, LANES), 1)
    run_tie = jnp.zeros((n_e, 1), F32)
    run_sel = jnp.zeros((n_e, 1), F32)
    cum = jnp.zeros((n_e, LANES), F32)
    for j in range(n_blk):
        cs = slice(j * SLOT_TILE, (j + 1) * SLOT_TILE)
        tie_f = jnp.where(tie[:, cs], 1.0, 0.0)
        tie_rank = _dot(tie_f.astype(BF16), tri) + run_tie
        run_tie = run_tie + jnp.sum(tie_f, axis=1, keepdims=True)
        sel_f = jnp.where(above[:, cs], 1.0, jnp.where(tie_rank < need, tie_f, 0.0))
        rank = _dot(sel_f.astype(BF16), tri) + run_sel
        rank_ref[:, cs] = jnp.where(sel_f > 0.0, rank, -1.0).astype(I32)
        cum = jnp.where(lane == j, run_sel, cum)
        run_sel = run_sel + jnp.sum(sel_f, axis=1, keepdims=True)
    cum = jnp.where(lane == n_blk, run_sel, cum)
    for i, plan in enumerate(plans):
        for ref, val in zip(plan_refs[3 * i:3 * i + 3], _slot_plan(cum, plan, cap)):
            ref[...] = val


def _topk(aff, cap, plans):
    b, n_e, s = aff.shape
    rows = b * n_e
    full = lambda r, w: pl.BlockSpec((r, w), lambda i: (0, 0))
    plan_specs, plan_shapes = [], []
    for plan in plans:
        for r in (rows, rows, rows // plan.group):
            plan_specs.append(full(r, LANES))
            plan_shapes.append(jax.ShapeDtypeStruct((r, LANES), I32))
    rank, *tables = pl.pallas_call(
        functools.partial(_topk_kernel, cap=cap, plans=plans),
        grid=(1,),
        in_specs=[full(rows, s)],
        out_specs=[full(rows, s)] + plan_specs,
        out_shape=[jax.ShapeDtypeStruct((rows, s), I32)] + plan_shapes,
        compiler_params=_cparams(("arbitrary",), 32),
        name="topk",
    )(aff.reshape(rows, s))
    tables = [t.reshape(-1) for t in tables]
    return rank.reshape(b, n_e, s), [tables[3 * i:3 * i + 3] for i in range(len(plans))]


def _window_start(table_ref, expert_row, tile):
    return pl.multiple_of(table_ref[expert_row * LANES + tile], SLOT_ALIGN)


def _run_blocks(fits, fast, slow):
    all_fit = functools.reduce(jnp.logical_and, fits)

    @pl.when(all_fit)
    def _():
        for jj in range(len(fits)):
            fast(jj)

    @pl.when(jnp.logical_not(all_fit))
    def _():
        for jj, fit in enumerate(fits):
            pl.when(fit)(functools.partial(fast, jj))
            pl.when(jnp.logical_not(fit))(functools.partial(slow, jj))


def _gather_kernel(fast_ref, slow_ref, fit_ref, rank_ref, aff_ref, h2_ref, xe_ref, gate_ref):
    n_e = rank_ref.shape[0]
    rows = h2_ref.shape[0]
    group = pl.program_id(0) * pl.num_programs(1) + pl.program_id(1)
    t = pl.program_id(2)

    @pl.when(t == 0)
    def _():
        xe_ref[...] = jnp.zeros_like(xe_ref)
        gate_ref[...] = jnp.zeros_like(gate_ref)

    row_fast = lax.broadcasted_iota(I32, (GATHER_PLAN.fast, GATHER_TOKENS), 0)
    row_slow = lax.broadcasted_iota(I32, (GATHER_PLAN.slow, GATHER_TOKENS), 0)
    n_blk = rows // GATHER_TOKENS
    tile = lambda jj: (t * n_blk + jj) * GATHER_PLAN.tiles

    def hit(jj, e, row_i):
        toks = slice(jj * GATHER_TOKENS, (jj + 1) * GATHER_TOKENS)
        table = fast_ref if row_i is row_fast else slow_ref
        lo = _window_start(table, group * n_e + e, tile(jj))
        return lo, (row_i + lo) == rank_ref[e:e + 1, toks]

    def add_window(jj, e, lo, hit_e, rows_e):
        toks = slice(jj * GATHER_TOKENS, (jj + 1) * GATHER_TOKENS)
        win = pl.ds(lo, hit_e.shape[0])
        xe_ref[e, win, :] += rows_e.astype(BF16)
        gate_ref[e, win, :] += jnp.sum(jnp.where(hit_e, aff_ref[e:e + 1, toks], 0.0), axis=1,
                                       keepdims=True)

    def fast(jj):
        hs = [hit(jj, e, row_fast) for e in range(n_e)]
        stack = jnp.concatenate([jnp.where(h, 1.0, 0.0).astype(BF16) for _, h in hs], axis=0)
        res = _dot(stack, h2_ref[jj * GATHER_TOKENS:(jj + 1) * GATHER_TOKENS, :])
        for e, (lo, h) in enumerate(hs):
            add_window(jj, e, lo, h, res[e * GATHER_PLAN.fast:(e + 1) * GATHER_PLAN.fast])

    def slow(jj):
        for e in range(n_e):
            lo, h = hit(jj, e, row_slow)
            add_window(jj, e, lo, h,
                       _dot(jnp.where(h, 1.0, 0.0).astype(BF16),
                            h2_ref[jj * GATHER_TOKENS:(jj + 1) * GATHER_TOKENS, :]))

    _run_blocks([fit_ref[group * LANES + tile(jj)] != 0 for jj in range(n_blk)], fast, slow)


def _gather(plan_tables, rank, aff, h2, cap):
    b, s, d = h2.shape
    n_e = rank.shape[1]
    rows = cap
    grp = GATHER_PLAN.group
    per_tok = pl.BlockSpec((None, grp, GATHER_STEP_TOKENS), lambda bi, eg, t, *_: (bi, eg, t))
    whole = lambda w: pl.BlockSpec((None, grp, rows, w), lambda bi, eg, t, *_: (bi, eg, 0, 0))
    grid_spec = pltpu.PrefetchScalarGridSpec(
        num_scalar_prefetch=3,
        grid=(b, n_e // grp, s // GATHER_STEP_TOKENS),
        in_specs=[per_tok, per_tok,
                  pl.BlockSpec((None, GATHER_STEP_TOKENS, d), lambda bi, eg, t, *_: (bi, t, 0))],
        out_specs=[whole(d), whole(LANES)],
    )
    return pl.pallas_call(
        _gather_kernel,
        grid_spec=grid_spec,
        out_shape=[jax.ShapeDtypeStruct((b, n_e, rows, d), BF16),
                   jax.ShapeDtypeStruct((b, n_e, rows, LANES), F32)],
        compiler_params=_cparams(("arbitrary", "arbitrary", "arbitrary"), 56),
        name="gather",
    )(*plan_tables, rank, aff, h2)


def _moe_ffn_kernel(xe_ref, gate_ref, wg_ref, wu_ref, wd_ref, ye_ref, wg_bf, wu_bf, wd_bf):
    n_seq, cap, d = xe_ref.shape

    @pl.when(pl.program_id(1) == 0)
    def _():
        wg_bf[...] = wg_ref[...].astype(BF16)
        wu_bf[...] = wu_ref[...].astype(BF16)
        wd_bf[...] = wd_ref[...].astype(BF16)

    xe = xe_ref[...].reshape(n_seq * cap, d)
    gate_h = _dot(xe, wg_bf[...])
    up_h = _dot(xe, wu_bf[...])
    hidden = (gate_h * _sigmoid(gate_h) * up_h).astype(BF16)
    ye = _dot(hidden, wd_bf[...]) * gate_ref[...].reshape(n_seq * cap, LANES)[:, 0:1]
    ye_ref[...] = ye.astype(BF16).reshape(n_seq, cap, d)


def _moe_ffn(xe, gate, wg, wu, wd, cap):
    b, n_e, rows, d = xe.shape
    hid = wg.shape[2]
    return pl.pallas_call(
        _moe_ffn_kernel,
        grid=(n_e, b // FFN_SEQS),
        in_specs=[
            pl.BlockSpec((FFN_SEQS, None, cap, d), lambda e, bi: (bi, e, 0, 0)),
            pl.BlockSpec((FFN_SEQS, None, cap, LANES), lambda e, bi: (bi, e, 0, 0)),
            pl.BlockSpec((None, d, hid), lambda e, bi: (e, 0, 0)),
            pl.BlockSpec((None, d, hid), lambda e, bi: (e, 0, 0)),
            pl.BlockSpec((None, hid, d), lambda e, bi: (e, 0, 0)),
        ],
        out_specs=pl.BlockSpec((FFN_SEQS, None, rows, d), lambda e, bi: (bi, e, 0, 0)),
        out_shape=jax.ShapeDtypeStruct((b, n_e, rows, d), BF16),
        scratch_shapes=[pltpu.VMEM((d, hid), BF16), pltpu.VMEM((d, hid), BF16),
                        pltpu.VMEM((hid, d), BF16)],
        compiler_params=_cparams(("arbitrary", "arbitrary"), 56),
        name="moe_ffn",
    )(xe, gate, wg, wu, wd)


def _combine_kernel(fast_ref, slow_ref, fit_ref, rank_ref, ye_ref, x1_ref, g_ref, y_ref, rhs_ref):
    n_e = rank_ref.shape[0]
    n_blk = x1_ref.shape[0] // SLOT_TILE
    bi = pl.program_id(0)
    t = pl.program_id(1)
    row_fast = lax.broadcasted_iota(I32, (FAST_WINDOW, SLOT_TILE), 0)
    row_slow = lax.broadcasted_iota(I32, (SLOT_WINDOW, SLOT_TILE), 0)

    def hits(jj, e, row_i):
        toks = slice(jj * SLOT_TILE, (jj + 1) * SLOT_TILE)
        table = fast_ref if row_i is row_fast else slow_ref
        lo = _window_start(table, bi * n_e + e, t * n_blk + jj)
        hit = (row_i + lo) == rank_ref[e:e + 1, toks]
        return pl.ds(lo, row_i.shape[0]), jnp.where(hit, 1.0, 0.0).astype(BF16)

    def finish(jj, moe):
        toks = slice(jj * SLOT_TILE, (jj + 1) * SLOT_TILE)
        y_ref[toks, :] = _rms(x1_ref[toks, :] + moe, g_ref[...])

    def fast(jj):
        rhs = rhs_ref.at[jj % rhs_ref.shape[0]]
        stack = []
        for e in range(n_e):
            win, hit = hits(jj, e, row_fast)
            rhs[e * FAST_WINDOW:(e + 1) * FAST_WINDOW, :] = ye_ref[e, win, :]
            stack.append(hit)
        finish(jj, _dot_tn(jnp.concatenate(stack, axis=0), rhs[...]))

    def slow(jj):
        acc = None
        for e in range(n_e):
            win, hit = hits(jj, e, row_slow)
            part = _dot_tn(hit, ye_ref[e, win, :])
            acc = part if acc is None else acc + part
        finish(jj, acc)

    _run_blocks([fit_ref[bi * LANES + t * n_blk + jj] != 0 for jj in range(n_blk)], fast, slow)


def _combine(plan_tables, rank, ye, x1, g):
    b, s, d = x1.shape
    n_e, ye_rows = ye.shape[1], ye.shape[2]
    row = pl.BlockSpec((None, COMBINE_ROWS, d), lambda bi, t, *_: (bi, t, 0))
    grid_spec = pltpu.PrefetchScalarGridSpec(
        num_scalar_prefetch=3,
        grid=(b, s // COMBINE_ROWS),
        in_specs=[
            pl.BlockSpec((None, n_e, COMBINE_ROWS), lambda bi, t, *_: (bi, 0, t)),
            pl.BlockSpec((None, n_e, ye_rows, d), lambda bi, t, *_: (bi, 0, 0, 0)),
            row,
            pl.BlockSpec(g.shape, lambda bi, t, *_: (0, 0)),
        ],
        out_specs=row,
        scratch_shapes=[pltpu.VMEM((2, n_e * FAST_WINDOW, d), BF16)],
    )
    return pl.pallas_call(
        _combine_kernel,
        grid_spec=grid_spec,
        out_shape=jax.ShapeDtypeStruct((b, s, d), F32),
        compiler_params=_cparams(("arbitrary", "arbitrary"), 60),
        name="combine",
    )(*plan_tables, rank, ye, x1, g)


def _moe_stages(aff, h2, x1, wg, wu, wd, g_final, cap):
    rank, (gather_tables, combine_tables) = _topk(aff, cap, (GATHER_PLAN, COMBINE_PLAN))
    xe, gate = _gather(gather_tables, rank, aff, h2, cap)
    ye = _moe_ffn(xe, gate, wg, wu, wd, cap)
    return _combine(combine_tables, rank, ye, x1, g_final)


def kernel(x, norm_mix_g, w_in, b_gate, gmlp_norm_g, w_spatial, b_spatial, w_proj_a, w_proj_b,
           w_out, norm_ffn_g, w_router, w_e_gate, w_e_up, w_e_down, norm_final_g):
    b, s, d = x.shape
    assert w_in.shape[0] == 1, "single-layer block"
    cap = CAPACITY_FACTOR * s // N_EXPERTS
    group_width = GMLP_WIDTH // GMLP_GROUPS
    ws_pairs = w_spatial[0].astype(BF16).reshape(GMLP_GROUPS // 2, 2 * CHUNK, CHUNK)
    bsp = jnp.repeat(b_spatial[0].T, group_width, axis=1)
    qkv, ta, gb = _mix_in(x, norm_mix_g, w_in[0], b_gate, gmlp_norm_g, ws_pairs, bsp,
                          w_proj_a[0].astype(BF16))
    os_, ls_ = [], []
    for (q, k, v), dil in zip(qkv, DILATIONS):
        o, lse = _attn_pattern(q, k, v, dil)
        os_.append(o)
        ls_.append(lse)
    x1, h2, aff = _mix_out(x, ta, gb, os_, ls_, w_proj_b[0], w_out[0], norm_ffn_g, w_router[0].T)
    return _moe_stages(aff, h2, x1, w_e_gate[0], w_e_up[0], w_e_down[0], norm_final_g[None], cap)
```

```python
import functools

from typing import NamedTuple

import jax
import jax.numpy as jnp
from jax import lax
from jax.experimental import pallas as pl
from jax.experimental.pallas import tpu as pltpu

F32 = jnp.float32
BF16 = jnp.bfloat16
I32 = jnp.int32

EPS = 1e-6
GMLP_WIDTH = 512
GMLP_GROUPS = 8
CHUNK = 128
N_HEADS = 8
HEAD_DIM = 64
ATTN_WIDTH = N_HEADS * HEAD_DIM
DILATIONS = (1, 4, 16)
DILATION_STEP = 4
assert all(b == a * DILATION_STEP for a, b in zip(DILATIONS, DILATIONS[1:]))
HALF_WINDOW = 64
N_EXPERTS = 16
CAPACITY_FACTOR = 2

LANES = 128
Q_TILE = 128
KEY_TILE = 2 * Q_TILE
ATTN_STEP_ROWS = 2048
STAT_LANES = LANES // N_HEADS
DEN_SHIFT = STAT_LANES // 2
SLOT_TILE = 128
SLOT_ALIGN = 16
COMBINE_ROWS = 1024
FFN_SEQS = 2
GATHER_STEP_TOKENS = 2048
GATHER_TOKENS = 256
ROW_TILE = 512
MIX_IN_ROWS = 1024
MIB = 1024 * 1024


def _cparams(sem, vmem_mib):
    return pltpu.CompilerParams(dimension_semantics=sem, vmem_limit_bytes=vmem_mib * MIB)


def _gelu_tanh(x):
    return 0.5 * x * (1.0 + jnp.tanh(0.7978845608028654 * (x + 0.044715 * (x * x * x))))


def _sigmoid(x):
    return 1.0 / (1.0 + jnp.exp(-x))


def _rms(x, g):
    return x * lax.rsqrt(jnp.mean(x * x, axis=-1, keepdims=True) + EPS) * g


def _dot(a, b):
    return jnp.dot(a, b, preferred_element_type=F32)


def _dot_nt(a, b):
    return lax.dot_general(a, b, (((1,), (1,)), ((), ())), preferred_element_type=F32)


def _dot_tn(a, b):
    return lax.dot_general(a, b, (((0,), (0,)), ((), ())), preferred_element_type=F32)


def _mix_qkv_kernel(x_ref, g_ref, w_ref, bg_ref, *refs):
    n_qkv = 3 * len(DILATIONS)
    qkv_refs = refs[:n_qkv]
    gb_ref, stage1, stage2 = refs[n_qkv:]
    rows, d_model = x_ref.shape
    h = _rms(x_ref[...], g_ref[...]).astype(BF16)
    for i in range(3):
        val = _dot(h, w_ref[:, i * ATTN_WIDTH:(i + 1) * ATTN_WIDTH])
        if i == 0:
            val = val * (HEAD_DIM ** -0.5)
        qkv_refs[i][0] = val.astype(BF16)
        n4, n16 = rows // DILATIONS[1], rows // DILATIONS[2]
        out4, out16 = qkv_refs[3 + i], qkv_refs[6 + i]
        for p in range(ATTN_WIDTH // LANES):
            cs = slice(p * LANES, (p + 1) * LANES)
            stage1[p] = val[:, cs]
            for r4 in range(DILATION_STEP):
                part = stage1[p, pl.ds(r4, n4, stride=DILATION_STEP), :]
                out4[r4, :, cs] = part.astype(BF16)
                stage2[p, r4] = part
                for c in range(DILATION_STEP):
                    out16[r4 + DILATION_STEP * c, :, cs] = (
                        stage2[p, r4, pl.ds(c, n16, stride=DILATION_STEP), :].astype(BF16))
    gb = _sigmoid(_dot(h, w_ref[:, 3 * ATTN_WIDTH:]) + bg_ref[:, d_model:])
    gb_ref[...] = gb.astype(BF16)


def _mix_gmlp_kernel(x_ref, g_ref, wuv_ref, wga0_ref, wga1_ref, bg_ref, g2_ref, ws_ref, bsp_ref,
                     pa_ref, ta_ref, w_bf):
    rows, d_model = x_ref.shape
    half = d_model // 2

    @pl.when(jnp.logical_and(pl.program_id(0) == 0, pl.program_id(1) == 0))
    def _():
        w_bf[:, :2 * GMLP_WIDTH] = wuv_ref[...].astype(BF16)
        w_bf[:, 2 * GMLP_WIDTH:2 * GMLP_WIDTH + half] = wga0_ref[...].astype(BF16)
        w_bf[:, 2 * GMLP_WIDTH + half:] = wga1_ref[...].astype(BF16)

    h = _rms(x_ref[...], g_ref[...]).astype(BF16)
    u = _gelu_tanh(_dot(h, w_bf[:, :GMLP_WIDTH]))
    v = _gelu_tanh(_dot(h, w_bf[:, GMLP_WIDTH:2 * GMLP_WIDTH]))
    ga = _sigmoid(_dot(h, w_bf[:, 2 * GMLP_WIDTH:]) + bg_ref[:, :d_model])

    vn = _rms(v, g2_ref[...]).astype(BF16)
    lane_lo = lax.broadcasted_iota(I32, (CHUNK, LANES), 1) < HEAD_DIM
    bsp = bsp_ref[...]
    n_chunk = rows // CHUNK
    mixed_slabs = []
    for p in range(GMLP_WIDTH // LANES):
        slab = jnp.concatenate(
            [vn[c * CHUNK:(c + 1) * CHUNK, p * LANES:(p + 1) * LANES] for c in range(n_chunk)],
            axis=1)
        r = _dot(ws_ref[p], slab)
        mixed_slabs.append([jnp.where(lane_lo, r[:CHUNK, c * LANES:(c + 1) * LANES],
                                      r[CHUNK:, c * LANES:(c + 1) * LANES])
                            for c in range(n_chunk)])
    a_chunks = []
    for c in range(n_chunk):
        rs = slice(c * CHUNK, (c + 1) * CHUNK)
        mixed = jnp.concatenate([slabs[c] for slabs in mixed_slabs], axis=1) + bsp
        a_chunks.append((u[rs] * mixed).astype(BF16))
    a = jnp.concatenate(a_chunks, axis=0)
    ta_ref[...] = (ga * _dot(a, pa_ref[...])).astype(BF16)


def _mix_in(x, g, w_in, b_gate, g2, ws_pairs, bsp, w_pa):
    b, s, d = x.shape
    rows = MIX_IN_ROWS
    const = lambda shape: pl.BlockSpec(shape, lambda bi, t: (0,) * len(shape))
    once = lambda shape, col=0: pl.BlockSpec(shape, lambda bi, t: (0, col),
                                             pipeline_mode=pl.Buffered(1))
    row = lambda w: pl.BlockSpec((None, rows, w), lambda bi, t: (bi, t, 0))
    uv, qkv_w, gate = 2 * GMLP_WIDTH, 3 * ATTN_WIDTH, d

    w_qkv = jnp.concatenate([w_in[:, uv:uv + qkv_w], w_in[:, uv + qkv_w + gate:]],
                            axis=1).astype(BF16)
    qkv_specs, qkv_shapes = [], []
    for dil in DILATIONS:
        spec = pl.BlockSpec((None, dil, rows // dil, ATTN_WIDTH), lambda bi, t: (bi, 0, t, 0))
        qkv_specs += [spec] * 3
        qkv_shapes += [jax.ShapeDtypeStruct((b, dil, s // dil, ATTN_WIDTH), BF16)] * 3
    outs = pl.pallas_call(
        _mix_qkv_kernel,
        grid=(b, s // rows),
        in_specs=[row(d), const(g.shape), once(w_qkv.shape), const(b_gate.shape)],
        out_specs=qkv_specs + [row(d)],
        out_shape=qkv_shapes + [jax.ShapeDtypeStruct((b, s, d), BF16)],
        scratch_shapes=[pltpu.VMEM((ATTN_WIDTH // LANES, rows, LANES), F32),
                        pltpu.VMEM((ATTN_WIDTH // LANES, DILATION_STEP, rows // DILATION_STEP,
                                    LANES), F32)],
        compiler_params=_cparams(("parallel", "parallel"), 56),
        name="mix_qkv",
    )(x, g, w_qkv, b_gate)
    n_qkv = 3 * len(DILATIONS)
    qkv = [outs[3 * i:3 * i + 3] for i in range(len(DILATIONS))]
    gb = outs[n_qkv]

    half = gate // 2
    assert (uv + qkv_w) % half == 0
    ga_col = (uv + qkv_w) // half
    ta = pl.pallas_call(
        _mix_gmlp_kernel,
        grid=(b, s // rows),
        in_specs=[row(d), const(g.shape), once((d, uv)), once((d, half), ga_col),
                  once((d, half), ga_col + 1), const(b_gate.shape), const(g2.shape),
                  const(ws_pairs.shape), const(bsp.shape), const(w_pa.shape)],
        out_specs=row(d),
        out_shape=jax.ShapeDtypeStruct((b, s, d), BF16),
        scratch_shapes=[pltpu.VMEM((d, uv + gate), BF16)],
        compiler_params=_cparams(("arbitrary", "arbitrary"), 56),
        name="mix_gmlp",
    )(x, g, w_in, w_in, w_in, b_gate, g2, ws_pairs, bsp, w_pa)
    return qkv, ta, gb


def _attn_kernel(q_ref, k_ref, v_ref, o_ref, l_ref, bias_ref, *, dil):
    n_res, rows, _ = q_ref.shape
    seq = k_ref.shape[1]
    t = pl.program_id(2)
    first = jnp.logical_and(jnp.logical_and(pl.program_id(0) == 0, pl.program_id(1) == 0), t == 0)

    @pl.when(first)
    def _():
        ii = lax.broadcasted_iota(I32, (Q_TILE, KEY_TILE), 0)
        jj = lax.broadcasted_iota(I32, (Q_TILE, KEY_TILE), 1)
        for var in range(3):
            absd = jnp.abs(jj - ii - var * HALF_WINDOW)
            valid = absd <= HALF_WINDOW
            absf = absd.astype(F32)
            for h in range(N_HEADS):
                slope = 2.0 ** (-8.0 * (h + 1) / N_HEADS)
                bias_ref[var, h] = jnp.where(valid, -(slope * dil) * absf, -jnp.inf)

    lane = lax.broadcasted_iota(I32, (Q_TILE, LANES), 1)
    lane_lo = lane < HEAD_DIM
    mask_lo = jnp.where(lane_lo, 1.0, 0.0).astype(BF16)
    mask_hi = jnp.where(lane_lo, 0.0, 1.0).astype(BF16)
    for rr in range(n_res):
        for qi in range(rows // Q_TILE):
            rs = slice(qi * Q_TILE, (qi + 1) * Q_TILE)
            i0 = t * rows + qi * Q_TILE
            start = pl.multiple_of(jnp.clip(i0 - HALF_WINDOW, 0, seq - KEY_TILE), HALF_WINDOW)
            var = (i0 - start) // HALF_WINDOW
            for p in range(ATTN_WIDTH // LANES):
                cs = slice(p * LANES, (p + 1) * LANES)
                qp = q_ref[rr, rs, cs]
                kp = k_ref[rr, pl.ds(start, KEY_TILE), cs]
                vp = v_ref[rr, pl.ds(start, KEY_TILE), cs]
                q2 = jnp.concatenate([qp * mask_lo, qp * mask_hi], axis=0)
                s2 = _dot_nt(q2, kp)
                probs = []
                for hh in range(2):
                    h = 2 * p + hh
                    s = s2[hh * Q_TILE:(hh + 1) * Q_TILE] + bias_ref[var, h]
                    m = jnp.max(s, axis=-1, keepdims=True)
                    e = jnp.exp(s - m)
                    den = jnp.sum(e, axis=-1, keepdims=True)
                    probs.append(e)
                    lo = h * STAT_LANES
                    l_ref[rr, rs, lo:lo + DEN_SHIFT] = jnp.broadcast_to(m, (Q_TILE, DEN_SHIFT))
                    l_ref[rr, rs, lo + DEN_SHIFT:lo + STAT_LANES] = jnp.broadcast_to(
                        den, (Q_TILE, DEN_SHIFT))
                o2 = _dot(jnp.concatenate(probs, axis=0).astype(BF16), vp)
                o_ref[rr, rs, cs] = jnp.where(lane_lo, o2[:Q_TILE], o2[Q_TILE:]).astype(BF16)


def _attn_pattern(q, k, v, dil):
    b, _, seq, w = q.shape
    rows = min(seq, ATTN_STEP_ROWS)
    n_res = ATTN_STEP_ROWS // rows
    qspec = lambda width: pl.BlockSpec((None, n_res, rows, width), lambda bi, r, t: (bi, r, t, 0))
    kspec = pl.BlockSpec((None, n_res, seq, w), lambda bi, r, t: (bi, r, 0, 0))
    return pl.pallas_call(
        functools.partial(_attn_kernel, dil=dil),
        grid=(b, dil // n_res, seq // rows),
        in_specs=[qspec(w), kspec, kspec],
        out_specs=[qspec(w), qspec(LANES)],
        out_shape=[jax.ShapeDtypeStruct(q.shape, BF16),
                   jax.ShapeDtypeStruct((b, dil, seq, LANES), F32)],
        scratch_shapes=[pltpu.VMEM((3, N_HEADS, Q_TILE, KEY_TILE), F32)],
        compiler_params=_cparams(("arbitrary", "arbitrary", "arbitrary"), 48),
        name=f"attn_d{dil}",
    )(q, k, v)


def _to_natural(src_ref, nat_ref, tmp_ref, rows):
    n_slab = nat_ref.shape[0]
    step = DILATION_STEP
    for p in range(n_slab):
        cs = slice(p * LANES, (p + 1) * LANES)
        for r4 in range(step):
            if tmp_ref is None:
                quarter = src_ref[r4][:, cs].astype(F32)
            else:
                for c in range(step):
                    tmp_ref[p, r4, pl.ds(c, rows // (step * step), stride=step), :] = (
                        src_ref[r4 + step * c][:, cs].astype(F32))
                quarter = tmp_ref[p, r4]
            nat_ref[p, pl.ds(r4, rows // step, stride=step), :] = quarter
    return jnp.concatenate([nat_ref[p] for p in range(n_slab)], axis=1)


def _mix_out_kernel(x_ref, ta_ref, gb_ref, *refs):
    n_pat = len(DILATIONS)
    o_refs = refs[:n_pat]
    l_refs = refs[n_pat:2 * n_pat]
    pb_f32, wo_f32, g_ref, wr_ref, x1_ref, h2_ref, aff_ref = refs[2 * n_pat:2 * n_pat + 7]
    pb_ref, wo_ref = refs[2 * n_pat + 7:2 * n_pat + 9]
    stage_refs = refs[2 * n_pat + 9:]
    rows = x_ref.shape[0]

    @pl.when(jnp.logical_and(pl.program_id(0) == 0, pl.program_id(1) == 0))
    def _():
        pb_ref[...] = pb_f32[...].astype(BF16)
        wo_ref[...] = wo_f32[...].astype(BF16)

    stage_refs = list(stage_refs)
    outs = [o_refs[0][0].astype(F32)]
    lses = [l_refs[0][0]]
    for di in (1, 2):
        for src, dest in ((o_refs[di], outs), (l_refs[di], lses)):
            nat = stage_refs.pop(0)
            tmp = stage_refs.pop(0) if di == 2 else None
            dest.append(_to_natural(src, nat, tmp, rows))

    dens = [pltpu.roll(st, LANES - DEN_SHIFT, 1) for st in lses]
    lses = [st + jnp.log(den) for st, den in zip(lses, dens)]
    m = functools.reduce(jnp.maximum, lses)
    ws = [jnp.exp(l - m) for l in lses]
    inv = 1.0 / functools.reduce(lambda a, c: a + c, ws)
    lane = lax.broadcasted_iota(I32, (rows, LANES), 1)
    used = lane % STAT_LANES < DEN_SHIFT
    ws = [jnp.where(used, w * inv / den, 0.0) for w, den in zip(ws, dens)]
    k_i = lax.broadcasted_iota(I32, (2 * LANES, ATTN_WIDTH), 0) % LANES
    c_i = lax.broadcasted_iota(I32, (2 * LANES, ATTN_WIDTH), 1)
    spread = jnp.where(k_i == (c_i // HEAD_DIM) * STAT_LANES, 1.0, 0.0).astype(BF16)
    parts = []
    for w in ws:
        w_hi = w.astype(BF16)
        w_lo = (w - w_hi.astype(F32)).astype(BF16)
        parts.append(jnp.concatenate([w_hi, w_lo], axis=1))
    factors = _dot(jnp.concatenate(parts, axis=0), spread)
    o = None
    for i, o_p in enumerate(outs):
        term = factors[i * rows:(i + 1) * rows] * o_p
        o = term if o is None else o + term

    ob = _dot(o.astype(BF16), pb_ref[...])
    merged = (ta_ref[...].astype(F32) + gb_ref[...].astype(F32) * ob).astype(BF16)
    x1 = x_ref[...] + _dot(merged, wo_ref[...])
    x1_ref[...] = x1
    h2 = _rms(x1, g_ref[...])
    h2_ref[...] = h2.astype(BF16)
    h_hi = h2.astype(BF16)
    h_lo = (h2 - h_hi.astype(F32)).astype(BF16)
    wr = wr_ref[...]
    w_hi = wr.astype(BF16)
    w_lo = (wr - w_hi.astype(F32)).astype(BF16)
    n_e = wr.shape[0]
    by_hi = _dot_nt(jnp.concatenate([w_hi, w_lo], axis=0), h_hi)
    logits = by_hi[:n_e] + (_dot_nt(w_hi, h_lo) + by_hi[n_e:])
    e = jnp.exp(logits - jnp.max(logits, axis=0, keepdims=True))
    aff_ref[...] = e / jnp.sum(e, axis=0, keepdims=True)


def _mix_out(x, ta, gb, os_, ls_, w_pb, w_out, g, w_router_t):
    b, s, d = x.shape
    n_e = w_router_t.shape[0]
    const = lambda shape: pl.BlockSpec(shape, lambda bi, t: (0,) * len(shape))
    once = lambda shape: pl.BlockSpec(shape, lambda bi, t: (0,) * len(shape),
                                      pipeline_mode=pl.Buffered(1))
    row = lambda w: pl.BlockSpec((None, ROW_TILE, w), lambda bi, t: (bi, t, 0))
    res = lambda dil, w: pl.BlockSpec((None, dil, ROW_TILE // dil, w), lambda bi, t: (bi, 0, t, 0))
    stage = []
    for di in (1, 2):
        for slabs in (ATTN_WIDTH // LANES, 1):
            stage.append(pltpu.VMEM((slabs, ROW_TILE, LANES), F32))
            if di == 2:
                stage.append(pltpu.VMEM((slabs, DILATION_STEP, ROW_TILE // DILATION_STEP, LANES),
                                        F32))
    return pl.pallas_call(
        _mix_out_kernel,
        grid=(b, s // ROW_TILE),
        in_specs=[row(d), row(d), row(d)]
                 + [res(dil, ATTN_WIDTH) for dil in DILATIONS]
                 + [res(dil, LANES) for dil in DILATIONS]
                 + [once(w_pb.shape), once(w_out.shape), const(g.shape), const(w_router_t.shape)],
        out_specs=[row(d), row(d), pl.BlockSpec((None, n_e, ROW_TILE), lambda bi, t: (bi, 0, t))],
        out_shape=[jax.ShapeDtypeStruct((b, s, d), F32), jax.ShapeDtypeStruct((b, s, d), BF16),
                   jax.ShapeDtypeStruct((b, n_e, s), F32)],
        scratch_shapes=[pltpu.VMEM(w_pb.shape, BF16), pltpu.VMEM(w_out.shape, BF16)] + stage,
        compiler_params=_cparams(("arbitrary", "arbitrary"), 48),
        name="mix_out",
    )(x, ta, gb, *os_, *ls_, w_pb, w_out, g, w_router_t)


class _Plan(NamedTuple):
    tiles: int
    fast: int
    slow: int
    group: int


GATHER_PLAN = _Plan(tiles=GATHER_TOKENS // SLOT_TILE, fast=80, slow=GATHER_TOKENS + SLOT_ALIGN,
                    group=N_EXPERTS // 2)
COMBINE_PLAN = _Plan(tiles=1, fast=48, slow=SLOT_TILE + SLOT_ALIGN, group=N_EXPERTS)
FAST_WINDOW, SLOT_WINDOW = COMBINE_PLAN.fast, COMBINE_PLAN.slow


def _slot_plan(cum, plan, cap):
    rows = cum.shape[0]
    nxt = pltpu.roll(cum, LANES - plan.tiles, 1)
    lo = jnp.floor(cum * (1.0 / SLOT_ALIGN)) * SLOT_ALIGN
    ok = jnp.where(nxt - lo <= plan.fast, 1.0, 0.0)
    fit = jnp.min(ok.reshape(rows // plan.group, plan.group, LANES), axis=1)
    assert (cap - plan.fast) % SLOT_ALIGN == 0 and (cap - plan.slow) % SLOT_ALIGN == 0
    return (jnp.minimum(lo, cap - plan.fast).astype(I32),
            jnp.minimum(lo, cap - plan.slow).astype(I32), fit.astype(I32))


def _topk_kernel(aff_ref, rank_ref, *plan_refs, cap, plans):
    n_e, s = aff_ref.shape
    n_blk = s // SLOT_TILE
    aff = aff_ref[...]
    thr = jnp.zeros((n_e, 1), I32)
    for bit in range(30, -1, -1):
        cand = thr | (1 << bit)
        cnt = jnp.sum((aff >= pltpu.bitcast(cand, F32)).astype(I32), axis=1, keepdims=True)
        thr = jnp.where(cnt >= cap, cand, thr)
    above = aff >= pltpu.bitcast(thr + 1, F32)
    tie = jnp.logical_and(aff >= pltpu.bitcast(thr, F32), jnp.logical_not(above))
    need = (cap - jnp.sum(above.astype(I32), axis=1, keepdims=True)).astype(F32)
    r_i = lax.broadcasted_iota(I32, (SLOT_TILE, SLOT_TILE), 0)
    c_i = lax.broadcasted_iota(I32, (SLOT_TILE, SLOT_TILE), 1)
    tri = jnp.where(r_i < c_i, 1.0, 0.0).astype(BF16)
    lane = lax.broadcasted_iota(I32, (n_e, LANES), 1)
    run_tie = jnp.zeros((n_e, 1), F32)
    run_sel = jnp.zeros((n_e, 1), F32)
    cum = jnp.zeros((n_e, LANES), F32)
    for j in range(n_blk):
        cs = slice(j * SLOT_TILE, (j + 1) * SLOT_TILE)
        tie_f = jnp.where(tie[:, cs], 1.0, 0.0)
        tie_rank = _dot(tie_f.astype(BF16), tri) + run_tie
        run_tie = run_tie + jnp.sum(tie_f, axis=1, keepdims=True)
        sel_f = jnp.where(above[:, cs], 1.0, jnp.where(tie_rank < need, tie_f, 0.0))
        rank = _dot(sel_f.astype(BF16), tri) + run_sel
        rank_ref[:, cs] = jnp.where(sel_f > 0.0, rank, -1.0).astype(I32)
        cum = jnp.where(lane == j, run_sel, cum)
        run_sel = run_sel + jnp.sum(sel_f, axis=1, keepdims=True)
    cum = jnp.where(lane == n_blk, run_sel, cum)
    for i, plan in enumerate(plans):
        for ref, val in zip(plan_refs[3 * i:3 * i + 3], _slot_plan(cum, plan, cap)):
            ref[...] = val


def _topk(aff, cap, plans):
    b, n_e, s = aff.shape
    rows = b * n_e
    full = lambda r, w: pl.BlockSpec((r, w), lambda i: (0, 0))
    plan_specs, plan_shapes = [], []
    for plan in plans:
        for r in (rows, rows, rows // plan.group):
            plan_specs.append(full(r, LANES))
            plan_shapes.append(jax.ShapeDtypeStruct((r, LANES), I32))
    rank, *tables = pl.pallas_call(
        functools.partial(_topk_kernel, cap=cap, plans=plans),
        grid=(1,),
        in_specs=[full(rows, s)],
        out_specs=[full(rows, s)] + plan_specs,
        out_shape=[jax.ShapeDtypeStruct((rows, s), I32)] + plan_shapes,
        compiler_params=_cparams(("arbitrary",), 32),
        name="topk",
    )(aff.reshape(rows, s))
    tables = [t.reshape(-1) for t in tables]
    return rank.reshape(b, n_e, s), [tables[3 * i:3 * i + 3] for i in range(len(plans))]


def _window_start(table_ref, expert_row, tile):
    return pl.multiple_of(table_ref[expert_row * LANES + tile], SLOT_ALIGN)


def _run_blocks(fits, fast, slow):
    all_fit = functools.reduce(jnp.logical_and, fits)

    @pl.when(all_fit)
    def _():
        for jj in range(len(fits)):
            fast(jj)

    @pl.when(jnp.logical_not(all_fit))
    def _():
        for jj, fit in enumerate(fits):
            pl.when(fit)(functools.partial(fast, jj))
            pl.when(jnp.logical_not(fit))(functools.partial(slow, jj))


def _gather_kernel(fast_ref, slow_ref, fit_ref, rank_ref, aff_ref, h2_ref, xe_ref, gate_ref):
    n_e = rank_ref.shape[0]
    rows = h2_ref.shape[0]
    group = pl.program_id(0) * pl.num_programs(1) + pl.program_id(1)
    t = pl.program_id(2)

    @pl.when(t == 0)
    def _():
        xe_ref[...] = jnp.zeros_like(xe_ref)
        gate_ref[...] = jnp.zeros_like(gate_ref)

    row_fast = lax.broadcasted_iota(I32, (GATHER_PLAN.fast, GATHER_TOKENS), 0)
    row_slow = lax.broadcasted_iota(I32, (GATHER_PLAN.slow, GATHER_TOKENS), 0)
    n_blk = rows // GATHER_TOKENS
    tile = lambda jj: (t * n_blk + jj) * GATHER_PLAN.tiles

    def hit(jj, e, row_i):
        toks = slice(jj * GATHER_TOKENS, (jj + 1) * GATHER_TOKENS)
        table = fast_ref if row_i is row_fast else slow_ref
        lo = _window_start(table, group * n_e + e, tile(jj))
        return lo, (row_i + lo) == rank_ref[e:e + 1, toks]

    def add_window(jj, e, lo, hit_e, rows_e):
        toks = slice(jj * GATHER_TOKENS, (jj + 1) * GATHER_TOKENS)
        win = pl.ds(lo, hit_e.shape[0])
        xe_ref[e, win, :] += rows_e.astype(BF16)
        gate_ref[e, win, :] += jnp.sum(jnp.where(hit_e, aff_ref[e:e + 1, toks], 0.0), axis=1,
                                       keepdims=True)

    def fast(jj):
        hs = [hit(jj, e, row_fast) for e in range(n_e)]
        stack = jnp.concatenate([jnp.where(h, 1.0, 0.0).astype(BF16) for _, h in hs], axis=0)
        res = _dot(stack, h2_ref[jj * GATHER_TOKENS:(jj + 1) * GATHER_TOKENS, :])
        for e, (lo, h) in enumerate(hs):
            add_window(jj, e, lo, h, res[e * GATHER_PLAN.fast:(e + 1) * GATHER_PLAN.fast])

    def slow(jj):
        for e in range(n_e):
            lo, h = hit(jj, e, row_slow)
            add_window(jj, e, lo, h,
                       _dot(jnp.where(h, 1.0, 0.0).astype(BF16),
                            h2_ref[jj * GATHER_TOKENS:(jj + 1) * GATHER_TOKENS, :]))

    _run_blocks([fit_ref[group * LANES + tile(jj)] != 0 for jj in range(n_blk)], fast, slow)


def _gather(plan_tables, rank, aff, h2, cap):
    b, s, d = h2.shape
    n_e = rank.shape[1]
    rows = cap
    grp = GATHER_PLAN.group
    per_tok = pl.BlockSpec((None, grp, GATHER_STEP_TOKENS), lambda bi, eg, t, *_: (bi, eg, t))
    whole = lambda w: pl.BlockSpec((None, grp, rows, w), lambda bi, eg, t, *_: (bi, eg, 0, 0))
    grid_spec = pltpu.PrefetchScalarGridSpec(
        num_scalar_prefetch=3,
        grid=(b, n_e // grp, s // GATHER_STEP_TOKENS),
        in_specs=[per_tok, per_tok,
                  pl.BlockSpec((None, GATHER_STEP_TOKENS, d), lambda bi, eg, t, *_: (bi, t, 0))],
        out_specs=[whole(d), whole(LANES)],
    )
    return pl.pallas_call(
        _gather_kernel,
        grid_spec=grid_spec,
        out_shape=[jax.ShapeDtypeStruct((b, n_e, rows, d), BF16),
                   jax.ShapeDtypeStruct((b, n_e, rows, LANES), F32)],
        compiler_params=_cparams(("arbitrary", "arbitrary", "arbitrary"), 56),
        name="gather",
    )(*plan_tables, rank, aff, h2)


def _moe_ffn_kernel(xe_ref, gate_ref, wg_ref, wu_ref, wd_ref, ye_ref, wg_bf, wu_bf, wd_bf):
    n_seq, cap, d = xe_ref.shape

    @pl.when(pl.program_id(1) == 0)
    def _():
        wg_bf[...] = wg_ref[...].astype(BF16)
        wu_bf[...] = wu_ref[...].astype(BF16)
        wd_bf[...] = wd_ref[...].astype(BF16)

    xe = xe_ref[...].reshape(n_seq * cap, d)
    gate_h = _dot(xe, wg_bf[...])
    up_h = _dot(xe, wu_bf[...])
    hidden = (gate_h * _sigmoid(gate_h) * up_h).astype(BF16)
    ye = _dot(hidden, wd_bf[...]) * gate_ref[...].reshape(n_seq * cap, LANES)[:, 0:1]
    ye_ref[...] = ye.astype(BF16).reshape(n_seq, cap, d)


def _moe_ffn(xe, gate, wg, wu, wd, cap):
    b, n_e, rows, d = xe.shape
    hid = wg.shape[2]
    return pl.pallas_call(
        _moe_ffn_kernel,
        grid=(n_e, b // FFN_SEQS),
        in_specs=[
            pl.BlockSpec((FFN_SEQS, None, cap, d), lambda e, bi: (bi, e, 0, 0)),
            pl.BlockSpec((FFN_SEQS, None, cap, LANES), lambda e, bi: (bi, e, 0, 0)),
            pl.BlockSpec((None, d, hid), lambda e, bi: (e, 0, 0)),
            pl.BlockSpec((None, d, hid), lambda e, bi: (e, 0, 0)),
            pl.BlockSpec((None, hid, d), lambda e, bi: (e, 0, 0)),
        ],
        out_specs=pl.BlockSpec((FFN_SEQS, None, rows, d), lambda e, bi: (bi, e, 0, 0)),
        out_shape=jax.ShapeDtypeStruct((b, n_e, rows, d), BF16),
        scratch_shapes=[pltpu.VMEM((d, hid), BF16), pltpu.VMEM((d, hid), BF16),
                        pltpu.VMEM((hid, d), BF16)],
        compiler_params=_cparams(("arbitrary", "arbitrary"), 56),
        name="moe_ffn",
    )(xe, gate, wg, wu, wd)


def _combine_kernel(fast_ref, slow_ref, fit_ref, rank_ref, ye_ref, x1_ref, g_ref, y_ref, rhs_ref):
    n_e = rank_ref.shape[0]
    n_blk = x1_ref.shape[0] // SLOT_TILE
    bi = pl.program_id(0)
    t = pl.program_id(1)
    row_fast = lax.broadcasted_iota(I32, (FAST_WINDOW, SLOT_TILE), 0)
    row_slow = lax.broadcasted_iota(I32, (SLOT_WINDOW, SLOT_TILE), 0)

    def hits(jj, e, row_i):
        toks = slice(jj * SLOT_TILE, (jj + 1) * SLOT_TILE)
        table = fast_ref if row_i is row_fast else slow_ref
        lo = _window_start(table, bi * n_e + e, t * n_blk + jj)
        hit = (row_i + lo) == rank_ref[e:e + 1, toks]
        return pl.ds(lo, row_i.shape[0]), jnp.where(hit, 1.0, 0.0).astype(BF16)

    def finish(jj, moe):
        toks = slice(jj * SLOT_TILE, (jj + 1) * SLOT_TILE)
        y_ref[toks, :] = _rms(x1_ref[toks, :] + moe, g_ref[...])

    def fast(jj):
        rhs = rhs_ref.at[jj % rhs_ref.shape[0]]
        stack = []
        for e in range(n_e):
            win, hit = hits(jj, e, row_fast)
            rhs[e * FAST_WINDOW:(e + 1) * FAST_WINDOW, :] = ye_ref[e, win, :]
            stack.append(hit)
        finish(jj, _dot_tn(jnp.concatenate(stack, axis=0), rhs[...]))

    def slow(jj):
        acc = None
        for e in range(n_e):
            win, hit = hits(jj, e, row_slow)
            part = _dot_tn(hit, ye_ref[e, win, :])
            acc = part if acc is None else acc + part
        finish(jj, acc)

    _run_blocks([fit_ref[bi * LANES + t * n_blk + jj] != 0 for jj in range(n_blk)], fast, slow)


def _combine(plan_tables, rank, ye, x1, g):
    b, s, d = x1.shape
    n_e, ye_rows = ye.shape[1], ye.shape[2]
    row = pl.BlockSpec((None, COMBINE_ROWS, d), lambda bi, t, *_: (bi, t, 0))
    grid_spec = pltpu.PrefetchScalarGridSpec(
        num_scalar_prefetch=3,
        grid=(b, s // COMBINE_ROWS),
        in_specs=[
            pl.BlockSpec((None, n_e, COMBINE_ROWS), lambda bi, t, *_: (bi, 0, t)),
            pl.BlockSpec((None, n_e, ye_rows, d), lambda bi, t, *_: (bi, 0, 0, 0)),
            row,
            pl.BlockSpec(g.shape, lambda bi, t, *_: (0, 0)),
        ],
        out_specs=row,
        scratch_shapes=[pltpu.VMEM((2, n_e * FAST_WINDOW, d), BF16)],
    )
    return pl.pallas_call(
        _combine_kernel,
        grid_spec=grid_spec,
        out_shape=jax.ShapeDtypeStruct((b, s, d), F32),
        compiler_params=_cparams(("arbitrary", "arbitrary"), 60),
        name="combine",
    )(*plan_tables, rank, ye, x1, g)


def _moe_stages(aff, h2, x1, wg, wu, wd, g_final, cap):
    rank, (gather_tables, combine_tables) = _topk(aff, cap, (GATHER_PLAN, COMBINE_PLAN))
    xe, gate = _gather(gather_tables, rank, aff, h2, cap)
    ye = _moe_ffn(xe, gate, wg, wu, wd, cap)
    return _combine(combine_tables, rank, ye, x1, g_final)


def kernel(x, norm_mix_g, w_in, b_gate, gmlp_norm_g, w_spatial, b_spatial, w_proj_a, w_proj_b,
           w_out, norm_ffn_g, w_router, w_e_gate, w_e_up, w_e_down, norm_final_g):
    b, s, d = x.shape
    assert w_in.shape[0] == 1, "single-layer block"
    cap = CAPACITY_FACTOR * s // N_EXPERTS
    group_width = GMLP_WIDTH // GMLP_GROUPS
    ws_pairs = w_spatial[0].astype(BF16).reshape(GMLP_GROUPS // 2, 2 * CHUNK, CHUNK)
    bsp = jnp.repeat(b_spatial[0].T, group_width, axis=1)
    qkv, ta, gb = _mix_in(x, norm_mix_g, w_in[0], b_gate, gmlp_norm_g, ws_pairs, bsp,
                          w_proj_a[0].astype(BF16))
    os_, ls_ = [], []
    for (q, k, v), dil in zip(qkv, DILATIONS):
        o, lse = _attn_pattern(q, k, v, dil)
        os_.append(o)
        ls_.append(lse)
    x1, h2, aff = _mix_out(x, ta, gb, os_, ls_, w_proj_b[0], w_out[0], norm_ffn_g, w_router[0].T)
    return _moe_stages(aff, h2, x1, w_e_gate[0], w_e_up[0], w_e_down[0], norm_final_g[None], cap)
```

```python
import functools

from typing import NamedTuple

import jax
import jax.numpy as jnp
from jax import lax
from jax.experimental import pallas as pl
from jax.experimental.pallas import tpu as pltpu

F32 = jnp.float32
BF16 = jnp.bfloat16
I32 = jnp.int32

EPS = 1e-6
GMLP_WIDTH = 512
GMLP_GROUPS = 8
CHUNK = 128
N_HEADS = 8
HEAD_DIM = 64
ATTN_WIDTH = N_HEADS * HEAD_DIM
DILATIONS = (1, 4, 16)
DILATION_STEP = 4
assert all(b == a * DILATION_STEP for a, b in zip(DILATIONS, DILATIONS[1:]))
HALF_WINDOW = 64
N_EXPERTS = 16
CAPACITY_FACTOR = 2

LANES = 128
Q_TILE = 128
KEY_TILE = 2 * Q_TILE
ATTN_STEP_ROWS = 2048
STAT_LANES = LANES // N_HEADS
DEN_SHIFT = STAT_LANES // 2
SLOT_TILE = 128
SLOT_ALIGN = 16
COMBINE_ROWS = 1024
FFN_SEQS = 2
GATHER_STEP_TOKENS = 2048
GATHER_TOKENS = 256
ROW_TILE = 512
MIB = 1024 * 1024


def _cparams(sem, vmem_mib):
    return pltpu.CompilerParams(dimension_semantics=sem, vmem_limit_bytes=vmem_mib * MIB)


def _gelu_tanh(x):
    return 0.5 * x * (1.0 + jnp.tanh(0.7978845608028654 * (x + 0.044715 * (x * x * x))))


def _sigmoid(x):
    return 1.0 / (1.0 + jnp.exp(-x))


def _rms(x, g):
    return x * lax.rsqrt(jnp.mean(x * x, axis=-1, keepdims=True) + EPS) * g


def _dot(a, b):
    return jnp.dot(a, b, preferred_element_type=F32)


def _dot_nt(a, b):
    return lax.dot_general(a, b, (((1,), (1,)), ((), ())), preferred_element_type=F32)


def _dot_tn(a, b):
    return lax.dot_general(a, b, (((0,), (0,)), ((), ())), preferred_element_type=F32)


def _mix_in_kernel(x_ref, g_ref, win_ref, bg_ref, g2_ref, ws_ref, bsp_ref, pa_ref, *refs):
    n_qkv = 3 * len(DILATIONS)
    qkv_refs = refs[:n_qkv]
    ta_ref, gb_ref = refs[n_qkv:n_qkv + 2]
    win_bf = refs[n_qkv + 2]
    stage_refs = refs[n_qkv + 3:]
    stage_refs = list(zip(stage_refs[0::2], stage_refs[1::2]))
    rows, d_model = x_ref.shape

    @pl.when(jnp.logical_and(pl.program_id(0) == 0, pl.program_id(1) == 0))
    def _():
        for c in range(0, win_ref.shape[1], GMLP_WIDTH):
            win_bf[:, c:c + GMLP_WIDTH] = win_ref[:, c:c + GMLP_WIDTH].astype(BF16)

    h = _rms(x_ref[...], g_ref[...]).astype(BF16)

    def proj(lo, width):
        return _dot(h, win_bf[:, lo:lo + width])

    c0 = 0
    u = _gelu_tanh(proj(c0, GMLP_WIDTH)); c0 += GMLP_WIDTH
    v = _gelu_tanh(proj(c0, GMLP_WIDTH)); c0 += GMLP_WIDTH
    for i in range(3):
        val = proj(c0, ATTN_WIDTH); c0 += ATTN_WIDTH
        if i == 0:
            val = val * (HEAD_DIM ** -0.5)
        qkv_refs[i][0] = val.astype(BF16)
        stage1, stage2 = stage_refs[i]
        n4, n16 = rows // DILATIONS[1], rows // DILATIONS[2]
        out4, out16 = qkv_refs[3 + i], qkv_refs[6 + i]
        for p in range(ATTN_WIDTH // LANES):
            cs = slice(p * LANES, (p + 1) * LANES)
            stage1[p] = val[:, cs]
            for r4 in range(DILATION_STEP):
                part = stage1[p, pl.ds(r4, n4, stride=DILATION_STEP), :]
                out4[r4, :, cs] = part.astype(BF16)
                stage2[p, r4] = part
                for c in range(DILATION_STEP):
                    out16[r4 + DILATION_STEP * c, :, cs] = (
                        stage2[p, r4, pl.ds(c, n16, stride=DILATION_STEP), :].astype(BF16))
    ga = _sigmoid(proj(c0, d_model) + bg_ref[:, :d_model]); c0 += d_model
    gb = _sigmoid(proj(c0, d_model) + bg_ref[:, d_model:])
    gb_ref[...] = gb.astype(BF16)

    vn = _rms(v, g2_ref[...]).astype(BF16)
    lane_lo = lax.broadcasted_iota(I32, (CHUNK, LANES), 1) < HEAD_DIM
    bsp = bsp_ref[...]
    n_chunk = rows // CHUNK
    mixed_slabs = []
    for p in range(GMLP_WIDTH // LANES):
        slab = jnp.concatenate(
            [vn[c * CHUNK:(c + 1) * CHUNK, p * LANES:(p + 1) * LANES] for c in range(n_chunk)],
            axis=1)
        r = _dot(ws_ref[p], slab)
        mixed_slabs.append([jnp.where(lane_lo, r[:CHUNK, c * LANES:(c + 1) * LANES],
                                      r[CHUNK:, c * LANES:(c + 1) * LANES])
                            for c in range(n_chunk)])
    a_chunks = []
    for c in range(n_chunk):
        rs = slice(c * CHUNK, (c + 1) * CHUNK)
        mixed = jnp.concatenate([slabs[c] for slabs in mixed_slabs], axis=1) + bsp
        a_chunks.append((u[rs] * mixed).astype(BF16))
    a = jnp.concatenate(a_chunks, axis=0)
    ta_ref[...] = (ga * _dot(a, pa_ref[...])).astype(BF16)


def _mix_in(x, g, w_in, b_gate, g2, ws_pairs, bsp, w_pa):
    b, s, d = x.shape
    const = lambda shape: pl.BlockSpec(shape, lambda bi, t: (0,) * len(shape))
    row = lambda w: pl.BlockSpec((None, ROW_TILE, w), lambda bi, t: (bi, t, 0))
    qkv_specs, qkv_shapes = [], []
    for dil in DILATIONS:
        spec = pl.BlockSpec((None, dil, ROW_TILE // dil, ATTN_WIDTH), lambda bi, t: (bi, 0, t, 0))
        qkv_specs += [spec] * 3
        qkv_shapes += [jax.ShapeDtypeStruct((b, dil, s // dil, ATTN_WIDTH), BF16)] * 3
    outs = pl.pallas_call(
        _mix_in_kernel,
        grid=(b, s // ROW_TILE),
        in_specs=[row(d), const(g.shape),
                  pl.BlockSpec(w_in.shape, lambda bi, t: (0, 0), pipeline_mode=pl.Buffered(1)),
                  const(b_gate.shape), const(g2.shape),
                  const(ws_pairs.shape), const(bsp.shape), const(w_pa.shape)],
        out_specs=qkv_specs + [row(d), row(d)],
        out_shape=qkv_shapes + [jax.ShapeDtypeStruct((b, s, d), BF16)] * 2,
        scratch_shapes=[pltpu.VMEM(w_in.shape, BF16)]
                       + [pltpu.VMEM((ATTN_WIDTH // LANES, ROW_TILE, LANES), F32),
                          pltpu.VMEM((ATTN_WIDTH // LANES, DILATION_STEP,
                                      ROW_TILE // DILATION_STEP, LANES), F32)] * 3,
        compiler_params=_cparams(("arbitrary", "arbitrary"), 60),
        name="mix_in",
    )(x, g, w_in, b_gate, g2, ws_pairs, bsp, w_pa)
    n_qkv = 3 * len(DILATIONS)
    qkv = [outs[3 * i:3 * i + 3] for i in range(len(DILATIONS))]
    return qkv, outs[n_qkv], outs[n_qkv + 1]


def _attn_kernel(q_ref, k_ref, v_ref, o_ref, l_ref, bias_ref, *, dil):
    n_res, rows, _ = q_ref.shape
    seq = k_ref.shape[1]
    t = pl.program_id(2)
    first = jnp.logical_and(jnp.logical_and(pl.program_id(0) == 0, pl.program_id(1) == 0), t == 0)

    @pl.when(first)
    def _():
        ii = lax.broadcasted_iota(I32, (Q_TILE, KEY_TILE), 0)
        jj = lax.broadcasted_iota(I32, (Q_TILE, KEY_TILE), 1)
        for var in range(3):
            absd = jnp.abs(jj - ii - var * HALF_WINDOW)
            valid = absd <= HALF_WINDOW
            absf = absd.astype(F32)
            for h in range(N_HEADS):
                slope = 2.0 ** (-8.0 * (h + 1) / N_HEADS)
                bias_ref[var, h] = jnp.where(valid, -(slope * dil) * absf, -jnp.inf)

    lane = lax.broadcasted_iota(I32, (Q_TILE, LANES), 1)
    lane_lo = lane < HEAD_DIM
    mask_lo = jnp.where(lane_lo, 1.0, 0.0).astype(BF16)
    mask_hi = jnp.where(lane_lo, 0.0, 1.0).astype(BF16)
    for rr in range(n_res):
        for qi in range(rows // Q_TILE):
            rs = slice(qi * Q_TILE, (qi + 1) * Q_TILE)
            i0 = t * rows + qi * Q_TILE
            start = pl.multiple_of(jnp.clip(i0 - HALF_WINDOW, 0, seq - KEY_TILE), HALF_WINDOW)
            var = (i0 - start) // HALF_WINDOW
            for p in range(ATTN_WIDTH // LANES):
                cs = slice(p * LANES, (p + 1) * LANES)
                qp = q_ref[rr, rs, cs]
                kp = k_ref[rr, pl.ds(start, KEY_TILE), cs]
                vp = v_ref[rr, pl.ds(start, KEY_TILE), cs]
                q2 = jnp.concatenate([qp * mask_lo, qp * mask_hi], axis=0)
                s2 = _dot_nt(q2, kp)
                probs = []
                for hh in range(2):
                    h = 2 * p + hh
                    s = s2[hh * Q_TILE:(hh + 1) * Q_TILE] + bias_ref[var, h]
                    m = jnp.max(s, axis=-1, keepdims=True)
                    e = jnp.exp(s - m)
                    den = jnp.sum(e, axis=-1, keepdims=True)
                    probs.append(e)
                    lo = h * STAT_LANES
                    l_ref[rr, rs, lo:lo + DEN_SHIFT] = jnp.broadcast_to(m, (Q_TILE, DEN_SHIFT))
                    l_ref[rr, rs, lo + DEN_SHIFT:lo + STAT_LANES] = jnp.broadcast_to(
                        den, (Q_TILE, DEN_SHIFT))
                o2 = _dot(jnp.concatenate(probs, axis=0).astype(BF16), vp)
                o_ref[rr, rs, cs] = jnp.where(lane_lo, o2[:Q_TILE], o2[Q_TILE:]).astype(BF16)


def _attn_pattern(q, k, v, dil):
    b, _, seq, w = q.shape
    rows = min(seq, ATTN_STEP_ROWS)
    n_res = ATTN_STEP_ROWS // rows
    qspec = lambda width: pl.BlockSpec((None, n_res, rows, width), lambda bi, r, t: (bi, r, t, 0))
    kspec = pl.BlockSpec((None, n_res, seq, w), lambda bi, r, t: (bi, r, 0, 0))
    return pl.pallas_call(
        functools.partial(_attn_kernel, dil=dil),
        grid=(b, dil // n_res, seq // rows),
        in_specs=[qspec(w), kspec, kspec],
        out_specs=[qspec(w), qspec(LANES)],
        out_shape=[jax.ShapeDtypeStruct(q.shape, BF16),
                   jax.ShapeDtypeStruct((b, dil, seq, LANES), F32)],
        scratch_shapes=[pltpu.VMEM((3, N_HEADS, Q_TILE, KEY_TILE), F32)],
        compiler_params=_cparams(("arbitrary", "arbitrary", "arbitrary"), 48),
        name=f"attn_d{dil}",
    )(q, k, v)


def _to_natural(src_ref, nat_ref, tmp_ref, rows):
    n_slab = nat_ref.shape[0]
    step = DILATION_STEP
    for p in range(n_slab):
        cs = slice(p * LANES, (p + 1) * LANES)
        for r4 in range(step):
            if tmp_ref is None:
                quarter = src_ref[r4][:, cs].astype(F32)
            else:
                for c in range(step):
                    tmp_ref[p, r4, pl.ds(c, rows // (step * step), stride=step), :] = (
                        src_ref[r4 + step * c][:, cs].astype(F32))
                quarter = tmp_ref[p, r4]
            nat_ref[p, pl.ds(r4, rows // step, stride=step), :] = quarter
    return jnp.concatenate([nat_ref[p] for p in range(n_slab)], axis=1)


def _mix_out_kernel(x_ref, ta_ref, gb_ref, *refs):
    n_pat = len(DILATIONS)
    o_refs = refs[:n_pat]
    l_refs = refs[n_pat:2 * n_pat]
    pb_f32, wo_f32, g_ref, wr_ref, x1_ref, h2_ref, aff_ref = refs[2 * n_pat:2 * n_pat + 7]
    pb_ref, wo_ref = refs[2 * n_pat + 7:2 * n_pat + 9]
    stage_refs = refs[2 * n_pat + 9:]
    rows = x_ref.shape[0]

    @pl.when(jnp.logical_and(pl.program_id(0) == 0, pl.program_id(1) == 0))
    def _():
        pb_ref[...] = pb_f32[...].astype(BF16)
        wo_ref[...] = wo_f32[...].astype(BF16)

    stage_refs = list(stage_refs)
    outs = [o_refs[0][0].astype(F32)]
    lses = [l_refs[0][0]]
    for di in (1, 2):
        for src, dest in ((o_refs[di], outs), (l_refs[di], lses)):
            nat = stage_refs.pop(0)
            tmp = stage_refs.pop(0) if di == 2 else None
            dest.append(_to_natural(src, nat, tmp, rows))

    dens = [pltpu.roll(st, LANES - DEN_SHIFT, 1) for st in lses]
    lses = [st + jnp.log(den) for st, den in zip(lses, dens)]
    m = functools.reduce(jnp.maximum, lses)
    ws = [jnp.exp(l - m) for l in lses]
    inv = 1.0 / functools.reduce(lambda a, c: a + c, ws)
    lane = lax.broadcasted_iota(I32, (rows, LANES), 1)
    used = lane % STAT_LANES < DEN_SHIFT
    ws = [jnp.where(used, w * inv / den, 0.0) for w, den in zip(ws, dens)]
    k_i = lax.broadcasted_iota(I32, (2 * LANES, ATTN_WIDTH), 0) % LANES
    c_i = lax.broadcasted_iota(I32, (2 * LANES, ATTN_WIDTH), 1)
    spread = jnp.where(k_i == (c_i // HEAD_DIM) * STAT_LANES, 1.0, 0.0).astype(BF16)
    parts = []
    for w in ws:
        w_hi = w.astype(BF16)
        w_lo = (w - w_hi.astype(F32)).astype(BF16)
        parts.append(jnp.concatenate([w_hi, w_lo], axis=1))
    factors = _dot(jnp.concatenate(parts, axis=0), spread)
    o = None
    for i, o_p in enumerate(outs):
        term = factors[i * rows:(i + 1) * rows] * o_p
        o = term if o is None else o + term

    ob = _dot(o.astype(BF16), pb_ref[...])
    merged = (ta_ref[...].astype(F32) + gb_ref[...].astype(F32) * ob).astype(BF16)
    x1 = x_ref[...] + _dot(merged, wo_ref[...])
    x1_ref[...] = x1
    h2 = _rms(x1, g_ref[...])
    h2_ref[...] = h2.astype(BF16)
    h_hi = h2.astype(BF16)
    h_lo = (h2 - h_hi.astype(F32)).astype(BF16)
    wr = wr_ref[...]
    w_hi = wr.astype(BF16)
    w_lo = (wr - w_hi.astype(F32)).astype(BF16)
    n_e = wr.shape[0]
    by_hi = _dot_nt(jnp.concatenate([w_hi, w_lo], axis=0), h_hi)
    logits = by_hi[:n_e] + (_dot_nt(w_hi, h_lo) + by_hi[n_e:])
    e = jnp.exp(logits - jnp.max(logits, axis=0, keepdims=True))
    aff_ref[...] = e / jnp.sum(e, axis=0, keepdims=True)


def _mix_out(x, ta, gb, os_, ls_, w_pb, w_out, g, w_router_t):
    b, s, d = x.shape
    n_e = w_router_t.shape[0]
    const = lambda shape: pl.BlockSpec(shape, lambda bi, t: (0,) * len(shape))
    once = lambda shape: pl.BlockSpec(shape, lambda bi, t: (0,) * len(shape),
                                      pipeline_mode=pl.Buffered(1))
    row = lambda w: pl.BlockSpec((None, ROW_TILE, w), lambda bi, t: (bi, t, 0))
    res = lambda dil, w: pl.BlockSpec((None, dil, ROW_TILE // dil, w), lambda bi, t: (bi, 0, t, 0))
    stage = []
    for di in (1, 2):
        for slabs in (ATTN_WIDTH // LANES, 1):
            stage.append(pltpu.VMEM((slabs, ROW_TILE, LANES), F32))
            if di == 2:
                stage.append(pltpu.VMEM((slabs, DILATION_STEP, ROW_TILE // DILATION_STEP, LANES),
                                        F32))
    return pl.pallas_call(
        _mix_out_kernel,
        grid=(b, s // ROW_TILE),
        in_specs=[row(d), row(d), row(d)]
                 + [res(dil, ATTN_WIDTH) for dil in DILATIONS]
                 + [res(dil, LANES) for dil in DILATIONS]
                 + [once(w_pb.shape), once(w_out.shape), const(g.shape), const(w_router_t.shape)],
        out_specs=[row(d), row(d), pl.BlockSpec((None, n_e, ROW_TILE), lambda bi, t: (bi, 0, t))],
        out_shape=[jax.ShapeDtypeStruct((b, s, d), F32), jax.ShapeDtypeStruct((b, s, d), BF16),
                   jax.ShapeDtypeStruct((b, n_e, s), F32)],
        scratch_shapes=[pltpu.VMEM(w_pb.shape, BF16), pltpu.VMEM(w_out.shape, BF16)] + stage,
        compiler_params=_cparams(("arbitrary", "arbitrary"), 48),
        name="mix_out",
    )(x, ta, gb, *os_, *ls_, w_pb, w_out, g, w_router_t)


class _Plan(NamedTuple):
    tiles: int
    fast: int
    slow: int
    group: int


GATHER_PLAN = _Plan(tiles=GATHER_TOKENS // SLOT_TILE, fast=64, slow=GATHER_TOKENS + SLOT_ALIGN,
                    group=N_EXPERTS // 2)
COMBINE_PLAN = _Plan(tiles=1, fast=48, slow=SLOT_TILE + SLOT_ALIGN, group=N_EXPERTS)
FAST_WINDOW, SLOT_WINDOW = COMBINE_PLAN.fast, COMBINE_PLAN.slow


def _slot_plan(cum, plan, cap):
    rows = cum.shape[0]
    nxt = pltpu.roll(cum, LANES - plan.tiles, 1)
    lo = jnp.floor(cum * (1.0 / SLOT_ALIGN)) * SLOT_ALIGN
    ok = jnp.where(nxt - lo <= plan.fast, 1.0, 0.0)
    fit = jnp.min(ok.reshape(rows // plan.group, plan.group, LANES), axis=1)
    assert (cap - plan.fast) % SLOT_ALIGN == 0 and (cap - plan.slow) % SLOT_ALIGN == 0
    return (jnp.minimum(lo, cap - plan.fast).astype(I32),
            jnp.minimum(lo, cap - plan.slow).astype(I32), fit.astype(I32))


def _topk_kernel(aff_ref, rank_ref, *plan_refs, cap, plans):
    n_e, s = aff_ref.shape
    n_blk = s // SLOT_TILE
    aff = aff_ref[...]
    thr = jnp.zeros((n_e, 1), I32)
    for bit in range(30, -1, -1):
        cand = thr | (1 << bit)
        cnt = jnp.sum((aff >= pltpu.bitcast(cand, F32)).astype(I32), axis=1, keepdims=True)
        thr = jnp.where(cnt >= cap, cand, thr)
    above = aff >= pltpu.bitcast(thr + 1, F32)
    tie = jnp.logical_and(aff >= pltpu.bitcast(thr, F32), jnp.logical_not(above))
    need = (cap - jnp.sum(above.astype(I32), axis=1, keepdims=True)).astype(F32)
    r_i = lax.broadcasted_iota(I32, (SLOT_TILE, SLOT_TILE), 0)
    c_i = lax.broadcasted_iota(I32, (SLOT_TILE, SLOT_TILE), 1)
    tri = jnp.where(r_i < c_i, 1.0, 0.0).astype(BF16)
    lane = lax.broadcasted_iota(I32, (n_e, LANES), 1)
    run_tie = jnp.zeros((n_e, 1), F32)
    run_sel = jnp.zeros((n_e, 1), F32)
    cum = jnp.zeros((n_e, LANES), F32)
    for j in range(n_blk):
        cs = slice(j * SLOT_TILE, (j + 1) * SLOT_TILE)
        tie_f = jnp.where(tie[:, cs], 1.0, 0.0)
        tie_rank = _dot(tie_f.astype(BF16), tri) + run_tie
        run_tie = run_tie + jnp.sum(tie_f, axis=1, keepdims=True)
        sel_f = jnp.where(above[:, cs], 1.0, jnp.where(tie_rank < need, tie_f, 0.0))
        rank = _dot(sel_f.astype(BF16), tri) + run_sel
        rank_ref[:, cs] = jnp.where(sel_f > 0.0, rank, -1.0).astype(I32)
        cum = jnp.where(lane == j, run_sel, cum)
        run_sel = run_sel + jnp.sum(sel_f, axis=1, keepdims=True)
    cum = jnp.where(lane == n_blk, run_sel, cum)
    for i, plan in enumerate(plans):
        for ref, val in zip(plan_refs[3 * i:3 * i + 3], _slot_plan(cum, plan, cap)):
            ref[...] = val


def _topk(aff, cap, plans):
    b, n_e, s = aff.shape
    rows = b * n_e
    full = lambda r, w: pl.BlockSpec((r, w), lambda i: (0, 0))
    plan_specs, plan_shapes = [], []
    for plan in plans:
        for r in (rows, rows, rows // plan.group):
            plan_specs.append(full(r, LANES))
            plan_shapes.append(jax.ShapeDtypeStruct((r, LANES), I32))
    rank, *tables = pl.pallas_call(
        functools.partial(_topk_kernel, cap=cap, plans=plans),
        grid=(1,),
        in_specs=[full(rows, s)],
        out_specs=[full(rows, s)] + plan_specs,
        out_shape=[jax.ShapeDtypeStruct((rows, s), I32)] + plan_shapes,
        compiler_params=_cparams(("arbitrary",), 32),
        name="topk",
    )(aff.reshape(rows, s))
    tables = [t.reshape(-1) for t in tables]
    return rank.reshape(b, n_e, s), [tables[3 * i:3 * i + 3] for i in range(len(plans))]


def _window_start(table_ref, expert_row, tile):
    return pl.multiple_of(table_ref[expert_row * LANES + tile], SLOT_ALIGN)


def _run_blocks(fits, fast, slow):
    all_fit = functools.reduce(jnp.logical_and, fits)

    @pl.when(all_fit)
    def _():
        for jj in range(len(fits)):
            fast(jj)

    @pl.when(jnp.logical_not(all_fit))
    def _():
        for jj, fit in enumerate(fits):
            pl.when(fit)(functools.partial(fast, jj))
            pl.when(jnp.logical_not(fit))(functools.partial(slow, jj))


def _gather_kernel(fast_ref, slow_ref, fit_ref, rank_ref, aff_ref, h2_ref, xe_ref, gate_ref):
    n_e = rank_ref.shape[0]
    rows = h2_ref.shape[0]
    group = pl.program_id(0) * pl.num_programs(1) + pl.program_id(1)
    t = pl.program_id(2)

    @pl.when(t == 0)
    def _():
        xe_ref[...] = jnp.zeros_like(xe_ref)
        gate_ref[...] = jnp.zeros_like(gate_ref)

    row_fast = lax.broadcasted_iota(I32, (GATHER_PLAN.fast, GATHER_TOKENS), 0)
    row_slow = lax.broadcasted_iota(I32, (GATHER_PLAN.slow, GATHER_TOKENS), 0)
    n_blk = rows // GATHER_TOKENS
    tile = lambda jj: (t * n_blk + jj) * GATHER_PLAN.tiles

    def hit(jj, e, row_i):
        toks = slice(jj * GATHER_TOKENS, (jj + 1) * GATHER_TOKENS)
        table = fast_ref if row_i is row_fast else slow_ref
        lo = _window_start(table, group * n_e + e, tile(jj))
        return lo, (row_i + lo) == rank_ref[e:e + 1, toks]

    def add_window(jj, e, lo, hit_e, rows_e):
        toks = slice(jj * GATHER_TOKENS, (jj + 1) * GATHER_TOKENS)
        win = pl.ds(lo, hit_e.shape[0])
        xe_ref[e, win, :] += rows_e.astype(BF16)
        gate_ref[e, win, :] += jnp.sum(jnp.where(hit_e, aff_ref[e:e + 1, toks], 0.0), axis=1,
                                       keepdims=True)

    def fast(jj):
        hs = [hit(jj, e, row_fast) for e in range(n_e)]
        stack = jnp.concatenate([jnp.where(h, 1.0, 0.0).astype(BF16) for _, h in hs], axis=0)
        res = _dot(stack, h2_ref[jj * GATHER_TOKENS:(jj + 1) * GATHER_TOKENS, :])
        for e, (lo, h) in enumerate(hs):
            add_window(jj, e, lo, h, res[e * GATHER_PLAN.fast:(e + 1) * GATHER_PLAN.fast])

    def slow(jj):
        for e in range(n_e):
            lo, h = hit(jj, e, row_slow)
            add_window(jj, e, lo, h,
                       _dot(jnp.where(h, 1.0, 0.0).astype(BF16),
                            h2_ref[jj * GATHER_TOKENS:(jj + 1) * GATHER_TOKENS, :]))

    _run_blocks([fit_ref[group * LANES + tile(jj)] != 0 for jj in range(n_blk)], fast, slow)


def _gather(plan_tables, rank, aff, h2, cap):
    b, s, d = h2.shape
    n_e = rank.shape[1]
    rows = cap
    grp = GATHER_PLAN.group
    per_tok = pl.BlockSpec((None, grp, GATHER_STEP_TOKENS), lambda bi, eg, t, *_: (bi, eg, t))
    whole = lambda w: pl.BlockSpec((None, grp, rows, w), lambda bi, eg, t, *_: (bi, eg, 0, 0))
    grid_spec = pltpu.PrefetchScalarGridSpec(
        num_scalar_prefetch=3,
        grid=(b, n_e // grp, s // GATHER_STEP_TOKENS),
        in_specs=[per_tok, per_tok,
                  pl.BlockSpec((None, GATHER_STEP_TOKENS, d), lambda bi, eg, t, *_: (bi, t, 0))],
        out_specs=[whole(d), whole(LANES)],
    )
    return pl.pallas_call(
        _gather_kernel,
        grid_spec=grid_spec,
        out_shape=[jax.ShapeDtypeStruct((b, n_e, rows, d), BF16),
                   jax.ShapeDtypeStruct((b, n_e, rows, LANES), F32)],
        compiler_params=_cparams(("arbitrary", "arbitrary", "arbitrary"), 56),
        name="gather",
    )(*plan_tables, rank, aff, h2)


def _moe_ffn_kernel(xe_ref, gate_ref, wg_ref, wu_ref, wd_ref, ye_ref, wg_bf, wu_bf, wd_bf):
    n_seq, cap, d = xe_ref.shape

    @pl.when(pl.program_id(1) == 0)
    def _():
        wg_bf[...] = wg_ref[...].astype(BF16)
        wu_bf[...] = wu_ref[...].astype(BF16)
        wd_bf[...] = wd_ref[...].astype(BF16)

    xe = xe_ref[...].reshape(n_seq * cap, d)
    gate_h = _dot(xe, wg_bf[...])
    up_h = _dot(xe, wu_bf[...])
    hidden = (gate_h * _sigmoid(gate_h) * up_h).astype(BF16)
    ye = _dot(hidden, wd_bf[...]) * gate_ref[...].reshape(n_seq * cap, LANES)[:, 0:1]
    ye_ref[...] = ye.astype(BF16).reshape(n_seq, cap, d)


def _moe_ffn(xe, gate, wg, wu, wd, cap):
    b, n_e, rows, d = xe.shape
    hid = wg.shape[2]
    return pl.pallas_call(
        _moe_ffn_kernel,
        grid=(n_e, b // FFN_SEQS),
        in_specs=[
            pl.BlockSpec((FFN_SEQS, None, cap, d), lambda e, bi: (bi, e, 0, 0)),
            pl.BlockSpec((FFN_SEQS, None, cap, LANES), lambda e, bi: (bi, e, 0, 0)),
            pl.BlockSpec((None, d, hid), lambda e, bi: (e, 0, 0)),
            pl.BlockSpec((None, d, hid), lambda e, bi: (e, 0, 0)),
            pl.BlockSpec((None, hid, d), lambda e, bi: (e, 0, 0)),
        ],
        out_specs=pl.BlockSpec((FFN_SEQS, None, rows, d), lambda e, bi: (bi, e, 0, 0)),
        out_shape=jax.ShapeDtypeStruct((b, n_e, rows, d), BF16),
        scratch_shapes=[pltpu.VMEM((d, hid), BF16), pltpu.VMEM((d, hid), BF16),
                        pltpu.VMEM((hid, d), BF16)],
        compiler_params=_cparams(("arbitrary", "arbitrary"), 56),
        name="moe_ffn",
    )(xe, gate, wg, wu, wd)


def _combine_kernel(fast_ref, slow_ref, fit_ref, rank_ref, ye_ref, x1_ref, g_ref, y_ref, rhs_ref):
    n_e = rank_ref.shape[0]
    n_blk = x1_ref.shape[0] // SLOT_TILE
    bi = pl.program_id(0)
    t = pl.program_id(1)
    row_fast = lax.broadcasted_iota(I32, (FAST_WINDOW, SLOT_TILE), 0)
    row_slow = lax.broadcasted_iota(I32, (SLOT_WINDOW, SLOT_TILE), 0)

    def hits(jj, e, row_i):
        toks = slice(jj * SLOT_TILE, (jj + 1) * SLOT_TILE)
        table = fast_ref if row_i is row_fast else slow_ref
        lo = _window_start(table, bi * n_e + e, t * n_blk + jj)
        hit = (row_i + lo) == rank_ref[e:e + 1, toks]
        return pl.ds(lo, row_i.shape[0]), jnp.where(hit, 1.0, 0.0).astype(BF16)

    def finish(jj, moe):
        toks = slice(jj * SLOT_TILE, (jj + 1) * SLOT_TILE)
        y_ref[toks, :] = _rms(x1_ref[toks, :] + moe, g_ref[...])

    def fast(jj):
        rhs = rhs_ref.at[jj % rhs_ref.shape[0]]
        stack = []
        for e in range(n_e):
            win, hit = hits(jj, e, row_fast)
            rhs[e * FAST_WINDOW:(e + 1) * FAST_WINDOW, :] = ye_ref[e, win, :]
            stack.append(hit)
        finish(jj, _dot_tn(jnp.concatenate(stack, axis=0), rhs[...]))

    def slow(jj):
        acc = None
        for e in range(n_e):
            win, hit = hits(jj, e, row_slow)
            part = _dot_tn(hit, ye_ref[e, win, :])
            acc = part if acc is None else acc + part
        finish(jj, acc)

    _run_blocks([fit_ref[bi * LANES + t * n_blk + jj] != 0 for jj in range(n_blk)], fast, slow)


def _combine(plan_tables, rank, ye, x1, g):
    b, s, d = x1.shape
    n_e, ye_rows = ye.shape[1], ye.shape[2]
    row = pl.BlockSpec((None, COMBINE_ROWS, d), lambda bi, t, *_: (bi, t, 0))
    grid_spec = pltpu.PrefetchScalarGridSpec(
        num_scalar_prefetch=3,
        grid=(b, s // COMBINE_ROWS),
        in_specs=[
            pl.BlockSpec((None, n_e, COMBINE_ROWS), lambda bi, t, *_: (bi, 0, t)),
            pl.BlockSpec((None, n_e, ye_rows, d), lambda bi, t, *_: (bi, 0, 0, 0)),
            row,
            pl.BlockSpec(g.shape, lambda bi, t, *_: (0, 0)),
        ],
        out_specs=row,
        scratch_shapes=[pltpu.VMEM((2, n_e * FAST_WINDOW, d), BF16)],
    )
    return pl.pallas_call(
        _combine_kernel,
        grid_spec=grid_spec,
        out_shape=jax.ShapeDtypeStruct((b, s, d), F32),
        compiler_params=_cparams(("arbitrary", "arbitrary"), 60),
        name="combine",
    )(*plan_tables, rank, ye, x1, g)


def _moe_stages(aff, h2, x1, wg, wu, wd, g_final, cap):
    rank, (gather_tables, combine_tables) = _topk(aff, cap, (GATHER_PLAN, COMBINE_PLAN))
    xe, gate = _gather(gather_tables, rank, aff, h2, cap)
    ye = _moe_ffn(xe, gate, wg, wu, wd, cap)
    return _combine(combine_tables, rank, ye, x1, g_final)


def kernel(x, norm_mix_g, w_in, b_gate, gmlp_norm_g, w_spatial, b_spatial, w_proj_a, w_proj_b,
           w_out, norm_ffn_g, w_router, w_e_gate, w_e_up, w_e_down, norm_final_g):
    b, s, d = x.shape
    assert w_in.shape[0] == 1, "single-layer block"
    cap = CAPACITY_FACTOR * s // N_EXPERTS
    group_width = GMLP_WIDTH // GMLP_GROUPS
    ws_pairs = w_spatial[0].astype(BF16).reshape(GMLP_GROUPS // 2, 2 * CHUNK, CHUNK)
    bsp = jnp.repeat(b_spatial[0].T, group_width, axis=1)
    qkv, ta, gb = _mix_in(x, norm_mix_g, w_in[0], b_gate, gmlp_norm_g, ws_pairs, bsp,
                          w_proj_a[0].astype(BF16))
    os_, ls_ = [], []
    for (q, k, v), dil in zip(qkv, DILATIONS):
        o, lse = _attn_pattern(q, k, v, dil)
        os_.append(o)
        ls_.append(lse)
    x1, h2, aff = _mix_out(x, ta, gb, os_, ls_, w_proj_b[0], w_out[0], norm_ffn_g, w_router[0].T)
    return _moe_stages(aff, h2, x1, w_e_gate[0], w_e_up[0], w_e_down[0], norm_final_g[None], cap)
```

```python
import functools

from typing import NamedTuple

import jax
import jax.numpy as jnp
from jax import lax
from jax.experimental import pallas as pl
from jax.experimental.pallas import tpu as pltpu

F32 = jnp.float32
BF16 = jnp.bfloat16
I32 = jnp.int32

EPS = 1e-6
GMLP_WIDTH = 512
GMLP_GROUPS = 8
CHUNK = 128
N_HEADS = 8
HEAD_DIM = 64
ATTN_WIDTH = N_HEADS * HEAD_DIM
DILATIONS = (1, 4, 16)
DILATION_STEP = 4
assert all(b == a * DILATION_STEP for a, b in zip(DILATIONS, DILATIONS[1:]))
HALF_WINDOW = 64
N_EXPERTS = 16
CAPACITY_FACTOR = 2

LANES = 128
Q_TILE = 128
KEY_TILE = 2 * Q_TILE
ATTN_STEP_ROWS = 2048
STAT_LANES = LANES // N_HEADS
DEN_SHIFT = STAT_LANES // 2
SLOT_TILE = 128
SLOT_ALIGN = 16
COMBINE_ROWS = 1024
FFN_SEQS = 2
GATHER_STEP_TOKENS = 2048
GATHER_TOKENS = 256
ROW_TILE = 512
MIB = 1024 * 1024


def _cparams(sem, vmem_mib):
    return pltpu.CompilerParams(dimension_semantics=sem, vmem_limit_bytes=vmem_mib * MIB)


def _gelu_tanh(x):
    return 0.5 * x * (1.0 + jnp.tanh(0.7978845608028654 * (x + 0.044715 * (x * x * x))))


def _sigmoid(x):
    return 1.0 / (1.0 + jnp.exp(-x))


def _rms(x, g):
    return x * lax.rsqrt(jnp.mean(x * x, axis=-1, keepdims=True) + EPS) * g


def _dot(a, b):
    return jnp.dot(a, b, preferred_element_type=F32)


def _dot_nt(a, b):
    return lax.dot_general(a, b, (((1,), (1,)), ((), ())), preferred_element_type=F32)


def _dot_tn(a, b):
    return lax.dot_general(a, b, (((0,), (0,)), ((), ())), preferred_element_type=F32)


def _mix_in_kernel(x_ref, g_ref, win_ref, bg_ref, g2_ref, ws_ref, bsp_ref, pa_ref, *refs):
    n_qkv = 3 * len(DILATIONS)
    qkv_refs = refs[:n_qkv]
    ta_ref, gb_ref = refs[n_qkv:n_qkv + 2]
    win_bf = refs[n_qkv + 2]
    stage_refs = refs[n_qkv + 3:]
    stage_refs = list(zip(stage_refs[0::2], stage_refs[1::2]))
    rows, d_model = x_ref.shape

    @pl.when(jnp.logical_and(pl.program_id(0) == 0, pl.program_id(1) == 0))
    def _():
        for c in range(0, win_ref.shape[1], GMLP_WIDTH):
            win_bf[:, c:c + GMLP_WIDTH] = win_ref[:, c:c + GMLP_WIDTH].astype(BF16)

    h = _rms(x_ref[...], g_ref[...]).astype(BF16)

    def proj(lo, width):
        return _dot(h, win_bf[:, lo:lo + width])

    c0 = 0
    u = _gelu_tanh(proj(c0, GMLP_WIDTH)); c0 += GMLP_WIDTH
    v = _gelu_tanh(proj(c0, GMLP_WIDTH)); c0 += GMLP_WIDTH
    for i in range(3):
        val = proj(c0, ATTN_WIDTH); c0 += ATTN_WIDTH
        if i == 0:
            val = val * (HEAD_DIM ** -0.5)
        qkv_refs[i][0] = val.astype(BF16)
        stage1, stage2 = stage_refs[i]
        n4, n16 = rows // DILATIONS[1], rows // DILATIONS[2]
        out4, out16 = qkv_refs[3 + i], qkv_refs[6 + i]
        for p in range(ATTN_WIDTH // LANES):
            cs = slice(p * LANES, (p + 1) * LANES)
            stage1[p] = val[:, cs]
            for r4 in range(DILATION_STEP):
                part = stage1[p, pl.ds(r4, n4, stride=DILATION_STEP), :]
                out4[r4, :, cs] = part.astype(BF16)
                stage2[p, r4] = part
                for c in range(DILATION_STEP):
                    out16[r4 + DILATION_STEP * c, :, cs] = (
                        stage2[p, r4, pl.ds(c, n16, stride=DILATION_STEP), :].astype(BF16))
    ga = _sigmoid(proj(c0, d_model) + bg_ref[:, :d_model]); c0 += d_model
    gb = _sigmoid(proj(c0, d_model) + bg_ref[:, d_model:])
    gb_ref[...] = gb.astype(BF16)

    vn = _rms(v, g2_ref[...]).astype(BF16)
    lane_lo = lax.broadcasted_iota(I32, (CHUNK, LANES), 1) < HEAD_DIM
    bsp = bsp_ref[...]
    n_chunk = rows // CHUNK
    mixed_slabs = []
    for p in range(GMLP_WIDTH // LANES):
        slab = jnp.concatenate(
            [vn[c * CHUNK:(c + 1) * CHUNK, p * LANES:(p + 1) * LANES] for c in range(n_chunk)],
            axis=1)
        r = _dot(ws_ref[p], slab)
        mixed_slabs.append([jnp.where(lane_lo, r[:CHUNK, c * LANES:(c + 1) * LANES],
                                      r[CHUNK:, c * LANES:(c + 1) * LANES])
                            for c in range(n_chunk)])
    a_chunks = []
    for c in range(n_chunk):
        rs = slice(c * CHUNK, (c + 1) * CHUNK)
        mixed = jnp.concatenate([slabs[c] for slabs in mixed_slabs], axis=1) + bsp
        a_chunks.append((u[rs] * mixed).astype(BF16))
    a = jnp.concatenate(a_chunks, axis=0)
    ta_ref[...] = (ga * _dot(a, pa_ref[...])).astype(BF16)


def _mix_in(x, g, w_in, b_gate, g2, ws_pairs, bsp, w_pa):
    b, s, d = x.shape
    const = lambda shape: pl.BlockSpec(shape, lambda bi, t: (0,) * len(shape))
    row = lambda w: pl.BlockSpec((None, ROW_TILE, w), lambda bi, t: (bi, t, 0))
    qkv_specs, qkv_shapes = [], []
    for dil in DILATIONS:
        spec = pl.BlockSpec((None, dil, ROW_TILE // dil, ATTN_WIDTH), lambda bi, t: (bi, 0, t, 0))
        qkv_specs += [spec] * 3
        qkv_shapes += [jax.ShapeDtypeStruct((b, dil, s // dil, ATTN_WIDTH), BF16)] * 3
    outs = pl.pallas_call(
        _mix_in_kernel,
        grid=(b, s // ROW_TILE),
        in_specs=[row(d), const(g.shape),
                  pl.BlockSpec(w_in.shape, lambda bi, t: (0, 0), pipeline_mode=pl.Buffered(1)),
                  const(b_gate.shape), const(g2.shape),
                  const(ws_pairs.shape), const(bsp.shape), const(w_pa.shape)],
        out_specs=qkv_specs + [row(d), row(d)],
        out_shape=qkv_shapes + [jax.ShapeDtypeStruct((b, s, d), BF16)] * 2,
        scratch_shapes=[pltpu.VMEM(w_in.shape, BF16)]
                       + [pltpu.VMEM((ATTN_WIDTH // LANES, ROW_TILE, LANES), F32),
                          pltpu.VMEM((ATTN_WIDTH // LANES, DILATION_STEP,
                                      ROW_TILE // DILATION_STEP, LANES), F32)] * 3,
        compiler_params=_cparams(("arbitrary", "arbitrary"), 60),
        name="mix_in",
    )(x, g, w_in, b_gate, g2, ws_pairs, bsp, w_pa)
    n_qkv = 3 * len(DILATIONS)
    qkv = [outs[3 * i:3 * i + 3] for i in range(len(DILATIONS))]
    return qkv, outs[n_qkv], outs[n_qkv + 1]


def _attn_kernel(q_ref, k_ref, v_ref, o_ref, l_ref, bias_ref, *, dil):
    n_res, rows, _ = q_ref.shape
    seq = k_ref.shape[1]
    t = pl.program_id(2)
    first = jnp.logical_and(jnp.logical_and(pl.program_id(0) == 0, pl.program_id(1) == 0), t == 0)

    @pl.when(first)
    def _():
        ii = lax.broadcasted_iota(I32, (Q_TILE, KEY_TILE), 0)
        jj = lax.broadcasted_iota(I32, (Q_TILE, KEY_TILE), 1)
        for var in range(3):
            absd = jnp.abs(jj - ii - var * HALF_WINDOW)
            valid = absd <= HALF_WINDOW
            absf = absd.astype(F32)
            for h in range(N_HEADS):
                slope = 2.0 ** (-8.0 * (h + 1) / N_HEADS)
                bias_ref[var, h] = jnp.where(valid, -(slope * dil) * absf, -jnp.inf)

    lane = lax.broadcasted_iota(I32, (Q_TILE, LANES), 1)
    lane_lo = lane < HEAD_DIM
    mask_lo = jnp.where(lane_lo, 1.0, 0.0).astype(BF16)
    mask_hi = jnp.where(lane_lo, 0.0, 1.0).astype(BF16)
    for rr in range(n_res):
        for qi in range(rows // Q_TILE):
            rs = slice(qi * Q_TILE, (qi + 1) * Q_TILE)
            i0 = t * rows + qi * Q_TILE
            start = pl.multiple_of(jnp.clip(i0 - HALF_WINDOW, 0, seq - KEY_TILE), HALF_WINDOW)
            var = (i0 - start) // HALF_WINDOW
            for p in range(ATTN_WIDTH // LANES):
                cs = slice(p * LANES, (p + 1) * LANES)
                qp = q_ref[rr, rs, cs]
                kp = k_ref[rr, pl.ds(start, KEY_TILE), cs]
                vp = v_ref[rr, pl.ds(start, KEY_TILE), cs]
                q2 = jnp.concatenate([qp * mask_lo, qp * mask_hi], axis=0)
                s2 = _dot_nt(q2, kp)
                probs = []
                for hh in range(2):
                    h = 2 * p + hh
                    s = s2[hh * Q_TILE:(hh + 1) * Q_TILE] + bias_ref[var, h]
                    m = jnp.max(s, axis=-1, keepdims=True)
                    e = jnp.exp(s - m)
                    den = jnp.sum(e, axis=-1, keepdims=True)
                    probs.append(e)
                    lo = h * STAT_LANES
                    l_ref[rr, rs, lo:lo + DEN_SHIFT] = jnp.broadcast_to(m, (Q_TILE, DEN_SHIFT))
                    l_ref[rr, rs, lo + DEN_SHIFT:lo + STAT_LANES] = jnp.broadcast_to(
                        den, (Q_TILE, DEN_SHIFT))
                o2 = _dot(jnp.concatenate(probs, axis=0).astype(BF16), vp)
                o_ref[rr, rs, cs] = jnp.where(lane_lo, o2[:Q_TILE], o2[Q_TILE:]).astype(BF16)


def _attn_pattern(q, k, v, dil):
    b, _, seq, w = q.shape
    rows = min(seq, ATTN_STEP_ROWS)
    n_res = ATTN_STEP_ROWS // rows
    qspec = lambda width: pl.BlockSpec((None, n_res, rows, width), lambda bi, r, t: (bi, r, t, 0))
    kspec = pl.BlockSpec((None, n_res, seq, w), lambda bi, r, t: (bi, r, 0, 0))
    return pl.pallas_call(
        functools.partial(_attn_kernel, dil=dil),
        grid=(b, dil // n_res, seq // rows),
        in_specs=[qspec(w), kspec, kspec],
        out_specs=[qspec(w), qspec(LANES)],
        out_shape=[jax.ShapeDtypeStruct(q.shape, BF16),
                   jax.ShapeDtypeStruct((b, dil, seq, LANES), F32)],
        scratch_shapes=[pltpu.VMEM((3, N_HEADS, Q_TILE, KEY_TILE), F32)],
        compiler_params=_cparams(("arbitrary", "arbitrary", "arbitrary"), 48),
        name=f"attn_d{dil}",
    )(q, k, v)


def _to_natural(src_ref, nat_ref, tmp_ref, rows):
    n_slab = nat_ref.shape[0]
    step = DILATION_STEP
    for p in range(n_slab):
        cs = slice(p * LANES, (p + 1) * LANES)
        for r4 in range(step):
            if tmp_ref is None:
                quarter = src_ref[r4][:, cs].astype(F32)
            else:
                for c in range(step):
                    tmp_ref[p, r4, pl.ds(c, rows // (step * step), stride=step), :] = (
                        src_ref[r4 + step * c][:, cs].astype(F32))
                quarter = tmp_ref[p, r4]
            nat_ref[p, pl.ds(r4, rows // step, stride=step), :] = quarter
    return jnp.concatenate([nat_ref[p] for p in range(n_slab)], axis=1)


def _mix_out_kernel(x_ref, ta_ref, gb_ref, *refs):
    n_pat = len(DILATIONS)
    o_refs = refs[:n_pat]
    l_refs = refs[n_pat:2 * n_pat]
    pb_f32, wo_f32, g_ref, wr_ref, x1_ref, h2_ref, aff_ref = refs[2 * n_pat:2 * n_pat + 7]
    pb_ref, wo_ref = refs[2 * n_pat + 7:2 * n_pat + 9]
    stage_refs = refs[2 * n_pat + 9:]
    rows = x_ref.shape[0]

    @pl.when(jnp.logical_and(pl.program_id(0) == 0, pl.program_id(1) == 0))
    def _():
        pb_ref[...] = pb_f32[...].astype(BF16)
        wo_ref[...] = wo_f32[...].astype(BF16)

    stage_refs = list(stage_refs)
    outs = [o_refs[0][0].astype(F32)]
    lses = [l_refs[0][0]]
    for di in (1, 2):
        for src, dest in ((o_refs[di], outs), (l_refs[di], lses)):
            nat = stage_refs.pop(0)
            tmp = stage_refs.pop(0) if di == 2 else None
            dest.append(_to_natural(src, nat, tmp, rows))

    dens = [pltpu.roll(st, LANES - DEN_SHIFT, 1) for st in lses]
    lses = [st + jnp.log(den) for st, den in zip(lses, dens)]
    m = functools.reduce(jnp.maximum, lses)
    ws = [jnp.exp(l - m) for l in lses]
    inv = 1.0 / functools.reduce(lambda a, c: a + c, ws)
    lane = lax.broadcasted_iota(I32, (rows, LANES), 1)
    used = lane % STAT_LANES < DEN_SHIFT
    ws = [jnp.where(used, w * inv / den, 0.0) for w, den in zip(ws, dens)]
    k_i = lax.broadcasted_iota(I32, (2 * LANES, ATTN_WIDTH), 0) % LANES
    c_i = lax.broadcasted_iota(I32, (2 * LANES, ATTN_WIDTH), 1)
    spread = jnp.where(k_i == (c_i // HEAD_DIM) * STAT_LANES, 1.0, 0.0).astype(BF16)
    parts = []
    for w in ws:
        w_hi = w.astype(BF16)
        w_lo = (w - w_hi.astype(F32)).astype(BF16)
        parts.append(jnp.concatenate([w_hi, w_lo], axis=1))
    factors = _dot(jnp.concatenate(parts, axis=0), spread)
    o = None
    for i, o_p in enumerate(outs):
        term = factors[i * rows:(i + 1) * rows] * o_p
        o = term if o is None else o + term

    ob = _dot(o.astype(BF16), pb_ref[...])
    merged = (ta_ref[...].astype(F32) + gb_ref[...].astype(F32) * ob).astype(BF16)
    x1 = x_ref[...] + _dot(merged, wo_ref[...])
    x1_ref[...] = x1
    h2 = _rms(x1, g_ref[...])
    h2_ref[...] = h2.astype(BF16)
    h_hi = h2.astype(BF16)
    h_lo = (h2 - h_hi.astype(F32)).astype(BF16)
    wr = wr_ref[...]
    w_hi = wr.astype(BF16)
    w_lo = (wr - w_hi.astype(F32)).astype(BF16)
    n_e = wr.shape[0]
    by_hi = _dot_nt(jnp.concatenate([w_hi, w_lo], axis=0), h_hi)
    logits = by_hi[:n_e] + (_dot_nt(w_hi, h_lo) + by_hi[n_e:])
    e = jnp.exp(logits - jnp.max(logits, axis=0, keepdims=True))
    aff_ref[...] = e / jnp.sum(e, axis=0, keepdims=True)


def _mix_out(x, ta, gb, os_, ls_, w_pb, w_out, g, w_router_t):
    b, s, d = x.shape
    n_e = w_router_t.shape[0]
    const = lambda shape: pl.BlockSpec(shape, lambda bi, t: (0,) * len(shape))
    once = lambda shape: pl.BlockSpec(shape, lambda bi, t: (0,) * len(shape),
                                      pipeline_mode=pl.Buffered(1))
    row = lambda w: pl.BlockSpec((None, ROW_TILE, w), lambda bi, t: (bi, t, 0))
    res = lambda dil, w: pl.BlockSpec((None, dil, ROW_TILE // dil, w), lambda bi, t: (bi, 0, t, 0))
    stage = []
    for di in (1, 2):
        for slabs in (ATTN_WIDTH // LANES, 1):
            stage.append(pltpu.VMEM((slabs, ROW_TILE, LANES), F32))
            if di == 2:
                stage.append(pltpu.VMEM((slabs, DILATION_STEP, ROW_TILE // DILATION_STEP, LANES),
                                        F32))
    return pl.pallas_call(
        _mix_out_kernel,
        grid=(b, s // ROW_TILE),
        in_specs=[row(d), row(d), row(d)]
                 + [res(dil, ATTN_WIDTH) for dil in DILATIONS]
                 + [res(dil, LANES) for dil in DILATIONS]
                 + [once(w_pb.shape), once(w_out.shape), const(g.shape), const(w_router_t.shape)],
        out_specs=[row(d), row(d), pl.BlockSpec((None, n_e, ROW_TILE), lambda bi, t: (bi, 0, t))],
        out_shape=[jax.ShapeDtypeStruct((b, s, d), F32), jax.ShapeDtypeStruct((b, s, d), BF16),
                   jax.ShapeDtypeStruct((b, n_e, s), F32)],
        scratch_shapes=[pltpu.VMEM(w_pb.shape, BF16), pltpu.VMEM(w_out.shape, BF16)] + stage,
        compiler_params=_cparams(("arbitrary", "arbitrary"), 48),
        name="mix_out",
    )(x, ta, gb, *os_, *ls_, w_pb, w_out, g, w_router_t)


class _Plan(NamedTuple):
    tiles: int
    fast: int
    slow: int
    group: int


GATHER_PLAN = _Plan(tiles=GATHER_TOKENS // SLOT_TILE, fast=64, slow=GATHER_TOKENS + SLOT_ALIGN,
                    group=N_EXPERTS // 2)
COMBINE_PLAN = _Plan(tiles=1, fast=48, slow=SLOT_TILE + SLOT_ALIGN, group=N_EXPERTS)
FAST_WINDOW, SLOT_WINDOW = COMBINE_PLAN.fast, COMBINE_PLAN.slow


def _slot_plan(cum, plan, cap):
    rows = cum.shape[0]
    nxt = pltpu.roll(cum, LANES - plan.tiles, 1)
    lo = jnp.floor(cum * (1.0 / SLOT_ALIGN)) * SLOT_ALIGN
    ok = jnp.where(nxt - lo <= plan.fast, 1.0, 0.0)
    fit = jnp.min(ok.reshape(rows // plan.group, plan.group, LANES), axis=1)
    assert (cap - plan.fast) % SLOT_ALIGN == 0 and (cap - plan.slow) % SLOT_ALIGN == 0
    return (jnp.minimum(lo, cap - plan.fast).astype(I32),
            jnp.minimum(lo, cap - plan.slow).astype(I32), fit.astype(I32))


def _topk_kernel(aff_ref, rank_ref, *plan_refs, cap, plans):
    n_e, s = aff_ref.shape
    n_blk = s // SLOT_TILE
    aff = aff_ref[...]
    thr = jnp.zeros((n_e, 1), I32)
    for bit in range(30, -1, -1):
        cand = thr | (1 << bit)
        cnt = jnp.sum((aff >= pltpu.bitcast(cand, F32)).astype(I32), axis=1, keepdims=True)
        thr = jnp.where(cnt >= cap, cand, thr)
    above = aff >= pltpu.bitcast(thr + 1, F32)
    tie = jnp.logical_and(aff >= pltpu.bitcast(thr, F32), jnp.logical_not(above))
    need = (cap - jnp.sum(above.astype(I32), axis=1, keepdims=True)).astype(F32)
    r_i = lax.broadcasted_iota(I32, (SLOT_TILE, SLOT_TILE), 0)
    c_i = lax.broadcasted_iota(I32, (SLOT_TILE, SLOT_TILE), 1)
    tri = jnp.where(r_i < c_i, 1.0, 0.0).astype(BF16)
    lane = lax.broadcasted_iota(I32, (n_e, LANES), 1)
    run_tie = jnp.zeros((n_e, 1), F32)
    run_sel = jnp.zeros((n_e, 1), F32)
    cum = jnp.zeros((n_e, LANES), F32)
    for j in range(n_blk):
        cs = slice(j * SLOT_TILE, (j + 1) * SLOT_TILE)
        tie_f = jnp.where(tie[:, cs], 1.0, 0.0)
        tie_rank = _dot(tie_f.astype(BF16), tri) + run_tie
        run_tie = run_tie + jnp.sum(tie_f, axis=1, keepdims=True)
        sel_f = jnp.where(above[:, cs], 1.0, jnp.where(tie_rank < need, tie_f, 0.0))
        rank = _dot(sel_f.astype(BF16), tri) + run_sel
        rank_ref[:, cs] = jnp.where(sel_f > 0.0, rank, -1.0).astype(I32)
        cum = jnp.where(lane == j, run_sel, cum)
        run_sel = run_sel + jnp.sum(sel_f, axis=1, keepdims=True)
    cum = jnp.where(lane == n_blk, run_sel, cum)
    for i, plan in enumerate(plans):
        for ref, val in zip(plan_refs[3 * i:3 * i + 3], _slot_plan(cum, plan, cap)):
            ref[...] = val


def _topk(aff, cap, plans):
    b, n_e, s = aff.shape
    rows = b * n_e
    full = lambda r, w: pl.BlockSpec((r, w), lambda i: (0, 0))
    plan_specs, plan_shapes = [], []
    for plan in plans:
        for r in (rows, rows, rows // plan.group):
            plan_specs.append(full(r, LANES))
            plan_shapes.append(jax.ShapeDtypeStruct((r, LANES), I32))
    rank, *tables = pl.pallas_call(
        functools.partial(_topk_kernel, cap=cap, plans=plans),
        grid=(1,),
        in_specs=[full(rows, s)],
        out_specs=[full(rows, s)] + plan_specs,
        out_shape=[jax.ShapeDtypeStruct((rows, s), I32)] + plan_shapes,
        compiler_params=_cparams(("arbitrary",), 32),
        name="topk",
    )(aff.reshape(rows, s))
    tables = [t.reshape(-1) for t in tables]
    return rank.reshape(b, n_e, s), [tables[3 * i:3 * i + 3] for i in range(len(plans))]


def _window_start(table_ref, expert_row, tile):
    return pl.multiple_of(table_ref[expert_row * LANES + tile], SLOT_ALIGN)


def _span(jj, size):
    if isinstance(jj, int):
        return slice(jj * size, (jj + 1) * size)
    return pl.ds(pl.multiple_of(jj * size, size), size)


def _run_blocks(fits, fast, general):
    all_fit = functools.reduce(jnp.logical_and, fits)

    @pl.when(all_fit)
    def _():
        for jj in range(len(fits)):
            fast(jj)

    @pl.when(jnp.logical_not(all_fit))
    def _():
        def body(jj, carry):
            general(jj)
            return carry

        lax.fori_loop(0, len(fits), body, 0)


def _gather_kernel(fast_ref, slow_ref, fit_ref, rank_ref, aff_ref, h2_ref, xe_ref, gate_ref):
    n_e = rank_ref.shape[0]
    rows = h2_ref.shape[0]
    group = pl.program_id(0) * pl.num_programs(1) + pl.program_id(1)
    t = pl.program_id(2)

    @pl.when(t == 0)
    def _():
        xe_ref[...] = jnp.zeros_like(xe_ref)
        gate_ref[...] = jnp.zeros_like(gate_ref)

    row_fast = lax.broadcasted_iota(I32, (GATHER_PLAN.fast, GATHER_TOKENS), 0)
    row_slow = lax.broadcasted_iota(I32, (GATHER_PLAN.slow, GATHER_TOKENS), 0)
    n_blk = rows // GATHER_TOKENS
    tile = lambda jj: (t * n_blk + jj) * GATHER_PLAN.tiles

    def window(jj, e, table_ref, row_i):
        toks = _span(jj, GATHER_TOKENS)
        lo = _window_start(table_ref, group * n_e + e, tile(jj))
        return lo, (row_i + lo) == rank_ref[e, :, toks], aff_ref[e, :, toks]

    def add_window(e, lo, hit, aff_row, rows_e):
        win = pl.ds(lo, hit.shape[0])
        xe_ref[e, win, :] += rows_e.astype(BF16)
        gate_ref[e, win, :] += jnp.sum(jnp.where(hit, aff_row, 0.0), axis=1, keepdims=True)

    def tokens(jj):
        return h2_ref[_span(jj, GATHER_TOKENS), :]

    def fast(jj):
        ws = [window(jj, e, fast_ref, row_fast) for e in range(n_e)]
        stack = jnp.concatenate([jnp.where(h, 1.0, 0.0).astype(BF16) for _, h, _ in ws], axis=0)
        res = _dot(stack, tokens(jj))
        for e, (lo, h, aff_row) in enumerate(ws):
            add_window(e, lo, h, aff_row, res[e * GATHER_PLAN.fast:(e + 1) * GATHER_PLAN.fast])

    def general(jj):
        def body(e, carry):
            lo, h, aff_row = window(jj, e, slow_ref, row_slow)
            add_window(e, lo, h, aff_row, _dot(jnp.where(h, 1.0, 0.0).astype(BF16), tokens(jj)))
            return carry

        lax.fori_loop(0, n_e, body, 0)

    _run_blocks([fit_ref[group * LANES + tile(jj)] != 0 for jj in range(n_blk)], fast, general)


def _gather(plan_tables, rank, aff, h2, cap):
    b, s, d = h2.shape
    n_e = rank.shape[1]
    rows = cap
    grp = GATHER_PLAN.group
    per_tok = pl.BlockSpec((None, grp, 1, GATHER_STEP_TOKENS),
                           lambda bi, eg, t, *_: (bi, eg, 0, t))
    whole = lambda w: pl.BlockSpec((None, grp, rows, w), lambda bi, eg, t, *_: (bi, eg, 0, 0))
    grid_spec = pltpu.PrefetchScalarGridSpec(
        num_scalar_prefetch=3,
        grid=(b, n_e // grp, s // GATHER_STEP_TOKENS),
        in_specs=[per_tok, per_tok,
                  pl.BlockSpec((None, GATHER_STEP_TOKENS, d), lambda bi, eg, t, *_: (bi, t, 0))],
        out_specs=[whole(d), whole(LANES)],
    )
    expert_rows = lambda a: a.reshape(b, n_e, 1, s)
    return pl.pallas_call(
        _gather_kernel,
        grid_spec=grid_spec,
        out_shape=[jax.ShapeDtypeStruct((b, n_e, rows, d), BF16),
                   jax.ShapeDtypeStruct((b, n_e, rows, LANES), F32)],
        compiler_params=_cparams(("arbitrary", "arbitrary", "arbitrary"), 56),
        name="gather",
    )(*plan_tables, expert_rows(rank), expert_rows(aff), h2)


def _moe_ffn_kernel(xe_ref, gate_ref, wg_ref, wu_ref, wd_ref, ye_ref, wg_bf, wu_bf, wd_bf):
    n_seq, cap, d = xe_ref.shape

    @pl.when(pl.program_id(1) == 0)
    def _():
        wg_bf[...] = wg_ref[...].astype(BF16)
        wu_bf[...] = wu_ref[...].astype(BF16)
        wd_bf[...] = wd_ref[...].astype(BF16)

    xe = xe_ref[...].reshape(n_seq * cap, d)
    gate_h = _dot(xe, wg_bf[...])
    up_h = _dot(xe, wu_bf[...])
    hidden = (gate_h * _sigmoid(gate_h) * up_h).astype(BF16)
    ye = _dot(hidden, wd_bf[...]) * gate_ref[...].reshape(n_seq * cap, LANES)[:, 0:1]
    ye_ref[...] = ye.astype(BF16).reshape(n_seq, cap, d)


def _moe_ffn(xe, gate, wg, wu, wd, cap):
    b, n_e, rows, d = xe.shape
    hid = wg.shape[2]
    return pl.pallas_call(
        _moe_ffn_kernel,
        grid=(n_e, b // FFN_SEQS),
        in_specs=[
            pl.BlockSpec((FFN_SEQS, None, cap, d), lambda e, bi: (bi, e, 0, 0)),
            pl.BlockSpec((FFN_SEQS, None, cap, LANES), lambda e, bi: (bi, e, 0, 0)),
            pl.BlockSpec((None, d, hid), lambda e, bi: (e, 0, 0)),
            pl.BlockSpec((None, d, hid), lambda e, bi: (e, 0, 0)),
            pl.BlockSpec((None, hid, d), lambda e, bi: (e, 0, 0)),
        ],
        out_specs=pl.BlockSpec((FFN_SEQS, None, rows, d), lambda e, bi: (bi, e, 0, 0)),
        out_shape=jax.ShapeDtypeStruct((b, n_e, rows, d), BF16),
        scratch_shapes=[pltpu.VMEM((d, hid), BF16), pltpu.VMEM((d, hid), BF16),
                        pltpu.VMEM((hid, d), BF16)],
        compiler_params=_cparams(("arbitrary", "arbitrary"), 56),
        name="moe_ffn",
    )(xe, gate, wg, wu, wd)


def _combine_kernel(fast_ref, slow_ref, fit_ref, rank_ref, ye_ref, x1_ref, g_ref, y_ref, rhs_ref,
                    acc_ref):
    n_e = rank_ref.shape[0]
    n_blk = x1_ref.shape[0] // SLOT_TILE
    bi = pl.program_id(0)
    t = pl.program_id(1)
    row_fast = lax.broadcasted_iota(I32, (FAST_WINDOW, SLOT_TILE), 0)
    row_slow = lax.broadcasted_iota(I32, (SLOT_WINDOW, SLOT_TILE), 0)

    def hits(jj, e, table_ref, row_i):
        lo = _window_start(table_ref, bi * n_e + e, t * n_blk + jj)
        hit = (row_i + lo) == rank_ref[e, :, _span(jj, SLOT_TILE)]
        return pl.ds(lo, row_i.shape[0]), jnp.where(hit, 1.0, 0.0).astype(BF16)

    def finish(jj, moe):
        toks = _span(jj, SLOT_TILE)
        y_ref[toks, :] = _rms(x1_ref[toks, :] + moe, g_ref[...])

    def fast(jj):
        rhs = rhs_ref.at[jj % rhs_ref.shape[0]]
        stack = []
        for e in range(n_e):
            win, hit = hits(jj, e, fast_ref, row_fast)
            rhs[e * FAST_WINDOW:(e + 1) * FAST_WINDOW, :] = ye_ref[e, win, :]
            stack.append(hit)
        finish(jj, _dot_tn(jnp.concatenate(stack, axis=0), rhs[...]))

    def general(jj):
        acc_ref[...] = jnp.zeros_like(acc_ref)

        def body(e, carry):
            win, hit = hits(jj, e, slow_ref, row_slow)
            acc_ref[...] += _dot_tn(hit, ye_ref[e, win, :])
            return carry

        lax.fori_loop(0, n_e, body, 0)
        finish(jj, acc_ref[...])

    _run_blocks([fit_ref[bi * LANES + t * n_blk + jj] != 0 for jj in range(n_blk)], fast, general)


def _combine(plan_tables, rank, ye, x1, g):
    b, s, d = x1.shape
    n_e, ye_rows = ye.shape[1], ye.shape[2]
    row = pl.BlockSpec((None, COMBINE_ROWS, d), lambda bi, t, *_: (bi, t, 0))
    grid_spec = pltpu.PrefetchScalarGridSpec(
        num_scalar_prefetch=3,
        grid=(b, s // COMBINE_ROWS),
        in_specs=[
            pl.BlockSpec((None, n_e, 1, COMBINE_ROWS), lambda bi, t, *_: (bi, 0, 0, t)),
            pl.BlockSpec((None, n_e, ye_rows, d), lambda bi, t, *_: (bi, 0, 0, 0)),
            row,
            pl.BlockSpec(g.shape, lambda bi, t, *_: (0, 0)),
        ],
        out_specs=row,
        scratch_shapes=[pltpu.VMEM((2, n_e * FAST_WINDOW, d), BF16),
                        pltpu.VMEM((SLOT_TILE, d), F32)],
    )
    return pl.pallas_call(
        _combine_kernel,
        grid_spec=grid_spec,
        out_shape=jax.ShapeDtypeStruct((b, s, d), F32),
        compiler_params=_cparams(("arbitrary", "arbitrary"), 60),
        name="combine",
    )(*plan_tables, rank.reshape(b, n_e, 1, s), ye, x1, g)


def _moe_stages(aff, h2, x1, wg, wu, wd, g_final, cap):
    rank, (gather_tables, combine_tables) = _topk(aff, cap, (GATHER_PLAN, COMBINE_PLAN))
    xe, gate = _gather(gather_tables, rank, aff, h2, cap)
    ye = _moe_ffn(xe, gate, wg, wu, wd, cap)
    return _combine(combine_tables, rank, ye, x1, g_final)


def kernel(x, norm_mix_g, w_in, b_gate, gmlp_norm_g, w_spatial, b_spatial, w_proj_a, w_proj_b,
           w_out, norm_ffn_g, w_router, w_e_gate, w_e_up, w_e_down, norm_final_g):
    b, s, d = x.shape
    assert w_in.shape[0] == 1, "single-layer block"
    cap = CAPACITY_FACTOR * s // N_EXPERTS
    group_width = GMLP_WIDTH // GMLP_GROUPS
    ws_pairs = w_spatial[0].astype(BF16).reshape(GMLP_GROUPS // 2, 2 * CHUNK, CHUNK)
    bsp = jnp.repeat(b_spatial[0].T, group_width, axis=1)
    qkv, ta, gb = _mix_in(x, norm_mix_g, w_in[0], b_gate, gmlp_norm_g, ws_pairs, bsp,
                          w_proj_a[0].astype(BF16))
    os_, ls_ = [], []
    for (q, k, v), dil in zip(qkv, DILATIONS):
        o, lse = _attn_pattern(q, k, v, dil)
        os_.append(o)
        ls_.append(lse)
    x1, h2, aff = _mix_out(x, ta, gb, os_, ls_, w_proj_b[0], w_out[0], norm_ffn_g, w_router[0].T)
    return _moe_stages(aff, h2, x1, w_e_gate[0], w_e_up[0], w_e_down[0], norm_final_g[None], cap)
```

```python
import functools

from typing import NamedTuple

import jax
import jax.numpy as jnp
from jax import lax
from jax.experimental import pallas as pl
from jax.experimental.pallas import tpu as pltpu

F32 = jnp.float32
BF16 = jnp.bfloat16
I32 = jnp.int32

EPS = 1e-6
GMLP_WIDTH = 512
GMLP_GROUPS = 8
CHUNK = 128
N_HEADS = 8
HEAD_DIM = 64
ATTN_WIDTH = N_HEADS * HEAD_DIM
DILATIONS = (1, 4, 16)
DILATION_STEP = 4
assert all(b == a * DILATION_STEP for a, b in zip(DILATIONS, DILATIONS[1:]))
HALF_WINDOW = 64
N_EXPERTS = 16
CAPACITY_FACTOR = 2

LANES = 128
Q_TILE = 128
KEY_TILE = 2 * Q_TILE
ATTN_STEP_ROWS = 2048
STAT_LANES = LANES // N_HEADS
DEN_SHIFT = STAT_LANES // 2
SLOT_TILE = 128
SLOT_ALIGN = 16
COMBINE_ROWS = 1024
FFN_SEQS = 2
GATHER_STEP_TOKENS = 2048
GATHER_TOKENS = 256
ROW_TILE = 512
MIB = 1024 * 1024


def _cparams(sem, vmem_mib):
    return pltpu.CompilerParams(dimension_semantics=sem, vmem_limit_bytes=vmem_mib * MIB)


def _gelu_tanh(x):
    return 0.5 * x * (1.0 + jnp.tanh(0.7978845608028654 * (x + 0.044715 * (x * x * x))))


def _sigmoid(x):
    return 1.0 / (1.0 + jnp.exp(-x))


def _rms(x, g):
    return x * lax.rsqrt(jnp.mean(x * x, axis=-1, keepdims=True) + EPS) * g


def _dot(a, b):
    return jnp.dot(a, b, preferred_element_type=F32)


def _dot_nt(a, b):
    return lax.dot_general(a, b, (((1,), (1,)), ((), ())), preferred_element_type=F32)


def _dot_tn(a, b):
    return lax.dot_general(a, b, (((0,), (0,)), ((), ())), preferred_element_type=F32)


def _mix_in_kernel(x_ref, g_ref, win_ref, bg_ref, g2_ref, ws_ref, bsp_ref, pa_ref, *refs):
    n_qkv = 3 * len(DILATIONS)
    qkv_refs = refs[:n_qkv]
    ta_ref, gb_ref = refs[n_qkv:n_qkv + 2]
    win_bf = refs[n_qkv + 2]
    stage_refs = refs[n_qkv + 3:]
    stage_refs = list(zip(stage_refs[0::2], stage_refs[1::2]))
    rows, d_model = x_ref.shape

    @pl.when(jnp.logical_and(pl.program_id(0) == 0, pl.program_id(1) == 0))
    def _():
        for c in range(0, win_ref.shape[1], GMLP_WIDTH):
            win_bf[:, c:c + GMLP_WIDTH] = win_ref[:, c:c + GMLP_WIDTH].astype(BF16)

    h = _rms(x_ref[...], g_ref[...]).astype(BF16)

    def proj(lo, width):
        return _dot(h, win_bf[:, lo:lo + width])

    c0 = 0
    u = _gelu_tanh(proj(c0, GMLP_WIDTH)); c0 += GMLP_WIDTH
    v = _gelu_tanh(proj(c0, GMLP_WIDTH)); c0 += GMLP_WIDTH
    for i in range(3):
        val = proj(c0, ATTN_WIDTH); c0 += ATTN_WIDTH
        if i == 0:
            val = val * (HEAD_DIM ** -0.5)
        qkv_refs[i][0] = val.astype(BF16)
        stage1, stage2 = stage_refs[i]
        n4, n16 = rows // DILATIONS[1], rows // DILATIONS[2]
        out4, out16 = qkv_refs[3 + i], qkv_refs[6 + i]
        for p in range(ATTN_WIDTH // LANES):
            cs = slice(p * LANES, (p + 1) * LANES)
            stage1[p] = val[:, cs]
            for r4 in range(DILATION_STEP):
                part = stage1[p, pl.ds(r4, n4, stride=DILATION_STEP), :]
                out4[r4, :, cs] = part.astype(BF16)
                stage2[p, r4] = part
                for c in range(DILATION_STEP):
                    out16[r4 + DILATION_STEP * c, :, cs] = (
                        stage2[p, r4, pl.ds(c, n16, stride=DILATION_STEP), :].astype(BF16))
    ga = _sigmoid(proj(c0, d_model) + bg_ref[:, :d_model]); c0 += d_model
    gb = _sigmoid(proj(c0, d_model) + bg_ref[:, d_model:])
    gb_ref[...] = gb.astype(BF16)

    vn = _rms(v, g2_ref[...]).astype(BF16)
    lane_lo = lax.broadcasted_iota(I32, (CHUNK, LANES), 1) < HEAD_DIM
    bsp = bsp_ref[...]
    n_chunk = rows // CHUNK
    mixed_slabs = []
    for p in range(GMLP_WIDTH // LANES):
        slab = jnp.concatenate(
            [vn[c * CHUNK:(c + 1) * CHUNK, p * LANES:(p + 1) * LANES] for c in range(n_chunk)],
            axis=1)
        r = _dot(ws_ref[p], slab)
        mixed_slabs.append([jnp.where(lane_lo, r[:CHUNK, c * LANES:(c + 1) * LANES],
                                      r[CHUNK:, c * LANES:(c + 1) * LANES])
                            for c in range(n_chunk)])
    a_chunks = []
    for c in range(n_chunk):
        rs = slice(c * CHUNK, (c + 1) * CHUNK)
        mixed = jnp.concatenate([slabs[c] for slabs in mixed_slabs], axis=1) + bsp
        a_chunks.append((u[rs] * mixed).astype(BF16))
    a = jnp.concatenate(a_chunks, axis=0)
    ta_ref[...] = (ga * _dot(a, pa_ref[...])).astype(BF16)


def _mix_in(x, g, w_in, b_gate, g2, ws_pairs, bsp, w_pa):
    b, s, d = x.shape
    const = lambda shape: pl.BlockSpec(shape, lambda bi, t: (0,) * len(shape))
    row = lambda w: pl.BlockSpec((None, ROW_TILE, w), lambda bi, t: (bi, t, 0))
    qkv_specs, qkv_shapes = [], []
    for dil in DILATIONS:
        spec = pl.BlockSpec((None, dil, ROW_TILE // dil, ATTN_WIDTH), lambda bi, t: (bi, 0, t, 0))
        qkv_specs += [spec] * 3
        qkv_shapes += [jax.ShapeDtypeStruct((b, dil, s // dil, ATTN_WIDTH), BF16)] * 3
    outs = pl.pallas_call(
        _mix_in_kernel,
        grid=(b, s // ROW_TILE),
        in_specs=[row(d), const(g.shape),
                  pl.BlockSpec(w_in.shape, lambda bi, t: (0, 0), pipeline_mode=pl.Buffered(1)),
                  const(b_gate.shape), const(g2.shape),
                  const(ws_pairs.shape), const(bsp.shape), const(w_pa.shape)],
        out_specs=qkv_specs + [row(d), row(d)],
        out_shape=qkv_shapes + [jax.ShapeDtypeStruct((b, s, d), BF16)] * 2,
        scratch_shapes=[pltpu.VMEM(w_in.shape, BF16)]
                       + [pltpu.VMEM((ATTN_WIDTH // LANES, ROW_TILE, LANES), F32),
                          pltpu.VMEM((ATTN_WIDTH // LANES, DILATION_STEP,
                                      ROW_TILE // DILATION_STEP, LANES), F32)] * 3,
        compiler_params=_cparams(("arbitrary", "arbitrary"), 60),
        name="mix_in",
    )(x, g, w_in, b_gate, g2, ws_pairs, bsp, w_pa)
    n_qkv = 3 * len(DILATIONS)
    qkv = [outs[3 * i:3 * i + 3] for i in range(len(DILATIONS))]
    return qkv, outs[n_qkv], outs[n_qkv + 1]


def _attn_kernel(q_ref, k_ref, v_ref, o_ref, l_ref, bias_ref, *, dil):
    n_res, rows, _ = q_ref.shape
    seq = k_ref.shape[1]
    t = pl.program_id(2)
    first = jnp.logical_and(jnp.logical_and(pl.program_id(0) == 0, pl.program_id(1) == 0), t == 0)

    @pl.when(first)
    def _():
        ii = lax.broadcasted_iota(I32, (Q_TILE, KEY_TILE), 0)
        jj = lax.broadcasted_iota(I32, (Q_TILE, KEY_TILE), 1)
        for var in range(3):
            absd = jnp.abs(jj - ii - var * HALF_WINDOW)
            valid = absd <= HALF_WINDOW
            absf = absd.astype(F32)
            for h in range(N_HEADS):
                slope = 2.0 ** (-8.0 * (h + 1) / N_HEADS)
                bias_ref[var, h] = jnp.where(valid, -(slope * dil) * absf, -jnp.inf)

    lane = lax.broadcasted_iota(I32, (Q_TILE, LANES), 1)
    lane_lo = lane < HEAD_DIM
    mask_lo = jnp.where(lane_lo, 1.0, 0.0).astype(BF16)
    mask_hi = jnp.where(lane_lo, 0.0, 1.0).astype(BF16)
    for rr in range(n_res):
        for qi in range(rows // Q_TILE):
            rs = slice(qi * Q_TILE, (qi + 1) * Q_TILE)
            i0 = t * rows + qi * Q_TILE
            start = pl.multiple_of(jnp.clip(i0 - HALF_WINDOW, 0, seq - KEY_TILE), HALF_WINDOW)
            var = (i0 - start) // HALF_WINDOW
            for p in range(ATTN_WIDTH // LANES):
                cs = slice(p * LANES, (p + 1) * LANES)
                qp = q_ref[rr, rs, cs]
                kp = k_ref[rr, pl.ds(start, KEY_TILE), cs]
                vp = v_ref[rr, pl.ds(start, KEY_TILE), cs]
                q2 = jnp.concatenate([qp * mask_lo, qp * mask_hi], axis=0)
                s2 = _dot_nt(q2, kp)
                probs = []
                for hh in range(2):
                    h = 2 * p + hh
                    s = s2[hh * Q_TILE:(hh + 1) * Q_TILE] + bias_ref[var, h]
                    m = jnp.max(s, axis=-1, keepdims=True)
                    e = jnp.exp(s - m)
                    den = jnp.sum(e, axis=-1, keepdims=True)
                    probs.append(e)
                    lo = h * STAT_LANES
                    l_ref[rr, rs, lo:lo + DEN_SHIFT] = jnp.broadcast_to(m, (Q_TILE, DEN_SHIFT))
                    l_ref[rr, rs, lo + DEN_SHIFT:lo + STAT_LANES] = jnp.broadcast_to(
                        den, (Q_TILE, DEN_SHIFT))
                o2 = _dot(jnp.concatenate(probs, axis=0).astype(BF16), vp)
                o_ref[rr, rs, cs] = jnp.where(lane_lo, o2[:Q_TILE], o2[Q_TILE:]).astype(BF16)


def _attn_pattern(q, k, v, dil):
    b, _, seq, w = q.shape
    rows = min(seq, ATTN_STEP_ROWS)
    n_res = ATTN_STEP_ROWS // rows
    qspec = lambda width: pl.BlockSpec((None, n_res, rows, width), lambda bi, r, t: (bi, r, t, 0))
    kspec = pl.BlockSpec((None, n_res, seq, w), lambda bi, r, t: (bi, r, 0, 0))
    return pl.pallas_call(
        functools.partial(_attn_kernel, dil=dil),
        grid=(b, dil // n_res, seq // rows),
        in_specs=[qspec(w), kspec, kspec],
        out_specs=[qspec(w), qspec(LANES)],
        out_shape=[jax.ShapeDtypeStruct(q.shape, BF16),
                   jax.ShapeDtypeStruct((b, dil, seq, LANES), F32)],
        scratch_shapes=[pltpu.VMEM((3, N_HEADS, Q_TILE, KEY_TILE), F32)],
        compiler_params=_cparams(("arbitrary", "arbitrary", "arbitrary"), 48),
        name=f"attn_d{dil}",
    )(q, k, v)


def _to_natural(src_ref, nat_ref, tmp_ref, rows):
    n_slab = nat_ref.shape[0]
    step = DILATION_STEP
    for p in range(n_slab):
        cs = slice(p * LANES, (p + 1) * LANES)
        for r4 in range(step):
            if tmp_ref is None:
                quarter = src_ref[r4][:, cs].astype(F32)
            else:
                for c in range(step):
                    tmp_ref[p, r4, pl.ds(c, rows // (step * step), stride=step), :] = (
                        src_ref[r4 + step * c][:, cs].astype(F32))
                quarter = tmp_ref[p, r4]
            nat_ref[p, pl.ds(r4, rows // step, stride=step), :] = quarter
    return jnp.concatenate([nat_ref[p] for p in range(n_slab)], axis=1)


def _mix_out_kernel(x_ref, ta_ref, gb_ref, *refs):
    n_pat = len(DILATIONS)
    o_refs = refs[:n_pat]
    l_refs = refs[n_pat:2 * n_pat]
    pb_f32, wo_f32, g_ref, wr_ref, x1_ref, h2_ref, aff_ref = refs[2 * n_pat:2 * n_pat + 7]
    pb_ref, wo_ref = refs[2 * n_pat + 7:2 * n_pat + 9]
    stage_refs = refs[2 * n_pat + 9:]
    rows = x_ref.shape[0]

    @pl.when(jnp.logical_and(pl.program_id(0) == 0, pl.program_id(1) == 0))
    def _():
        pb_ref[...] = pb_f32[...].astype(BF16)
        wo_ref[...] = wo_f32[...].astype(BF16)

    stage_refs = list(stage_refs)
    outs = [o_refs[0][0].astype(F32)]
    lses = [l_refs[0][0]]
    for di in (1, 2):
        for src, dest in ((o_refs[di], outs), (l_refs[di], lses)):
            nat = stage_refs.pop(0)
            tmp = stage_refs.pop(0) if di == 2 else None
            dest.append(_to_natural(src, nat, tmp, rows))

    dens = [pltpu.roll(st, LANES - DEN_SHIFT, 1) for st in lses]
    lses = [st + jnp.log(den) for st, den in zip(lses, dens)]
    m = functools.reduce(jnp.maximum, lses)
    ws = [jnp.exp(l - m) for l in lses]
    inv = 1.0 / functools.reduce(lambda a, c: a + c, ws)
    lane = lax.broadcasted_iota(I32, (rows, LANES), 1)
    used = lane % STAT_LANES < DEN_SHIFT
    ws = [jnp.where(used, w * inv / den, 0.0) for w, den in zip(ws, dens)]
    k_i = lax.broadcasted_iota(I32, (2 * LANES, ATTN_WIDTH), 0) % LANES
    c_i = lax.broadcasted_iota(I32, (2 * LANES, ATTN_WIDTH), 1)
    spread = jnp.where(k_i == (c_i // HEAD_DIM) * STAT_LANES, 1.0, 0.0).astype(BF16)
    parts = []
    for w in ws:
        w_hi = w.astype(BF16)
        w_lo = (w - w_hi.astype(F32)).astype(BF16)
        parts.append(jnp.concatenate([w_hi, w_lo], axis=1))
    factors = _dot(jnp.concatenate(parts, axis=0), spread)
    o = None
    for i, o_p in enumerate(outs):
        term = factors[i * rows:(i + 1) * rows] * o_p
        o = term if o is None else o + term

    ob = _dot(o.astype(BF16), pb_ref[...])
    merged = (ta_ref[...].astype(F32) + gb_ref[...].astype(F32) * ob).astype(BF16)
    x1 = x_ref[...] + _dot(merged, wo_ref[...])
    x1_ref[...] = x1
    h2 = _rms(x1, g_ref[...])
    h2_ref[...] = h2.astype(BF16)
    h_hi = h2.astype(BF16)
    h_lo = (h2 - h_hi.astype(F32)).astype(BF16)
    wr = wr_ref[...]
    w_hi = wr.astype(BF16)
    w_lo = (wr - w_hi.astype(F32)).astype(BF16)
    n_e = wr.shape[0]
    by_hi = _dot_nt(jnp.concatenate([w_hi, w_lo], axis=0), h_hi)
    logits = by_hi[:n_e] + (_dot_nt(w_hi, h_lo) + by_hi[n_e:])
    e = jnp.exp(logits - jnp.max(logits, axis=0, keepdims=True))
    aff_ref[...] = e / jnp.sum(e, axis=0, keepdims=True)


def _mix_out(x, ta, gb, os_, ls_, w_pb, w_out, g, w_router_t):
    b, s, d = x.shape
    n_e = w_router_t.shape[0]
    const = lambda shape: pl.BlockSpec(shape, lambda bi, t: (0,) * len(shape))
    once = lambda shape: pl.BlockSpec(shape, lambda bi, t: (0,) * len(shape),
                                      pipeline_mode=pl.Buffered(1))
    row = lambda w: pl.BlockSpec((None, ROW_TILE, w), lambda bi, t: (bi, t, 0))
    res = lambda dil, w: pl.BlockSpec((None, dil, ROW_TILE // dil, w), lambda bi, t: (bi, 0, t, 0))
    stage = []
    for di in (1, 2):
        for slabs in (ATTN_WIDTH // LANES, 1):
            stage.append(pltpu.VMEM((slabs, ROW_TILE, LANES), F32))
            if di == 2:
                stage.append(pltpu.VMEM((slabs, DILATION_STEP, ROW_TILE // DILATION_STEP, LANES),
                                        F32))
    return pl.pallas_call(
        _mix_out_kernel,
        grid=(b, s // ROW_TILE),
        in_specs=[row(d), row(d), row(d)]
                 + [res(dil, ATTN_WIDTH) for dil in DILATIONS]
                 + [res(dil, LANES) for dil in DILATIONS]
                 + [once(w_pb.shape), once(w_out.shape), const(g.shape), const(w_router_t.shape)],
        out_specs=[row(d), row(d), pl.BlockSpec((None, n_e, ROW_TILE), lambda bi, t: (bi, 0, t))],
        out_shape=[jax.ShapeDtypeStruct((b, s, d), F32), jax.ShapeDtypeStruct((b, s, d), BF16),
                   jax.ShapeDtypeStruct((b, n_e, s), F32)],
        scratch_shapes=[pltpu.VMEM(w_pb.shape, BF16), pltpu.VMEM(w_out.shape, BF16)] + stage,
        compiler_params=_cparams(("arbitrary", "arbitrary"), 48),
        name="mix_out",
    )(x, ta, gb, *os_, *ls_, w_pb, w_out, g, w_router_t)


class _Plan(NamedTuple):
    tiles: int
    fast: int
    slow: int
    group: int


GATHER_PLAN = _Plan(tiles=GATHER_TOKENS // SLOT_TILE, fast=64, slow=GATHER_TOKENS + SLOT_ALIGN,
                    group=N_EXPERTS // 2)
COMBINE_PLAN = _Plan(tiles=1, fast=48, slow=SLOT_TILE + SLOT_ALIGN, group=N_EXPERTS)
FAST_WINDOW, SLOT_WINDOW = COMBINE_PLAN.fast, COMBINE_PLAN.slow


def _slot_plan(cum, plan, cap):
    rows = cum.shape[0]
    nxt = pltpu.roll(cum, LANES - plan.tiles, 1)
    lo = jnp.floor(cum * (1.0 / SLOT_ALIGN)) * SLOT_ALIGN
    ok = jnp.where(nxt - lo <= plan.fast, 1.0, 0.0)
    fit = jnp.min(ok.reshape(rows // plan.group, plan.group, LANES), axis=1)
    assert (cap - plan.fast) % SLOT_ALIGN == 0 and (cap - plan.slow) % SLOT_ALIGN == 0
    return (jnp.minimum(lo, cap - plan.fast).astype(I32),
            jnp.minimum(lo, cap - plan.slow).astype(I32), fit.astype(I32))


def _topk_kernel(aff_ref, rank_ref, *plan_refs, cap, plans):
    n_e, s = aff_ref.shape
    n_blk = s // SLOT_TILE
    aff = aff_ref[...]
    thr = jnp.zeros((n_e, 1), I32)
    for bit in range(30, -1, -1):
        cand = thr | (1 << bit)
        cnt = jnp.sum((aff >= pltpu.bitcast(cand, F32)).astype(I32), axis=1, keepdims=True)
        thr = jnp.where(cnt >= cap, cand, thr)
    above = aff >= pltpu.bitcast(thr + 1, F32)
    tie = jnp.logical_and(aff >= pltpu.bitcast(thr, F32), jnp.logical_not(above))
    need = (cap - jnp.sum(above.astype(I32), axis=1, keepdims=True)).astype(F32)
    r_i = lax.broadcasted_iota(I32, (SLOT_TILE, SLOT_TILE), 0)
    c_i = lax.broadcasted_iota(I32, (SLOT_TILE, SLOT_TILE), 1)
    tri = jnp.where(r_i < c_i, 1.0, 0.0).astype(BF16)
    lane = lax.broadcasted_iota(I32, (n_e, LANES), 1)
    run_tie = jnp.zeros((n_e, 1), F32)
    run_sel = jnp.zeros((n_e, 1), F32)
    cum = jnp.zeros((n_e, LANES), F32)
    for j in range(n_blk):
        cs = slice(j * SLOT_TILE, (j + 1) * SLOT_TILE)
        tie_f = jnp.where(tie[:, cs], 1.0, 0.0)
        tie_rank = _dot(tie_f.astype(BF16), tri) + run_tie
        run_tie = run_tie + jnp.sum(tie_f, axis=1, keepdims=True)
        sel_f = jnp.where(above[:, cs], 1.0, jnp.where(tie_rank < need, tie_f, 0.0))
        rank = _dot(sel_f.astype(BF16), tri) + run_sel
        rank_ref[:, cs] = jnp.where(sel_f > 0.0, rank, -1.0).astype(I32)
        cum = jnp.where(lane == j, run_sel, cum)
        run_sel = run_sel + jnp.sum(sel_f, axis=1, keepdims=True)
    cum = jnp.where(lane == n_blk, run_sel, cum)
    for i, plan in enumerate(plans):
        for ref, val in zip(plan_refs[3 * i:3 * i + 3], _slot_plan(cum, plan, cap)):
            ref[...] = val


def _topk(aff, cap, plans):
    b, n_e, s = aff.shape
    rows = b * n_e
    full = lambda r, w: pl.BlockSpec((r, w), lambda i: (0, 0))
    plan_specs, plan_shapes = [], []
    for plan in plans:
        for r in (rows, rows, rows // plan.group):
            plan_specs.append(full(r, LANES))
            plan_shapes.append(jax.ShapeDtypeStruct((r, LANES), I32))
    rank, *tables = pl.pallas_call(
        functools.partial(_topk_kernel, cap=cap, plans=plans),
        grid=(1,),
        in_specs=[full(rows, s)],
        out_specs=[full(rows, s)] + plan_specs,
        out_shape=[jax.ShapeDtypeStruct((rows, s), I32)] + plan_shapes,
        compiler_params=_cparams(("arbitrary",), 32),
        name="topk",
    )(aff.reshape(rows, s))
    tables = [t.reshape(-1) for t in tables]
    return rank.reshape(b, n_e, s), [tables[3 * i:3 * i + 3] for i in range(len(plans))]


def _window_start(table_ref, expert_row, tile):
    return pl.multiple_of(table_ref[expert_row * LANES + tile], SLOT_ALIGN)


def _span(jj, size):
    if isinstance(jj, int):
        return slice(jj * size, (jj + 1) * size)
    return pl.ds(pl.multiple_of(jj * size, size), size)


def _run_blocks(fits, fast, general):
    all_fit = functools.reduce(jnp.logical_and, fits)

    @pl.when(all_fit)
    def _():
        for jj in range(len(fits)):
            fast(jj)

    @pl.when(jnp.logical_not(all_fit))
    def _():
        def body(jj, carry):
            general(jj)
            return carry

        lax.fori_loop(0, len(fits), body, 0)


def _gather_kernel(fast_ref, slow_ref, fit_ref, rank_ref, aff_ref, h2_ref, xe_ref, gate_ref):
    n_all = rank_ref.shape[0]
    n_e = GATHER_PLAN.group
    rows = h2_ref.shape[0]
    bi = pl.program_id(0)
    t = pl.program_id(1)
    e0 = 0

    @pl.when(t == 0)
    def _():
        xe_ref[...] = jnp.zeros_like(xe_ref)
        gate_ref[...] = jnp.zeros_like(gate_ref)

    row_fast = lax.broadcasted_iota(I32, (GATHER_PLAN.fast, GATHER_TOKENS), 0)
    row_slow = lax.broadcasted_iota(I32, (GATHER_PLAN.slow, GATHER_TOKENS), 0)
    n_blk = rows // GATHER_TOKENS
    tile = lambda jj: (t * n_blk + jj) * GATHER_PLAN.tiles

    def window(jj, e, table_ref, row_i):
        toks = _span(jj, GATHER_TOKENS)
        lo = _window_start(table_ref, bi * n_all + e0 + e, tile(jj))
        return lo, (row_i + lo) == rank_ref[e0 + e, :, toks], aff_ref[e0 + e, :, toks]

    def add_window(e, lo, hit, aff_row, rows_e):
        win = pl.ds(lo, hit.shape[0])
        xe_ref[e0 + e, win, :] += rows_e.astype(BF16)
        gate_ref[e0 + e, win, :] += jnp.sum(jnp.where(hit, aff_row, 0.0), axis=1, keepdims=True)

    def tokens(jj):
        return h2_ref[_span(jj, GATHER_TOKENS), :]

    def fast(jj):
        ws = [window(jj, e, fast_ref, row_fast) for e in range(n_e)]
        stack = jnp.concatenate([jnp.where(h, 1.0, 0.0).astype(BF16) for _, h, _ in ws], axis=0)
        res = _dot(stack, tokens(jj))
        for e, (lo, h, aff_row) in enumerate(ws):
            add_window(e, lo, h, aff_row, res[e * GATHER_PLAN.fast:(e + 1) * GATHER_PLAN.fast])

    def general(jj):
        def body(e, carry):
            lo, h, aff_row = window(jj, e, slow_ref, row_slow)
            add_window(e, lo, h, aff_row, _dot(jnp.where(h, 1.0, 0.0).astype(BF16), tokens(jj)))
            return carry

        lax.fori_loop(0, n_e, body, 0)

    n_groups = n_all // n_e
    for g in range(n_groups):
        e0 = g * n_e
        fit_row = (bi * n_groups + g) * LANES
        _run_blocks([fit_ref[fit_row + tile(jj)] != 0 for jj in range(n_blk)], fast, general)


def _gather(plan_tables, rank, aff, h2, cap):
    b, s, d = h2.shape
    n_e = rank.shape[1]
    rows = cap
    per_tok = pl.BlockSpec((None, n_e, 1, GATHER_STEP_TOKENS), lambda bi, t, *_: (bi, 0, 0, t))
    whole = lambda w: pl.BlockSpec((None, n_e, rows, w), lambda bi, t, *_: (bi, 0, 0, 0))
    grid_spec = pltpu.PrefetchScalarGridSpec(
        num_scalar_prefetch=3,
        grid=(b, s // GATHER_STEP_TOKENS),
        in_specs=[per_tok, per_tok,
                  pl.BlockSpec((None, GATHER_STEP_TOKENS, d), lambda bi, t, *_: (bi, t, 0))],
        out_specs=[whole(d), whole(LANES)],
    )
    expert_rows = lambda a: a.reshape(b, n_e, 1, s)
    return pl.pallas_call(
        _gather_kernel,
        grid_spec=grid_spec,
        out_shape=[jax.ShapeDtypeStruct((b, n_e, rows, d), BF16),
                   jax.ShapeDtypeStruct((b, n_e, rows, LANES), F32)],
        compiler_params=_cparams(("arbitrary", "arbitrary"), 60),
        name="gather",
    )(*plan_tables, expert_rows(rank), expert_rows(aff), h2)


def _moe_ffn_kernel(xe_ref, gate_ref, wg_ref, wu_ref, wd_ref, ye_ref, wg_bf, wu_bf, wd_bf):
    n_seq, cap, d = xe_ref.shape

    @pl.when(pl.program_id(1) == 0)
    def _():
        wg_bf[...] = wg_ref[...].astype(BF16)
        wu_bf[...] = wu_ref[...].astype(BF16)
        wd_bf[...] = wd_ref[...].astype(BF16)

    xe = xe_ref[...].reshape(n_seq * cap, d)
    gate_h = _dot(xe, wg_bf[...])
    up_h = _dot(xe, wu_bf[...])
    hidden = (gate_h * _sigmoid(gate_h) * up_h).astype(BF16)
    ye = _dot(hidden, wd_bf[...]) * gate_ref[...].reshape(n_seq * cap, LANES)[:, 0:1]
    ye_ref[...] = ye.astype(BF16).reshape(n_seq, cap, d)


def _moe_ffn(xe, gate, wg, wu, wd, cap):
    b, n_e, rows, d = xe.shape
    hid = wg.shape[2]
    return pl.pallas_call(
        _moe_ffn_kernel,
        grid=(n_e, b // FFN_SEQS),
        in_specs=[
            pl.BlockSpec((FFN_SEQS, None, cap, d), lambda e, bi: (bi, e, 0, 0)),
            pl.BlockSpec((FFN_SEQS, None, cap, LANES), lambda e, bi: (bi, e, 0, 0)),
            pl.BlockSpec((None, d, hid), lambda e, bi: (e, 0, 0)),
            pl.BlockSpec((None, d, hid), lambda e, bi: (e, 0, 0)),
            pl.BlockSpec((None, hid, d), lambda e, bi: (e, 0, 0)),
        ],
        out_specs=pl.BlockSpec((FFN_SEQS, None, rows, d), lambda e, bi: (bi, e, 0, 0)),
        out_shape=jax.ShapeDtypeStruct((b, n_e, rows, d), BF16),
        scratch_shapes=[pltpu.VMEM((d, hid), BF16), pltpu.VMEM((d, hid), BF16),
                        pltpu.VMEM((hid, d), BF16)],
        compiler_params=_cparams(("arbitrary", "arbitrary"), 56),
        name="moe_ffn",
    )(xe, gate, wg, wu, wd)


def _combine_kernel(fast_ref, slow_ref, fit_ref, rank_ref, ye_ref, x1_ref, g_ref, y_ref, rhs_ref,
                    acc_ref):
    n_e = rank_ref.shape[0]
    n_blk = x1_ref.shape[0] // SLOT_TILE
    bi = pl.program_id(0)
    t = pl.program_id(1)
    row_fast = lax.broadcasted_iota(I32, (FAST_WINDOW, SLOT_TILE), 0)
    row_slow = lax.broadcasted_iota(I32, (SLOT_WINDOW, SLOT_TILE), 0)

    def hits(jj, e, table_ref, row_i):
        lo = _window_start(table_ref, bi * n_e + e, t * n_blk + jj)
        hit = (row_i + lo) == rank_ref[e, :, _span(jj, SLOT_TILE)]
        return pl.ds(lo, row_i.shape[0]), jnp.where(hit, 1.0, 0.0).astype(BF16)

    def finish(jj, moe):
        toks = _span(jj, SLOT_TILE)
        y_ref[toks, :] = _rms(x1_ref[toks, :] + moe, g_ref[...])

    def fast(jj):
        rhs = rhs_ref.at[jj % rhs_ref.shape[0]]
        stack = []
        for e in range(n_e):
            win, hit = hits(jj, e, fast_ref, row_fast)
            rhs[e * FAST_WINDOW:(e + 1) * FAST_WINDOW, :] = ye_ref[e, win, :]
            stack.append(hit)
        finish(jj, _dot_tn(jnp.concatenate(stack, axis=0), rhs[...]))

    def general(jj):
        acc_ref[...] = jnp.zeros_like(acc_ref)

        def body(e, carry):
            win, hit = hits(jj, e, slow_ref, row_slow)
            acc_ref[...] += _dot_tn(hit, ye_ref[e, win, :])
            return carry

        lax.fori_loop(0, n_e, body, 0)
        finish(jj, acc_ref[...])

    _run_blocks([fit_ref[bi * LANES + t * n_blk + jj] != 0 for jj in range(n_blk)], fast, general)


def _combine(plan_tables, rank, ye, x1, g):
    b, s, d = x1.shape
    n_e, ye_rows = ye.shape[1], ye.shape[2]
    row = pl.BlockSpec((None, COMBINE_ROWS, d), lambda bi, t, *_: (bi, t, 0))
    grid_spec = pltpu.PrefetchScalarGridSpec(
        num_scalar_prefetch=3,
        grid=(b, s // COMBINE_ROWS),
        in_specs=[
            pl.BlockSpec((None, n_e, 1, COMBINE_ROWS), lambda bi, t, *_: (bi, 0, 0, t)),
            pl.BlockSpec((None, n_e, ye_rows, d), lambda bi, t, *_: (bi, 0, 0, 0)),
            row,
            pl.BlockSpec(g.shape, lambda bi, t, *_: (0, 0)),
        ],
        out_specs=row,
        scratch_shapes=[pltpu.VMEM((2, n_e * FAST_WINDOW, d), BF16),
                        pltpu.VMEM((SLOT_TILE, d), F32)],
    )
    return pl.pallas_call(
        _combine_kernel,
        grid_spec=grid_spec,
        out_shape=jax.ShapeDtypeStruct((b, s, d), F32),
        compiler_params=_cparams(("arbitrary", "arbitrary"), 60),
        name="combine",
    )(*plan_tables, rank.reshape(b, n_e, 1, s), ye, x1, g)


def _moe_stages(aff, h2, x1, wg, wu, wd, g_final, cap):
    rank, (gather_tables, combine_tables) = _topk(aff, cap, (GATHER_PLAN, COMBINE_PLAN))
    xe, gate = _gather(gather_tables, rank, aff, h2, cap)
    ye = _moe_ffn(xe, gate, wg, wu, wd, cap)
    return _combine(combine_tables, rank, ye, x1, g_final)


def kernel(x, norm_mix_g, w_in, b_gate, gmlp_norm_g, w_spatial, b_spatial, w_proj_a, w_proj_b,
           w_out, norm_ffn_g, w_router, w_e_gate, w_e_up, w_e_down, norm_final_g):
    b, s, d = x.shape
    assert w_in.shape[0] == 1, "single-layer block"
    cap = CAPACITY_FACTOR * s // N_EXPERTS
    group_width = GMLP_WIDTH // GMLP_GROUPS
    ws_pairs = w_spatial[0].astype(BF16).reshape(GMLP_GROUPS // 2, 2 * CHUNK, CHUNK)
    bsp = jnp.repeat(b_spatial[0].T, group_width, axis=1)
    qkv, ta, gb = _mix_in(x, norm_mix_g, w_in[0], b_gate, gmlp_norm_g, ws_pairs, bsp,
                          w_proj_a[0].astype(BF16))
    os_, ls_ = [], []
    for (q, k, v), dil in zip(qkv, DILATIONS):
        o, lse = _attn_pattern(q, k, v, dil)
        os_.append(o)
        ls_.append(lse)
    x1, h2, aff = _mix_out(x, ta, gb, os_, ls_, w_proj_b[0], w_out[0], norm_ffn_g, w_router[0].T)
    return _moe_stages(aff, h2, x1, w_e_gate[0], w_e_up[0], w_e_down[0], norm_final_g[None], cap)
```

```python
import functools

from typing import NamedTuple

import jax
import jax.numpy as jnp
from jax import lax
from jax.experimental import pallas as pl
from jax.experimental.pallas import tpu as pltpu

F32 = jnp.float32
BF16 = jnp.bfloat16
I32 = jnp.int32

EPS = 1e-6
GMLP_WIDTH = 512
GMLP_GROUPS = 8
CHUNK = 128
N_HEADS = 8
HEAD_DIM = 64
ATTN_WIDTH = N_HEADS * HEAD_DIM
DILATIONS = (1, 4, 16)
DILATION_STEP = 4
assert all(b == a * DILATION_STEP for a, b in zip(DILATIONS, DILATIONS[1:]))
HALF_WINDOW = 64
N_EXPERTS = 16
CAPACITY_FACTOR = 2

LANES = 128
Q_TILE = 128
KEY_TILE = 2 * Q_TILE
ATTN_STEP_ROWS = 2048
STAT_LANES = LANES // N_HEADS
DEN_SHIFT = STAT_LANES // 2
SLOT_TILE = 128
SLOT_ALIGN = 16
COMBINE_ROWS = 1024
X1_RING_SLOTS = 3
FFN_SEQS = 2
GATHER_STEP_TOKENS = 2048
GATHER_TOKENS = 256
ROW_TILE = 512
MIB = 1024 * 1024


def _cparams(sem, vmem_mib):
    return pltpu.CompilerParams(dimension_semantics=sem, vmem_limit_bytes=vmem_mib * MIB)


def _gelu_tanh(x):
    return 0.5 * x * (1.0 + jnp.tanh(0.7978845608028654 * (x + 0.044715 * (x * x * x))))


def _sigmoid(x):
    return 1.0 / (1.0 + jnp.exp(-x))


def _rms(x, g):
    return x * lax.rsqrt(jnp.mean(x * x, axis=-1, keepdims=True) + EPS) * g


def _dot(a, b):
    return jnp.dot(a, b, preferred_element_type=F32)


def _dot_nt(a, b):
    return lax.dot_general(a, b, (((1,), (1,)), ((), ())), preferred_element_type=F32)


def _dot_tn(a, b):
    return lax.dot_general(a, b, (((0,), (0,)), ((), ())), preferred_element_type=F32)


def _mix_in_kernel(x_ref, g_ref, win_ref, bg_ref, g2_ref, ws_ref, bsp_ref, pa_ref, *refs):
    n_qkv = 3 * len(DILATIONS)
    qkv_refs = refs[:n_qkv]
    ta_ref, gb_ref = refs[n_qkv:n_qkv + 2]
    win_bf = refs[n_qkv + 2]
    stage_refs = refs[n_qkv + 3:]
    stage_refs = list(zip(stage_refs[0::2], stage_refs[1::2]))
    rows, d_model = x_ref.shape

    @pl.when(jnp.logical_and(pl.program_id(0) == 0, pl.program_id(1) == 0))
    def _():
        for c in range(0, win_ref.shape[1], GMLP_WIDTH):
            win_bf[:, c:c + GMLP_WIDTH] = win_ref[:, c:c + GMLP_WIDTH].astype(BF16)

    h = _rms(x_ref[...], g_ref[...]).astype(BF16)

    def proj(lo, width):
        return _dot(h, win_bf[:, lo:lo + width])

    c0 = 0
    u = _gelu_tanh(proj(c0, GMLP_WIDTH)); c0 += GMLP_WIDTH
    v = _gelu_tanh(proj(c0, GMLP_WIDTH)); c0 += GMLP_WIDTH
    for i in range(3):
        val = proj(c0, ATTN_WIDTH); c0 += ATTN_WIDTH
        if i == 0:
            val = val * (HEAD_DIM ** -0.5)
        qkv_refs[i][0] = val.astype(BF16)
        stage1, stage2 = stage_refs[i]
        n4, n16 = rows // DILATIONS[1], rows // DILATIONS[2]
        out4, out16 = qkv_refs[3 + i], qkv_refs[6 + i]
        for p in range(ATTN_WIDTH // LANES):
            cs = slice(p * LANES, (p + 1) * LANES)
            stage1[p] = val[:, cs]
            for r4 in range(DILATION_STEP):
                part = stage1[p, pl.ds(r4, n4, stride=DILATION_STEP), :]
                out4[r4, :, cs] = part.astype(BF16)
                stage2[p, r4] = part
                for c in range(DILATION_STEP):
                    out16[r4 + DILATION_STEP * c, :, cs] = (
                        stage2[p, r4, pl.ds(c, n16, stride=DILATION_STEP), :].astype(BF16))
    ga = _sigmoid(proj(c0, d_model) + bg_ref[:, :d_model]); c0 += d_model
    gb = _sigmoid(proj(c0, d_model) + bg_ref[:, d_model:])
    gb_ref[...] = gb.astype(BF16)

    vn = _rms(v, g2_ref[...]).astype(BF16)
    lane_lo = lax.broadcasted_iota(I32, (CHUNK, LANES), 1) < HEAD_DIM
    bsp = bsp_ref[...]
    n_chunk = rows // CHUNK
    mixed_slabs = []
    for p in range(GMLP_WIDTH // LANES):
        slab = jnp.concatenate(
            [vn[c * CHUNK:(c + 1) * CHUNK, p * LANES:(p + 1) * LANES] for c in range(n_chunk)],
            axis=1)
        r = _dot(ws_ref[p], slab)
        mixed_slabs.append([jnp.where(lane_lo, r[:CHUNK, c * LANES:(c + 1) * LANES],
                                      r[CHUNK:, c * LANES:(c + 1) * LANES])
                            for c in range(n_chunk)])
    a_chunks = []
    for c in range(n_chunk):
        rs = slice(c * CHUNK, (c + 1) * CHUNK)
        mixed = jnp.concatenate([slabs[c] for slabs in mixed_slabs], axis=1) + bsp
        a_chunks.append((u[rs] * mixed).astype(BF16))
    a = jnp.concatenate(a_chunks, axis=0)
    ta_ref[...] = (ga * _dot(a, pa_ref[...])).astype(BF16)


def _mix_in(x, g, w_in, b_gate, g2, ws_pairs, bsp, w_pa):
    b, s, d = x.shape
    const = lambda shape: pl.BlockSpec(shape, lambda bi, t: (0,) * len(shape))
    row = lambda w: pl.BlockSpec((None, ROW_TILE, w), lambda bi, t: (bi, t, 0))
    qkv_specs, qkv_shapes = [], []
    for dil in DILATIONS:
        spec = pl.BlockSpec((None, dil, ROW_TILE // dil, ATTN_WIDTH), lambda bi, t: (bi, 0, t, 0))
        qkv_specs += [spec] * 3
        qkv_shapes += [jax.ShapeDtypeStruct((b, dil, s // dil, ATTN_WIDTH), BF16)] * 3
    outs = pl.pallas_call(
        _mix_in_kernel,
        grid=(b, s // ROW_TILE),
        in_specs=[row(d), const(g.shape),
                  pl.BlockSpec(w_in.shape, lambda bi, t: (0, 0), pipeline_mode=pl.Buffered(1)),
                  const(b_gate.shape), const(g2.shape),
                  const(ws_pairs.shape), const(bsp.shape), const(w_pa.shape)],
        out_specs=qkv_specs + [row(d), row(d)],
        out_shape=qkv_shapes + [jax.ShapeDtypeStruct((b, s, d), BF16)] * 2,
        scratch_shapes=[pltpu.VMEM(w_in.shape, BF16)]
                       + [pltpu.VMEM((ATTN_WIDTH // LANES, ROW_TILE, LANES), F32),
                          pltpu.VMEM((ATTN_WIDTH // LANES, DILATION_STEP,
                                      ROW_TILE // DILATION_STEP, LANES), F32)] * 3,
        compiler_params=_cparams(("arbitrary", "arbitrary"), 60),
        name="mix_in",
    )(x, g, w_in, b_gate, g2, ws_pairs, bsp, w_pa)
    n_qkv = 3 * len(DILATIONS)
    qkv = [outs[3 * i:3 * i + 3] for i in range(len(DILATIONS))]
    return qkv, outs[n_qkv], outs[n_qkv + 1]


def _attn_kernel(q_ref, k_ref, v_ref, o_ref, l_ref, bias_ref, *, dil):
    n_res, rows, _ = q_ref.shape
    seq = k_ref.shape[1]
    t = pl.program_id(2)
    first = jnp.logical_and(jnp.logical_and(pl.program_id(0) == 0, pl.program_id(1) == 0), t == 0)

    @pl.when(first)
    def _():
        ii = lax.broadcasted_iota(I32, (Q_TILE, KEY_TILE), 0)
        jj = lax.broadcasted_iota(I32, (Q_TILE, KEY_TILE), 1)
        for var in range(3):
            absd = jnp.abs(jj - ii - var * HALF_WINDOW)
            valid = absd <= HALF_WINDOW
            absf = absd.astype(F32)
            for h in range(N_HEADS):
                slope = 2.0 ** (-8.0 * (h + 1) / N_HEADS)
                bias_ref[var, h] = jnp.where(valid, -(slope * dil) * absf, -jnp.inf)

    lane = lax.broadcasted_iota(I32, (Q_TILE, LANES), 1)
    lane_lo = lane < HEAD_DIM
    mask_lo = jnp.where(lane_lo, 1.0, 0.0).astype(BF16)
    mask_hi = jnp.where(lane_lo, 0.0, 1.0).astype(BF16)
    for rr in range(n_res):
        for qi in range(rows // Q_TILE):
            rs = slice(qi * Q_TILE, (qi + 1) * Q_TILE)
            i0 = t * rows + qi * Q_TILE
            start = pl.multiple_of(jnp.clip(i0 - HALF_WINDOW, 0, seq - KEY_TILE), HALF_WINDOW)
            var = (i0 - start) // HALF_WINDOW
            for p in range(ATTN_WIDTH // LANES):
                cs = slice(p * LANES, (p + 1) * LANES)
                qp = q_ref[rr, rs, cs]
                kp = k_ref[rr, pl.ds(start, KEY_TILE), cs]
                vp = v_ref[rr, pl.ds(start, KEY_TILE), cs]
                q2 = jnp.concatenate([qp * mask_lo, qp * mask_hi], axis=0)
                s2 = _dot_nt(q2, kp)
                probs = []
                for hh in range(2):
                    h = 2 * p + hh
                    s = s2[hh * Q_TILE:(hh + 1) * Q_TILE] + bias_ref[var, h]
                    m = jnp.max(s, axis=-1, keepdims=True)
                    e = jnp.exp(s - m)
                    den = jnp.sum(e, axis=-1, keepdims=True)
                    probs.append(e)
                    lo = h * STAT_LANES
                    l_ref[rr, rs, lo:lo + DEN_SHIFT] = jnp.broadcast_to(m, (Q_TILE, DEN_SHIFT))
                    l_ref[rr, rs, lo + DEN_SHIFT:lo + STAT_LANES] = jnp.broadcast_to(
                        den, (Q_TILE, DEN_SHIFT))
                o2 = _dot(jnp.concatenate(probs, axis=0).astype(BF16), vp)
                o_ref[rr, rs, cs] = jnp.where(lane_lo, o2[:Q_TILE], o2[Q_TILE:]).astype(BF16)


def _attn_pattern(q, k, v, dil):
    b, _, seq, w = q.shape
    rows = min(seq, ATTN_STEP_ROWS)
    n_res = ATTN_STEP_ROWS // rows
    qspec = lambda width: pl.BlockSpec((None, n_res, rows, width), lambda bi, r, t: (bi, r, t, 0))
    kspec = pl.BlockSpec((None, n_res, seq, w), lambda bi, r, t: (bi, r, 0, 0))
    return pl.pallas_call(
        functools.partial(_attn_kernel, dil=dil),
        grid=(b, dil // n_res, seq // rows),
        in_specs=[qspec(w), kspec, kspec],
        out_specs=[qspec(w), qspec(LANES)],
        out_shape=[jax.ShapeDtypeStruct(q.shape, BF16),
                   jax.ShapeDtypeStruct((b, dil, seq, LANES), F32)],
        scratch_shapes=[pltpu.VMEM((3, N_HEADS, Q_TILE, KEY_TILE), F32)],
        compiler_params=_cparams(("arbitrary", "arbitrary", "arbitrary"), 48),
        name=f"attn_d{dil}",
    )(q, k, v)


def _to_natural(src_ref, nat_ref, tmp_ref, rows):
    n_slab = nat_ref.shape[0]
    step = DILATION_STEP
    for p in range(n_slab):
        cs = slice(p * LANES, (p + 1) * LANES)
        for r4 in range(step):
            if tmp_ref is None:
                quarter = src_ref[r4][:, cs].astype(F32)
            else:
                for c in range(step):
                    tmp_ref[p, r4, pl.ds(c, rows // (step * step), stride=step), :] = (
                        src_ref[r4 + step * c][:, cs].astype(F32))
                quarter = tmp_ref[p, r4]
            nat_ref[p, pl.ds(r4, rows // step, stride=step), :] = quarter
    return jnp.concatenate([nat_ref[p] for p in range(n_slab)], axis=1)


def _mix_out_kernel(x_ref, ta_ref, gb_ref, *refs):
    n_pat = len(DILATIONS)
    o_refs = refs[:n_pat]
    l_refs = refs[n_pat:2 * n_pat]
    pb_f32, wo_f32, g_ref, wr_ref, x1_ref, h2_ref, aff_ref = refs[2 * n_pat:2 * n_pat + 7]
    pb_ref, wo_ref = refs[2 * n_pat + 7:2 * n_pat + 9]
    stage_refs = refs[2 * n_pat + 9:]
    rows = x_ref.shape[0]

    @pl.when(jnp.logical_and(pl.program_id(0) == 0, pl.program_id(1) == 0))
    def _():
        pb_ref[...] = pb_f32[...].astype(BF16)
        wo_ref[...] = wo_f32[...].astype(BF16)

    stage_refs = list(stage_refs)
    outs = [o_refs[0][0].astype(F32)]
    lses = [l_refs[0][0]]
    for di in (1, 2):
        for src, dest in ((o_refs[di], outs), (l_refs[di], lses)):
            nat = stage_refs.pop(0)
            tmp = stage_refs.pop(0) if di == 2 else None
            dest.append(_to_natural(src, nat, tmp, rows))

    dens = [pltpu.roll(st, LANES - DEN_SHIFT, 1) for st in lses]
    lses = [st + jnp.log(den) for st, den in zip(lses, dens)]
    m = functools.reduce(jnp.maximum, lses)
    ws = [jnp.exp(l - m) for l in lses]
    inv = 1.0 / functools.reduce(lambda a, c: a + c, ws)
    lane = lax.broadcasted_iota(I32, (rows, LANES), 1)
    used = lane % STAT_LANES < DEN_SHIFT
    ws = [jnp.where(used, w * inv / den, 0.0) for w, den in zip(ws, dens)]
    k_i = lax.broadcasted_iota(I32, (2 * LANES, ATTN_WIDTH), 0) % LANES
    c_i = lax.broadcasted_iota(I32, (2 * LANES, ATTN_WIDTH), 1)
    spread = jnp.where(k_i == (c_i // HEAD_DIM) * STAT_LANES, 1.0, 0.0).astype(BF16)
    parts = []
    for w in ws:
        w_hi = w.astype(BF16)
        w_lo = (w - w_hi.astype(F32)).astype(BF16)
        parts.append(jnp.concatenate([w_hi, w_lo], axis=1))
    factors = _dot(jnp.concatenate(parts, axis=0), spread)
    o = None
    for i, o_p in enumerate(outs):
        term = factors[i * rows:(i + 1) * rows] * o_p
        o = term if o is None else o + term

    ob = _dot(o.astype(BF16), pb_ref[...])
    merged = (ta_ref[...].astype(F32) + gb_ref[...].astype(F32) * ob).astype(BF16)
    x1 = x_ref[...] + _dot(merged, wo_ref[...])
    x1_ref[...] = x1
    h2 = _rms(x1, g_ref[...])
    h2_ref[...] = h2.astype(BF16)
    h_hi = h2.astype(BF16)
    h_lo = (h2 - h_hi.astype(F32)).astype(BF16)
    wr = wr_ref[...]
    w_hi = wr.astype(BF16)
    w_lo = (wr - w_hi.astype(F32)).astype(BF16)
    n_e = wr.shape[0]
    by_hi = _dot_nt(jnp.concatenate([w_hi, w_lo], axis=0), h_hi)
    logits = by_hi[:n_e] + (_dot_nt(w_hi, h_lo) + by_hi[n_e:])
    e = jnp.exp(logits - jnp.max(logits, axis=0, keepdims=True))
    aff_ref[...] = e / jnp.sum(e, axis=0, keepdims=True)


def _mix_out(x, ta, gb, os_, ls_, w_pb, w_out, g, w_router_t):
    b, s, d = x.shape
    n_e = w_router_t.shape[0]
    const = lambda shape: pl.BlockSpec(shape, lambda bi, t: (0,) * len(shape))
    once = lambda shape: pl.BlockSpec(shape, lambda bi, t: (0,) * len(shape),
                                      pipeline_mode=pl.Buffered(1))
    row = lambda w: pl.BlockSpec((None, ROW_TILE, w), lambda bi, t: (bi, t, 0))
    res = lambda dil, w: pl.BlockSpec((None, dil, ROW_TILE // dil, w), lambda bi, t: (bi, 0, t, 0))
    stage = []
    for di in (1, 2):
        for slabs in (ATTN_WIDTH // LANES, 1):
            stage.append(pltpu.VMEM((slabs, ROW_TILE, LANES), F32))
            if di == 2:
                stage.append(pltpu.VMEM((slabs, DILATION_STEP, ROW_TILE // DILATION_STEP, LANES),
                                        F32))
    return pl.pallas_call(
        _mix_out_kernel,
        grid=(b, s // ROW_TILE),
        in_specs=[row(d), row(d), row(d)]
                 + [res(dil, ATTN_WIDTH) for dil in DILATIONS]
                 + [res(dil, LANES) for dil in DILATIONS]
                 + [once(w_pb.shape), once(w_out.shape), const(g.shape), const(w_router_t.shape)],
        out_specs=[row(d), row(d), pl.BlockSpec((None, n_e, ROW_TILE), lambda bi, t: (bi, 0, t))],
        out_shape=[jax.ShapeDtypeStruct((b, s, d), F32), jax.ShapeDtypeStruct((b, s, d), BF16),
                   jax.ShapeDtypeStruct((b, n_e, s), F32)],
        scratch_shapes=[pltpu.VMEM(w_pb.shape, BF16), pltpu.VMEM(w_out.shape, BF16)] + stage,
        compiler_params=_cparams(("arbitrary", "arbitrary"), 48),
        name="mix_out",
    )(x, ta, gb, *os_, *ls_, w_pb, w_out, g, w_router_t)


class _Plan(NamedTuple):
    tiles: int
    fast: int
    slow: int
    group: int


GATHER_PLAN = _Plan(tiles=GATHER_TOKENS // SLOT_TILE, fast=64, slow=GATHER_TOKENS + SLOT_ALIGN,
                    group=N_EXPERTS // 2)
COMBINE_PLAN = _Plan(tiles=1, fast=48, slow=SLOT_TILE + SLOT_ALIGN, group=N_EXPERTS)
FAST_WINDOW, SLOT_WINDOW = COMBINE_PLAN.fast, COMBINE_PLAN.slow


def _slot_plan(cum, plan, cap):
    rows = cum.shape[0]
    nxt = pltpu.roll(cum, LANES - plan.tiles, 1)
    lo = jnp.floor(cum * (1.0 / SLOT_ALIGN)) * SLOT_ALIGN
    ok = jnp.where(nxt - lo <= plan.fast, 1.0, 0.0)
    fit = jnp.min(ok.reshape(rows // plan.group, plan.group, LANES), axis=1)
    assert (cap - plan.fast) % SLOT_ALIGN == 0 and (cap - plan.slow) % SLOT_ALIGN == 0
    return (jnp.minimum(lo, cap - plan.fast).astype(I32),
            jnp.minimum(lo, cap - plan.slow).astype(I32), fit.astype(I32))


def _topk_kernel(aff_ref, rank_ref, *plan_refs, cap, plans):
    n_e, s = aff_ref.shape
    n_blk = s // SLOT_TILE
    aff = aff_ref[...]
    thr = jnp.zeros((n_e, 1), I32)
    for bit in range(30, -1, -1):
        cand = thr | (1 << bit)
        cnt = jnp.sum((aff >= pltpu.bitcast(cand, F32)).astype(I32), axis=1, keepdims=True)
        thr = jnp.where(cnt >= cap, cand, thr)
    above = aff >= pltpu.bitcast(thr + 1, F32)
    tie = jnp.logical_and(aff >= pltpu.bitcast(thr, F32), jnp.logical_not(above))
    need = (cap - jnp.sum(above.astype(I32), axis=1, keepdims=True)).astype(F32)
    r_i = lax.broadcasted_iota(I32, (SLOT_TILE, SLOT_TILE), 0)
    c_i = lax.broadcasted_iota(I32, (SLOT_TILE, SLOT_TILE), 1)
    tri = jnp.where(r_i < c_i, 1.0, 0.0).astype(BF16)
    lane = lax.broadcasted_iota(I32, (n_e, LANES), 1)
    run_tie = jnp.zeros((n_e, 1), F32)
    run_sel = jnp.zeros((n_e, 1), F32)
    cum = jnp.zeros((n_e, LANES), F32)
    for j in range(n_blk):
        cs = slice(j * SLOT_TILE, (j + 1) * SLOT_TILE)
        tie_f = jnp.where(tie[:, cs], 1.0, 0.0)
        tie_rank = _dot(tie_f.astype(BF16), tri) + run_tie
        run_tie = run_tie + jnp.sum(tie_f, axis=1, keepdims=True)
        sel_f = jnp.where(above[:, cs], 1.0, jnp.where(tie_rank < need, tie_f, 0.0))
        rank = _dot(sel_f.astype(BF16), tri) + run_sel
        rank_ref[:, cs] = jnp.where(sel_f > 0.0, rank, -1.0).astype(I32)
        cum = jnp.where(lane == j, run_sel, cum)
        run_sel = run_sel + jnp.sum(sel_f, axis=1, keepdims=True)
    cum = jnp.where(lane == n_blk, run_sel, cum)
    for i, plan in enumerate(plans):
        for ref, val in zip(plan_refs[3 * i:3 * i + 3], _slot_plan(cum, plan, cap)):
            ref[...] = val


def _topk(aff, cap, plans):
    b, n_e, s = aff.shape
    rows = b * n_e
    full = lambda r, w: pl.BlockSpec((r, w), lambda i: (0, 0))
    plan_specs, plan_shapes = [], []
    for plan in plans:
        for r in (rows, rows, rows // plan.group):
            plan_specs.append(full(r, LANES))
            plan_shapes.append(jax.ShapeDtypeStruct((r, LANES), I32))
    rank, *tables = pl.pallas_call(
        functools.partial(_topk_kernel, cap=cap, plans=plans),
        grid=(1,),
        in_specs=[full(rows, s)],
        out_specs=[full(rows, s)] + plan_specs,
        out_shape=[jax.ShapeDtypeStruct((rows, s), I32)] + plan_shapes,
        compiler_params=_cparams(("arbitrary",), 32),
        name="topk",
    )(aff.reshape(rows, s))
    tables = [t.reshape(-1) for t in tables]
    return rank.reshape(b, n_e, s), [tables[3 * i:3 * i + 3] for i in range(len(plans))]


def _window_start(table_ref, expert_row, tile):
    return pl.multiple_of(table_ref[expert_row * LANES + tile], SLOT_ALIGN)


def _span(jj, size):
    if isinstance(jj, int):
        return slice(jj * size, (jj + 1) * size)
    return pl.ds(pl.multiple_of(jj * size, size), size)


def _run_blocks(fits, fast, general):
    all_fit = functools.reduce(jnp.logical_and, fits)

    @pl.when(all_fit)
    def _():
        for jj in range(len(fits)):
            fast(jj)

    @pl.when(jnp.logical_not(all_fit))
    def _():
        def body(jj, carry):
            general(jj)
            return carry

        lax.fori_loop(0, len(fits), body, 0)


def _gather_kernel(fast_ref, slow_ref, fit_ref, rank_ref, aff_ref, h2_ref, xe_ref, gate_ref):
    n_all = rank_ref.shape[0]
    n_e = GATHER_PLAN.group
    rows = h2_ref.shape[0]
    bi = pl.program_id(0)
    t = pl.program_id(1)
    e0 = 0

    @pl.when(t == 0)
    def _():
        xe_ref[...] = jnp.zeros_like(xe_ref)
        gate_ref[...] = jnp.zeros_like(gate_ref)

    row_fast = lax.broadcasted_iota(I32, (GATHER_PLAN.fast, GATHER_TOKENS), 0)
    row_slow = lax.broadcasted_iota(I32, (GATHER_PLAN.slow, GATHER_TOKENS), 0)
    n_blk = rows // GATHER_TOKENS
    tile = lambda jj: (t * n_blk + jj) * GATHER_PLAN.tiles

    def window(jj, e, table_ref, row_i):
        toks = _span(jj, GATHER_TOKENS)
        lo = _window_start(table_ref, bi * n_all + e0 + e, tile(jj))
        return lo, (row_i + lo) == rank_ref[e0 + e, :, toks], aff_ref[e0 + e, :, toks]

    def add_window(e, lo, hit, aff_row, rows_e):
        win = pl.ds(lo, hit.shape[0])
        xe_ref[e0 + e, win, :] += rows_e.astype(BF16)
        gate_ref[e0 + e, win, :] += jnp.sum(jnp.where(hit, aff_row, 0.0), axis=1, keepdims=True)

    def tokens(jj):
        return h2_ref[_span(jj, GATHER_TOKENS), :]

    def fast(jj):
        ws = [window(jj, e, fast_ref, row_fast) for e in range(n_e)]
        stack = jnp.concatenate([jnp.where(h, 1.0, 0.0).astype(BF16) for _, h, _ in ws], axis=0)
        res = _dot(stack, tokens(jj))
        for e, (lo, h, aff_row) in enumerate(ws):
            add_window(e, lo, h, aff_row, res[e * GATHER_PLAN.fast:(e + 1) * GATHER_PLAN.fast])

    def general(jj):
        def body(e, carry):
            lo, h, aff_row = window(jj, e, slow_ref, row_slow)
            add_window(e, lo, h, aff_row, _dot(jnp.where(h, 1.0, 0.0).astype(BF16), tokens(jj)))
            return carry

        lax.fori_loop(0, n_e, body, 0)

    n_groups = n_all // n_e
    for g in range(n_groups):
        e0 = g * n_e
        fit_row = (bi * n_groups + g) * LANES
        _run_blocks([fit_ref[fit_row + tile(jj)] != 0 for jj in range(n_blk)], fast, general)


def _gather(plan_tables, rank, aff, h2, cap):
    b, s, d = h2.shape
    n_e = rank.shape[1]
    rows = cap
    per_tok = pl.BlockSpec((None, n_e, 1, GATHER_STEP_TOKENS), lambda bi, t, *_: (bi, 0, 0, t))
    whole = lambda w: pl.BlockSpec((None, n_e, rows, w), lambda bi, t, *_: (bi, 0, 0, 0))
    grid_spec = pltpu.PrefetchScalarGridSpec(
        num_scalar_prefetch=3,
        grid=(b, s // GATHER_STEP_TOKENS),
        in_specs=[per_tok, per_tok,
                  pl.BlockSpec((None, GATHER_STEP_TOKENS, d), lambda bi, t, *_: (bi, t, 0))],
        out_specs=[whole(d), whole(LANES)],
    )
    expert_rows = lambda a: a.reshape(b, n_e, 1, s)
    return pl.pallas_call(
        _gather_kernel,
        grid_spec=grid_spec,
        out_shape=[jax.ShapeDtypeStruct((b, n_e, rows, d), BF16),
                   jax.ShapeDtypeStruct((b, n_e, rows, LANES), F32)],
        compiler_params=_cparams(("arbitrary", "arbitrary"), 60),
        name="gather",
    )(*plan_tables, expert_rows(rank), expert_rows(aff), h2)


def _moe_ffn_kernel(xe_ref, gate_ref, wg_ref, wu_ref, wd_ref, ye_ref, wg_bf, wu_bf, wd_bf):
    n_seq, cap, d = xe_ref.shape

    @pl.when(pl.program_id(1) == 0)
    def _():
        wg_bf[...] = wg_ref[...].astype(BF16)
        wu_bf[...] = wu_ref[...].astype(BF16)
        wd_bf[...] = wd_ref[...].astype(BF16)

    xe = xe_ref[...].reshape(n_seq * cap, d)
    gate_h = _dot(xe, wg_bf[...])
    up_h = _dot(xe, wu_bf[...])
    hidden = (gate_h * _sigmoid(gate_h) * up_h).astype(BF16)
    ye = _dot(hidden, wd_bf[...]) * gate_ref[...].reshape(n_seq * cap, LANES)[:, 0:1]
    ye_ref[...] = ye.astype(BF16).reshape(n_seq, cap, d)


def _moe_ffn(xe, gate, wg, wu, wd, cap):
    b, n_e, rows, d = xe.shape
    hid = wg.shape[2]
    return pl.pallas_call(
        _moe_ffn_kernel,
        grid=(n_e, b // FFN_SEQS),
        in_specs=[
            pl.BlockSpec((FFN_SEQS, None, cap, d), lambda e, bi: (bi, e, 0, 0)),
            pl.BlockSpec((FFN_SEQS, None, cap, LANES), lambda e, bi: (bi, e, 0, 0)),
            pl.BlockSpec((None, d, hid), lambda e, bi: (e, 0, 0)),
            pl.BlockSpec((None, d, hid), lambda e, bi: (e, 0, 0)),
            pl.BlockSpec((None, hid, d), lambda e, bi: (e, 0, 0)),
        ],
        out_specs=pl.BlockSpec((FFN_SEQS, None, rows, d), lambda e, bi: (bi, e, 0, 0)),
        out_shape=jax.ShapeDtypeStruct((b, n_e, rows, d), BF16),
        scratch_shapes=[pltpu.VMEM((d, hid), BF16), pltpu.VMEM((d, hid), BF16),
                        pltpu.VMEM((hid, d), BF16)],
        compiler_params=_cparams(("arbitrary", "arbitrary"), 56),
        name="moe_ffn",
    )(xe, gate, wg, wu, wd)


def _combine_kernel(fast_ref, slow_ref, fit_ref, rank_ref, ye_ref, x1_hbm, g_ref, y_ref, rhs_ref,
                    acc_ref, x1_ring, ring_sem, *, steps_per_seq, n_steps):
    n_e = rank_ref.shape[0]
    rows = y_ref.shape[0]
    n_blk = rows // SLOT_TILE
    bi = pl.program_id(0)
    t = pl.program_id(1)
    step = bi * steps_per_seq + t

    def x1_copy(s):
        tile = x1_hbm.at[s // steps_per_seq, pl.ds((s % steps_per_seq) * rows, rows), :]
        slot = s % X1_RING_SLOTS
        return pltpu.make_async_copy(tile, x1_ring.at[slot], ring_sem.at[slot])

    @pl.when(step == 0)
    def _():
        for s in range(min(X1_RING_SLOTS - 1, n_steps)):
            x1_copy(s).start()

    @pl.when(step + (X1_RING_SLOTS - 1) < n_steps)
    def _():
        x1_copy(step + (X1_RING_SLOTS - 1)).start()

    x1_copy(step).wait()
    x1_ref = x1_ring.at[step % X1_RING_SLOTS]
    row_fast = lax.broadcasted_iota(I32, (FAST_WINDOW, SLOT_TILE), 0)
    row_slow = lax.broadcasted_iota(I32, (SLOT_WINDOW, SLOT_TILE), 0)

    def hits(jj, e, table_ref, row_i):
        lo = _window_start(table_ref, bi * n_e + e, t * n_blk + jj)
        hit = (row_i + lo) == rank_ref[e, :, _span(jj, SLOT_TILE)]
        return pl.ds(lo, row_i.shape[0]), jnp.where(hit, 1.0, 0.0).astype(BF16)

    def finish(jj, moe):
        toks = _span(jj, SLOT_TILE)
        y_ref[toks, :] = _rms(x1_ref[toks, :] + moe, g_ref[...])

    def fast(jj):
        rhs = rhs_ref.at[jj % rhs_ref.shape[0]]
        stack = []
        for e in range(n_e):
            win, hit = hits(jj, e, fast_ref, row_fast)
            rhs[e * FAST_WINDOW:(e + 1) * FAST_WINDOW, :] = ye_ref[e, win, :]
            stack.append(hit)
        finish(jj, _dot_tn(jnp.concatenate(stack, axis=0), rhs[...]))

    def general(jj):
        acc_ref[...] = jnp.zeros_like(acc_ref)

        def body(e, carry):
            win, hit = hits(jj, e, slow_ref, row_slow)
            acc_ref[...] += _dot_tn(hit, ye_ref[e, win, :])
            return carry

        lax.fori_loop(0, n_e, body, 0)
        finish(jj, acc_ref[...])

    _run_blocks([fit_ref[bi * LANES + t * n_blk + jj] != 0 for jj in range(n_blk)], fast, general)


def _combine(plan_tables, rank, ye, x1, g):
    b, s, d = x1.shape
    n_e, ye_rows = ye.shape[1], ye.shape[2]
    row = pl.BlockSpec((None, COMBINE_ROWS, d), lambda bi, t, *_: (bi, t, 0))
    grid_spec = pltpu.PrefetchScalarGridSpec(
        num_scalar_prefetch=3,
        grid=(b, s // COMBINE_ROWS),
        in_specs=[
            pl.BlockSpec((None, n_e, 1, COMBINE_ROWS), lambda bi, t, *_: (bi, 0, 0, t)),
            pl.BlockSpec((None, n_e, ye_rows, d), lambda bi, t, *_: (bi, 0, 0, 0)),
            pl.BlockSpec(memory_space=pl.ANY),
            pl.BlockSpec(g.shape, lambda bi, t, *_: (0, 0)),
        ],
        out_specs=row,
        scratch_shapes=[pltpu.VMEM((2, n_e * FAST_WINDOW, d), BF16),
                        pltpu.VMEM((SLOT_TILE, d), F32),
                        pltpu.VMEM((X1_RING_SLOTS, COMBINE_ROWS, d), F32),
                        pltpu.SemaphoreType.DMA((X1_RING_SLOTS,))],
    )
    steps_per_seq = s // COMBINE_ROWS
    return pl.pallas_call(
        functools.partial(_combine_kernel, steps_per_seq=steps_per_seq,
                          n_steps=b * steps_per_seq),
        grid_spec=grid_spec,
        out_shape=jax.ShapeDtypeStruct((b, s, d), F32),
        compiler_params=_cparams(("arbitrary", "arbitrary"), 60),
        name="combine",
    )(*plan_tables, rank.reshape(b, n_e, 1, s), ye, x1, g)


def _moe_stages(aff, h2, x1, wg, wu, wd, g_final, cap):
    rank, (gather_tables, combine_tables) = _topk(aff, cap, (GATHER_PLAN, COMBINE_PLAN))
    xe, gate = _gather(gather_tables, rank, aff, h2, cap)
    ye = _moe_ffn(xe, gate, wg, wu, wd, cap)
    return _combine(combine_tables, rank, ye, x1, g_final)


def kernel(x, norm_mix_g, w_in, b_gate, gmlp_norm_g, w_spatial, b_spatial, w_proj_a, w_proj_b,
           w_out, norm_ffn_g, w_router, w_e_gate, w_e_up, w_e_down, norm_final_g):
    b, s, d = x.shape
    assert w_in.shape[0] == 1, "single-layer block"
    cap = CAPACITY_FACTOR * s // N_EXPERTS
    group_width = GMLP_WIDTH // GMLP_GROUPS
    ws_pairs = w_spatial[0].astype(BF16).reshape(GMLP_GROUPS // 2, 2 * CHUNK, CHUNK)
    bsp = jnp.repeat(b_spatial[0].T, group_width, axis=1)
    qkv, ta, gb = _mix_in(x, norm_mix_g, w_in[0], b_gate, gmlp_norm_g, ws_pairs, bsp,
                          w_proj_a[0].astype(BF16))
    os_, ls_ = [], []
    for (q, k, v), dil in zip(qkv, DILATIONS):
        o, lse = _attn_pattern(q, k, v, dil)
        os_.append(o)
        ls_.append(lse)
    x1, h2, aff = _mix_out(x, ta, gb, os_, ls_, w_proj_b[0], w_out[0], norm_ffn_g, w_router[0].T)
    return _moe_stages(aff, h2, x1, w_e_gate[0], w_e_up[0], w_e_down[0], norm_final_g[None], cap)
```

```python
import functools

from typing import NamedTuple

import jax
import jax.numpy as jnp
from jax import lax
from jax.experimental import pallas as pl
from jax.experimental.pallas import tpu as pltpu

F32 = jnp.float32
BF16 = jnp.bfloat16
I32 = jnp.int32

EPS = 1e-6
GMLP_WIDTH = 512
GMLP_GROUPS = 8
CHUNK = 128
N_HEADS = 8
HEAD_DIM = 64
ATTN_WIDTH = N_HEADS * HEAD_DIM
DILATIONS = (1, 4, 16)
DILATION_STEP = 4
assert all(b == a * DILATION_STEP for a, b in zip(DILATIONS, DILATIONS[1:]))
HALF_WINDOW = 64
N_EXPERTS = 16
CAPACITY_FACTOR = 2

LANES = 128
Q_TILE = 128
KEY_TILE = 2 * Q_TILE
ATTN_STEP_ROWS = 2048
STAT_LANES = LANES // N_HEADS
DEN_SHIFT = STAT_LANES // 2
SLOT_TILE = 128
SLOT_ALIGN = 16
COMBINE_ROWS = 1024
X1_RING_SLOTS = 3
FFN_SEQS = 2
GATHER_STEP_TOKENS = 2048
GATHER_TOKENS = 256
ROW_TILE = 512
MIB = 1024 * 1024


def _cparams(sem, vmem_mib):
    return pltpu.CompilerParams(dimension_semantics=sem, vmem_limit_bytes=vmem_mib * MIB)


def _gelu_tanh(x):
    return 0.5 * x * (1.0 + jnp.tanh(0.7978845608028654 * (x + 0.044715 * (x * x * x))))


def _sigmoid(x):
    return 1.0 / (1.0 + jnp.exp(-x))


def _rms(x, g):
    return x * lax.rsqrt(jnp.mean(x * x, axis=-1, keepdims=True) + EPS) * g


def _dot(a, b):
    return jnp.dot(a, b, preferred_element_type=F32)


def _dot_nt(a, b):
    return lax.dot_general(a, b, (((1,), (1,)), ((), ())), preferred_element_type=F32)


def _dot_tn(a, b):
    return lax.dot_general(a, b, (((0,), (0,)), ((), ())), preferred_element_type=F32)


def _mix_in_kernel(x_ref, g_ref, win_ref, bg_ref, g2_ref, ws_ref, bsp_ref, pa_ref, *refs):
    n_qkv = 3 * len(DILATIONS)
    qkv_refs = refs[:n_qkv]
    ta_ref, gb_ref = refs[n_qkv:n_qkv + 2]
    win_bf = refs[n_qkv + 2]
    stage_refs = refs[n_qkv + 3:]
    stage_refs = list(zip(stage_refs[0::2], stage_refs[1::2]))
    rows, d_model = x_ref.shape

    @pl.when(jnp.logical_and(pl.program_id(0) == 0, pl.program_id(1) == 0))
    def _():
        for c in range(0, win_ref.shape[1], GMLP_WIDTH):
            win_bf[:, c:c + GMLP_WIDTH] = win_ref[:, c:c + GMLP_WIDTH].astype(BF16)

    h = _rms(x_ref[...], g_ref[...]).astype(BF16)

    def proj(lo, width):
        return _dot(h, win_bf[:, lo:lo + width])

    c0 = 0
    u = _gelu_tanh(proj(c0, GMLP_WIDTH)); c0 += GMLP_WIDTH
    v = _gelu_tanh(proj(c0, GMLP_WIDTH)); c0 += GMLP_WIDTH
    for i in range(3):
        val = proj(c0, ATTN_WIDTH); c0 += ATTN_WIDTH
        if i == 0:
            val = val * (HEAD_DIM ** -0.5)
        qkv_refs[i][0] = val.astype(BF16)
        stage1, stage2 = stage_refs[i]
        n4, n16 = rows // DILATIONS[1], rows // DILATIONS[2]
        out4, out16 = qkv_refs[3 + i], qkv_refs[6 + i]
        for p in range(ATTN_WIDTH // LANES):
            cs = slice(p * LANES, (p + 1) * LANES)
            stage1[p] = val[:, cs]
            for r4 in range(DILATION_STEP):
                part = stage1[p, pl.ds(r4, n4, stride=DILATION_STEP), :]
                out4[r4, :, cs] = part.astype(BF16)
                stage2[p, r4] = part
                for c in range(DILATION_STEP):
                    out16[r4 + DILATION_STEP * c, :, cs] = (
                        stage2[p, r4, pl.ds(c, n16, stride=DILATION_STEP), :].astype(BF16))
    ga = _sigmoid(proj(c0, d_model) + bg_ref[:, :d_model]); c0 += d_model
    gb = _sigmoid(proj(c0, d_model) + bg_ref[:, d_model:])
    gb_ref[...] = gb.astype(BF16)

    vn = _rms(v, g2_ref[...]).astype(BF16)
    lane_lo = lax.broadcasted_iota(I32, (CHUNK, LANES), 1) < HEAD_DIM
    bsp = bsp_ref[...]
    n_chunk = rows // CHUNK
    mixed_slabs = []
    for p in range(GMLP_WIDTH // LANES):
        slab = jnp.concatenate(
            [vn[c * CHUNK:(c + 1) * CHUNK, p * LANES:(p + 1) * LANES] for c in range(n_chunk)],
            axis=1)
        r = _dot(ws_ref[p], slab)
        mixed_slabs.append([jnp.where(lane_lo, r[:CHUNK, c * LANES:(c + 1) * LANES],
                                      r[CHUNK:, c * LANES:(c + 1) * LANES])
                            for c in range(n_chunk)])
    a_chunks = []
    for c in range(n_chunk):
        rs = slice(c * CHUNK, (c + 1) * CHUNK)
        mixed = jnp.concatenate([slabs[c] for slabs in mixed_slabs], axis=1) + bsp
        a_chunks.append((u[rs] * mixed).astype(BF16))
    a = jnp.concatenate(a_chunks, axis=0)
    ta_ref[...] = (ga * _dot(a, pa_ref[...])).astype(BF16)


def _mix_in(x, g, w_in, b_gate, g2, ws_pairs, bsp, w_pa):
    b, s, d = x.shape
    const = lambda shape: pl.BlockSpec(shape, lambda bi, t: (0,) * len(shape))
    row = lambda w: pl.BlockSpec((None, ROW_TILE, w), lambda bi, t: (bi, t, 0))
    qkv_specs, qkv_shapes = [], []
    for dil in DILATIONS:
        spec = pl.BlockSpec((None, dil, ROW_TILE // dil, ATTN_WIDTH), lambda bi, t: (bi, 0, t, 0))
        qkv_specs += [spec] * 3
        qkv_shapes += [jax.ShapeDtypeStruct((b, dil, s // dil, ATTN_WIDTH), BF16)] * 3
    outs = pl.pallas_call(
        _mix_in_kernel,
        grid=(b, s // ROW_TILE),
        in_specs=[row(d), const(g.shape),
                  pl.BlockSpec(w_in.shape, lambda bi, t: (0, 0), pipeline_mode=pl.Buffered(1)),
                  const(b_gate.shape), const(g2.shape),
                  const(ws_pairs.shape), const(bsp.shape), const(w_pa.shape)],
        out_specs=qkv_specs + [row(d), row(d)],
        out_shape=qkv_shapes + [jax.ShapeDtypeStruct((b, s, d), BF16)] * 2,
        scratch_shapes=[pltpu.VMEM(w_in.shape, BF16)]
                       + [pltpu.VMEM((ATTN_WIDTH // LANES, ROW_TILE, LANES), F32),
                          pltpu.VMEM((ATTN_WIDTH // LANES, DILATION_STEP,
                                      ROW_TILE // DILATION_STEP, LANES), F32)] * 3,
        compiler_params=_cparams(("arbitrary", "arbitrary"), 60),
        name="mix_in",
    )(x, g, w_in, b_gate, g2, ws_pairs, bsp, w_pa)
    n_qkv = 3 * len(DILATIONS)
    qkv = [outs[3 * i:3 * i + 3] for i in range(len(DILATIONS))]
    return qkv, outs[n_qkv], outs[n_qkv + 1]


def _attn_kernel(q_ref, k_ref, v_ref, o_ref, l_ref, bias_ref, *, dil):
    n_res, rows, _ = q_ref.shape
    seq = k_ref.shape[1]
    t = pl.program_id(2)
    first = jnp.logical_and(jnp.logical_and(pl.program_id(0) == 0, pl.program_id(1) == 0), t == 0)

    @pl.when(first)
    def _():
        ii = lax.broadcasted_iota(I32, (Q_TILE, KEY_TILE), 0)
        jj = lax.broadcasted_iota(I32, (Q_TILE, KEY_TILE), 1)
        for var in range(3):
            absd = jnp.abs(jj - ii - var * HALF_WINDOW)
            valid = absd <= HALF_WINDOW
            absf = absd.astype(F32)
            for h in range(N_HEADS):
                slope = 2.0 ** (-8.0 * (h + 1) / N_HEADS)
                bias_ref[var, h] = jnp.where(valid, -(slope * dil) * absf, -jnp.inf)

    lane = lax.broadcasted_iota(I32, (Q_TILE, LANES), 1)
    lane_lo = lane < HEAD_DIM
    mask_lo = jnp.where(lane_lo, 1.0, 0.0).astype(BF16)
    mask_hi = jnp.where(lane_lo, 0.0, 1.0).astype(BF16)
    for rr in range(n_res):
        for qi in range(rows // Q_TILE):
            rs = slice(qi * Q_TILE, (qi + 1) * Q_TILE)
            i0 = t * rows + qi * Q_TILE
            start = pl.multiple_of(jnp.clip(i0 - HALF_WINDOW, 0, seq - KEY_TILE), HALF_WINDOW)
            var = (i0 - start) // HALF_WINDOW
            for p in range(ATTN_WIDTH // LANES):
                cs = slice(p * LANES, (p + 1) * LANES)
                qp = q_ref[rr, rs, cs]
                kp = k_ref[rr, pl.ds(start, KEY_TILE), cs]
                vp = v_ref[rr, pl.ds(start, KEY_TILE), cs]
                q2 = jnp.concatenate([qp * mask_lo, qp * mask_hi], axis=0)
                s2 = _dot_nt(q2, kp)
                probs = []
                for hh in range(2):
                    h = 2 * p + hh
                    s = s2[hh * Q_TILE:(hh + 1) * Q_TILE] + bias_ref[var, h]
                    m = jnp.max(s, axis=-1, keepdims=True)
                    e = jnp.exp(s - m)
                    den = jnp.sum(e, axis=-1, keepdims=True)
                    probs.append(e)
                    lo = h * STAT_LANES
                    l_ref[rr, rs, lo:lo + DEN_SHIFT] = jnp.broadcast_to(m, (Q_TILE, DEN_SHIFT))
                    l_ref[rr, rs, lo + DEN_SHIFT:lo + STAT_LANES] = jnp.broadcast_to(
                        den, (Q_TILE, DEN_SHIFT))
                o2 = _dot(jnp.concatenate(probs, axis=0).astype(BF16), vp)
                o_ref[rr, rs, cs] = jnp.where(lane_lo, o2[:Q_TILE], o2[Q_TILE:]).astype(BF16)


def _attn_pattern(q, k, v, dil):
    b, _, seq, w = q.shape
    rows = min(seq, ATTN_STEP_ROWS)
    n_res = ATTN_STEP_ROWS // rows
    qspec = lambda width: pl.BlockSpec((None, n_res, rows, width), lambda bi, r, t: (bi, r, t, 0))
    kspec = pl.BlockSpec((None, n_res, seq, w), lambda bi, r, t: (bi, r, 0, 0))
    return pl.pallas_call(
        functools.partial(_attn_kernel, dil=dil),
        grid=(b, dil // n_res, seq // rows),
        in_specs=[qspec(w), kspec, kspec],
        out_specs=[qspec(w), qspec(LANES)],
        out_shape=[jax.ShapeDtypeStruct(q.shape, BF16),
                   jax.ShapeDtypeStruct((b, dil, seq, LANES), F32)],
        scratch_shapes=[pltpu.VMEM((3, N_HEADS, Q_TILE, KEY_TILE), F32)],
        compiler_params=_cparams(("arbitrary", "arbitrary", "arbitrary"), 48),
        name=f"attn_d{dil}",
    )(q, k, v)


def _to_natural(src_ref, nat_ref, tmp_ref, rows):
    n_slab = nat_ref.shape[0]
    step = DILATION_STEP
    for p in range(n_slab):
        cs = slice(p * LANES, (p + 1) * LANES)
        for r4 in range(step):
            if tmp_ref is None:
                quarter = src_ref[r4][:, cs].astype(F32)
            else:
                for c in range(step):
                    tmp_ref[p, r4, pl.ds(c, rows // (step * step), stride=step), :] = (
                        src_ref[r4 + step * c][:, cs].astype(F32))
                quarter = tmp_ref[p, r4]
            nat_ref[p, pl.ds(r4, rows // step, stride=step), :] = quarter
    return jnp.concatenate([nat_ref[p] for p in range(n_slab)], axis=1)


def _mix_out_kernel(x_ref, ta_ref, gb_ref, *refs):
    n_pat = len(DILATIONS)
    o_refs = refs[:n_pat]
    l_refs = refs[n_pat:2 * n_pat]
    pb_f32, wo_f32, g_ref, wr_ref, x1_ref, h2_ref, aff_ref = refs[2 * n_pat:2 * n_pat + 7]
    pb_ref, wo_ref = refs[2 * n_pat + 7:2 * n_pat + 9]
    stage_refs = refs[2 * n_pat + 9:]
    rows = x_ref.shape[0]

    @pl.when(jnp.logical_and(pl.program_id(0) == 0, pl.program_id(1) == 0))
    def _():
        pb_ref[...] = pb_f32[...].astype(BF16)
        wo_ref[...] = wo_f32[...].astype(BF16)

    stage_refs = list(stage_refs)
    outs = [o_refs[0][0].astype(F32)]
    lses = [l_refs[0][0]]
    for di in (1, 2):
        for src, dest in ((o_refs[di], outs), (l_refs[di], lses)):
            nat = stage_refs.pop(0)
            tmp = stage_refs.pop(0) if di == 2 else None
            dest.append(_to_natural(src, nat, tmp, rows))

    dens = [pltpu.roll(st, LANES - DEN_SHIFT, 1) for st in lses]
    lses = [st + jnp.log(den) for st, den in zip(lses, dens)]
    m = functools.reduce(jnp.maximum, lses)
    ws = [jnp.exp(l - m) for l in lses]
    inv = 1.0 / functools.reduce(lambda a, c: a + c, ws)
    lane = lax.broadcasted_iota(I32, (rows, LANES), 1)
    used = lane % STAT_LANES < DEN_SHIFT
    ws = [jnp.where(used, w * inv / den, 0.0) for w, den in zip(ws, dens)]
    k_i = lax.broadcasted_iota(I32, (2 * LANES, ATTN_WIDTH), 0) % LANES
    c_i = lax.broadcasted_iota(I32, (2 * LANES, ATTN_WIDTH), 1)
    spread = jnp.where(k_i == (c_i // HEAD_DIM) * STAT_LANES, 1.0, 0.0).astype(BF16)
    parts = []
    for w in ws:
        w_hi = w.astype(BF16)
        w_lo = (w - w_hi.astype(F32)).astype(BF16)
        parts.append(jnp.concatenate([w_hi, w_lo], axis=1))
    factors = _dot(jnp.concatenate(parts, axis=0), spread)
    o = None
    for i, o_p in enumerate(outs):
        term = factors[i * rows:(i + 1) * rows] * o_p
        o = term if o is None else o + term

    ob = _dot(o.astype(BF16), pb_ref[...])
    merged = (ta_ref[...].astype(F32) + gb_ref[...].astype(F32) * ob).astype(BF16)
    x1 = x_ref[...] + _dot(merged, wo_ref[...])
    x1_ref[...] = x1
    h2 = _rms(x1, g_ref[...])
    h2_ref[...] = h2.astype(BF16)
    h_hi = h2.astype(BF16)
    h_lo = (h2 - h_hi.astype(F32)).astype(BF16)
    wr = wr_ref[...]
    w_hi = wr.astype(BF16)
    w_lo = (wr - w_hi.astype(F32)).astype(BF16)
    n_e = wr.shape[0]
    by_hi = _dot_nt(jnp.concatenate([w_hi, w_lo], axis=0), h_hi)
    logits = by_hi[:n_e] + (_dot_nt(w_hi, h_lo) + by_hi[n_e:])
    e = jnp.exp(logits - jnp.max(logits, axis=0, keepdims=True))
    aff_ref[...] = e / jnp.sum(e, axis=0, keepdims=True)


def _mix_out(x, ta, gb, os_, ls_, w_pb, w_out, g, w_router_t):
    b, s, d = x.shape
    n_e = w_router_t.shape[0]
    const = lambda shape: pl.BlockSpec(shape, lambda bi, t: (0,) * len(shape))
    once = lambda shape: pl.BlockSpec(shape, lambda bi, t: (0,) * len(shape),
                                      pipeline_mode=pl.Buffered(1))
    row = lambda w: pl.BlockSpec((None, ROW_TILE, w), lambda bi, t: (bi, t, 0))
    res = lambda dil, w: pl.BlockSpec((None, dil, ROW_TILE // dil, w), lambda bi, t: (bi, 0, t, 0))
    stage = []
    for di in (1, 2):
        for slabs in (ATTN_WIDTH // LANES, 1):
            stage.append(pltpu.VMEM((slabs, ROW_TILE, LANES), F32))
            if di == 2:
                stage.append(pltpu.VMEM((slabs, DILATION_STEP, ROW_TILE // DILATION_STEP, LANES),
                                        F32))
    return pl.pallas_call(
        _mix_out_kernel,
        grid=(b, s // ROW_TILE),
        in_specs=[row(d), row(d), row(d)]
                 + [res(dil, ATTN_WIDTH) for dil in DILATIONS]
                 + [res(dil, LANES) for dil in DILATIONS]
                 + [once(w_pb.shape), once(w_out.shape), const(g.shape), const(w_router_t.shape)],
        out_specs=[row(d), row(d), pl.BlockSpec((None, n_e, ROW_TILE), lambda bi, t: (bi, 0, t))],
        out_shape=[jax.ShapeDtypeStruct((b, s, d), F32), jax.ShapeDtypeStruct((b, s, d), BF16),
                   jax.ShapeDtypeStruct((b, n_e, s), F32)],
        scratch_shapes=[pltpu.VMEM(w_pb.shape, BF16), pltpu.VMEM(w_out.shape, BF16)] + stage,
        compiler_params=_cparams(("arbitrary", "arbitrary"), 48),
        name="mix_out",
    )(x, ta, gb, *os_, *ls_, w_pb, w_out, g, w_router_t)


class _Plan(NamedTuple):
    tiles: int
    fast: int
    slow: int
    group: int


GATHER_PLAN = _Plan(tiles=GATHER_TOKENS // SLOT_TILE, fast=64, slow=GATHER_TOKENS + SLOT_ALIGN,
                    group=N_EXPERTS // 2)
COMBINE_PLAN = _Plan(tiles=1, fast=48, slow=SLOT_TILE + SLOT_ALIGN, group=N_EXPERTS)
FAST_WINDOW, SLOT_WINDOW = COMBINE_PLAN.fast, COMBINE_PLAN.slow


def _slot_plan(cum, plan, cap):
    rows = cum.shape[0]
    nxt = pltpu.roll(cum, LANES - plan.tiles, 1)
    lo = jnp.floor(cum * (1.0 / SLOT_ALIGN)) * SLOT_ALIGN
    ok = jnp.where(nxt - lo <= plan.fast, 1.0, 0.0)
    fit = jnp.min(ok.reshape(rows // plan.group, plan.group, LANES), axis=1)
    assert (cap - plan.fast) % SLOT_ALIGN == 0 and (cap - plan.slow) % SLOT_ALIGN == 0
    return (jnp.minimum(lo, cap - plan.fast).astype(I32),
            jnp.minimum(lo, cap - plan.slow).astype(I32), fit.astype(I32))


def _topk_kernel(aff_ref, rank_ref, *plan_refs, cap, plans):
    n_e, s = aff_ref.shape
    n_blk = s // SLOT_TILE
    aff = aff_ref[...]
    thr = jnp.zeros((n_e, 1), I32)
    for bit in range(30, -1, -1):
        cand = thr | (1 << bit)
        cnt = jnp.sum((aff >= pltpu.bitcast(cand, F32)).astype(I32), axis=1, keepdims=True)
        thr = jnp.where(cnt >= cap, cand, thr)
    above = aff >= pltpu.bitcast(thr + 1, F32)
    tie = jnp.logical_and(aff >= pltpu.bitcast(thr, F32), jnp.logical_not(above))
    need = (cap - jnp.sum(above.astype(I32), axis=1, keepdims=True)).astype(F32)
    r_i = lax.broadcasted_iota(I32, (SLOT_TILE, SLOT_TILE), 0)
    c_i = lax.broadcasted_iota(I32, (SLOT_TILE, SLOT_TILE), 1)
    tri = jnp.where(r_i < c_i, 1.0, 0.0).astype(BF16)
    lane = lax.broadcasted_iota(I32, (n_e, LANES), 1)
    run_tie = jnp.zeros((n_e, 1), F32)
    run_sel = jnp.zeros((n_e, 1), F32)
    cum = jnp.zeros((n_e, LANES), F32)
    for j in range(n_blk):
        cs = slice(j * SLOT_TILE, (j + 1) * SLOT_TILE)
        tie_f = jnp.where(tie[:, cs], 1.0, 0.0)
        tie_rank = _dot(tie_f.astype(BF16), tri) + run_tie
        run_tie = run_tie + jnp.sum(tie_f, axis=1, keepdims=True)
        sel_f = jnp.where(above[:, cs], 1.0, jnp.where(tie_rank < need, tie_f, 0.0))
        rank = _dot(sel_f.astype(BF16), tri) + run_sel
        rank_ref[:, cs] = jnp.where(sel_f > 0.0, rank, -1.0).astype(I32)
        cum = jnp.where(lane == j, run_sel, cum)
        run_sel = run_sel + jnp.sum(sel_f, axis=1, keepdims=True)
    cum = jnp.where(lane == n_blk, run_sel, cum)
    for i, plan in enumerate(plans):
        for ref, val in zip(plan_refs[3 * i:3 * i + 3], _slot_plan(cum, plan, cap)):
            ref[...] = val


def _topk(aff, cap, plans):
    b, n_e, s = aff.shape
    rows = b * n_e
    full = lambda r, w: pl.BlockSpec((r, w), lambda i: (0, 0))
    plan_specs, plan_shapes = [], []
    for plan in plans:
        for r in (rows, rows, rows // plan.group):
            plan_specs.append(full(r, LANES))
            plan_shapes.append(jax.ShapeDtypeStruct((r, LANES), I32))
    rank, *tables = pl.pallas_call(
        functools.partial(_topk_kernel, cap=cap, plans=plans),
        grid=(1,),
        in_specs=[full(rows, s)],
        out_specs=[full(rows, s)] + plan_specs,
        out_shape=[jax.ShapeDtypeStruct((rows, s), I32)] + plan_shapes,
        compiler_params=_cparams(("arbitrary",), 32),
        name="topk",
    )(aff.reshape(rows, s))
    tables = [t.reshape(-1) for t in tables]
    return rank.reshape(b, n_e, s), [tables[3 * i:3 * i + 3] for i in range(len(plans))]


def _window_start(table_ref, expert_row, tile):
    return pl.multiple_of(table_ref[expert_row * LANES + tile], SLOT_ALIGN)


def _span(jj, size):
    if isinstance(jj, int):
        return slice(jj * size, (jj + 1) * size)
    return pl.ds(pl.multiple_of(jj * size, size), size)


def _run_blocks(fits, fast, general):
    all_fit = functools.reduce(jnp.logical_and, fits)

    @pl.when(all_fit)
    def _():
        for jj in range(len(fits)):
            fast(jj)

    @pl.when(jnp.logical_not(all_fit))
    def _():
        def body(jj, carry):
            general(jj)
            return carry

        lax.fori_loop(0, len(fits), body, 0)


def _gather_kernel(fast_ref, slow_ref, fit_ref, rank_ref, aff_ref, h2_ref, xe_ref, gate_ref):
    n_all = rank_ref.shape[0]
    n_e = GATHER_PLAN.group
    rows = h2_ref.shape[0]
    bi = pl.program_id(0)
    t = pl.program_id(1)
    e0 = 0

    @pl.when(t == 0)
    def _():
        xe_ref[...] = jnp.zeros_like(xe_ref)
        gate_ref[...] = jnp.zeros_like(gate_ref)

    row_fast = lax.broadcasted_iota(I32, (GATHER_PLAN.fast, GATHER_TOKENS), 0)
    row_slow = lax.broadcasted_iota(I32, (GATHER_PLAN.slow, GATHER_TOKENS), 0)
    n_blk = rows // GATHER_TOKENS
    tile = lambda jj: (t * n_blk + jj) * GATHER_PLAN.tiles

    def window(jj, e, table_ref, row_i):
        toks = _span(jj, GATHER_TOKENS)
        lo = _window_start(table_ref, bi * n_all + e0 + e, tile(jj))
        return lo, (row_i + lo) == rank_ref[e0 + e, :, toks], aff_ref[e0 + e, :, toks]

    def add_window(e, lo, hit, aff_row, rows_e):
        win = pl.ds(lo, hit.shape[0])
        xe_ref[e0 + e, win, :] += rows_e.astype(BF16)
        gate_ref[e0 + e, win, :] += jnp.sum(jnp.where(hit, aff_row, 0.0), axis=1, keepdims=True)

    def tokens(jj):
        return h2_ref[_span(jj, GATHER_TOKENS), :]

    def fast(jj):
        ws = [window(jj, e, fast_ref, row_fast) for e in range(n_e)]
        stack = jnp.concatenate([jnp.where(h, 1.0, 0.0).astype(BF16) for _, h, _ in ws], axis=0)
        res = _dot(stack, tokens(jj))
        for e, (lo, h, aff_row) in enumerate(ws):
            add_window(e, lo, h, aff_row, res[e * GATHER_PLAN.fast:(e + 1) * GATHER_PLAN.fast])

    def general(jj):
        def body(e, carry):
            lo, h, aff_row = window(jj, e, slow_ref, row_slow)
            add_window(e, lo, h, aff_row, _dot(jnp.where(h, 1.0, 0.0).astype(BF16), tokens(jj)))
            return carry

        lax.fori_loop(0, n_e, body, 0)

    n_groups = n_all // n_e
    for g in range(n_groups):
        e0 = g * n_e
        fit_row = (bi * n_groups + g) * LANES
        _run_blocks([fit_ref[fit_row + tile(jj)] != 0 for jj in range(n_blk)], fast, general)


def _gather(plan_tables, rank, aff, h2, cap):
    b, s, d = h2.shape
    n_e = rank.shape[1]
    rows = cap
    per_tok = pl.BlockSpec((None, n_e, 1, GATHER_STEP_TOKENS), lambda bi, t, *_: (bi, 0, 0, t))
    whole = lambda w: pl.BlockSpec((None, n_e, rows, w), lambda bi, t, *_: (bi, 0, 0, 0))
    grid_spec = pltpu.PrefetchScalarGridSpec(
        num_scalar_prefetch=3,
        grid=(b, s // GATHER_STEP_TOKENS),
        in_specs=[per_tok, per_tok,
                  pl.BlockSpec((None, GATHER_STEP_TOKENS, d), lambda bi, t, *_: (bi, t, 0))],
        out_specs=[whole(d), whole(LANES)],
    )
    expert_rows = lambda a: a.reshape(b, n_e, 1, s)
    return pl.pallas_call(
        _gather_kernel,
        grid_spec=grid_spec,
        out_shape=[jax.ShapeDtypeStruct((b, n_e, rows, d), BF16),
                   jax.ShapeDtypeStruct((b, n_e, rows, LANES), F32)],
        compiler_params=_cparams(("arbitrary", "arbitrary"), 60),
        name="gather",
    )(*plan_tables, expert_rows(rank), expert_rows(aff), h2)


def _moe_ffn_kernel(xe_ref, gate_ref, wg_ref, wu_ref, wd_ref, ye_ref, wg_bf, wu_bf, wd_bf):
    n_seq, cap, d = xe_ref.shape

    @pl.when(pl.program_id(1) == 0)
    def _():
        wg_bf[...] = wg_ref[...].astype(BF16)
        wu_bf[...] = wu_ref[...].astype(BF16)
        wd_bf[...] = wd_ref[...].astype(BF16)

    xe = xe_ref[...].reshape(n_seq * cap, d)
    gate_h = _dot(xe, wg_bf[...])
    up_h = _dot(xe, wu_bf[...])
    hidden = (gate_h * _sigmoid(gate_h) * up_h).astype(BF16)
    ye = _dot(hidden, wd_bf[...]) * gate_ref[...].reshape(n_seq * cap, LANES)[:, 0:1]
    ye_ref[...] = ye.astype(BF16).reshape(n_seq, cap, d)


def _moe_ffn(xe, gate, wg, wu, wd, cap):
    b, n_e, rows, d = xe.shape
    hid = wg.shape[2]
    return pl.pallas_call(
        _moe_ffn_kernel,
        grid=(n_e, b // FFN_SEQS),
        in_specs=[
            pl.BlockSpec((FFN_SEQS, None, cap, d), lambda e, bi: (bi, e, 0, 0)),
            pl.BlockSpec((FFN_SEQS, None, cap, LANES), lambda e, bi: (bi, e, 0, 0)),
            pl.BlockSpec((None, d, hid), lambda e, bi: (e, 0, 0)),
            pl.BlockSpec((None, d, hid), lambda e, bi: (e, 0, 0)),
            pl.BlockSpec((None, hid, d), lambda e, bi: (e, 0, 0)),
        ],
        out_specs=pl.BlockSpec((FFN_SEQS, None, rows, d), lambda e, bi: (bi, e, 0, 0)),
        out_shape=jax.ShapeDtypeStruct((b, n_e, rows, d), BF16),
        scratch_shapes=[pltpu.VMEM((d, hid), BF16), pltpu.VMEM((d, hid), BF16),
                        pltpu.VMEM((hid, d), BF16)],
        compiler_params=_cparams(("arbitrary", "arbitrary"), 56),
        name="moe_ffn",
    )(xe, gate, wg, wu, wd)


def _combine_kernel(fast_ref, slow_ref, fit_ref, rank_ref, ye_hbm, x1_hbm, g_ref, y_ref, rhs_ref,
                    acc_ref, x1_ring, ring_sem, ye_buf, ye_sem, *, steps_per_seq, n_steps):
    n_e = rank_ref.shape[0]
    rows = y_ref.shape[0]
    n_blk = rows // SLOT_TILE
    bi = pl.program_id(0)
    t = pl.program_id(1)
    step = bi * steps_per_seq + t
    n_seq = n_steps // steps_per_seq

    def ye_copy(seq):
        return pltpu.make_async_copy(ye_hbm.at[seq], ye_buf.at[seq % 2], ye_sem.at[seq % 2])

    @pl.when(step == 0)
    def _():
        ye_copy(0).start()

    @pl.when(jnp.logical_and(t == min(1, steps_per_seq - 1), bi + 1 < n_seq))
    def _():
        ye_copy(bi + 1).start()

    @pl.when(t == 0)
    def _():
        ye_copy(bi).wait()

    ye_ref = ye_buf.at[bi % 2]

    def x1_copy(s):
        tile = x1_hbm.at[s // steps_per_seq, pl.ds((s % steps_per_seq) * rows, rows), :]
        slot = s % X1_RING_SLOTS
        return pltpu.make_async_copy(tile, x1_ring.at[slot], ring_sem.at[slot])

    @pl.when(step == 0)
    def _():
        for s in range(min(X1_RING_SLOTS - 1, n_steps)):
            x1_copy(s).start()

    @pl.when(step + (X1_RING_SLOTS - 1) < n_steps)
    def _():
        x1_copy(step + (X1_RING_SLOTS - 1)).start()

    x1_copy(step).wait()
    x1_ref = x1_ring.at[step % X1_RING_SLOTS]
    row_fast = lax.broadcasted_iota(I32, (FAST_WINDOW, SLOT_TILE), 0)
    row_slow = lax.broadcasted_iota(I32, (SLOT_WINDOW, SLOT_TILE), 0)

    def hits(jj, e, table_ref, row_i):
        lo = _window_start(table_ref, bi * n_e + e, t * n_blk + jj)
        hit = (row_i + lo) == rank_ref[e, :, _span(jj, SLOT_TILE)]
        return pl.ds(lo, row_i.shape[0]), jnp.where(hit, 1.0, 0.0).astype(BF16)

    def finish(jj, moe):
        toks = _span(jj, SLOT_TILE)
        y_ref[toks, :] = _rms(x1_ref[toks, :] + moe, g_ref[...])

    def fast(jj):
        rhs = rhs_ref.at[jj % rhs_ref.shape[0]]
        stack = []
        for e in range(n_e):
            win, hit = hits(jj, e, fast_ref, row_fast)
            rhs[e * FAST_WINDOW:(e + 1) * FAST_WINDOW, :] = ye_ref[e, win, :]
            stack.append(hit)
        finish(jj, _dot_tn(jnp.concatenate(stack, axis=0), rhs[...]))

    def general(jj):
        acc_ref[...] = jnp.zeros_like(acc_ref)

        def body(e, carry):
            win, hit = hits(jj, e, slow_ref, row_slow)
            acc_ref[...] += _dot_tn(hit, ye_ref[e, win, :])
            return carry

        lax.fori_loop(0, n_e, body, 0)
        finish(jj, acc_ref[...])

    _run_blocks([fit_ref[bi * LANES + t * n_blk + jj] != 0 for jj in range(n_blk)], fast, general)


def _combine(plan_tables, rank, ye, x1, g):
    b, s, d = x1.shape
    n_e, ye_rows = ye.shape[1], ye.shape[2]
    row = pl.BlockSpec((None, COMBINE_ROWS, d), lambda bi, t, *_: (bi, t, 0))
    grid_spec = pltpu.PrefetchScalarGridSpec(
        num_scalar_prefetch=3,
        grid=(b, s // COMBINE_ROWS),
        in_specs=[
            pl.BlockSpec((None, n_e, 1, COMBINE_ROWS), lambda bi, t, *_: (bi, 0, 0, t)),
            pl.BlockSpec(memory_space=pl.ANY),
            pl.BlockSpec(memory_space=pl.ANY),
            pl.BlockSpec(g.shape, lambda bi, t, *_: (0, 0)),
        ],
        out_specs=row,
        scratch_shapes=[pltpu.VMEM((2, n_e * FAST_WINDOW, d), BF16),
                        pltpu.VMEM((SLOT_TILE, d), F32),
                        pltpu.VMEM((X1_RING_SLOTS, COMBINE_ROWS, d), F32),
                        pltpu.SemaphoreType.DMA((X1_RING_SLOTS,)),
                        pltpu.VMEM((2, n_e, ye_rows, d), BF16),
                        pltpu.SemaphoreType.DMA((2,))],
    )
    steps_per_seq = s // COMBINE_ROWS
    return pl.pallas_call(
        functools.partial(_combine_kernel, steps_per_seq=steps_per_seq,
                          n_steps=b * steps_per_seq),
        grid_spec=grid_spec,
        out_shape=jax.ShapeDtypeStruct((b, s, d), F32),
        compiler_params=_cparams(("arbitrary", "arbitrary"), 60),
        name="combine",
    )(*plan_tables, rank.reshape(b, n_e, 1, s), ye, x1, g)


def _moe_stages(aff, h2, x1, wg, wu, wd, g_final, cap):
    rank, (gather_tables, combine_tables) = _topk(aff, cap, (GATHER_PLAN, COMBINE_PLAN))
    xe, gate = _gather(gather_tables, rank, aff, h2, cap)
    ye = _moe_ffn(xe, gate, wg, wu, wd, cap)
    return _combine(combine_tables, rank, ye, x1, g_final)


def kernel(x, norm_mix_g, w_in, b_gate, gmlp_norm_g, w_spatial, b_spatial, w_proj_a, w_proj_b,
           w_out, norm_ffn_g, w_router, w_e_gate, w_e_up, w_e_down, norm_final_g):
    b, s, d = x.shape
    assert w_in.shape[0] == 1, "single-layer block"
    cap = CAPACITY_FACTOR * s // N_EXPERTS
    group_width = GMLP_WIDTH // GMLP_GROUPS
    ws_pairs = w_spatial[0].astype(BF16).reshape(GMLP_GROUPS // 2, 2 * CHUNK, CHUNK)
    bsp = jnp.repeat(b_spatial[0].T, group_width, axis=1)
    qkv, ta, gb = _mix_in(x, norm_mix_g, w_in[0], b_gate, gmlp_norm_g, ws_pairs, bsp,
                          w_proj_a[0].astype(BF16))
    os_, ls_ = [], []
    for (q, k, v), dil in zip(qkv, DILATIONS):
        o, lse = _attn_pattern(q, k, v, dil)
        os_.append(o)
        ls_.append(lse)
    x1, h2, aff = _mix_out(x, ta, gb, os_, ls_, w_proj_b[0], w_out[0], norm_ffn_g, w_router[0].T)
    return _moe_stages(aff, h2, x1, w_e_gate[0], w_e_up[0], w_e_down[0], norm_final_g[None], cap)
```

```python
import functools

from typing import NamedTuple

import jax
import jax.numpy as jnp
from jax import lax
from jax.experimental import pallas as pl
from jax.experimental.pallas import tpu as pltpu

F32 = jnp.float32
BF16 = jnp.bfloat16
I32 = jnp.int32

EPS = 1e-6
GMLP_WIDTH = 512
GMLP_GROUPS = 8
CHUNK = 128
N_HEADS = 8
HEAD_DIM = 64
ATTN_WIDTH = N_HEADS * HEAD_DIM
DILATIONS = (1, 4, 16)
DILATION_STEP = 4
assert all(b == a * DILATION_STEP for a, b in zip(DILATIONS, DILATIONS[1:]))
HALF_WINDOW = 64
N_EXPERTS = 16
CAPACITY_FACTOR = 2

LANES = 128
Q_TILE = 128
KEY_TILE = 2 * Q_TILE
ATTN_STEP_ROWS = 2048
STAT_LANES = LANES // N_HEADS
DEN_SHIFT = STAT_LANES // 2
SLOT_TILE = 128
SLOT_ALIGN = 16
COMBINE_ROWS = 1024
X1_RING_SLOTS = 3
FFN_SEQS = 2
GATHER_STEP_TOKENS = 2048
GATHER_TOKENS = 256
ROW_TILE = 512
MIB = 1024 * 1024


def _cparams(sem, vmem_mib):
    return pltpu.CompilerParams(dimension_semantics=sem, vmem_limit_bytes=vmem_mib * MIB)


def _gelu_tanh(x):
    return 0.5 * x * (1.0 + jnp.tanh(0.7978845608028654 * (x + 0.044715 * (x * x * x))))


def _sigmoid(x):
    return 1.0 / (1.0 + jnp.exp(-x))


def _rms(x, g):
    return x * lax.rsqrt(jnp.mean(x * x, axis=-1, keepdims=True) + EPS) * g


def _dot(a, b):
    return jnp.dot(a, b, preferred_element_type=F32)


def _dot_nt(a, b):
    return lax.dot_general(a, b, (((1,), (1,)), ((), ())), preferred_element_type=F32)


def _dot_tn(a, b):
    return lax.dot_general(a, b, (((0,), (0,)), ((), ())), preferred_element_type=F32)


def _cast_kernel(w_ref, o_ref):
    o_ref[...] = w_ref[...].astype(BF16)


def _cast_bf16(w):
    k, n = w.shape
    spec = pl.BlockSpec((k, GMLP_WIDTH), lambda j: (0, j))
    return pl.pallas_call(
        _cast_kernel, grid=(n // GMLP_WIDTH,), in_specs=[spec], out_specs=spec,
        out_shape=jax.ShapeDtypeStruct((k, n), BF16),
        compiler_params=_cparams(("parallel",), 16), name="cast_w_in")(w)


def _mix_in_kernel(x_ref, g_ref, win_bf, bg_ref, g2_ref, ws_ref, bsp_ref, pa_ref, *refs):
    n_qkv = 3 * len(DILATIONS)
    qkv_refs = refs[:n_qkv]
    ta_ref, gb_ref = refs[n_qkv:n_qkv + 2]
    stage_refs = refs[n_qkv + 2:]
    stage_refs = list(zip(stage_refs[0::2], stage_refs[1::2]))
    rows, d_model = x_ref.shape

    h = _rms(x_ref[...], g_ref[...]).astype(BF16)

    def proj(lo, width):
        return _dot(h, win_bf[:, lo:lo + width])

    c0 = 0
    u = _gelu_tanh(proj(c0, GMLP_WIDTH)); c0 += GMLP_WIDTH
    v = _gelu_tanh(proj(c0, GMLP_WIDTH)); c0 += GMLP_WIDTH
    for i in range(3):
        val = proj(c0, ATTN_WIDTH); c0 += ATTN_WIDTH
        if i == 0:
            val = val * (HEAD_DIM ** -0.5)
        qkv_refs[i][0] = val.astype(BF16)
        stage1, stage2 = stage_refs[i]
        n4, n16 = rows // DILATIONS[1], rows // DILATIONS[2]
        out4, out16 = qkv_refs[3 + i], qkv_refs[6 + i]
        for p in range(ATTN_WIDTH // LANES):
            cs = slice(p * LANES, (p + 1) * LANES)
            stage1[p] = val[:, cs]
            for r4 in range(DILATION_STEP):
                part = stage1[p, pl.ds(r4, n4, stride=DILATION_STEP), :]
                out4[r4, :, cs] = part.astype(BF16)
                stage2[p, r4] = part
                for c in range(DILATION_STEP):
                    out16[r4 + DILATION_STEP * c, :, cs] = (
                        stage2[p, r4, pl.ds(c, n16, stride=DILATION_STEP), :].astype(BF16))
    ga = _sigmoid(proj(c0, d_model) + bg_ref[:, :d_model]); c0 += d_model
    gb = _sigmoid(proj(c0, d_model) + bg_ref[:, d_model:])
    gb_ref[...] = gb.astype(BF16)

    vn = _rms(v, g2_ref[...]).astype(BF16)
    lane_lo = lax.broadcasted_iota(I32, (CHUNK, LANES), 1) < HEAD_DIM
    bsp = bsp_ref[...]
    n_chunk = rows // CHUNK
    mixed_slabs = []
    for p in range(GMLP_WIDTH // LANES):
        slab = jnp.concatenate(
            [vn[c * CHUNK:(c + 1) * CHUNK, p * LANES:(p + 1) * LANES] for c in range(n_chunk)],
            axis=1)
        r = _dot(ws_ref[p], slab)
        mixed_slabs.append([jnp.where(lane_lo, r[:CHUNK, c * LANES:(c + 1) * LANES],
                                      r[CHUNK:, c * LANES:(c + 1) * LANES])
                            for c in range(n_chunk)])
    a_chunks = []
    for c in range(n_chunk):
        rs = slice(c * CHUNK, (c + 1) * CHUNK)
        mixed = jnp.concatenate([slabs[c] for slabs in mixed_slabs], axis=1) + bsp
        a_chunks.append((u[rs] * mixed).astype(BF16))
    a = jnp.concatenate(a_chunks, axis=0)
    ta_ref[...] = (ga * _dot(a, pa_ref[...])).astype(BF16)


def _mix_in(x, g, w_in, b_gate, g2, ws_pairs, bsp, w_pa):
    b, s, d = x.shape
    const = lambda shape: pl.BlockSpec(shape, lambda bi, t: (0,) * len(shape))
    row = lambda w: pl.BlockSpec((None, ROW_TILE, w), lambda bi, t: (bi, t, 0))
    qkv_specs, qkv_shapes = [], []
    for dil in DILATIONS:
        spec = pl.BlockSpec((None, dil, ROW_TILE // dil, ATTN_WIDTH), lambda bi, t: (bi, 0, t, 0))
        qkv_specs += [spec] * 3
        qkv_shapes += [jax.ShapeDtypeStruct((b, dil, s // dil, ATTN_WIDTH), BF16)] * 3
    outs = pl.pallas_call(
        _mix_in_kernel,
        grid=(b, s // ROW_TILE),
        in_specs=[row(d), const(g.shape),
                  pl.BlockSpec(w_in.shape, lambda bi, t: (0, 0), pipeline_mode=pl.Buffered(1)),
                  const(b_gate.shape), const(g2.shape),
                  const(ws_pairs.shape), const(bsp.shape), const(w_pa.shape)],
        out_specs=qkv_specs + [row(d), row(d)],
        out_shape=qkv_shapes + [jax.ShapeDtypeStruct((b, s, d), BF16)] * 2,
        scratch_shapes=[pltpu.VMEM((ATTN_WIDTH // LANES, ROW_TILE, LANES), F32),
                        pltpu.VMEM((ATTN_WIDTH // LANES, DILATION_STEP,
                                    ROW_TILE // DILATION_STEP, LANES), F32)] * 3,
        compiler_params=_cparams(("parallel", "parallel"), 56),
        name="mix_in",
    )(x, g, w_in, b_gate, g2, ws_pairs, bsp, w_pa)
    n_qkv = 3 * len(DILATIONS)
    qkv = [outs[3 * i:3 * i + 3] for i in range(len(DILATIONS))]
    return qkv, outs[n_qkv], outs[n_qkv + 1]


def _attn_kernel(q_ref, k_ref, v_ref, o_ref, l_ref, bias_ref, *, dil):
    n_res, rows, _ = q_ref.shape
    seq = k_ref.shape[1]
    t = pl.program_id(2)
    first = jnp.logical_and(jnp.logical_and(pl.program_id(0) == 0, pl.program_id(1) == 0), t == 0)

    @pl.when(first)
    def _():
        ii = lax.broadcasted_iota(I32, (Q_TILE, KEY_TILE), 0)
        jj = lax.broadcasted_iota(I32, (Q_TILE, KEY_TILE), 1)
        for var in range(3):
            absd = jnp.abs(jj - ii - var * HALF_WINDOW)
            valid = absd <= HALF_WINDOW
            absf = absd.astype(F32)
            for h in range(N_HEADS):
                slope = 2.0 ** (-8.0 * (h + 1) / N_HEADS)
                bias_ref[var, h] = jnp.where(valid, -(slope * dil) * absf, -jnp.inf)

    lane = lax.broadcasted_iota(I32, (Q_TILE, LANES), 1)
    lane_lo = lane < HEAD_DIM
    mask_lo = jnp.where(lane_lo, 1.0, 0.0).astype(BF16)
    mask_hi = jnp.where(lane_lo, 0.0, 1.0).astype(BF16)
    for rr in range(n_res):
        for qi in range(rows // Q_TILE):
            rs = slice(qi * Q_TILE, (qi + 1) * Q_TILE)
            i0 = t * rows + qi * Q_TILE
            start = pl.multiple_of(jnp.clip(i0 - HALF_WINDOW, 0, seq - KEY_TILE), HALF_WINDOW)
            var = (i0 - start) // HALF_WINDOW
            for p in range(ATTN_WIDTH // LANES):
                cs = slice(p * LANES, (p + 1) * LANES)
                qp = q_ref[rr, rs, cs]
                kp = k_ref[rr, pl.ds(start, KEY_TILE), cs]
                vp = v_ref[rr, pl.ds(start, KEY_TILE), cs]
                q2 = jnp.concatenate([qp * mask_lo, qp * mask_hi], axis=0)
                s2 = _dot_nt(q2, kp)
                probs = []
                for hh in range(2):
                    h = 2 * p + hh
                    s = s2[hh * Q_TILE:(hh + 1) * Q_TILE] + bias_ref[var, h]
                    m = jnp.max(s, axis=-1, keepdims=True)
                    e = jnp.exp(s - m)
                    den = jnp.sum(e, axis=-1, keepdims=True)
                    probs.append(e)
                    lo = h * STAT_LANES
                    l_ref[rr, rs, lo:lo + DEN_SHIFT] = jnp.broadcast_to(m, (Q_TILE, DEN_SHIFT))
                    l_ref[rr, rs, lo + DEN_SHIFT:lo + STAT_LANES] = jnp.broadcast_to(
                        den, (Q_TILE, DEN_SHIFT))
                o2 = _dot(jnp.concatenate(probs, axis=0).astype(BF16), vp)
                o_ref[rr, rs, cs] = jnp.where(lane_lo, o2[:Q_TILE], o2[Q_TILE:]).astype(BF16)


def _attn_pattern(q, k, v, dil):
    b, _, seq, w = q.shape
    rows = min(seq, ATTN_STEP_ROWS)
    n_res = ATTN_STEP_ROWS // rows
    qspec = lambda width: pl.BlockSpec((None, n_res, rows, width), lambda bi, r, t: (bi, r, t, 0))
    kspec = pl.BlockSpec((None, n_res, seq, w), lambda bi, r, t: (bi, r, 0, 0))
    return pl.pallas_call(
        functools.partial(_attn_kernel, dil=dil),
        grid=(b, dil // n_res, seq // rows),
        in_specs=[qspec(w), kspec, kspec],
        out_specs=[qspec(w), qspec(LANES)],
        out_shape=[jax.ShapeDtypeStruct(q.shape, BF16),
                   jax.ShapeDtypeStruct((b, dil, seq, LANES), F32)],
        scratch_shapes=[pltpu.VMEM((3, N_HEADS, Q_TILE, KEY_TILE), F32)],
        compiler_params=_cparams(("arbitrary", "arbitrary", "arbitrary"), 48),
        name=f"attn_d{dil}",
    )(q, k, v)


def _to_natural(src_ref, nat_ref, tmp_ref, rows):
    n_slab = nat_ref.shape[0]
    step = DILATION_STEP
    for p in range(n_slab):
        cs = slice(p * LANES, (p + 1) * LANES)
        for r4 in range(step):
            if tmp_ref is None:
                quarter = src_ref[r4][:, cs].astype(F32)
            else:
                for c in range(step):
                    tmp_ref[p, r4, pl.ds(c, rows // (step * step), stride=step), :] = (
                        src_ref[r4 + step * c][:, cs].astype(F32))
                quarter = tmp_ref[p, r4]
            nat_ref[p, pl.ds(r4, rows // step, stride=step), :] = quarter
    return jnp.concatenate([nat_ref[p] for p in range(n_slab)], axis=1)


def _mix_out_kernel(x_ref, ta_ref, gb_ref, *refs):
    n_pat = len(DILATIONS)
    o_refs = refs[:n_pat]
    l_refs = refs[n_pat:2 * n_pat]
    pb_f32, wo_f32, g_ref, wr_ref, x1_ref, h2_ref, aff_ref = refs[2 * n_pat:2 * n_pat + 7]
    pb_ref, wo_ref = refs[2 * n_pat + 7:2 * n_pat + 9]
    stage_refs = refs[2 * n_pat + 9:]
    rows = x_ref.shape[0]

    @pl.when(jnp.logical_and(pl.program_id(0) == 0, pl.program_id(1) == 0))
    def _():
        pb_ref[...] = pb_f32[...].astype(BF16)
        wo_ref[...] = wo_f32[...].astype(BF16)

    stage_refs = list(stage_refs)
    outs = [o_refs[0][0].astype(F32)]
    lses = [l_refs[0][0]]
    for di in (1, 2):
        for src, dest in ((o_refs[di], outs), (l_refs[di], lses)):
            nat = stage_refs.pop(0)
            tmp = stage_refs.pop(0) if di == 2 else None
            dest.append(_to_natural(src, nat, tmp, rows))

    dens = [pltpu.roll(st, LANES - DEN_SHIFT, 1) for st in lses]
    lses = [st + jnp.log(den) for st, den in zip(lses, dens)]
    m = functools.reduce(jnp.maximum, lses)
    ws = [jnp.exp(l - m) for l in lses]
    inv = 1.0 / functools.reduce(lambda a, c: a + c, ws)
    lane = lax.broadcasted_iota(I32, (rows, LANES), 1)
    used = lane % STAT_LANES < DEN_SHIFT
    ws = [jnp.where(used, w * inv / den, 0.0) for w, den in zip(ws, dens)]
    k_i = lax.broadcasted_iota(I32, (2 * LANES, ATTN_WIDTH), 0) % LANES
    c_i = lax.broadcasted_iota(I32, (2 * LANES, ATTN_WIDTH), 1)
    spread = jnp.where(k_i == (c_i // HEAD_DIM) * STAT_LANES, 1.0, 0.0).astype(BF16)
    parts = []
    for w in ws:
        w_hi = w.astype(BF16)
        w_lo = (w - w_hi.astype(F32)).astype(BF16)
        parts.append(jnp.concatenate([w_hi, w_lo], axis=1))
    factors = _dot(jnp.concatenate(parts, axis=0), spread)
    o = None
    for i, o_p in enumerate(outs):
        term = factors[i * rows:(i + 1) * rows] * o_p
        o = term if o is None else o + term

    ob = _dot(o.astype(BF16), pb_ref[...])
    merged = (ta_ref[...].astype(F32) + gb_ref[...].astype(F32) * ob).astype(BF16)
    x1 = x_ref[...] + _dot(merged, wo_ref[...])
    x1_ref[...] = x1
    h2 = _rms(x1, g_ref[...])
    h2_ref[...] = h2.astype(BF16)
    h_hi = h2.astype(BF16)
    h_lo = (h2 - h_hi.astype(F32)).astype(BF16)
    wr = wr_ref[...]
    w_hi = wr.astype(BF16)
    w_lo = (wr - w_hi.astype(F32)).astype(BF16)
    n_e = wr.shape[0]
    by_hi = _dot_nt(jnp.concatenate([w_hi, w_lo], axis=0), h_hi)
    logits = by_hi[:n_e] + (_dot_nt(w_hi, h_lo) + by_hi[n_e:])
    e = jnp.exp(logits - jnp.max(logits, axis=0, keepdims=True))
    aff_ref[...] = e / jnp.sum(e, axis=0, keepdims=True)


def _mix_out(x, ta, gb, os_, ls_, w_pb, w_out, g, w_router_t):
    b, s, d = x.shape
    n_e = w_router_t.shape[0]
    const = lambda shape: pl.BlockSpec(shape, lambda bi, t: (0,) * len(shape))
    once = lambda shape: pl.BlockSpec(shape, lambda bi, t: (0,) * len(shape),
                                      pipeline_mode=pl.Buffered(1))
    row = lambda w: pl.BlockSpec((None, ROW_TILE, w), lambda bi, t: (bi, t, 0))
    res = lambda dil, w: pl.BlockSpec((None, dil, ROW_TILE // dil, w), lambda bi, t: (bi, 0, t, 0))
    stage = []
    for di in (1, 2):
        for slabs in (ATTN_WIDTH // LANES, 1):
            stage.append(pltpu.VMEM((slabs, ROW_TILE, LANES), F32))
            if di == 2:
                stage.append(pltpu.VMEM((slabs, DILATION_STEP, ROW_TILE // DILATION_STEP, LANES),
                                        F32))
    return pl.pallas_call(
        _mix_out_kernel,
        grid=(b, s // ROW_TILE),
        in_specs=[row(d), row(d), row(d)]
                 + [res(dil, ATTN_WIDTH) for dil in DILATIONS]
                 + [res(dil, LANES) for dil in DILATIONS]
                 + [once(w_pb.shape), once(w_out.shape), const(g.shape), const(w_router_t.shape)],
        out_specs=[row(d), row(d), pl.BlockSpec((None, n_e, ROW_TILE), lambda bi, t: (bi, 0, t))],
        out_shape=[jax.ShapeDtypeStruct((b, s, d), F32), jax.ShapeDtypeStruct((b, s, d), BF16),
                   jax.ShapeDtypeStruct((b, n_e, s), F32)],
        scratch_shapes=[pltpu.VMEM(w_pb.shape, BF16), pltpu.VMEM(w_out.shape, BF16)] + stage,
        compiler_params=_cparams(("arbitrary", "arbitrary"), 48),
        name="mix_out",
    )(x, ta, gb, *os_, *ls_, w_pb, w_out, g, w_router_t)


class _Plan(NamedTuple):
    tiles: int
    fast: int
    slow: int
    group: int


GATHER_PLAN = _Plan(tiles=GATHER_TOKENS // SLOT_TILE, fast=64, slow=GATHER_TOKENS + SLOT_ALIGN,
                    group=N_EXPERTS // 2)
COMBINE_PLAN = _Plan(tiles=1, fast=48, slow=SLOT_TILE + SLOT_ALIGN, group=N_EXPERTS)
FAST_WINDOW, SLOT_WINDOW = COMBINE_PLAN.fast, COMBINE_PLAN.slow


def _slot_plan(cum, plan, cap):
    rows = cum.shape[0]
    nxt = pltpu.roll(cum, LANES - plan.tiles, 1)
    lo = jnp.floor(cum * (1.0 / SLOT_ALIGN)) * SLOT_ALIGN
    ok = jnp.where(nxt - lo <= plan.fast, 1.0, 0.0)
    fit = jnp.min(ok.reshape(rows // plan.group, plan.group, LANES), axis=1)
    assert (cap - plan.fast) % SLOT_ALIGN == 0 and (cap - plan.slow) % SLOT_ALIGN == 0
    return (jnp.minimum(lo, cap - plan.fast).astype(I32),
            jnp.minimum(lo, cap - plan.slow).astype(I32), fit.astype(I32))


def _topk_kernel(aff_ref, rank_ref, *plan_refs, cap, plans):
    n_e, s = aff_ref.shape
    n_blk = s // SLOT_TILE
    aff = aff_ref[...]
    thr = jnp.zeros((n_e, 1), I32)
    for bit in range(30, -1, -1):
        cand = thr | (1 << bit)
        cnt = jnp.sum((aff >= pltpu.bitcast(cand, F32)).astype(I32), axis=1, keepdims=True)
        thr = jnp.where(cnt >= cap, cand, thr)
    above = aff >= pltpu.bitcast(thr + 1, F32)
    tie = jnp.logical_and(aff >= pltpu.bitcast(thr, F32), jnp.logical_not(above))
    need = (cap - jnp.sum(above.astype(I32), axis=1, keepdims=True)).astype(F32)
    r_i = lax.broadcasted_iota(I32, (SLOT_TILE, SLOT_TILE), 0)
    c_i = lax.broadcasted_iota(I32, (SLOT_TILE, SLOT_TILE), 1)
    tri = jnp.where(r_i < c_i, 1.0, 0.0).astype(BF16)
    lane = lax.broadcasted_iota(I32, (n_e, LANES), 1)
    run_tie = jnp.zeros((n_e, 1), F32)
    run_sel = jnp.zeros((n_e, 1), F32)
    cum = jnp.zeros((n_e, LANES), F32)
    for j in range(n_blk):
        cs = slice(j * SLOT_TILE, (j + 1) * SLOT_TILE)
        tie_f = jnp.where(tie[:, cs], 1.0, 0.0)
        tie_rank = _dot(tie_f.astype(BF16), tri) + run_tie
        run_tie = run_tie + jnp.sum(tie_f, axis=1, keepdims=True)
        sel_f = jnp.where(above[:, cs], 1.0, jnp.where(tie_rank < need, tie_f, 0.0))
        rank = _dot(sel_f.astype(BF16), tri) + run_sel
        rank_ref[:, cs] = jnp.where(sel_f > 0.0, rank, -1.0).astype(I32)
        cum = jnp.where(lane == j, run_sel, cum)
        run_sel = run_sel + jnp.sum(sel_f, axis=1, keepdims=True)
    cum = jnp.where(lane == n_blk, run_sel, cum)
    for i, plan in enumerate(plans):
        for ref, val in zip(plan_refs[3 * i:3 * i + 3], _slot_plan(cum, plan, cap)):
            ref[...] = val


def _topk(aff, cap, plans):
    b, n_e, s = aff.shape
    rows = b * n_e
    full = lambda r, w: pl.BlockSpec((r, w), lambda i: (0, 0))
    plan_specs, plan_shapes = [], []
    for plan in plans:
        for r in (rows, rows, rows // plan.group):
            plan_specs.append(full(r, LANES))
            plan_shapes.append(jax.ShapeDtypeStruct((r, LANES), I32))
    rank, *tables = pl.pallas_call(
        functools.partial(_topk_kernel, cap=cap, plans=plans),
        grid=(1,),
        in_specs=[full(rows, s)],
        out_specs=[full(rows, s)] + plan_specs,
        out_shape=[jax.ShapeDtypeStruct((rows, s), I32)] + plan_shapes,
        compiler_params=_cparams(("arbitrary",), 32),
        name="topk",
    )(aff.reshape(rows, s))
    tables = [t.reshape(-1) for t in tables]
    return rank.reshape(b, n_e, s), [tables[3 * i:3 * i + 3] for i in range(len(plans))]


def _window_start(table_ref, expert_row, tile):
    return pl.multiple_of(table_ref[expert_row * LANES + tile], SLOT_ALIGN)


def _span(jj, size):
    if isinstance(jj, int):
        return slice(jj * size, (jj + 1) * size)
    return pl.ds(pl.multiple_of(jj * size, size), size)


def _run_blocks(fits, fast, general):
    all_fit = functools.reduce(jnp.logical_and, fits)

    @pl.when(all_fit)
    def _():
        for jj in range(len(fits)):
            fast(jj)

    @pl.when(jnp.logical_not(all_fit))
    def _():
        def body(jj, carry):
            general(jj)
            return carry

        lax.fori_loop(0, len(fits), body, 0)


def _gather_kernel(fast_ref, slow_ref, fit_ref, rank_ref, aff_ref, h2_ref, xe_ref, gate_ref):
    n_all = rank_ref.shape[0]
    n_e = GATHER_PLAN.group
    rows = h2_ref.shape[0]
    bi = pl.program_id(0)
    t = pl.program_id(1)
    e0 = 0

    @pl.when(t == 0)
    def _():
        xe_ref[...] = jnp.zeros_like(xe_ref)
        gate_ref[...] = jnp.zeros_like(gate_ref)

    row_fast = lax.broadcasted_iota(I32, (GATHER_PLAN.fast, GATHER_TOKENS), 0)
    row_slow = lax.broadcasted_iota(I32, (GATHER_PLAN.slow, GATHER_TOKENS), 0)
    n_blk = rows // GATHER_TOKENS
    tile = lambda jj: (t * n_blk + jj) * GATHER_PLAN.tiles

    def window(jj, e, table_ref, row_i):
        toks = _span(jj, GATHER_TOKENS)
        lo = _window_start(table_ref, bi * n_all + e0 + e, tile(jj))
        return lo, (row_i + lo) == rank_ref[e0 + e, :, toks], aff_ref[e0 + e, :, toks]

    def add_window(e, lo, hit, aff_row, rows_e):
        win = pl.ds(lo, hit.shape[0])
        xe_ref[e0 + e, win, :] += rows_e.astype(BF16)
        gate_ref[e0 + e, win, :] += jnp.sum(jnp.where(hit, aff_row, 0.0), axis=1, keepdims=True)

    def tokens(jj):
        return h2_ref[_span(jj, GATHER_TOKENS), :]

    def fast(jj):
        ws = [window(jj, e, fast_ref, row_fast) for e in range(n_e)]
        stack = jnp.concatenate([jnp.where(h, 1.0, 0.0).astype(BF16) for _, h, _ in ws], axis=0)
        res = _dot(stack, tokens(jj))
        for e, (lo, h, aff_row) in enumerate(ws):
            add_window(e, lo, h, aff_row, res[e * GATHER_PLAN.fast:(e + 1) * GATHER_PLAN.fast])

    def general(jj):
        def body(e, carry):
            lo, h, aff_row = window(jj, e, slow_ref, row_slow)
            add_window(e, lo, h, aff_row, _dot(jnp.where(h, 1.0, 0.0).astype(BF16), tokens(jj)))
            return carry

        lax.fori_loop(0, n_e, body, 0)

    n_groups = n_all // n_e
    for g in range(n_groups):
        e0 = g * n_e
        fit_row = (bi * n_groups + g) * LANES
        _run_blocks([fit_ref[fit_row + tile(jj)] != 0 for jj in range(n_blk)], fast, general)


def _gather(plan_tables, rank, aff, h2, cap):
    b, s, d = h2.shape
    n_e = rank.shape[1]
    rows = cap
    per_tok = pl.BlockSpec((None, n_e, 1, GATHER_STEP_TOKENS), lambda bi, t, *_: (bi, 0, 0, t))
    whole = lambda w: pl.BlockSpec((None, n_e, rows, w), lambda bi, t, *_: (bi, 0, 0, 0))
    grid_spec = pltpu.PrefetchScalarGridSpec(
        num_scalar_prefetch=3,
        grid=(b, s // GATHER_STEP_TOKENS),
        in_specs=[per_tok, per_tok,
                  pl.BlockSpec((None, GATHER_STEP_TOKENS, d), lambda bi, t, *_: (bi, t, 0))],
        out_specs=[whole(d), whole(LANES)],
    )
    expert_rows = lambda a: a.reshape(b, n_e, 1, s)
    return pl.pallas_call(
        _gather_kernel,
        grid_spec=grid_spec,
        out_shape=[jax.ShapeDtypeStruct((b, n_e, rows, d), BF16),
                   jax.ShapeDtypeStruct((b, n_e, rows, LANES), F32)],
        compiler_params=_cparams(("arbitrary", "arbitrary"), 60),
        name="gather",
    )(*plan_tables, expert_rows(rank), expert_rows(aff), h2)


def _moe_ffn_kernel(xe_ref, gate_ref, wg_ref, wu_ref, wd_ref, ye_ref, wg_bf, wu_bf, wd_bf):
    n_seq, cap, d = xe_ref.shape

    @pl.when(pl.program_id(1) == 0)
    def _():
        wg_bf[...] = wg_ref[...].astype(BF16)
        wu_bf[...] = wu_ref[...].astype(BF16)
        wd_bf[...] = wd_ref[...].astype(BF16)

    xe = xe_ref[...].reshape(n_seq * cap, d)
    gate_h = _dot(xe, wg_bf[...])
    up_h = _dot(xe, wu_bf[...])
    hidden = (gate_h * _sigmoid(gate_h) * up_h).astype(BF16)
    ye = _dot(hidden, wd_bf[...]) * gate_ref[...].reshape(n_seq * cap, LANES)[:, 0:1]
    ye_ref[...] = ye.astype(BF16).reshape(n_seq, cap, d)


def _moe_ffn(xe, gate, wg, wu, wd, cap):
    b, n_e, rows, d = xe.shape
    hid = wg.shape[2]
    return pl.pallas_call(
        _moe_ffn_kernel,
        grid=(n_e, b // FFN_SEQS),
        in_specs=[
            pl.BlockSpec((FFN_SEQS, None, cap, d), lambda e, bi: (bi, e, 0, 0)),
            pl.BlockSpec((FFN_SEQS, None, cap, LANES), lambda e, bi: (bi, e, 0, 0)),
            pl.BlockSpec((None, d, hid), lambda e, bi: (e, 0, 0)),
            pl.BlockSpec((None, d, hid), lambda e, bi: (e, 0, 0)),
            pl.BlockSpec((None, hid, d), lambda e, bi: (e, 0, 0)),
        ],
        out_specs=pl.BlockSpec((FFN_SEQS, None, rows, d), lambda e, bi: (bi, e, 0, 0)),
        out_shape=jax.ShapeDtypeStruct((b, n_e, rows, d), BF16),
        scratch_shapes=[pltpu.VMEM((d, hid), BF16), pltpu.VMEM((d, hid), BF16),
                        pltpu.VMEM((hid, d), BF16)],
        compiler_params=_cparams(("arbitrary", "arbitrary"), 56),
        name="moe_ffn",
    )(xe, gate, wg, wu, wd)


def _combine_kernel(fast_ref, slow_ref, fit_ref, rank_ref, ye_ref, x1_hbm, g_ref, y_ref, rhs_ref,
                    acc_ref, x1_ring, ring_sem, *, steps_per_seq, n_steps):
    n_e = rank_ref.shape[0]
    rows = y_ref.shape[0]
    n_blk = rows // SLOT_TILE
    bi = pl.program_id(0)
    t = pl.program_id(1)
    step = bi * steps_per_seq + t

    def x1_copy(s):
        tile = x1_hbm.at[s // steps_per_seq, pl.ds((s % steps_per_seq) * rows, rows), :]
        slot = s % X1_RING_SLOTS
        return pltpu.make_async_copy(tile, x1_ring.at[slot], ring_sem.at[slot])

    @pl.when(step == 0)
    def _():
        for s in range(min(X1_RING_SLOTS - 1, n_steps)):
            x1_copy(s).start()

    @pl.when(step + (X1_RING_SLOTS - 1) < n_steps)
    def _():
        x1_copy(step + (X1_RING_SLOTS - 1)).start()

    x1_copy(step).wait()
    x1_ref = x1_ring.at[step % X1_RING_SLOTS]
    row_fast = lax.broadcasted_iota(I32, (FAST_WINDOW, SLOT_TILE), 0)
    row_slow = lax.broadcasted_iota(I32, (SLOT_WINDOW, SLOT_TILE), 0)

    def hits(jj, e, table_ref, row_i):
        lo = _window_start(table_ref, bi * n_e + e, t * n_blk + jj)
        hit = (row_i + lo) == rank_ref[e, :, _span(jj, SLOT_TILE)]
        return pl.ds(lo, row_i.shape[0]), jnp.where(hit, 1.0, 0.0).astype(BF16)

    def finish(jj, moe):
        toks = _span(jj, SLOT_TILE)
        y_ref[toks, :] = _rms(x1_ref[toks, :] + moe, g_ref[...])

    def fast(jj):
        rhs = rhs_ref.at[jj % rhs_ref.shape[0]]
        stack = []
        for e in range(n_e):
            win, hit = hits(jj, e, fast_ref, row_fast)
            rhs[e * FAST_WINDOW:(e + 1) * FAST_WINDOW, :] = ye_ref[e, win, :]
            stack.append(hit)
        finish(jj, _dot_tn(jnp.concatenate(stack, axis=0), rhs[...]))

    def general(jj):
        acc_ref[...] = jnp.zeros_like(acc_ref)

        def body(e, carry):
            win, hit = hits(jj, e, slow_ref, row_slow)
            acc_ref[...] += _dot_tn(hit, ye_ref[e, win, :])
            return carry

        lax.fori_loop(0, n_e, body, 0)
        finish(jj, acc_ref[...])

    _run_blocks([fit_ref[bi * LANES + t * n_blk + jj] != 0 for jj in range(n_blk)], fast, general)


def _combine(plan_tables, rank, ye, x1, g):
    b, s, d = x1.shape
    n_e, ye_rows = ye.shape[1], ye.shape[2]
    row = pl.BlockSpec((None, COMBINE_ROWS, d), lambda bi, t, *_: (bi, t, 0))
    grid_spec = pltpu.PrefetchScalarGridSpec(
        num_scalar_prefetch=3,
        grid=(b, s // COMBINE_ROWS),
        in_specs=[
            pl.BlockSpec((None, n_e, 1, COMBINE_ROWS), lambda bi, t, *_: (bi, 0, 0, t)),
            pl.BlockSpec((None, n_e, ye_rows, d), lambda bi, t, *_: (bi, 0, 0, 0)),
            pl.BlockSpec(memory_space=pl.ANY),
            pl.BlockSpec(g.shape, lambda bi, t, *_: (0, 0)),
        ],
        out_specs=row,
        scratch_shapes=[pltpu.VMEM((2, n_e * FAST_WINDOW, d), BF16),
                        pltpu.VMEM((SLOT_TILE, d), F32),
                        pltpu.VMEM((X1_RING_SLOTS, COMBINE_ROWS, d), F32),
                        pltpu.SemaphoreType.DMA((X1_RING_SLOTS,))],
    )
    steps_per_seq = s // COMBINE_ROWS
    return pl.pallas_call(
        functools.partial(_combine_kernel, steps_per_seq=steps_per_seq,
                          n_steps=b * steps_per_seq),
        grid_spec=grid_spec,
        out_shape=jax.ShapeDtypeStruct((b, s, d), F32),
        compiler_params=_cparams(("arbitrary", "arbitrary"), 60),
        name="combine",
    )(*plan_tables, rank.reshape(b, n_e, 1, s), ye, x1, g)


def _moe_stages(aff, h2, x1, wg, wu, wd, g_final, cap):
    rank, (gather_tables, combine_tables) = _topk(aff, cap, (GATHER_PLAN, COMBINE_PLAN))
    xe, gate = _gather(gather_tables, rank, aff, h2, cap)
    ye = _moe_ffn(xe, gate, wg, wu, wd, cap)
    return _combine(combine_tables, rank, ye, x1, g_final)


def kernel(x, norm_mix_g, w_in, b_gate, gmlp_norm_g, w_spatial, b_spatial, w_proj_a, w_proj_b,
           w_out, norm_ffn_g, w_router, w_e_gate, w_e_up, w_e_down, norm_final_g):
    b, s, d = x.shape
    assert w_in.shape[0] == 1, "single-layer block"
    cap = CAPACITY_FACTOR * s // N_EXPERTS
    group_width = GMLP_WIDTH // GMLP_GROUPS
    ws_pairs = w_spatial[0].astype(BF16).reshape(GMLP_GROUPS // 2, 2 * CHUNK, CHUNK)
    bsp = jnp.repeat(b_spatial[0].T, group_width, axis=1)
    qkv, ta, gb = _mix_in(x, norm_mix_g, _cast_bf16(w_in[0]), b_gate, gmlp_norm_g, ws_pairs, bsp,
                          w_proj_a[0].astype(BF16))
    os_, ls_ = [], []
    for (q, k, v), dil in zip(qkv, DILATIONS):
        o, lse = _attn_pattern(q, k, v, dil)
        os_.append(o)
        ls_.append(lse)
    x1, h2, aff = _mix_out(x, ta, gb, os_, ls_, w_proj_b[0], w_out[0], norm_ffn_g, w_router[0].T)
    return _moe_stages(aff, h2, x1, w_e_gate[0], w_e_up[0], w_e_down[0], norm_final_g[None], cap)
```
